```python
import jax, jax.numpy as jnp
from jax import lax
import numpy as np

D_MODEL = 1024
BATCH = 8
SEQ = 16384
DEPTH = 4

CHUNK = 64
N_MIXERS = 3
N_LAYERS_A = (DEPTH + 2) // 3
N_LAYERS_B = (DEPTH + 1) // 3
N_LAYERS_C = DEPTH // 3
SHORT_CONV_WIDTH = 3
POOL_WINDOWS = (2, 4, 8, 16)
N_POOL_GROUPS = len(POOL_WINDOWS)
POOL_GROUP_DIM = D_MODEL // N_POOL_GROUPS
CONFORMER_CONV_WIDTH = 31
FFN_DIM = 2816
FFN_CONV_WIDTH = 3
RMS_EPS = 1e-6
LN_EPS = 1e-5

kernel_name = "hybrid_conv_pool_conformer_trunk"


def rmsnorm(x, g):
    xf = x.astype(jnp.float32)
    y = xf * lax.rsqrt(jnp.mean(xf * xf, axis=-1, keepdims=True) + RMS_EPS)
    return (y * g.astype(jnp.float32)).astype(x.dtype)


def layernorm(x, g, b):
    xf = x.astype(jnp.float32)
    mu = jnp.mean(xf, axis=-1, keepdims=True)
    var = jnp.mean(jnp.square(xf - mu), axis=-1, keepdims=True)
    y = (xf - mu) * lax.rsqrt(var + LN_EPS)
    return (y * g.astype(jnp.float32) + b.astype(jnp.float32)).astype(x.dtype)


def causal_depthwise_conv(x, w, b=None):
    k, ch = w.shape
    xp = jnp.pad(x, ((0, 0), (k - 1, 0), (0, 0)))
    y = lax.conv_general_dilated(
        xp, w[:, None, :].astype(x.dtype), window_strides=(1,), padding="VALID",
        dimension_numbers=("NWC", "WIO", "NWC"), feature_group_count=ch)
    if b is not None:
        y = y + b
    return y


def short_conv_mixer(h, w_in, w_conv, w_out):
    bgate, cgate, hin = jnp.split(h @ w_in, 3, axis=-1)
    u = causal_depthwise_conv(cgate * hin, w_conv)
    return (bgate * u) @ w_out


def trailing_mean(xf, window):
    s = xf.shape[1]
    csum = jnp.cumsum(xf, axis=1)
    shifted = jnp.pad(csum, ((0, 0), (window, 0), (0, 0)))[:, :s]
    count = jnp.minimum(jnp.arange(1, s + 1), window).astype(jnp.float32)[None, :, None]
    return (csum - shifted) / count


def pooling_mixer(h, w_group, b_group, scale):
    bsz, s, d = h.shape
    hf = h.astype(jnp.float32).reshape(bsz, s, N_POOL_GROUPS, POOL_GROUP_DIM)
    pooled = jnp.stack(
        [trailing_mean(hf[:, :, g], w) - hf[:, :, g] for g, w in enumerate(POOL_WINDOWS)],
        axis=2).astype(h.dtype)
    y = jnp.einsum("bsgc,gcd->bsgd", pooled, w_group).reshape(bsz, s, d)
    return (y + b_group) * scale


def conformer_conv_module(h, w_pw1, b_pw1, w_dw, b_dw, ln_g, ln_b, w_pw2, b_pw2):
    a, g = jnp.split(h @ w_pw1 + b_pw1, 2, axis=-1)
    u = a * jax.nn.sigmoid(g)
    u = causal_depthwise_conv(u, w_dw, b_dw)
    u = jax.nn.silu(layernorm(u, ln_g, ln_b))
    return u @ w_pw2 + b_pw2


def conv_ffn(h, w_up, w_conv, b_conv, w_down):
    gate, val = jnp.split(h @ w_up, 2, axis=-1)
    gate = causal_depthwise_conv(gate, w_conv, b_conv)
    return (jax.nn.silu(gate) * val) @ w_down


def _fwd_setup_inputs(seed: int = 0) -> dict:
    key = jax.random.key(seed)
    ks = iter(jax.random.split(key, 32))
    D = D_MODEL

    def nrm(shape, scale):
        return jax.random.normal(next(ks), shape, jnp.float32) * scale

    return {
        "x": nrm((BATCH, SEQ, D), 1.0),
        "c": nrm((BATCH, D), 1.0),
        "w_mod": nrm((DEPTH, D, 6 * D), 0.5 * D ** -0.5),
        "b_mod": nrm((DEPTH, 6 * D), 0.02),
        "norm_g": 1.0 + nrm((DEPTH, 4, D), 0.05),
        "sc_w_in": nrm((N_LAYERS_A, D, 3 * D), D ** -0.5),
        "sc_conv": nrm((N_LAYERS_A, SHORT_CONV_WIDTH, D), SHORT_CONV_WIDTH ** -0.5),
        "sc_w_out": nrm((N_LAYERS_A, D, D), D ** -0.5),
        "pool_w": nrm((N_LAYERS_B, N_POOL_GROUPS, POOL_GROUP_DIM, POOL_GROUP_DIM), POOL_GROUP_DIM ** -0.5),
        "pool_b": nrm((N_LAYERS_B, D), 0.02),
        "pool_scale": 1.0 + nrm((N_LAYERS_B, D), 0.05),
        "cf_w_pw1": nrm((N_LAYERS_C, D, 2 * D), D ** -0.5),
        "cf_b_pw1": nrm((N_LAYERS_C, 2 * D), 0.02),
        "cf_w_dw": nrm((N_LAYERS_C, CONFORMER_CONV_WIDTH, D), CONFORMER_CONV_WIDTH ** -0.5),
        "cf_b_dw": nrm((N_LAYERS_C, D), 0.02),
        "cf_ln_g": 1.0 + nrm((N_LAYERS_C, D), 0.05),
        "cf_ln_b": nrm((N_LAYERS_C, D), 0.02),
        "cf_w_pw2": nrm((N_LAYERS_C, D, D), D ** -0.5),
        "cf_b_pw2": nrm((N_LAYERS_C, D), 0.02),
        "ffn_w_up": nrm((DEPTH, D, 2 * FFN_DIM), D ** -0.5),
        "ffn_conv": nrm((DEPTH, FFN_CONV_WIDTH, FFN_DIM), FFN_CONV_WIDTH ** -0.5),
        "ffn_b_conv": nrm((DEPTH, FFN_DIM), 0.02),
        "ffn_w_down": nrm((DEPTH, FFN_DIM, D), FFN_DIM ** -0.5),
    }


def _fwd_reference(x, c, w_mod, b_mod, norm_g, sc_w_in, sc_conv, sc_w_out, pool_w, pool_b, pool_scale,
              cf_w_pw1, cf_b_pw1, cf_w_dw, cf_b_dw, cf_ln_g, cf_ln_b, cf_w_pw2, cf_b_pw2,
              ffn_w_up, ffn_conv, ffn_b_conv, ffn_w_down):
    c_act = jax.nn.silu(c)
    for i in range(DEPTH):
        mod = (c_act @ w_mod[i] + b_mod[i])[:, None, :]
        sh1, sc1, g1, sh2, sc2, g2 = jnp.split(mod, 6, axis=-1)

        h = rmsnorm(x, norm_g[i, 0]) * (1.0 + sc1) + sh1
        kind, j = i % N_MIXERS, i // N_MIXERS
        if kind == 0:
            m = short_conv_mixer(h, sc_w_in[j], sc_conv[j], sc_w_out[j])
        elif kind == 1:
            m = pooling_mixer(h, pool_w[j], pool_b[j], pool_scale[j])
        else:
            m = conformer_conv_module(h, cf_w_pw1[j], cf_b_pw1[j], cf_w_dw[j], cf_b_dw[j],
                                      cf_ln_g[j], cf_ln_b[j], cf_w_pw2[j], cf_b_pw2[j])
        x = x + g1 * rmsnorm(m, norm_g[i, 1])

        h = rmsnorm(x, norm_g[i, 2]) * (1.0 + sc2) + sh2
        f = conv_ffn(h, ffn_w_up[i], ffn_conv[i], ffn_b_conv[i], ffn_w_down[i])
        x = x + g2 * rmsnorm(f, norm_g[i, 3])
    return x


import jax as _jax
import jax.numpy as _jnp

TWIN_FORMAT = 'train_step'
FWD_PARAMS = ['x', 'c', 'w_mod', 'b_mod', 'norm_g', 'sc_w_in', 'sc_conv', 'sc_w_out', 'pool_w', 'pool_b', 'pool_scale', 'cf_w_pw1', 'cf_b_pw1', 'cf_w_dw', 'cf_b_dw', 'cf_ln_g', 'cf_ln_b', 'cf_w_pw2', 'cf_b_pw2', 'ffn_w_up', 'ffn_conv', 'ffn_b_conv', 'ffn_w_down']
TWIN_WEIGHTS = ['w_mod', 'b_mod', 'norm_g', 'sc_w_in', 'sc_conv', 'sc_w_out', 'pool_w', 'pool_b', 'pool_scale', 'cf_w_pw1', 'cf_b_pw1', 'cf_w_dw', 'cf_b_dw', 'cf_ln_g', 'cf_ln_b', 'cf_w_pw2', 'cf_b_pw2', 'ffn_w_up', 'ffn_conv', 'ffn_b_conv', 'ffn_w_down']
TWIN_DIFF_INPUT = 'x'
TWIN_INPUTS = ['x', 'c', 'w_mod', 'b_mod', 'norm_g', 'sc_w_in', 'sc_conv', 'sc_w_out', 'pool_w', 'pool_b', 'pool_scale', 'cf_w_pw1', 'cf_b_pw1', 'cf_w_dw', 'cf_b_dw', 'cf_ln_g', 'cf_ln_b', 'cf_w_pw2', 'cf_b_pw2', 'ffn_w_up', 'ffn_conv', 'ffn_b_conv', 'ffn_w_down', 'loss_target', 'm_w_mod', 'm_b_mod', 'm_norm_g', 'm_sc_w_in', 'm_sc_conv', 'm_sc_w_out', 'm_pool_w', 'm_pool_b', 'm_pool_scale', 'm_cf_w_pw1', 'm_cf_b_pw1', 'm_cf_w_dw', 'm_cf_b_dw', 'm_cf_ln_g', 'm_cf_ln_b', 'm_cf_w_pw2', 'm_cf_b_pw2', 'm_ffn_w_up', 'm_ffn_conv', 'm_ffn_b_conv', 'm_ffn_w_down', 'v_w_mod', 'v_b_mod', 'v_norm_g', 'v_sc_w_in', 'v_sc_conv', 'v_sc_w_out', 'v_pool_w', 'v_pool_b', 'v_pool_scale', 'v_cf_w_pw1', 'v_cf_b_pw1', 'v_cf_w_dw', 'v_cf_b_dw', 'v_cf_ln_g', 'v_cf_ln_b', 'v_cf_w_pw2', 'v_cf_b_pw2', 'v_ffn_w_up', 'v_ffn_conv', 'v_ffn_b_conv', 'v_ffn_w_down']
TWIN_OUTPUTS = ['loss', 'grad_x', 'grad_w_mod', 'grad_b_mod', 'grad_norm_g', 'grad_sc_w_in', 'grad_sc_conv', 'grad_sc_w_out', 'grad_pool_w', 'grad_pool_b', 'grad_pool_scale', 'grad_cf_w_pw1', 'grad_cf_b_pw1', 'grad_cf_w_dw', 'grad_cf_b_dw', 'grad_cf_ln_g', 'grad_cf_ln_b', 'grad_cf_w_pw2', 'grad_cf_b_pw2', 'grad_ffn_w_up', 'grad_ffn_conv', 'grad_ffn_b_conv', 'grad_ffn_w_down', 'delta_w_mod', 'delta_b_mod', 'delta_norm_g', 'delta_sc_w_in', 'delta_sc_conv', 'delta_sc_w_out', 'delta_pool_w', 'delta_pool_b', 'delta_pool_scale', 'delta_cf_w_pw1', 'delta_cf_b_pw1', 'delta_cf_w_dw', 'delta_cf_b_dw', 'delta_cf_ln_g', 'delta_cf_ln_b', 'delta_cf_w_pw2', 'delta_cf_b_pw2', 'delta_ffn_w_up', 'delta_ffn_conv', 'delta_ffn_b_conv', 'delta_ffn_w_down', 'new_m_w_mod', 'new_m_b_mod', 'new_m_norm_g', 'new_m_sc_w_in', 'new_m_sc_conv', 'new_m_sc_w_out', 'new_m_pool_w', 'new_m_pool_b', 'new_m_pool_scale', 'new_m_cf_w_pw1', 'new_m_cf_b_pw1', 'new_m_cf_w_dw', 'new_m_cf_b_dw', 'new_m_cf_ln_g', 'new_m_cf_ln_b', 'new_m_cf_w_pw2', 'new_m_cf_b_pw2', 'new_m_ffn_w_up', 'new_m_ffn_conv', 'new_m_ffn_b_conv', 'new_m_ffn_w_down', 'new_v_w_mod', 'new_v_b_mod', 'new_v_norm_g', 'new_v_sc_w_in', 'new_v_sc_conv', 'new_v_sc_w_out', 'new_v_pool_w', 'new_v_pool_b', 'new_v_pool_scale', 'new_v_cf_w_pw1', 'new_v_cf_b_pw1', 'new_v_cf_w_dw', 'new_v_cf_b_dw', 'new_v_cf_ln_g', 'new_v_cf_ln_b', 'new_v_cf_w_pw2', 'new_v_cf_b_pw2', 'new_v_ffn_w_up', 'new_v_ffn_conv', 'new_v_ffn_b_conv', 'new_v_ffn_w_down']
TWIN_LEAF_KINDS = {'loss': 'loss', 'grad_x': 'grad_x', 'grad_w_mod': 'grad_w', 'grad_b_mod': 'grad_w', 'grad_norm_g': 'grad_w', 'grad_sc_w_in': 'grad_w', 'grad_sc_conv': 'grad_w', 'grad_sc_w_out': 'grad_w', 'grad_pool_w': 'grad_w', 'grad_pool_b': 'grad_w', 'grad_pool_scale': 'grad_w', 'grad_cf_w_pw1': 'grad_w', 'grad_cf_b_pw1': 'grad_w', 'grad_cf_w_dw': 'grad_w', 'grad_cf_b_dw': 'grad_w', 'grad_cf_ln_g': 'grad_w', 'grad_cf_ln_b': 'grad_w', 'grad_cf_w_pw2': 'grad_w', 'grad_cf_b_pw2': 'grad_w', 'grad_ffn_w_up': 'grad_w', 'grad_ffn_conv': 'grad_w', 'grad_ffn_b_conv': 'grad_w', 'grad_ffn_w_down': 'grad_w', 'delta_w_mod': 'delta_w', 'delta_b_mod': 'delta_w', 'delta_norm_g': 'delta_w', 'delta_sc_w_in': 'delta_w', 'delta_sc_conv': 'delta_w', 'delta_sc_w_out': 'delta_w', 'delta_pool_w': 'delta_w', 'delta_pool_b': 'delta_w', 'delta_pool_scale': 'delta_w', 'delta_cf_w_pw1': 'delta_w', 'delta_cf_b_pw1': 'delta_w', 'delta_cf_w_dw': 'delta_w', 'delta_cf_b_dw': 'delta_w', 'delta_cf_ln_g': 'delta_w', 'delta_cf_ln_b': 'delta_w', 'delta_cf_w_pw2': 'delta_w', 'delta_cf_b_pw2': 'delta_w', 'delta_ffn_w_up': 'delta_w', 'delta_ffn_conv': 'delta_w', 'delta_ffn_b_conv': 'delta_w', 'delta_ffn_w_down': 'delta_w', 'new_m_w_mod': 'new_m', 'new_m_b_mod': 'new_m', 'new_m_norm_g': 'new_m', 'new_m_sc_w_in': 'new_m', 'new_m_sc_conv': 'new_m', 'new_m_sc_w_out': 'new_m', 'new_m_pool_w': 'new_m', 'new_m_pool_b': 'new_m', 'new_m_pool_scale': 'new_m', 'new_m_cf_w_pw1': 'new_m', 'new_m_cf_b_pw1': 'new_m', 'new_m_cf_w_dw': 'new_m', 'new_m_cf_b_dw': 'new_m', 'new_m_cf_ln_g': 'new_m', 'new_m_cf_ln_b': 'new_m', 'new_m_cf_w_pw2': 'new_m', 'new_m_cf_b_pw2': 'new_m', 'new_m_ffn_w_up': 'new_m', 'new_m_ffn_conv': 'new_m', 'new_m_ffn_b_conv': 'new_m', 'new_m_ffn_w_down': 'new_m', 'new_v_w_mod': 'new_v', 'new_v_b_mod': 'new_v', 'new_v_norm_g': 'new_v', 'new_v_sc_w_in': 'new_v', 'new_v_sc_conv': 'new_v', 'new_v_sc_w_out': 'new_v', 'new_v_pool_w': 'new_v', 'new_v_pool_b': 'new_v', 'new_v_pool_scale': 'new_v', 'new_v_cf_w_pw1': 'new_v', 'new_v_cf_b_pw1': 'new_v', 'new_v_cf_w_dw': 'new_v', 'new_v_cf_b_dw': 'new_v', 'new_v_cf_ln_g': 'new_v', 'new_v_cf_ln_b': 'new_v', 'new_v_cf_w_pw2': 'new_v', 'new_v_cf_b_pw2': 'new_v', 'new_v_ffn_w_up': 'new_v', 'new_v_ffn_conv': 'new_v', 'new_v_ffn_b_conv': 'new_v', 'new_v_ffn_w_down': 'new_v'}


def _forward(args):
    return _fwd_reference(*[args[k] for k in FWD_PARAMS])


def _output_shape():
    def fwd():
        inp = _fwd_setup_inputs(0)
        return _fwd_reference(*[inp[k] for k in FWD_PARAMS])
    out = _jax.eval_shape(fwd)
    return out.shape, out.dtype

N_MICROBATCH = 1
ADAM_LR = 0.001
ADAM_B1 = 0.9
ADAM_B2 = 0.999
ADAM_EPS = 1e-08
ADAM_WD = 0.01
ADAM_STEP = 10
PER_EXAMPLE_BATCH_AXIS = {'x': 0, 'c': 0, 'loss_target': 0}
SHARED_INPUTS = []
_WEIGHT_DTYPES = {'w_mod': _jnp.float32, 'b_mod': _jnp.float32, 'norm_g': _jnp.float32, 'sc_w_in': _jnp.float32, 'sc_conv': _jnp.float32, 'sc_w_out': _jnp.float32, 'pool_w': _jnp.float32, 'pool_b': _jnp.float32, 'pool_scale': _jnp.float32, 'cf_w_pw1': _jnp.float32, 'cf_b_pw1': _jnp.float32, 'cf_w_dw': _jnp.float32, 'cf_b_dw': _jnp.float32, 'cf_ln_g': _jnp.float32, 'cf_ln_b': _jnp.float32, 'cf_w_pw2': _jnp.float32, 'cf_b_pw2': _jnp.float32, 'ffn_w_up': _jnp.float32, 'ffn_conv': _jnp.float32, 'ffn_b_conv': _jnp.float32, 'ffn_w_down': _jnp.float32}
MOMENT_SCALE = {'w_mod': 4.664524e+00, 'b_mod': 1.043338e+01, 'norm_g': 9.249386e+00, 'sc_w_in': 2.395689e-01, 'sc_conv': 2.435092e-01, 'sc_w_out': 2.626555e-01, 'pool_w': 7.052551e-01, 'pool_b': 1.004274e+00, 'pool_scale': 6.165917e+00, 'cf_w_pw1': 2.640708e-01, 'cf_b_pw1': 1.089986e+00, 'cf_w_dw': 3.831584e-01, 'cf_b_dw': 2.575480e+00, 'cf_ln_g': 1.119812e+00, 'cf_ln_b': 1.818340e+00, 'cf_w_pw2': 6.957041e-01, 'cf_b_pw2': 3.801166e+00, 'ffn_w_up': 1.615097e-01, 'ffn_conv': 1.692361e-01, 'ffn_b_conv': 2.479922e-01, 'ffn_w_down': 3.027822e-01}


def _to_microbatches(a, axis):
    t = _jnp.moveaxis(a, axis, 0)
    t = t.reshape((N_MICROBATCH, t.shape[0] // N_MICROBATCH) + t.shape[1:])
    return _jnp.moveaxis(t, 1, axis + 1)


def setup_inputs(seed: int = 0) -> dict:
    inp = _fwd_setup_inputs(seed)
    key = _jax.random.fold_in(_jax.random.key(seed), 7919)
    shape, _ = _output_shape()
    out = dict(inp)
    out["loss_target"] = _jax.random.normal(_jax.random.fold_in(key, 0), shape, _jnp.float32)
    for i, name in enumerate(TWIN_WEIGHTS):
        w = inp[name].astype(_jnp.float32)
        if MOMENT_SCALE is None:
            s = _jnp.sqrt(_jnp.mean(_jnp.square(w)) + 1e-30)
        else:
            s = MOMENT_SCALE[name]
        km, kv = _jax.random.split(_jax.random.fold_in(key, i + 1))
        out[name] = w
        out["m_" + name] = s * _jax.random.normal(km, w.shape, _jnp.float32)
        out["v_" + name] = (s * s) * _jax.random.uniform(kv, w.shape, _jnp.float32, 0.5, 1.5)
    if N_MICROBATCH > 1:
        for name, axis in PER_EXAMPLE_BATCH_AXIS.items():
            out[name] = _to_microbatches(out[name], axis)
    return {'x': out['x'], 'c': out['c'], 'w_mod': out['w_mod'], 'b_mod': out['b_mod'], 'norm_g': out['norm_g'], 'sc_w_in': out['sc_w_in'], 'sc_conv': out['sc_conv'], 'sc_w_out': out['sc_w_out'], 'pool_w': out['pool_w'], 'pool_b': out['pool_b'], 'pool_scale': out['pool_scale'], 'cf_w_pw1': out['cf_w_pw1'], 'cf_b_pw1': out['cf_b_pw1'], 'cf_w_dw': out['cf_w_dw'], 'cf_b_dw': out['cf_b_dw'], 'cf_ln_g': out['cf_ln_g'], 'cf_ln_b': out['cf_ln_b'], 'cf_w_pw2': out['cf_w_pw2'], 'cf_b_pw2': out['cf_b_pw2'], 'ffn_w_up': out['ffn_w_up'], 'ffn_conv': out['ffn_conv'], 'ffn_b_conv': out['ffn_b_conv'], 'ffn_w_down': out['ffn_w_down'], 'loss_target': out['loss_target'], 'm_w_mod': out['m_w_mod'], 'm_b_mod': out['m_b_mod'], 'm_norm_g': out['m_norm_g'], 'm_sc_w_in': out['m_sc_w_in'], 'm_sc_conv': out['m_sc_conv'], 'm_sc_w_out': out['m_sc_w_out'], 'm_pool_w': out['m_pool_w'], 'm_pool_b': out['m_pool_b'], 'm_pool_scale': out['m_pool_scale'], 'm_cf_w_pw1': out['m_cf_w_pw1'], 'm_cf_b_pw1': out['m_cf_b_pw1'], 'm_cf_w_dw': out['m_cf_w_dw'], 'm_cf_b_dw': out['m_cf_b_dw'], 'm_cf_ln_g': out['m_cf_ln_g'], 'm_cf_ln_b': out['m_cf_ln_b'], 'm_cf_w_pw2': out['m_cf_w_pw2'], 'm_cf_b_pw2': out['m_cf_b_pw2'], 'm_ffn_w_up': out['m_ffn_w_up'], 'm_ffn_conv': out['m_ffn_conv'], 'm_ffn_b_conv': out['m_ffn_b_conv'], 'm_ffn_w_down': out['m_ffn_w_down'], 'v_w_mod': out['v_w_mod'], 'v_b_mod': out['v_b_mod'], 'v_norm_g': out['v_norm_g'], 'v_sc_w_in': out['v_sc_w_in'], 'v_sc_conv': out['v_sc_conv'], 'v_sc_w_out': out['v_sc_w_out'], 'v_pool_w': out['v_pool_w'], 'v_pool_b': out['v_pool_b'], 'v_pool_scale': out['v_pool_scale'], 'v_cf_w_pw1': out['v_cf_w_pw1'], 'v_cf_b_pw1': out['v_cf_b_pw1'], 'v_cf_w_dw': out['v_cf_w_dw'], 'v_cf_b_dw': out['v_cf_b_dw'], 'v_cf_ln_g': out['v_cf_ln_g'], 'v_cf_ln_b': out['v_cf_ln_b'], 'v_cf_w_pw2': out['v_cf_w_pw2'], 'v_cf_b_pw2': out['v_cf_b_pw2'], 'v_ffn_w_up': out['v_ffn_w_up'], 'v_ffn_conv': out['v_ffn_conv'], 'v_ffn_b_conv': out['v_ffn_b_conv'], 'v_ffn_w_down': out['v_ffn_w_down']}


def _loss(weights, diff, rest, loss_target):
    with _jax.named_scope("forward"):
        args = {**rest, TWIN_DIFF_INPUT: diff, **{k: w.astype(_WEIGHT_DTYPES[k]) for k, w in weights.items()}}
        y = _forward(args)
    with _jax.named_scope("loss_head"):
        err = _jnp.square(y.astype(_jnp.float32) - loss_target)
        return 0.5 * _jnp.sum(_jnp.mean(err, axis=-1)) if err.ndim else 0.5 * err


def _adamw(w, g, m, v):
    m = ADAM_B1 * m + (1.0 - ADAM_B1) * g
    v = ADAM_B2 * v + (1.0 - ADAM_B2) * _jnp.square(g)
    m_hat = m / (1.0 - ADAM_B1 ** ADAM_STEP)
    v_hat = v / (1.0 - ADAM_B2 ** ADAM_STEP)
    delta = -ADAM_LR * (m_hat / (_jnp.sqrt(v_hat) + ADAM_EPS) + ADAM_WD * w)
    return delta, m, v


def reference(x, c, w_mod, b_mod, norm_g, sc_w_in, sc_conv, sc_w_out, pool_w, pool_b, pool_scale, cf_w_pw1, cf_b_pw1, cf_w_dw, cf_b_dw, cf_ln_g, cf_ln_b, cf_w_pw2, cf_b_pw2, ffn_w_up, ffn_conv, ffn_b_conv, ffn_w_down, loss_target, m_w_mod, m_b_mod, m_norm_g, m_sc_w_in, m_sc_conv, m_sc_w_out, m_pool_w, m_pool_b, m_pool_scale, m_cf_w_pw1, m_cf_b_pw1, m_cf_w_dw, m_cf_b_dw, m_cf_ln_g, m_cf_ln_b, m_cf_w_pw2, m_cf_b_pw2, m_ffn_w_up, m_ffn_conv, m_ffn_b_conv, m_ffn_w_down, v_w_mod, v_b_mod, v_norm_g, v_sc_w_in, v_sc_conv, v_sc_w_out, v_pool_w, v_pool_b, v_pool_scale, v_cf_w_pw1, v_cf_b_pw1, v_cf_w_dw, v_cf_b_dw, v_cf_ln_g, v_cf_ln_b, v_cf_w_pw2, v_cf_b_pw2, v_ffn_w_up, v_ffn_conv, v_ffn_b_conv, v_ffn_w_down):
    given = dict(x=x, c=c, w_mod=w_mod, b_mod=b_mod, norm_g=norm_g, sc_w_in=sc_w_in, sc_conv=sc_conv, sc_w_out=sc_w_out, pool_w=pool_w, pool_b=pool_b, pool_scale=pool_scale, cf_w_pw1=cf_w_pw1, cf_b_pw1=cf_b_pw1, cf_w_dw=cf_w_dw, cf_b_dw=cf_b_dw, cf_ln_g=cf_ln_g, cf_ln_b=cf_ln_b, cf_w_pw2=cf_w_pw2, cf_b_pw2=cf_b_pw2, ffn_w_up=ffn_w_up, ffn_conv=ffn_conv, ffn_b_conv=ffn_b_conv, ffn_w_down=ffn_w_down, loss_target=loss_target, m_w_mod=m_w_mod, m_b_mod=m_b_mod, m_norm_g=m_norm_g, m_sc_w_in=m_sc_w_in, m_sc_conv=m_sc_conv, m_sc_w_out=m_sc_w_out, m_pool_w=m_pool_w, m_pool_b=m_pool_b, m_pool_scale=m_pool_scale, m_cf_w_pw1=m_cf_w_pw1, m_cf_b_pw1=m_cf_b_pw1, m_cf_w_dw=m_cf_w_dw, m_cf_b_dw=m_cf_b_dw, m_cf_ln_g=m_cf_ln_g, m_cf_ln_b=m_cf_ln_b, m_cf_w_pw2=m_cf_w_pw2, m_cf_b_pw2=m_cf_b_pw2, m_ffn_w_up=m_ffn_w_up, m_ffn_conv=m_ffn_conv, m_ffn_b_conv=m_ffn_b_conv, m_ffn_w_down=m_ffn_w_down, v_w_mod=v_w_mod, v_b_mod=v_b_mod, v_norm_g=v_norm_g, v_sc_w_in=v_sc_w_in, v_sc_conv=v_sc_conv, v_sc_w_out=v_sc_w_out, v_pool_w=v_pool_w, v_pool_b=v_pool_b, v_pool_scale=v_pool_scale, v_cf_w_pw1=v_cf_w_pw1, v_cf_b_pw1=v_cf_b_pw1, v_cf_w_dw=v_cf_w_dw, v_cf_b_dw=v_cf_b_dw, v_cf_ln_g=v_cf_ln_g, v_cf_ln_b=v_cf_ln_b, v_cf_w_pw2=v_cf_w_pw2, v_cf_b_pw2=v_cf_b_pw2, v_ffn_w_up=v_ffn_w_up, v_ffn_conv=v_ffn_conv, v_ffn_b_conv=v_ffn_b_conv, v_ffn_w_down=v_ffn_w_down)
    weights = {n: given[n] for n in TWIN_WEIGHTS}
    shared = {n: given[n] for n in SHARED_INPUTS}
    per_example = {n: given[n] for n in ['x', 'c']}
    grad_fn = _jax.value_and_grad(_loss, argnums=(0, 1))

    def one_microbatch(ex, loss_target):
        ex = dict(ex)
        diff = ex.pop(TWIN_DIFF_INPUT)
        return grad_fn(weights, diff, {**shared, **ex}, loss_target)

    if N_MICROBATCH == 1:
        loss, (grad_w, grad_x) = one_microbatch(per_example, given["loss_target"])
    else:
        def body(carry, xs):
            loss_sum, grad_sum = carry
            l_k, (gw_k, gx_k) = one_microbatch(xs[0], xs[1])
            with _jax.named_scope("update"):
                return (loss_sum + l_k, _jax.tree.map(_jnp.add, grad_sum, gw_k)), gx_k

        init = (_jnp.zeros((), _jnp.float32), _jax.tree.map(_jnp.zeros_like, weights))
        (loss, grad_w), grad_x = _jax.lax.scan(body, init, (per_example, given["loss_target"]))
    with _jax.named_scope("update"):
        delta_w, new_m, new_v = {}, {}, {}
        for n in TWIN_WEIGHTS:
            delta_w[n], new_m[n], new_v[n] = _adamw(weights[n], grad_w[n], given["m_" + n], given["v_" + n])
    return (loss, grad_x, *[grad_w[n] for n in TWIN_WEIGHTS], *[delta_w[n] for n in TWIN_WEIGHTS],
            *[new_m[n] for n in TWIN_WEIGHTS], *[new_v[n] for n in TWIN_WEIGHTS])
```

```python
import functools

import jax
import jax.numpy as jnp
from jax import lax
from jax.experimental import pallas as pl
from jax.experimental.pallas import tpu as pltpu

D = 1024
F = 2816
DEPTH = 4
POOL_WINDOWS = (2, 4, 8, 16)
POOL_GROUP = 256
CF_TAPS = 31
RMS_EPS = 1e-6
LN_EPS = 1e-5
ADAM_LR = 0.001
ADAM_B1 = 0.9
ADAM_B2 = 0.999
ADAM_EPS = 1e-08
ADAM_WD = 0.01
ADAM_STEP = 10

BF = jnp.bfloat16
F32 = jnp.float32
MESH = pl.DeviceIdType.MESH
SDS = jax.ShapeDtypeStruct
N_CHIPS = 4
N_DEV = 8
VMEM_LIMIT_MB = 56
ROW_TILE_D = 256
ROW_TILE_F = 128
HALO = 8
HALO_BF = 16
CF_HALO = 32
POOL_HALO = 16


def _cp(n_axes):
    return pltpu.CompilerParams(dimension_semantics=("arbitrary",) * n_axes,
                                vmem_limit_bytes=VMEM_LIMIT_MB * 1024 * 1024)


def _tile(n, pref):
    t = min(n, pref)
    assert n % t == 0, (n, pref)
    return t


def _const(shape):
    nd = len(shape)
    return pl.BlockSpec(shape, lambda *_: (0,) * nd)


def _vecs(rows, width):
    v = jnp.stack([r.reshape(width).astype(F32) for r in rows])
    pad = (-v.shape[0]) % 8
    return jnp.pad(v, ((0, pad), (0, 0)))


def _sigmoid(v):
    return 1.0 / (1.0 + jnp.exp(-v))


def _mm(name, a, b, out_sds, grid, a_spec, b_spec, o_spec, acc_shape, dn):
    nk = grid[2]

    def body(a_ref, b_ref, o_ref, *acc):
        part = lax.dot_general(a_ref[...], b_ref[...], dn, preferred_element_type=F32)
        if nk == 1:
            o_ref[...] = part.astype(o_ref.dtype)
        else:
            acc_ref = acc[0]
            k = pl.program_id(2)

            @pl.when(k == 0)
            def _():
                acc_ref[...] = part

            @pl.when(k > 0)
            def _():
                acc_ref[...] += part

            @pl.when(k == nk - 1)
            def _():
                o_ref[...] = acc_ref[...].astype(o_ref.dtype)

    scratch = [] if nk == 1 else [pltpu.VMEM(acc_shape, F32)]
    return pl.pallas_call(body, name=name, out_shape=out_sds, grid=grid, in_specs=[a_spec, b_spec],
                          out_specs=o_spec, scratch_shapes=scratch, compiler_params=_cp(3))(a, b)


NN = (((1,), (0,)), ((), ()))
NT = (((1,), (1,)), ((), ()))
TN = (((0,), (0,)), ((), ()))


def _mm_nn(name, a, w, layer, col_sharded, out_dtype, tm=1024, tn=None):
    m, k = a.shape
    tm = _tile(m, tm)
    if col_sharded:
        n4 = w.shape[3]
        tn = n4 if tn is None else tn
        tpc = n4 // tn
        n = N_CHIPS * n4
        b_spec = pl.BlockSpec((None, None, k, tn), lambda i, j, kk: (layer, j // tpc, 0, j % tpc))
    else:
        n = w.shape[2]
        tn = n if tn is None else tn
        b_spec = pl.BlockSpec((None, k, tn), lambda i, j, kk: (layer, 0, j))
    return _mm(name, a, w, SDS((m, n), out_dtype), (m // tm, n // tn, 1),
               pl.BlockSpec((tm, k), lambda i, j, kk: (i, 0)), b_spec,
               pl.BlockSpec((tm, tn), lambda i, j, kk: (i, j)), None, NN)


def _mm_nt(name, g, w, layer, col_sharded, out_dtype, tm=1024, tn=None):
    m, n = g.shape
    tm = _tile(m, tm)
    if col_sharded:
        kdim, n4 = w.shape[2], w.shape[3]
        tn = kdim if tn is None else tn
        tk = n4
        b_spec = pl.BlockSpec((None, None, tn, tk), lambda i, j, kk: (layer, kk, j, 0))
    else:
        kdim = w.shape[1]
        tn = kdim if tn is None else tn
        tk = n
        b_spec = pl.BlockSpec((None, tn, tk), lambda i, j, kk: (layer, j, kk))
    return _mm(name, g, w, SDS((m, kdim), out_dtype), (m // tm, kdim // tn, n // tk),
               pl.BlockSpec((tm, tk), lambda i, j, kk: (i, kk)), b_spec,
               pl.BlockSpec((tm, tn), lambda i, j, kk: (i, j)), (tm, tn), NT)


def _mm_tn(name, a, g, col_sharded, tk=None, tn=None, ts=512):
    s, k = a.shape
    n = g.shape[1]
    ts = _tile(s, ts)
    tk = k if tk is None else tk
    if col_sharded:
        n4 = n // N_CHIPS
        tn = n4 if tn is None else tn
        tpc = n4 // tn
        out_sds = SDS((N_CHIPS, k, n4), BF)
        o_spec = pl.BlockSpec((None, tk, tn), lambda i, j, ss: (j // tpc, i, j % tpc))
    else:
        tn = n if tn is None else tn
        out_sds = SDS((k, n), BF)
        o_spec = pl.BlockSpec((tk, tn), lambda i, j, ss: (i, j))
    return _mm(name, a, g, out_sds, (k // tk, n // tn, s // ts),
               pl.BlockSpec((ts, tk), lambda i, j, ss: (ss, i)),
               pl.BlockSpec((ts, tn), lambda i, j, ss: (ss, j)), o_spec, (tk, tn), TN)


def _rows(tm, width, col=0):
    return pl.BlockSpec((tm, width), lambda i: (i, col))


def _pro_fwd(name, x, vec):
    s = x.shape[0]
    tm = _tile(s, ROW_TILE_D)

    def body(x_ref, v_ref, h_ref):
        xv = x_ref[...]
        r = lax.rsqrt(jnp.mean(xv * xv, axis=-1, keepdims=True) + RMS_EPS)
        a = v_ref[0:1, :] * (1.0 + v_ref[1:2, :])
        h_ref[...] = (xv * r * a + v_ref[2:3, :]).astype(h_ref.dtype)

    return pl.pallas_call(body, name=name, out_shape=SDS((s, D), BF), grid=(s // tm,),
                          in_specs=[_rows(tm, D), _const((8, D))], out_specs=_rows(tm, D),
                          compiler_params=_cp(1))(x, vec)


def _epi_fwd(name, x, m, vec):
    s = x.shape[0]
    tm = _tile(s, ROW_TILE_D)

    def body(x_ref, m_ref, v_ref, o_ref):
        mv = m_ref[...]
        rm = lax.rsqrt(jnp.mean(mv * mv, axis=-1, keepdims=True) + RMS_EPS)
        o_ref[...] = x_ref[...] + v_ref[0:1, :] * (mv * rm * v_ref[1:2, :])

    return pl.pallas_call(body, name=name, out_shape=SDS((s, D), F32), grid=(s // tm,),
                          in_specs=[_rows(tm, D), _rows(tm, D), _const((8, D))], out_specs=_rows(tm, D),
                          compiler_params=_cp(1))(x, m, vec)


def _loss_fwd_bwd(y, t):
    s = y.shape[0]
    tm = _tile(s, ROW_TILE_D)

    def body(y_ref, t_ref, dy_ref, acc_ref):
        @pl.when(pl.program_id(0) == 0)
        def _():
            acc_ref[...] = jnp.zeros_like(acc_ref)

        e = y_ref[...] - t_ref[...]
        dy_ref[...] = e * (1.0 / D)
        acc_ref[0:1, :] += jnp.sum(e * e, axis=0, keepdims=True) * (0.5 / D)

    return pl.pallas_call(body, name="loss", out_shape=(SDS((s, D), F32), SDS((8, D), F32)), grid=(s // tm,),
                          in_specs=[_rows(tm, D), _rows(tm, D)], out_specs=(_rows(tm, D), _const((8, D))),
                          compiler_params=_cp(1))(y, t)


def _epi_bwd(name, dxo, m, vec):
    s = dxo.shape[0]
    tm = _tile(s, ROW_TILE_D)

    def body(d_ref, m_ref, v_ref, dm_ref, acc_ref):
        @pl.when(pl.program_id(0) == 0)
        def _():
            acc_ref[...] = jnp.zeros_like(acc_ref)

        mv = m_ref[...]
        dv = d_ref[...]
        g = v_ref[0:1, :]
        ngb = v_ref[1:2, :]
        rm = lax.rsqrt(jnp.mean(mv * mv, axis=-1, keepdims=True) + RMS_EPS)
        mn = mv * rm
        dmn = dv * (g * ngb)
        dm = rm * (dmn - mn * jnp.mean(dmn * mn, axis=-1, keepdims=True))
        dm_ref[...] = dm.astype(dm_ref.dtype)
        t = dv * mn
        acc_ref[0:1, :] += jnp.sum(t, axis=0, keepdims=True) * ngb
        acc_ref[1:2, :] += jnp.sum(t, axis=0, keepdims=True) * g
        acc_ref[2:3, :] += jnp.sum(dm, axis=0, keepdims=True)

    return pl.pallas_call(body, name=name, out_shape=(SDS((s, D), BF), SDS((8, D), F32)), grid=(s // tm,),
                          in_specs=[_rows(tm, D), _rows(tm, D), _const((8, D))],
                          out_specs=(_rows(tm, D), _const((8, D))), compiler_params=_cp(1))(dxo, m, vec)


def _pro_bwd(name, dh, x, dxo, vec):
    s = x.shape[0]
    tm = _tile(s, ROW_TILE_D)

    def body(dh_ref, x_ref, d_ref, v_ref, dx_ref, acc_ref):
        @pl.when(pl.program_id(0) == 0)
        def _():
            acc_ref[...] = jnp.zeros_like(acc_ref)

        xv = x_ref[...]
        dhv = dh_ref[...].astype(F32)
        ng = v_ref[0:1, :]
        sc1 = 1.0 + v_ref[1:2, :]
        r = lax.rsqrt(jnp.mean(xv * xv, axis=-1, keepdims=True) + RMS_EPS)
        xn = xv * r
        dxn = dhv * (ng * sc1)
        dx_ref[...] = d_ref[...] + r * (dxn - xn * jnp.mean(dxn * xn, axis=-1, keepdims=True))
        t = jnp.sum(dhv * xn, axis=0, keepdims=True)
        acc_ref[0:1, :] += jnp.sum(dhv, axis=0, keepdims=True)
        acc_ref[1:2, :] += t * ng
        acc_ref[2:3, :] += t * sc1

    return pl.pallas_call(body, name=name, out_shape=(SDS((s, D), F32), SDS((8, D), F32)), grid=(s // tm,),
                          in_specs=[_rows(tm, D), _rows(tm, D), _rows(tm, D), _const((8, D))],
                          out_specs=(_rows(tm, D), _const((8, D))), compiler_params=_cp(1))(dh, x, dxo, vec)


def _carry_down(buf, tm, first):
    @pl.when(first)
    def _():
        buf[0:HALO, :] = jnp.zeros((HALO, buf.shape[1]), F32)

    @pl.when(jnp.logical_not(first))
    def _():
        buf[0:HALO, :] = buf[tm:tm + HALO, :]


def _carry_up(buf, tm, first, halo=HALO):
    @pl.when(first)
    def _():
        buf[tm:tm + halo, :] = jnp.zeros((halo, buf.shape[1]), F32)

    @pl.when(jnp.logical_not(first))
    def _():
        buf[tm:tm + halo, :] = buf[0:halo, :]


def _ffn_act_fwd(name, up, vec):
    s = up.shape[0]
    tm = _tile(s, ROW_TILE_F)

    def body(g_ref, v_ref, w_ref, a_ref, buf):
        _carry_down(buf, tm, pl.program_id(0) == 0)
        buf[HALO:HALO + tm, :] = g_ref[...].astype(F32)
        gc = (w_ref[2:3, :] * buf[HALO:HALO + tm, :] + w_ref[1:2, :] * buf[HALO - 1:HALO - 1 + tm, :]
              + w_ref[0:1, :] * buf[HALO - 2:HALO - 2 + tm, :] + w_ref[3:4, :])
        a_ref[...] = (gc * _sigmoid(gc) * v_ref[...].astype(F32)).astype(a_ref.dtype)

    return pl.pallas_call(body, name=name, out_shape=SDS((s, F), BF), grid=(s // tm,),
                          in_specs=[_rows(tm, F, 0), _rows(tm, F, 1), _const((8, F))], out_specs=_rows(tm, F),
                          scratch_shapes=[pltpu.VMEM((tm + HALO, F), F32)], compiler_params=_cp(1))(up, up, vec)


def _prev_halo_spec(nt, tm, width, col):
    per = tm // HALO_BF
    return pl.BlockSpec((HALO_BF, width), lambda i: (jnp.maximum((nt - 1 - i) * per - 1, 0), col))


def _ffn_act_bwd(name, up, da, vec):
    s = up.shape[0]
    tm = _tile(s, ROW_TILE_F)
    nt = s // tm
    rev = lambda col: pl.BlockSpec((tm, F), lambda i: (nt - 1 - i, col))

    def body(g_ref, gp_ref, v_ref, da_ref, w_ref, a_ref, dup_ref, acc_ref, pbuf, nbuf):
        step = pl.program_id(0)

        @pl.when(step == 0)
        def _():
            acc_ref[...] = jnp.zeros_like(acc_ref)

        gp = g_ref[...].astype(F32)
        pbuf[0:HALO_BF, :] = jnp.where(step < nt - 1, gp_ref[...].astype(F32), 0.0)
        pbuf[HALO_BF:HALO_BF + tm, :] = gp
        gc = (w_ref[2:3, :] * gp + w_ref[1:2, :] * pbuf[HALO_BF - 1:HALO_BF - 1 + tm, :]
              + w_ref[0:1, :] * pbuf[HALO_BF - 2:HALO_BF - 2 + tm, :] + w_ref[3:4, :])
        sg = _sigmoid(gc)
        sl = gc * sg
        val = v_ref[...].astype(F32)
        dav = da_ref[...].astype(F32)
        a_ref[...] = (sl * val).astype(a_ref.dtype)
        dup_ref[:, F:2 * F] = (dav * sl).astype(dup_ref.dtype)
        dgc = dav * val * (sg * (1.0 + gc * (1.0 - sg)))
        _carry_up(nbuf, tm, step == 0)
        nbuf[0:tm, :] = dgc
        d1 = nbuf[1:1 + tm, :]
        d2 = nbuf[2:2 + tm, :]
        dup_ref[:, 0:F] = (w_ref[2:3, :] * dgc + w_ref[1:2, :] * d1 + w_ref[0:1, :] * d2).astype(dup_ref.dtype)
        acc_ref[0:1, :] += jnp.sum(dgc, axis=0, keepdims=True)
        acc_ref[1:2, :] += jnp.sum(gp * d2, axis=0, keepdims=True)
        acc_ref[2:3, :] += jnp.sum(gp * d1, axis=0, keepdims=True)
        acc_ref[3:4, :] += jnp.sum(gp * dgc, axis=0, keepdims=True)

    return pl.pallas_call(
        body, name=name, out_shape=(SDS((s, F), BF), SDS((s, 2 * F), BF), SDS((8, F), F32)), grid=(nt,),
        in_specs=[rev(0), _prev_halo_spec(nt, tm, F, 0), rev(1), rev(0), _const((8, F))],
        out_specs=(rev(0), pl.BlockSpec((tm, 2 * F), lambda i: (nt - 1 - i, 0)), _const((8, F))),
        scratch_shapes=[pltpu.VMEM((tm + HALO_BF, F), F32), pltpu.VMEM((tm + HALO, F), F32)],
        compiler_params=_cp(1))(up, up, up, da, vec)


def _sc_act_fwd(name, p, vec):
    s = p.shape[0]
    tm = _tile(s, ROW_TILE_D)

    def body(b_ref, c_ref, h_ref, w_ref, z_ref, buf):
        _carry_down(buf, tm, pl.program_id(0) == 0)
        buf[HALO:HALO + tm, :] = c_ref[...].astype(F32) * h_ref[...].astype(F32)
        u = (w_ref[2:3, :] * buf[HALO:HALO + tm, :] + w_ref[1:2, :] * buf[HALO - 1:HALO - 1 + tm, :]
             + w_ref[0:1, :] * buf[HALO - 2:HALO - 2 + tm, :])
        z_ref[...] = (b_ref[...].astype(F32) * u).astype(z_ref.dtype)

    return pl.pallas_call(body, name=name, out_shape=SDS((s, D), BF), grid=(s // tm,),
                          in_specs=[_rows(tm, D, 0), _rows(tm, D, 1), _rows(tm, D, 2), _const((8, D))],
                          out_specs=_rows(tm, D), scratch_shapes=[pltpu.VMEM((tm + HALO, D), F32)],
                          compiler_params=_cp(1))(p, p, p, vec)


def _sc_act_bwd(name, p, dz, vec):
    s = p.shape[0]
    tm = _tile(s, ROW_TILE_D)
    nt = s // tm
    rev = lambda col: pl.BlockSpec((tm, D), lambda i: (nt - 1 - i, col))

    def body(b_ref, c_ref, h_ref, cp_ref, hp_ref, dz_ref, w_ref, z_ref, dp_ref, acc_ref, pbuf, nbuf):
        step = pl.program_id(0)

        @pl.when(step == 0)
        def _():
            acc_ref[...] = jnp.zeros_like(acc_ref)

        cg = c_ref[...].astype(F32)
        hin = h_ref[...].astype(F32)
        bg = b_ref[...].astype(F32)
        q = cg * hin
        pbuf[0:HALO_BF, :] = jnp.where(step < nt - 1, cp_ref[...].astype(F32) * hp_ref[...].astype(F32), 0.0)
        pbuf[HALO_BF:HALO_BF + tm, :] = q
        u = (w_ref[2:3, :] * q + w_ref[1:2, :] * pbuf[HALO_BF - 1:HALO_BF - 1 + tm, :]
             + w_ref[0:1, :] * pbuf[HALO_BF - 2:HALO_BF - 2 + tm, :])
        dzv = dz_ref[...].astype(F32)
        z_ref[...] = (bg * u).astype(z_ref.dtype)
        dp_ref[:, 0:D] = (dzv * u).astype(dp_ref.dtype)
        du = dzv * bg
        _carry_up(nbuf, tm, step == 0)
        nbuf[0:tm, :] = du
        d1 = nbuf[1:1 + tm, :]
        d2 = nbuf[2:2 + tm, :]
        dq = w_ref[2:3, :] * du + w_ref[1:2, :] * d1 + w_ref[0:1, :] * d2
        dp_ref[:, D:2 * D] = (dq * hin).astype(dp_ref.dtype)
        dp_ref[:, 2 * D:3 * D] = (dq * cg).astype(dp_ref.dtype)
        acc_ref[0:1, :] += jnp.sum(q * d2, axis=0, keepdims=True)
        acc_ref[1:2, :] += jnp.sum(q * d1, axis=0, keepdims=True)
        acc_ref[2:3, :] += jnp.sum(q * du, axis=0, keepdims=True)

    return pl.pallas_call(
        body, name=name, out_shape=(SDS((s, D), BF), SDS((s, 3 * D), BF), SDS((8, D), F32)), grid=(nt,),
        in_specs=[rev(0), rev(1), rev(2), _prev_halo_spec(nt, tm, D, 1), _prev_halo_spec(nt, tm, D, 2), rev(0),
                  _const((8, D))],
        out_specs=(rev(0), pl.BlockSpec((tm, 3 * D), lambda i: (nt - 1 - i, 0)), _const((8, D))),
        scratch_shapes=[pltpu.VMEM((tm + HALO_BF, D), F32), pltpu.VMEM((tm + HALO, D), F32)],
        compiler_params=_cp(1))(p, p, p, p, p, dz, vec)


def _cf_act_fwd(name, p, taps, vec):
    s = p.shape[0]
    tm = _tile(s, ROW_TILE_D)
    base = CF_HALO - (CF_TAPS - 1)

    def body(a_ref, g_ref, t_ref, v_ref, w_ref, cv_ref, buf):
        first = pl.program_id(0) == 0

        @pl.when(first)
        def _():
            buf[0:CF_HALO, :] = jnp.zeros((CF_HALO, D), F32)

        @pl.when(jnp.logical_not(first))
        def _():
            buf[0:CF_HALO, :] = buf[tm:tm + CF_HALO, :]

        a = a_ref[...].astype(F32) + v_ref[0:1, :]
        g = g_ref[...].astype(F32) + v_ref[1:2, :]
        buf[CF_HALO:CF_HALO + tm, :] = a * _sigmoid(g)
        cv = jnp.zeros((tm, D), F32) + v_ref[2:3, :]
        for k in range(CF_TAPS):
            cv = cv + t_ref[k:k + 1, :] * buf[base + k:base + k + tm, :]
        cv_ref[...] = cv
        mu = jnp.mean(cv, axis=-1, keepdims=True)
        cc = cv - mu
        rstd = lax.rsqrt(jnp.mean(cc * cc, axis=-1, keepdims=True) + LN_EPS)
        ln = cc * rstd * v_ref[3:4, :] + v_ref[4:5, :]
        w_ref[...] = (ln * _sigmoid(ln)).astype(w_ref.dtype)

    return pl.pallas_call(body, name=name, out_shape=(SDS((s, D), BF), SDS((s, D), F32)), grid=(s // tm,),
                          in_specs=[_rows(tm, D, 0), _rows(tm, D, 1), _const((32, D)), _const((8, D))],
                          out_specs=(_rows(tm, D), _rows(tm, D)),
                          scratch_shapes=[pltpu.VMEM((tm + CF_HALO, D), F32)], compiler_params=_cp(1))(p, p, taps, vec)


def _cf_act_bwd(name, p, cv, dw, taps, vec):
    s = p.shape[0]
    tm = _tile(s, ROW_TILE_D)
    nt = s // tm
    rev = lambda col: pl.BlockSpec((tm, D), lambda i: (nt - 1 - i, col))

    def body(a_ref, g_ref, cv_ref, dw_ref, t_ref, v_ref, dp_ref, tacc_ref, acc_ref, nbuf):
        step = pl.program_id(0)

        @pl.when(step == 0)
        def _():
            acc_ref[...] = jnp.zeros_like(acc_ref)
            tacc_ref[...] = jnp.zeros_like(tacc_ref)

        a = a_ref[...].astype(F32) + v_ref[0:1, :]
        g = g_ref[...].astype(F32) + v_ref[1:2, :]
        sg = _sigmoid(g)
        u = a * sg
        cvv = cv_ref[...]
        mu = jnp.mean(cvv, axis=-1, keepdims=True)
        cc = cvv - mu
        rstd = lax.rsqrt(jnp.mean(cc * cc, axis=-1, keepdims=True) + LN_EPS)
        vhat = cc * rstd
        ln = vhat * v_ref[3:4, :] + v_ref[4:5, :]
        s2 = _sigmoid(ln)
        dln = dw_ref[...].astype(F32) * (s2 * (1.0 + ln * (1.0 - s2)))
        acc_ref[1:2, :] += jnp.sum(dln * vhat, axis=0, keepdims=True)
        acc_ref[2:3, :] += jnp.sum(dln, axis=0, keepdims=True)
        dvh = dln * v_ref[3:4, :]
        dcv = rstd * (dvh - jnp.mean(dvh, axis=-1, keepdims=True)
                      - vhat * jnp.mean(dvh * vhat, axis=-1, keepdims=True))
        acc_ref[0:1, :] += jnp.sum(dcv, axis=0, keepdims=True)
        _carry_up(nbuf, tm, step == 0, CF_HALO)
        nbuf[0:tm, :] = dcv
        du = jnp.zeros((tm, D), F32)
        for j in range(CF_TAPS):
            k = CF_TAPS - 1 - j
            sh = nbuf[j:j + tm, :]
            du = du + t_ref[k:k + 1, :] * sh
            tacc_ref[k:k + 1, :] += jnp.sum(u * sh, axis=0, keepdims=True)
        da = du * sg
        dg = du * a * sg * (1.0 - sg)
        dp_ref[:, 0:D] = da.astype(dp_ref.dtype)
        dp_ref[:, D:2 * D] = dg.astype(dp_ref.dtype)
        acc_ref[3:4, :] += jnp.sum(da, axis=0, keepdims=True)
        acc_ref[4:5, :] += jnp.sum(dg, axis=0, keepdims=True)

    return pl.pallas_call(
        body, name=name, out_shape=(SDS((s, 2 * D), BF), SDS((32, D), F32), SDS((8, D), F32)), grid=(nt,),
        in_specs=[rev(0), rev(1), rev(0), rev(0), _const((32, D)), _const((8, D))],
        out_specs=(pl.BlockSpec((tm, 2 * D), lambda i: (nt - 1 - i, 0)), _const((32, D)), _const((8, D))),
        scratch_shapes=[pltpu.VMEM((tm + CF_HALO, D), F32)], compiler_params=_cp(1))(p, p, cv, dw, taps, vec)


def _inv_count(row0, tm, window):
    t = row0 + lax.broadcasted_iota(jnp.int32, (tm, 1), 0)
    return 1.0 / jnp.minimum(t + 1, window).astype(F32)


def _pool_fwd(name, x, w, vec):
    s = x.shape[0]
    tm = _tile(s, ROW_TILE_D)
    G = POOL_GROUP

    def body(x_ref, w_ref, v_ref, pl_ref, m_ref, buf):
        i = pl.program_id(0)

        @pl.when(i == 0)
        def _():
            buf[0:POOL_HALO, :] = jnp.zeros((POOL_HALO, D), F32)

        @pl.when(i > 0)
        def _():
            buf[0:POOL_HALO, :] = buf[tm:tm + POOL_HALO, :]

        xv = x_ref[...]
        r = lax.rsqrt(jnp.mean(xv * xv, axis=-1, keepdims=True) + RMS_EPS)
        buf[POOL_HALO:POOL_HALO + tm, :] = xv * r * (v_ref[0:1, :] * (1.0 + v_ref[1:2, :])) + v_ref[2:3, :]
        for gi, win in enumerate(POOL_WINDOWS):
            cols = slice(gi * G, (gi + 1) * G)
            acc = buf[POOL_HALO:POOL_HALO + tm, cols]
            hg = acc
            for j in range(1, win):
                acc = acc + buf[POOL_HALO - j:POOL_HALO - j + tm, cols]
            pooled = (acc * _inv_count(i * tm, tm, win) - hg).astype(BF)
            pl_ref[:, cols] = pooled
            yg = jnp.dot(pooled, w_ref[gi], preferred_element_type=F32)
            m_ref[:, cols] = (yg + v_ref[3:4, cols]) * v_ref[4:5, cols]

    return pl.pallas_call(body, name=name, out_shape=(SDS((s, D), BF), SDS((s, D), F32)), grid=(s // tm,),
                          in_specs=[_rows(tm, D), _const((4, G, G)), _const((8, D))],
                          out_specs=(_rows(tm, D), _rows(tm, D)),
                          scratch_shapes=[pltpu.VMEM((tm + POOL_HALO, D), F32)], compiler_params=_cp(1))(x, w, vec)


def _pool_bwd(name, pooled, dm, w, vec):
    s = pooled.shape[0]
    tm = _tile(s, ROW_TILE_D)
    nt = s // tm
    G = POOL_GROUP
    rev = pl.BlockSpec((tm, D), lambda i: (nt - 1 - i, 0))

    def body(p_ref, dm_ref, w_ref, v_ref, dh_ref, dw_ref, acc_ref, nbuf):
        step = pl.program_id(0)
        row0 = (nt - 1 - step) * tm

        @pl.when(step == 0)
        def _():
            acc_ref[...] = jnp.zeros_like(acc_ref)
            dw_ref[...] = jnp.zeros_like(dw_ref)

        _carry_up(nbuf, tm, step == 0, POOL_HALO)
        dmv = dm_ref[...].astype(F32)
        acc_ref[0:1, :] += jnp.sum(dmv, axis=0, keepdims=True) * v_ref[1:2, :]
        dps = []
        for gi, win in enumerate(POOL_WINDOWS):
            cols = slice(gi * G, (gi + 1) * G)
            pg = p_ref[:, cols]
            yb = jnp.dot(pg, w_ref[gi], preferred_element_type=F32) + v_ref[0:1, cols]
            acc_ref[1:2, cols] += jnp.sum(dmv[:, cols] * yb, axis=0, keepdims=True)
            dy = (dmv[:, cols] * v_ref[1:2, cols]).astype(BF)
            dw_ref[gi] += lax.dot_general(pg, dy, TN, preferred_element_type=F32)
            dpg = lax.dot_general(dy, w_ref[gi], NT, preferred_element_type=F32)
            dps.append(dpg)
            nbuf[0:tm, cols] = dpg * _inv_count(row0, tm, win)
        for gi, win in enumerate(POOL_WINDOWS):
            cols = slice(gi * G, (gi + 1) * G)
            acc = nbuf[0:tm, cols]
            for j in range(1, win):
                acc = acc + nbuf[j:j + tm, cols]
            dh_ref[:, cols] = acc - dps[gi]

    return pl.pallas_call(
        body, name=name, out_shape=(SDS((s, D), F32), SDS((4, G, G), F32), SDS((8, D), F32)), grid=(nt,),
        in_specs=[rev, rev, _const((4, G, G)), _const((8, D))],
        out_specs=(rev, _const((4, G, G)), _const((8, D))),
        scratch_shapes=[pltpu.VMEM((tm + POOL_HALO, D), F32)], compiler_params=_cp(1))(pooled, dm, w, vec)


def _row_tile_2d(rows, width, bytes_per_row_elem=4, budget=2 * 1024 * 1024):
    t = max(8, budget // (width * bytes_per_row_elem))
    t = min(rows, 1 << (t.bit_length() - 1))
    while rows % t:
        t //= 2
    return t


def _add2(name, a, b, out_dtype):
    rows, width = a.shape
    tm = _row_tile_2d(rows, width)

    def body(a_ref, b_ref, o_ref):
        o_ref[...] = (a_ref[...].astype(F32) + b_ref[...].astype(F32)).astype(o_ref.dtype)

    return pl.pallas_call(body, name=name, out_shape=SDS((rows, width), out_dtype), grid=(rows // tm,),
                          in_specs=[_rows(tm, width), _rows(tm, width)], out_specs=_rows(tm, width),
                          compiler_params=_cp(1))(a, b)


def _add_slots(name, r):
    nl, _, k, n = r.shape
    tk = _row_tile_2d(k, n, 8)

    def body(r_ref, o_ref):
        o_ref[...] = ((r_ref[3].astype(F32) + r_ref[0].astype(F32)) + r_ref[1].astype(F32)) + r_ref[2].astype(F32)

    return pl.pallas_call(body, name=name, out_shape=SDS((nl, k, n), F32), grid=(nl, k // tk),
                          in_specs=[pl.BlockSpec((None, 4, tk, n), lambda l, i: (l, 0, i, 0))],
                          out_specs=pl.BlockSpec((None, tk, n), lambda l, i: (l, i, 0)),
                          compiler_params=_cp(2))(r)


def _adamw(name, w, g, m, v):
    rows, width = w.shape
    tm = _row_tile_2d(rows, width, 4, 1024 * 1024)
    c1 = 1.0 - ADAM_B1 ** ADAM_STEP
    c2 = 1.0 - ADAM_B2 ** ADAM_STEP

    def body(w_ref, g_ref, m_ref, v_ref, d_ref, nm_ref, nv_ref):
        gv = g_ref[...]
        nm = ADAM_B1 * m_ref[...] + (1.0 - ADAM_B1) * gv
        nv = ADAM_B2 * v_ref[...] + (1.0 - ADAM_B2) * (gv * gv)
        nm_ref[...] = nm
        nv_ref[...] = nv
        d_ref[...] = -ADAM_LR * ((nm / c1) / (jnp.sqrt(nv / c2) + ADAM_EPS) + ADAM_WD * w_ref[...])

    spec = _rows(tm, width)
    sds = SDS((rows, width), F32)
    return pl.pallas_call(body, name=name, out_shape=(sds, sds, sds), grid=(rows // tm,),
                          in_specs=[spec] * 4, out_specs=(spec,) * 3, compiler_params=_cp(1))(w, g, m, v)


def _mod_fwd(c16, w_mod, b_sh):
    n = w_mod.shape[2]
    tn = _tile(n, 512)

    def body(c_ref, w_ref, b_ref, o_ref):
        cv = c_ref[...]
        ca = (cv * _sigmoid(cv)).astype(BF)
        o_ref[...] = jnp.dot(ca, w_ref[...].astype(BF), preferred_element_type=F32) + b_ref[0:1, :]

    return pl.pallas_call(body, name="mod_fwd", out_shape=SDS((DEPTH, 16, n), F32), grid=(DEPTH, n // tn),
                          in_specs=[_const((16, D)), pl.BlockSpec((None, D, tn), lambda l, j: (l, 0, j)),
                                    pl.BlockSpec((None, 8, tn), lambda l, j: (l, 0, j))],
                          out_specs=pl.BlockSpec((None, 16, tn), lambda l, j: (l, 0, j)),
                          compiler_params=_cp(2))(c16, w_mod, b_sh)


def _mod_bwd(c16, dmod):
    n = dmod.shape[2]
    tn = _tile(n, 512)

    def body(c_ref, d_ref, o_ref):
        cv = c_ref[...]
        ca = (cv * _sigmoid(cv)).astype(BF)
        o_ref[...] = lax.dot_general(ca, d_ref[...].astype(BF), TN, preferred_element_type=F32)

    return pl.pallas_call(body, name="mod_bwd", out_shape=SDS((DEPTH, D, n), F32), grid=(DEPTH, n // tn),
                          in_specs=[_const((16, D)), pl.BlockSpec((None, 16, tn), lambda l, j: (l, 0, j))],
                          out_specs=pl.BlockSpec((None, D, tn), lambda l, j: (l, 0, j)),
                          compiler_params=_cp(2))(c16, dmod)


def _place():
    x, y, c = lax.axis_index("x"), lax.axis_index("y"), lax.axis_index("c")
    other_chips = [(1 - x, y), (x, 1 - y), (1 - x, 1 - y)]
    return x, y, c, other_chips


def _allgather_small(name, v, with_sum):
    m, n = v.shape

    def body(x_ref, out_ref, *rest):
        if with_sum:
            sum_ref, send_sems, recv_sems, local_sem = rest
        else:
            send_sems, recv_sems, local_sem = rest
        x, y, c, chips = _place()
        me, sibling = (x, y, c), (x, y, 1 - c)

        def rows(px, py, pc):
            return out_ref.at[pl.ds((4 * px + 2 * py + pc) * m, m), :]

        def copy(k, block, to, src=None):
            return pltpu.make_async_remote_copy(
                src_ref=rows(*block) if src is None else src, dst_ref=rows(*block),
                send_sem=send_sems.at[k], recv_sem=recv_sems.at[k], device_id=to, device_id_type=MESH)

        mine = pltpu.make_async_copy(x_ref, rows(*me), local_sem)
        mine.start()
        first = [copy(0, me, sibling, src=x_ref)]
        first += [copy(1 + j, me, (*chip, c), src=x_ref) for j, chip in enumerate(chips)]
        for cp in first:
            cp.start()
        passed = [copy(4 + j, (*chip, c), sibling) for j, chip in enumerate(chips)]
        for j, chip in enumerate(chips):
            copy(1 + j, (*chip, c), me).wait_recv()
            passed[j].start()
        copy(0, sibling, me).wait_recv()
        for j, chip in enumerate(chips):
            copy(4 + j, (*chip, 1 - c), me).wait_recv()
        for cp in first + passed:
            cp.wait_send()
        mine.wait()
        if with_sum:
            acc = out_ref[0:m, :]
            for k in range(1, N_DEV):
                acc = acc + out_ref[k * m:(k + 1) * m, :]
            sum_ref[...] = acc

    vm = pl.BlockSpec(memory_space=pltpu.VMEM)
    out_shape = [SDS((N_DEV * m, n), F32)] + ([SDS((m, n), F32)] if with_sum else [])
    res = pl.pallas_call(
        body, name=name, out_shape=tuple(out_shape), in_specs=[vm], out_specs=tuple([vm] * len(out_shape)),
        scratch_shapes=[pltpu.SemaphoreType.DMA((7,)), pltpu.SemaphoreType.DMA((7,)), pltpu.SemaphoreType.DMA],
        compiler_params=pltpu.CompilerParams(vmem_limit_bytes=VMEM_LIMIT_MB * 1024 * 1024))(v)
    return res if with_sum else res[0]


HBM = pl.BlockSpec(memory_space=pltpu.HBM)


def _sem_scratch(n_remote, n_local):
    return [pltpu.SemaphoreType.DMA((n_remote,)), pltpu.SemaphoreType.DMA((n_remote,)),
            pltpu.SemaphoreType.DMA((n_local,))]


def _gather_weights(shards):
    nq = len(shards)

    def body(*refs):
        ins, outs = refs[:nq], refs[nq:2 * nq]
        send_sems, recv_sems, local_sems = refs[2 * nq:]
        x, y, c, chips = _place()
        me_chip = 2 * x + y
        local, remote = [], []
        for q in range(nq):
            local.append(pltpu.make_async_copy(ins[q], outs[q].at[:, me_chip], local_sems.at[q]))
            for r, (px, py) in enumerate(chips):
                remote.append(pltpu.make_async_remote_copy(
                    src_ref=ins[q], dst_ref=outs[q].at[:, me_chip], send_sem=send_sems.at[3 * q + r],
                    recv_sem=recv_sems.at[3 * q + r], device_id=(px, py, c), device_id_type=MESH))
        for cp in local + remote:
            cp.start()
        for q in range(nq):
            for r, (px, py) in enumerate(chips):
                pltpu.make_async_remote_copy(
                    src_ref=ins[q], dst_ref=outs[q].at[:, 2 * px + py], send_sem=send_sems.at[3 * q + r],
                    recv_sem=recv_sems.at[3 * q + r], device_id=(px, py, c), device_id_type=MESH).wait_recv()
        for cp in remote:
            cp.wait_send()
        for cp in local:
            cp.wait()

    out_shape = tuple(SDS((s.shape[0], N_CHIPS) + s.shape[1:], s.dtype) for s in shards)
    return pl.pallas_call(body, name="gather_weights", out_shape=out_shape, in_specs=[HBM] * nq,
                          out_specs=tuple([HBM] * nq), scratch_shapes=_sem_scratch(3 * nq, nq))(*shards)


def _pair_exchange(gs):
    nq = len(gs)

    def body(*refs):
        ins, own, sib = refs[:nq], refs[nq:2 * nq], refs[2 * nq:3 * nq]
        send_sems, recv_sems, local_sems = refs[3 * nq:]
        x, y, c, _ = _place()
        local, remote = [], []
        for q in range(nq):
            local.append(pltpu.make_async_copy(ins[q].at[:, :, c], own[q], local_sems.at[q]))
            remote.append(pltpu.make_async_remote_copy(
                src_ref=ins[q].at[:, :, 1 - c], dst_ref=sib[q], send_sem=send_sems.at[q], recv_sem=recv_sems.at[q],
                device_id=(x, y, 1 - c), device_id_type=MESH))
        for cp in local + remote:
            cp.start()
        for cp in remote:
            cp.wait_recv()
        for cp in remote:
            cp.wait_send()
        for cp in local:
            cp.wait()

    half = tuple(SDS((g.shape[0], g.shape[1], g.shape[3], g.shape[4]), g.dtype) for g in gs)
    res = pl.pallas_call(body, name="grad_pair_exchange", out_shape=half + half, in_specs=[HBM] * nq,
                         out_specs=tuple([HBM] * (2 * nq)), scratch_shapes=_sem_scratch(nq, nq))(*gs)
    return res[:nq], res[nq:]


def _chip_exchange(gs):
    nq = len(gs)

    def body(*refs):
        ins, outs = refs[:nq], refs[nq:2 * nq]
        send_sems, recv_sems, local_sems = refs[2 * nq:]
        x, y, c, chips = _place()
        local, remote = [], []
        for q in range(nq):
            local.append(pltpu.make_async_copy(ins[q].at[:, 2 * x + y], outs[q].at[:, 3], local_sems.at[q]))
            for r, (px, py) in enumerate(chips):
                remote.append(pltpu.make_async_remote_copy(
                    src_ref=ins[q].at[:, 2 * px + py], dst_ref=outs[q].at[:, r], send_sem=send_sems.at[3 * q + r],
                    recv_sem=recv_sems.at[3 * q + r], device_id=(px, py, c), device_id_type=MESH))
        for cp in local + remote:
            cp.start()
        for cp in remote:
            cp.wait_recv()
        for cp in remote:
            cp.wait_send()
        for cp in local:
            cp.wait()

    out_shape = tuple(SDS(g.shape, g.dtype) for g in gs)
    return pl.pallas_call(body, name="grad_chip_exchange", out_shape=out_shape, in_specs=[HBM] * nq,
                          out_specs=tuple([HBM] * nq), scratch_shapes=_sem_scratch(3 * nq, nq))(*gs)


def _pair_share(rs):
    nq = len(rs)

    def body(*refs):
        ins, outs = refs[:nq], refs[nq:2 * nq]
        send_sems, recv_sems, local_sems = refs[2 * nq:]
        x, y, c, _ = _place()
        local, remote = [], []
        for q in range(nq):
            local.append(pltpu.make_async_copy(ins[q], outs[q].at[:, c], local_sems.at[q]))
            remote.append(pltpu.make_async_remote_copy(
                src_ref=ins[q], dst_ref=outs[q].at[:, c], send_sem=send_sems.at[q], recv_sem=recv_sems.at[q],
                device_id=(x, y, 1 - c), device_id_type=MESH))
        for cp in local + remote:
            cp.start()
        for q in range(nq):
            pltpu.make_async_remote_copy(
                src_ref=ins[q], dst_ref=outs[q].at[:, 1 - c], send_sem=send_sems.at[q], recv_sem=recv_sems.at[q],
                device_id=(x, y, 1 - c), device_id_type=MESH).wait_recv()
        for cp in remote:
            cp.wait_send()
        for cp in local:
            cp.wait()

    out_shape = tuple(SDS((r.shape[0], 2) + r.shape[1:], r.dtype) for r in rs)
    return pl.pallas_call(body, name="grad_pair_share", out_shape=out_shape, in_specs=[HBM] * nq,
                          out_specs=tuple([HBM] * nq), scratch_shapes=_sem_scratch(nq, nq))(*rs)


def _pack(arrs, rows_multiple=8):
    flat = jnp.concatenate([a.astype(F32).reshape(-1) for a in arrs])
    pad = (-flat.shape[0]) % (128 * rows_multiple)
    return jnp.pad(flat, (0, pad)).reshape(-1, 128)


def _unpack(slab, shapes):
    flat = slab.reshape(-1)
    out, off = [], 0
    for shp in shapes:
        n = 1
        for d in shp:
            n *= d
        out.append(flat[off:off + n].reshape(shp))
        off += n
    return out


def _shard_last(a, chip, n):
    return lax.dynamic_slice_in_dim(a, chip * n, n, axis=a.ndim - 1)


def kernel(x, c, w_mod, b_mod, norm_g, sc_w_in, sc_conv, sc_w_out, pool_w, pool_b, pool_scale, cf_w_pw1, cf_b_pw1, cf_w_dw, cf_b_dw, cf_ln_g, cf_ln_b, cf_w_pw2, cf_b_pw2, ffn_w_up, ffn_conv, ffn_b_conv, ffn_w_down, loss_target, m_w_mod, m_b_mod, m_norm_g, m_sc_w_in, m_sc_conv, m_sc_w_out, m_pool_w, m_pool_b, m_pool_scale, m_cf_w_pw1, m_cf_b_pw1, m_cf_w_dw, m_cf_b_dw, m_cf_ln_g, m_cf_ln_b, m_cf_w_pw2, m_cf_b_pw2, m_ffn_w_up, m_ffn_conv, m_ffn_b_conv, m_ffn_w_down, v_w_mod, v_b_mod, v_norm_g, v_sc_w_in, v_sc_conv, v_sc_w_out, v_pool_w, v_pool_b, v_pool_scale, v_cf_w_pw1, v_cf_b_pw1, v_cf_w_dw, v_cf_b_dw, v_cf_ln_g, v_cf_ln_b, v_cf_w_pw2, v_cf_b_pw2, v_ffn_w_up, v_ffn_conv, v_ffn_b_conv, v_ffn_w_down):
    ax, ay, ac = lax.axis_index("x"), lax.axis_index("y"), lax.axis_index("c")
    chip = 2 * ax + ay
    dev = 4 * ax + 2 * ay + ac
    xs = x[0]
    target = loss_target[0]

    small_sharded = [norm_g, sc_conv, cf_b_pw1, cf_w_dw, cf_b_dw, cf_ln_g, cf_ln_b, cf_b_pw2, ffn_conv]
    slab = _pack([c] + small_sharded)
    gathered = _allgather_small("gather_small_params", slab, False).reshape(N_DEV, -1, 128)
    parts = [_unpack(gathered[d], [c.shape] + [a.shape for a in small_sharded]) for d in range(N_DEV)]
    c_all = jnp.concatenate([p[0] for p in parts], axis=0)
    full = [jnp.concatenate([parts[2 * j][1 + i] for j in range(N_CHIPS)], axis=-1)
            for i in range(len(small_sharded))]
    norm_g_f, sc_conv_f, cf_b_pw1_f, cf_w_dw_f, cf_b_dw_f, cf_ln_g_f, cf_ln_b_f, cf_b_pw2_f, ffn_conv_f = full
    c16 = jnp.pad(c_all, ((0, 8), (0, 0)))

    n_mod = w_mod.shape[2]
    b_sh = jnp.broadcast_to(_shard_last(b_mod, chip, n_mod)[:, None, :], (DEPTH, 8, n_mod))
    mod_part = _mod_fwd(c16, w_mod, b_sh)
    mod_g = _allgather_small("gather_mod", mod_part.reshape(DEPTH * 16, n_mod), False)
    mod_g = mod_g.reshape(N_DEV, DEPTH, 16, n_mod)
    mod_mine = jnp.concatenate(
        [lax.dynamic_index_in_dim(mod_g[2 * j], dev, axis=1, keepdims=False) for j in range(N_CHIPS)], axis=-1)
    mod = mod_mine.reshape(DEPTH, 6, D)

    shards = [sc_w_in.astype(BF), sc_w_out.astype(BF), pool_w[0].astype(BF), cf_w_pw1.astype(BF),
              cf_w_pw2.astype(BF), ffn_w_up.astype(BF), ffn_w_down.astype(BF)]
    w_in_f, w_out_f, pool_f, pw1_f, pw2_f, w_up_f, w_down_f = _gather_weights(shards)
    w_out_f = w_out_f.reshape(w_out_f.shape[0], D, D)
    pool_f = pool_f.reshape(4, POOL_GROUP, POOL_GROUP)
    pw2_f = pw2_f.reshape(1, D, D)
    w_down_f = w_down_f.reshape(DEPTH, F, D)

    saved = []
    xcur = xs
    for i in range(DEPTH):
        kind, j = i % 3, i // 3
        sh1, sc1, g1, sh2, sc2, g2 = [mod[i, k] for k in range(6)]
        st = {"x0": xcur}
        pro_vec = _vecs([norm_g_f[i, 0], sc1, sh1], D)
        if kind == 0:
            h = _pro_fwd(f"pro1_fwd_{i}", xcur, pro_vec)
            p = _mm_nn(f"sc_in_{i}", h, w_in_f, j, True, BF)
            z = _sc_act_fwd(f"sc_act_fwd_{i}", p, _vecs(list(sc_conv_f[j]), D))
            m = _mm_nn(f"sc_out_{i}", z, w_out_f, j, False, F32)
            st.update(h=h, p=p)
        elif kind == 1:
            pool_vec = _vecs([norm_g_f[i, 0], sc1, sh1, pool_b[j], pool_scale[j]], D)
            pooled, m = _pool_fwd(f"pool_fwd_{i}", xcur, pool_f, pool_vec)
            st.update(pooled=pooled)
        else:
            h = _pro_fwd(f"pro1_fwd_{i}", xcur, pro_vec)
            p = _mm_nn(f"cf_pw1_{i}", h, pw1_f, j, True, BF)
            taps = jnp.pad(cf_w_dw_f[j], ((0, 1), (0, 0)))
            cf_vec = _vecs([cf_b_pw1_f[j, :D], cf_b_pw1_f[j, D:], cf_b_dw_f[j], cf_ln_g_f[j], cf_ln_b_f[j]], D)
            wact, cv = _cf_act_fwd(f"cf_act_fwd_{i}", p, taps, cf_vec)
            m = _mm_nn(f"cf_pw2_{i}", wact, pw2_f, j, False, F32)
            m = _add_bias(f"cf_bias_{i}", m, cf_b_pw2_f[j])
            st.update(h=h, p=p, wact=wact, cv=cv, taps=taps, cf_vec=cf_vec)
        x1 = _epi_fwd(f"epi1_fwd_{i}", xcur, m, _vecs([g1, norm_g_f[i, 1]], D))
        st.update(m=m, x1=x1)
        h2 = _pro_fwd(f"pro2_fwd_{i}", x1, _vecs([norm_g_f[i, 2], sc2, sh2], D))
        up = _mm_nn(f"ffn_up_{i}", h2, w_up_f, i, True, BF)
        ffn_vec = _vecs(list(ffn_conv_f[i]) + [ffn_b_conv[i]], F)
        a = _ffn_act_fwd(f"ffn_act_fwd_{i}", up, ffn_vec)
        f = _mm_nn(f"ffn_down_{i}", a, w_down_f, i, False, F32, tm=512)
        xcur = _epi_fwd(f"epi2_fwd_{i}", x1, f, _vecs([g2, norm_g_f[i, 3]], D))
        st.update(h2=h2, up=up, f=f, ffn_vec=ffn_vec)
        saved.append(st)

    dy, loss_cols = _loss_fwd_bwd(xcur, target)
    loss = lax.psum(jnp.sum(loss_cols[0]), ("x", "y", "c"))

    dmod = [None] * DEPTH
    d_norm_g = [None] * DEPTH
    d_sc_conv = [None, None]
    d_ffn_conv, d_ffn_b = [None] * DEPTH, [None] * DEPTH
    g_w_in, g_w_out, g_w_up, g_w_down = [None, None], [None, None], [None] * DEPTH, [None] * DEPTH
    dxo = dy
    for i in reversed(range(DEPTH)):
        kind, j = i % 3, i // 3
        st = saved[i]
        sh1, sc1, g1, sh2, sc2, g2 = [mod[i, k] for k in range(6)]
        df, e2 = _epi_bwd(f"epi2_bwd_{i}", dxo, st["f"], _vecs([g2, norm_g_f[i, 3]], D))
        da = _mm_nt(f"ffn_down_dx_{i}", df, w_down_f, i, False, BF, tn=F // 2)
        a, dup, fsum = _ffn_act_bwd(f"ffn_act_bwd_{i}", st["up"], da, st["ffn_vec"])
        g_w_down[i] = _mm_tn(f"ffn_down_dw_{i}", a, df, False, tk=F // 2)
        dh2 = _mm_nt(f"ffn_up_dx_{i}", dup, w_up_f, i, True, F32)
        g_w_up[i] = _mm_tn(f"ffn_up_dw_{i}", st["h2"], dup, True)
        dx1, p2 = _pro_bwd(f"pro2_bwd_{i}", dh2, st["x1"], dxo, _vecs([norm_g_f[i, 2], sc2], D))
        dm, e1 = _epi_bwd(f"epi1_bwd_{i}", dx1, st["m"], _vecs([g1, norm_g_f[i, 1]], D))
        if kind == 0:
            dz = _mm_nt(f"sc_out_dx_{i}", dm, w_out_f, j, False, BF)
            z, dp, ssum = _sc_act_bwd(f"sc_act_bwd_{i}", st["p"], dz, _vecs(list(sc_conv_f[j]), D))
            g_w_out[j] = _mm_tn(f"sc_out_dw_{i}", z, dm, False)
            dh = _mm_nt(f"sc_in_dx_{i}", dp, w_in_f, j, True, F32)
            g_w_in[j] = _mm_tn(f"sc_in_dw_{i}", st["h"], dp, True)
            d_sc_conv[j] = ssum[0:3]
        elif kind == 1:
            dh, g_pool, psum = _pool_bwd(f"pool_bwd_{i}", st["pooled"], dm, pool_f, _vecs([pool_b[j], pool_scale[j]], D))
        else:
            dwact = _mm_nt(f"cf_pw2_dx_{i}", dm, pw2_f, j, False, BF)
            g_pw2 = _mm_tn(f"cf_pw2_dw_{i}", st["wact"], dm, False)
            dp, tsum, csum = _cf_act_bwd(f"cf_act_bwd_{i}", st["p"], st["cv"], dwact, st["taps"], st["cf_vec"])
            dh = _mm_nt(f"cf_pw1_dx_{i}", dp, pw1_f, j, True, F32)
            g_pw1 = _mm_tn(f"cf_pw1_dw_{i}", st["h"], dp, True)
            d_cf = dict(b_pw1=jnp.concatenate([csum[3], csum[4]])[None], w_dw=tsum[None, :CF_TAPS], b_dw=csum[0:1],
                        ln_g=csum[1:2], ln_b=csum[2:3], b_pw2=e1[2:3])
        dxo, p1 = _pro_bwd(f"pro1_bwd_{i}", dh, st["x0"], dx1, _vecs([norm_g_f[i, 0], sc1], D))
        dmod[i] = jnp.concatenate([p1[0], p1[1], e1[0], p2[0], p2[1], e2[0]])
        d_norm_g[i] = jnp.stack([p1[2], e1[1], p2[2], e2[1]])
        if kind == 1:
            d_pool_b, d_pool_scale = psum[0:1], psum[1:2]
        saved[i] = None
        st = None
        if i == 0:
            grad_x = dxo[None]
        d_ffn_conv[i], d_ffn_b[i] = fsum[1:4], fsum[0]

    small_shapes = [(DEPTH, 6 * D), (DEPTH, 4, D), (2, 3, D), (1, D), (1, D), (1, 2 * D), (1, CF_TAPS, D), (1, D),
                    (1, D), (1, D), (1, D), (DEPTH, 3, F), (DEPTH, F)]
    small = [jnp.stack(dmod), jnp.stack(d_norm_g), jnp.stack(d_sc_conv), d_pool_b, d_pool_scale, d_cf["b_pw1"],
             d_cf["w_dw"], d_cf["b_dw"], d_cf["ln_g"], d_cf["ln_b"], d_cf["b_pw2"], jnp.stack(d_ffn_conv),
             jnp.stack(d_ffn_b)]
    gsl, ssl = _allgather_small("reduce_small_grads", _pack(small), True)
    tot = _unpack(ssl, small_shapes)
    (gb_mod, gt_norm_g, gt_sc_conv, g_pool_b, g_pool_scale, gt_b_pw1, gt_w_dw, gt_b_dw, gt_ln_g, gt_ln_b, gt_b_pw2,
     gt_ffn_conv, g_ffn_b) = tot
    dmod_all = jnp.stack([_unpack(gsl.reshape(N_DEV, -1, 128)[d], small_shapes[:1])[0] for d in range(N_DEV)], axis=1)
    dmod_sh = jnp.pad(_shard_last(dmod_all, chip, n_mod), ((0, 0), (0, 8), (0, 0)))
    g_w_mod = _mod_bwd(c16, dmod_sh)
    g_norm_g = _shard_last(gt_norm_g, chip, D // 4)
    g_sc_conv = _shard_last(gt_sc_conv, chip, D // 4)
    g_b_pw1 = _shard_last(gt_b_pw1, chip, 2 * D // 4)
    g_w_dw = _shard_last(gt_w_dw, chip, D // 4)
    g_b_dw = _shard_last(gt_b_dw, chip, D // 4)
    g_ln_g = _shard_last(gt_ln_g, chip, D // 4)
    g_ln_b = _shard_last(gt_ln_b, chip, D // 4)
    g_b_pw2 = _shard_last(gt_b_pw2, chip, D // 4)
    g_ffn_conv = _shard_last(gt_ffn_conv, chip, F // 4)

    def halves(gl):
        g = jnp.stack(gl) if isinstance(gl, list) else gl
        if g.ndim == 3:
            g = g.reshape(g.shape[0], N_CHIPS, g.shape[1] // N_CHIPS, g.shape[2])
        nl, _, k, n = g.shape
        return g.reshape(nl, N_CHIPS, 2, k // 2, n)

    partial = [halves(g_w_in), halves(g_w_out), halves(g_pool.astype(BF)), halves([g_pw1]), halves([g_pw2]),
               halves(g_w_up), halves(g_w_down)]
    own, sib = _pair_exchange(partial)
    chip_sum = []
    for q in range(len(own)):
        shp = own[q].shape
        s2 = _add2(f"grad_pair_sum_{q}", own[q].reshape(-1, shp[-1]), sib[q].reshape(-1, shp[-1]), BF)
        chip_sum.append(s2.reshape(shp))
    pieces = _chip_exchange(chip_sum)
    reduced = [_add_slots(f"grad_chip_sum_{q}", pieces[q]) for q in range(len(pieces))]
    shared = _pair_share(reduced)
    big_w = [sc_w_in, sc_w_out, pool_w, cf_w_pw1, cf_w_pw2, ffn_w_up, ffn_w_down]
    big_g = [shared[q].reshape(big_w[q].shape) for q in range(len(big_w))]
    g_sc_w_in, g_sc_w_out, g_pool_w, g_cf_w_pw1, g_cf_w_pw2, g_ffn_w_up, g_ffn_w_down = big_g

    def adam_big(name, w, g, m, v):
        shp = w.shape
        two = lambda t: t.reshape(-1, shp[-1])
        return [o.reshape(shp) for o in _adamw(name, two(w), two(g), two(m), two(v))]

    grads = dict(w_mod=g_w_mod, b_mod=gb_mod, norm_g=g_norm_g, sc_w_in=g_sc_w_in, sc_conv=g_sc_conv,
                 sc_w_out=g_sc_w_out, pool_w=g_pool_w, pool_b=g_pool_b, pool_scale=g_pool_scale,
                 cf_w_pw1=g_cf_w_pw1, cf_b_pw1=g_b_pw1, cf_w_dw=g_w_dw, cf_b_dw=g_b_dw, cf_ln_g=g_ln_g,
                 cf_ln_b=g_ln_b, cf_w_pw2=g_cf_w_pw2, cf_b_pw2=g_b_pw2, ffn_w_up=g_ffn_w_up, ffn_conv=g_ffn_conv,
                 ffn_b_conv=g_ffn_b, ffn_w_down=g_ffn_w_down)
    weights = dict(w_mod=w_mod, b_mod=b_mod, norm_g=norm_g, sc_w_in=sc_w_in, sc_conv=sc_conv, sc_w_out=sc_w_out,
                   pool_w=pool_w, pool_b=pool_b, pool_scale=pool_scale, cf_w_pw1=cf_w_pw1, cf_b_pw1=cf_b_pw1,
                   cf_w_dw=cf_w_dw, cf_b_dw=cf_b_dw, cf_ln_g=cf_ln_g, cf_ln_b=cf_ln_b, cf_w_pw2=cf_w_pw2,
                   cf_b_pw2=cf_b_pw2, ffn_w_up=ffn_w_up, ffn_conv=ffn_conv, ffn_b_conv=ffn_b_conv,
                   ffn_w_down=ffn_w_down)
    m_in = dict(w_mod=m_w_mod, b_mod=m_b_mod, norm_g=m_norm_g, sc_w_in=m_sc_w_in, sc_conv=m_sc_conv,
                sc_w_out=m_sc_w_out, pool_w=m_pool_w, pool_b=m_pool_b, pool_scale=m_pool_scale,
                cf_w_pw1=m_cf_w_pw1, cf_b_pw1=m_cf_b_pw1, cf_w_dw=m_cf_w_dw, cf_b_dw=m_cf_b_dw, cf_ln_g=m_cf_ln_g,
                cf_ln_b=m_cf_ln_b, cf_w_pw2=m_cf_w_pw2, cf_b_pw2=m_cf_b_pw2, ffn_w_up=m_ffn_w_up,
                ffn_conv=m_ffn_conv, ffn_b_conv=m_ffn_b_conv, ffn_w_down=m_ffn_w_down)
    v_in = dict(w_mod=v_w_mod, b_mod=v_b_mod, norm_g=v_norm_g, sc_w_in=v_sc_w_in, sc_conv=v_sc_conv,
                sc_w_out=v_sc_w_out, pool_w=v_pool_w, pool_b=v_pool_b, pool_scale=v_pool_scale,
                cf_w_pw1=v_cf_w_pw1, cf_b_pw1=v_cf_b_pw1, cf_w_dw=v_cf_w_dw, cf_b_dw=v_cf_b_dw, cf_ln_g=v_cf_ln_g,
                cf_ln_b=v_cf_ln_b, cf_w_pw2=v_cf_w_pw2, cf_b_pw2=v_cf_b_pw2, ffn_w_up=v_ffn_w_up,
                ffn_conv=v_ffn_conv, ffn_b_conv=v_ffn_b_conv, ffn_w_down=v_ffn_w_down)
    names = list(weights)
    big_names = ["w_mod", "sc_w_in", "sc_w_out", "pool_w", "cf_w_pw1", "cf_w_pw2", "ffn_w_up", "ffn_w_down"]
    small_names = [n for n in names if n not in big_names]
    delta, new_m, new_v = {}, {}, {}
    for n in big_names:
        delta[n], new_m[n], new_v[n] = adam_big(f"adamw_{n}", weights[n], grads[n], m_in[n], v_in[n])
    grads = {n: grads[n].reshape(weights[n].shape) for n in names}
    sm_shapes = [weights[n].shape for n in small_names]
    sd, sm, sv = _adamw("adamw_small", _pack([weights[n] for n in small_names]), _pack([grads[n] for n in small_names]),
                        _pack([m_in[n] for n in small_names]), _pack([v_in[n] for n in small_names]))
    for n, d_, m_, v_ in zip(small_names, _unpack(sd, sm_shapes), _unpack(sm, sm_shapes), _unpack(sv, sm_shapes)):
        delta[n], new_m[n], new_v[n] = d_, m_, v_

    return (loss, grad_x, *[grads[n] for n in names], *[delta[n] for n in names], *[new_m[n] for n in names],
            *[new_v[n] for n in names])


def _add_bias(name, m, b):
    s = m.shape[0]
    tm = _tile(s, ROW_TILE_D)

    def body(m_ref, b_ref, o_ref):
        o_ref[...] = m_ref[...] + b_ref[0:1, :]

    return pl.pallas_call(body, name=name, out_shape=SDS((s, D), F32), grid=(s // tm,),
                          in_specs=[_rows(tm, D), _const((8, D))], out_specs=_rows(tm, D),
                          compiler_params=_cp(1))(m, _vecs([b], D))
```

```python
import itertools

import jax
import jax.numpy as jnp
from jax import lax
from jax.experimental import pallas as pl
from jax.experimental.pallas import tpu as pltpu

D = 1024
F = 2816
DEPTH = 4
POOL_WINDOWS = (2, 4, 8, 16)
POOL_GROUP = 256
CF_TAPS = 31
RMS_EPS = 1e-6
LN_EPS = 1e-5
ADAM_LR = 0.001
ADAM_B1 = 0.9
ADAM_B2 = 0.999
ADAM_EPS = 1e-08
ADAM_WD = 0.01
ADAM_STEP = 10

BF = jnp.bfloat16
F32 = jnp.float32
MESH = pl.DeviceIdType.MESH
SDS = jax.ShapeDtypeStruct
N_CHIPS = 4
N_DEV = 8
VMEM_LIMIT_MB = 56
ROW_TILE_D = 256
ROW_TILE_F = 256
HALO = 8
HALO_BF = 16
CF_HALO = 32
POOL_HALO = 16


def _cp(n_axes):
    return pltpu.CompilerParams(dimension_semantics=("arbitrary",) * n_axes,
                                vmem_limit_bytes=VMEM_LIMIT_MB * 1024 * 1024)


def _tile(n, pref):
    t = min(n, pref)
    assert n % t == 0, (n, pref)
    return t


def _const(shape):
    nd = len(shape)
    return pl.BlockSpec(shape, lambda *_: (0,) * nd)


def _vecs(rows, width):
    v = jnp.stack([r.reshape(width).astype(F32) for r in rows])
    pad = (-v.shape[0]) % 8
    return jnp.pad(v, ((0, pad), (0, 0)))


def _sigmoid(v):
    return 0.5 * jnp.tanh(0.5 * v) + 0.5


def _down(prev8, g, k):
    n = g.shape[0]
    return pltpu.roll(jnp.concatenate([prev8, g], axis=0), k, 0)[8:8 + n]


def _up(g, next8, k):
    n = g.shape[0]
    return pltpu.roll(jnp.concatenate([g, next8], axis=0), n + 8 - k, 0)[0:n]


def _fold8(v):
    acc = v[0:8]
    for r in range(8, v.shape[0], 8):
        acc = acc + v[r:r + 8]
    return acc


def _mm(name, a, b, out_sds, grid, a_spec, b_spec, o_spec, acc_shape, dn):
    nk = grid[2]

    def body(a_ref, b_ref, o_ref, *acc):
        part = lax.dot_general(a_ref[...], b_ref[...], dn, preferred_element_type=F32)
        if nk == 1:
            o_ref[...] = part.astype(o_ref.dtype)
        else:
            acc_ref = acc[0]
            k = pl.program_id(2)

            @pl.when(k == 0)
            def _():
                acc_ref[...] = part

            @pl.when(k > 0)
            def _():
                acc_ref[...] += part

            @pl.when(k == nk - 1)
            def _():
                o_ref[...] = acc_ref[...].astype(o_ref.dtype)

    scratch = [] if nk == 1 else [pltpu.VMEM(acc_shape, F32)]
    return pl.pallas_call(body, name=name, out_shape=out_sds, grid=grid, in_specs=[a_spec, b_spec],
                          out_specs=o_spec, scratch_shapes=scratch, compiler_params=_cp(3))(a, b)


NN = (((1,), (0,)), ((), ()))
NT = (((1,), (1,)), ((), ()))
TN = (((0,), (0,)), ((), ()))


def _mm_nn(name, a, w, layer, col_sharded, out_dtype, tm=1024, tn=None):
    m, k = a.shape
    tm = _tile(m, tm)
    if col_sharded:
        n4 = w.shape[3]
        tn = n4 if tn is None else tn
        tpc = n4 // tn
        n = N_CHIPS * n4
        b_spec = pl.BlockSpec((None, None, k, tn), lambda i, j, kk: (layer, j // tpc, 0, j % tpc))
    else:
        n = w.shape[2]
        tn = n if tn is None else tn
        b_spec = pl.BlockSpec((None, k, tn), lambda i, j, kk: (layer, 0, j))
    return _mm(name, a, w, SDS((m, n), out_dtype), (m // tm, n // tn, 1),
               pl.BlockSpec((tm, k), lambda i, j, kk: (i, 0)), b_spec,
               pl.BlockSpec((tm, tn), lambda i, j, kk: (i, j)), None, NN)


def _mm_nt(name, g, w, layer, col_sharded, out_dtype, tm=1024, tn=None):
    m, n = g.shape
    tm = _tile(m, tm)
    if col_sharded:
        kdim, n4 = w.shape[2], w.shape[3]
        tn = kdim if tn is None else tn
        tk = n4
        b_spec = pl.BlockSpec((None, None, tn, tk), lambda i, j, kk: (layer, kk, j, 0))
    else:
        kdim = w.shape[1]
        tn = kdim if tn is None else tn
        tk = n
        b_spec = pl.BlockSpec((None, tn, tk), lambda i, j, kk: (layer, j, kk))
    return _mm(name, g, w, SDS((m, kdim), out_dtype), (m // tm, kdim // tn, n // tk),
               pl.BlockSpec((tm, tk), lambda i, j, kk: (i, kk)), b_spec,
               pl.BlockSpec((tm, tn), lambda i, j, kk: (i, j)), (tm, tn), NT)


def _mm_tn(name, a, g, col_sharded, tk=None, tn=None, ts=512):
    s, k = a.shape
    n = g.shape[1]
    ts = _tile(s, ts)
    tk = k if tk is None else tk
    if col_sharded:
        n4 = n // N_CHIPS
        tn = n4 if tn is None else tn
        tpc = n4 // tn
        out_sds = SDS((N_CHIPS, k, n4), BF)
        o_spec = pl.BlockSpec((None, tk, tn), lambda i, j, ss: (j // tpc, i, j % tpc))
    else:
        tn = n if tn is None else tn
        out_sds = SDS((k, n), BF)
        o_spec = pl.BlockSpec((tk, tn), lambda i, j, ss: (i, j))
    return _mm(name, a, g, out_sds, (k // tk, n // tn, s // ts),
               pl.BlockSpec((ts, tk), lambda i, j, ss: (ss, i)),
               pl.BlockSpec((ts, tn), lambda i, j, ss: (ss, j)), o_spec, (tk, tn), TN)


def _rows(tm, width, col=0):
    return pl.BlockSpec((tm, width), lambda i: (i, col))


def _pro_fwd(name, x, vec):
    s = x.shape[0]
    tm = _tile(s, ROW_TILE_D)

    def body(x_ref, v_ref, h_ref):
        xv = x_ref[...]
        r = lax.rsqrt(jnp.mean(xv * xv, axis=-1, keepdims=True) + RMS_EPS)
        a = v_ref[0:1, :] * (1.0 + v_ref[1:2, :])
        h_ref[...] = (xv * r * a + v_ref[2:3, :]).astype(h_ref.dtype)

    return pl.pallas_call(body, name=name, out_shape=SDS((s, D), BF), grid=(s // tm,),
                          in_specs=[_rows(tm, D), _const((8, D))], out_specs=_rows(tm, D),
                          compiler_params=_cp(1))(x, vec)


def _epi_fwd(name, x, m, vec):
    s = x.shape[0]
    tm = _tile(s, ROW_TILE_D)

    def body(x_ref, m_ref, v_ref, o_ref):
        mv = m_ref[...]
        rm = lax.rsqrt(jnp.mean(mv * mv, axis=-1, keepdims=True) + RMS_EPS)
        o_ref[...] = x_ref[...] + v_ref[0:1, :] * (mv * rm * v_ref[1:2, :])

    return pl.pallas_call(body, name=name, out_shape=SDS((s, D), F32), grid=(s // tm,),
                          in_specs=[_rows(tm, D), _rows(tm, D), _const((8, D))], out_specs=_rows(tm, D),
                          compiler_params=_cp(1))(x, m, vec)


def _loss_fwd_bwd(y, t):
    s = y.shape[0]
    tm = _tile(s, ROW_TILE_D)

    def body(y_ref, t_ref, dy_ref, acc_ref):
        @pl.when(pl.program_id(0) == 0)
        def _():
            acc_ref[...] = jnp.zeros_like(acc_ref)

        e = y_ref[...] - t_ref[...]
        dy_ref[...] = e * (1.0 / D)
        acc_ref[0:1, :] += jnp.sum(e * e, axis=0, keepdims=True) * (0.5 / D)

    return pl.pallas_call(body, name="loss", out_shape=(SDS((s, D), F32), SDS((8, D), F32)), grid=(s // tm,),
                          in_specs=[_rows(tm, D), _rows(tm, D)], out_specs=(_rows(tm, D), _const((8, D))),
                          compiler_params=_cp(1))(y, t)


def _epi_bwd(name, dxo, m, vec):
    s = dxo.shape[0]
    tm = _tile(s, ROW_TILE_D)

    def body(d_ref, m_ref, v_ref, dm_ref, acc_ref):
        @pl.when(pl.program_id(0) == 0)
        def _():
            acc_ref[...] = jnp.zeros_like(acc_ref)

        mv = m_ref[...]
        dv = d_ref[...]
        g = v_ref[0:1, :]
        ngb = v_ref[1:2, :]
        rm = lax.rsqrt(jnp.mean(mv * mv, axis=-1, keepdims=True) + RMS_EPS)
        mn = mv * rm
        dmn = dv * (g * ngb)
        dm = rm * (dmn - mn * jnp.mean(dmn * mn, axis=-1, keepdims=True))
        dm_ref[...] = dm.astype(dm_ref.dtype)
        t = dv * mn
        acc_ref[0:1, :] += jnp.sum(t, axis=0, keepdims=True) * ngb
        acc_ref[1:2, :] += jnp.sum(t, axis=0, keepdims=True) * g
        acc_ref[2:3, :] += jnp.sum(dm, axis=0, keepdims=True)

    return pl.pallas_call(body, name=name, out_shape=(SDS((s, D), BF), SDS((8, D), F32)), grid=(s // tm,),
                          in_specs=[_rows(tm, D), _rows(tm, D), _const((8, D))],
                          out_specs=(_rows(tm, D), _const((8, D))), compiler_params=_cp(1))(dxo, m, vec)


def _pro_bwd(name, dh, x, dxo, vec):
    s = x.shape[0]
    tm = _tile(s, ROW_TILE_D)

    def body(dh_ref, x_ref, d_ref, v_ref, dx_ref, acc_ref):
        @pl.when(pl.program_id(0) == 0)
        def _():
            acc_ref[...] = jnp.zeros_like(acc_ref)

        xv = x_ref[...]
        dhv = dh_ref[...].astype(F32)
        ng = v_ref[0:1, :]
        sc1 = 1.0 + v_ref[1:2, :]
        r = lax.rsqrt(jnp.mean(xv * xv, axis=-1, keepdims=True) + RMS_EPS)
        xn = xv * r
        dxn = dhv * (ng * sc1)
        dx_ref[...] = d_ref[...] + r * (dxn - xn * jnp.mean(dxn * xn, axis=-1, keepdims=True))
        t = jnp.sum(dhv * xn, axis=0, keepdims=True)
        acc_ref[0:1, :] += jnp.sum(dhv, axis=0, keepdims=True)
        acc_ref[1:2, :] += t * ng
        acc_ref[2:3, :] += t * sc1

    return pl.pallas_call(body, name=name, out_shape=(SDS((s, D), F32), SDS((8, D), F32)), grid=(s // tm,),
                          in_specs=[_rows(tm, D), _rows(tm, D), _rows(tm, D), _const((8, D))],
                          out_specs=(_rows(tm, D), _const((8, D))), compiler_params=_cp(1))(dh, x, dxo, vec)


def _carry_down(buf, tm, first):
    @pl.when(first)
    def _():
        buf[0:HALO, :] = jnp.zeros((HALO, buf.shape[1]), F32)

    @pl.when(jnp.logical_not(first))
    def _():
        buf[0:HALO, :] = buf[tm:tm + HALO, :]


def _carry_up(buf, tm, first, halo=HALO):
    @pl.when(first)
    def _():
        buf[tm:tm + halo, :] = jnp.zeros((halo, buf.shape[1]), F32)

    @pl.when(jnp.logical_not(first))
    def _():
        buf[tm:tm + halo, :] = buf[0:halo, :]


def _ffn_act_fwd(name, up, vec):
    s = up.shape[0]
    tm = _tile(s, ROW_TILE_F)
    rb_n = _tile(tm, 64)
    cw = 256

    def body(g_ref, v_ref, w_ref, a_ref, carry):
        @pl.when(pl.program_id(0) == 0)
        def _():
            carry[...] = jnp.zeros_like(carry)

        for cc in range(F // cw):
            cols = slice(cc * cw, (cc + 1) * cw)
            w0, w1, w2, b = w_ref[0:1, cols], w_ref[1:2, cols], w_ref[2:3, cols], w_ref[3:4, cols]

            def rb_body(rb, prev8):
                rows = pl.ds(pl.multiple_of(rb * rb_n, rb_n), rb_n)
                g = g_ref[rows, cols].astype(F32)
                gc = w2 * g + w1 * _down(prev8, g, 1) + w0 * _down(prev8, g, 2) + b
                a_ref[rows, cols] = (gc * _sigmoid(gc) * v_ref[rows, cols].astype(F32)).astype(a_ref.dtype)
                return g[rb_n - 8:rb_n]

            carry[:, cols] = lax.fori_loop(0, tm // rb_n, rb_body, carry[:, cols])

    return pl.pallas_call(body, name=name, out_shape=SDS((s, F), BF), grid=(s // tm,),
                          in_specs=[_rows(tm, F, 0), _rows(tm, F, 1), _const((8, F))], out_specs=_rows(tm, F),
                          scratch_shapes=[pltpu.VMEM((HALO, F), F32)], compiler_params=_cp(1))(up, up, vec)


def _prev_halo_spec(nt, tm, width, col):
    per = tm // HALO_BF
    return pl.BlockSpec((HALO_BF, width), lambda i: (jnp.maximum((nt - 1 - i) * per - 1, 0), col))


def _ffn_act_bwd(name, up, da, vec):
    s = up.shape[0]
    tm = _tile(s, ROW_TILE_F)
    nt = s // tm
    rev = lambda col: pl.BlockSpec((tm, F), lambda i: (nt - 1 - i, col))

    rb_n = _tile(tm, 64)
    nrb = tm // rb_n
    cw = 128

    def body(g_ref, gp_ref, v_ref, da_ref, w_ref, a_ref, dup_ref, acc_ref, carry):
        step = pl.program_id(0)

        @pl.when(step == 0)
        def _():
            acc_ref[...] = jnp.zeros_like(acc_ref)
            carry[...] = jnp.zeros_like(carry)

        for cc in range(F // cw):
            cols = slice(cc * cw, (cc + 1) * cw)
            w0, w1, w2, b = w_ref[0:1, cols], w_ref[1:2, cols], w_ref[2:3, cols], w_ref[3:4, cols]
            halo = jnp.where(step < nt - 1, gp_ref[:, cols].astype(F32)[HALO_BF - 8:HALO_BF], 0.0)

            def rb_body(it, st):
                nxt8, ab, a0, a1, a2 = st
                rb = nrb - 1 - it
                r0 = pl.multiple_of(rb * rb_n, rb_n)
                rows = pl.ds(r0, rb_n)
                g = g_ref[rows, cols].astype(F32)
                ra = pl.multiple_of(jnp.maximum(r0 - HALO_BF, 0), HALO_BF)
                above = g_ref[pl.ds(ra, HALO_BF), cols].astype(F32)[HALO_BF - 8:HALO_BF]
                prev8 = jnp.where(rb == 0, halo, above)
                g1 = _down(prev8, g, 1)
                g2 = _down(prev8, g, 2)
                gc = w2 * g + w1 * g1 + w0 * g2 + b
                sg = _sigmoid(gc)
                sl = gc * sg
                val = v_ref[rows, cols].astype(F32)
                dav = da_ref[rows, cols].astype(F32)
                a_ref[rows, cols] = (sl * val).astype(a_ref.dtype)
                dup_ref[rows, F + cc * cw:F + (cc + 1) * cw] = (dav * sl).astype(dup_ref.dtype)
                dgc = (dav * val) * (sg + sl * (1.0 - sg))
                dup_ref[rows, cols] = (w2 * dgc + w1 * _up(dgc, nxt8, 1) + w0 * _up(dgc, nxt8, 2)).astype(dup_ref.dtype)
                return (dgc[0:8], ab + _fold8(dgc), a0 + _fold8(g2 * dgc), a1 + _fold8(g1 * dgc),
                        a2 + _fold8(g * dgc))

            z = jnp.zeros((8, cw), F32)
            nxt8, ab, a0, a1, a2 = lax.fori_loop(0, nrb, rb_body, (carry[:, cols], z, z, z, z))
            carry[:, cols] = nxt8
            acc_ref[0:1, cols] += jnp.sum(ab, axis=0, keepdims=True)
            acc_ref[1:2, cols] += jnp.sum(a0, axis=0, keepdims=True)
            acc_ref[2:3, cols] += jnp.sum(a1, axis=0, keepdims=True)
            acc_ref[3:4, cols] += jnp.sum(a2, axis=0, keepdims=True)

    return pl.pallas_call(
        body, name=name, out_shape=(SDS((s, F), BF), SDS((s, 2 * F), BF), SDS((8, F), F32)), grid=(nt,),
        in_specs=[rev(0), _prev_halo_spec(nt, tm, F, 0), rev(1), rev(0), _const((8, F))],
        out_specs=(rev(0), pl.BlockSpec((tm, 2 * F), lambda i: (nt - 1 - i, 0)), _const((8, F))),
        scratch_shapes=[pltpu.VMEM((HALO, F), F32)], compiler_params=_cp(1))(up, up, up, da, vec)


def _sc_act_fwd(name, p, vec):
    s = p.shape[0]
    tm = _tile(s, ROW_TILE_D)

    def body(b_ref, c_ref, h_ref, w_ref, z_ref, buf):
        _carry_down(buf, tm, pl.program_id(0) == 0)
        buf[HALO:HALO + tm, :] = c_ref[...].astype(F32) * h_ref[...].astype(F32)
        u = (w_ref[2:3, :] * buf[HALO:HALO + tm, :] + w_ref[1:2, :] * buf[HALO - 1:HALO - 1 + tm, :]
             + w_ref[0:1, :] * buf[HALO - 2:HALO - 2 + tm, :])
        z_ref[...] = (b_ref[...].astype(F32) * u).astype(z_ref.dtype)

    return pl.pallas_call(body, name=name, out_shape=SDS((s, D), BF), grid=(s // tm,),
                          in_specs=[_rows(tm, D, 0), _rows(tm, D, 1), _rows(tm, D, 2), _const((8, D))],
                          out_specs=_rows(tm, D), scratch_shapes=[pltpu.VMEM((tm + HALO, D), F32)],
                          compiler_params=_cp(1))(p, p, p, vec)


def _sc_act_bwd(name, p, dz, vec):
    s = p.shape[0]
    tm = _tile(s, ROW_TILE_D)
    nt = s // tm
    rev = lambda col: pl.BlockSpec((tm, D), lambda i: (nt - 1 - i, col))

    def body(b_ref, c_ref, h_ref, cp_ref, hp_ref, dz_ref, w_ref, z_ref, dp_ref, acc_ref, pbuf, nbuf):
        step = pl.program_id(0)

        @pl.when(step == 0)
        def _():
            acc_ref[...] = jnp.zeros_like(acc_ref)

        cg = c_ref[...].astype(F32)
        hin = h_ref[...].astype(F32)
        bg = b_ref[...].astype(F32)
        q = cg * hin
        pbuf[0:HALO_BF, :] = jnp.where(step < nt - 1, cp_ref[...].astype(F32) * hp_ref[...].astype(F32), 0.0)
        pbuf[HALO_BF:HALO_BF + tm, :] = q
        u = (w_ref[2:3, :] * q + w_ref[1:2, :] * pbuf[HALO_BF - 1:HALO_BF - 1 + tm, :]
             + w_ref[0:1, :] * pbuf[HALO_BF - 2:HALO_BF - 2 + tm, :])
        dzv = dz_ref[...].astype(F32)
        z_ref[...] = (bg * u).astype(z_ref.dtype)
        dp_ref[:, 0:D] = (dzv * u).astype(dp_ref.dtype)
        du = dzv * bg
        _carry_up(nbuf, tm, step == 0)
        nbuf[0:tm, :] = du
        d1 = nbuf[1:1 + tm, :]
        d2 = nbuf[2:2 + tm, :]
        dq = w_ref[2:3, :] * du + w_ref[1:2, :] * d1 + w_ref[0:1, :] * d2
        dp_ref[:, D:2 * D] = (dq * hin).astype(dp_ref.dtype)
        dp_ref[:, 2 * D:3 * D] = (dq * cg).astype(dp_ref.dtype)
        acc_ref[0:1, :] += jnp.sum(q * d2, axis=0, keepdims=True)
        acc_ref[1:2, :] += jnp.sum(q * d1, axis=0, keepdims=True)
        acc_ref[2:3, :] += jnp.sum(q * du, axis=0, keepdims=True)

    return pl.pallas_call(
        body, name=name, out_shape=(SDS((s, D), BF), SDS((s, 3 * D), BF), SDS((8, D), F32)), grid=(nt,),
        in_specs=[rev(0), rev(1), rev(2), _prev_halo_spec(nt, tm, D, 1), _prev_halo_spec(nt, tm, D, 2), rev(0),
                  _const((8, D))],
        out_specs=(rev(0), pl.BlockSpec((tm, 3 * D), lambda i: (nt - 1 - i, 0)), _const((8, D))),
        scratch_shapes=[pltpu.VMEM((tm + HALO_BF, D), F32), pltpu.VMEM((tm + HALO, D), F32)],
        compiler_params=_cp(1))(p, p, p, p, p, dz, vec)


def _cf_act_fwd(name, p, taps, vec):
    s = p.shape[0]
    tm = _tile(s, ROW_TILE_D)
    base = CF_HALO - (CF_TAPS - 1)

    def body(a_ref, g_ref, t_ref, v_ref, w_ref, cv_ref, buf):
        first = pl.program_id(0) == 0

        @pl.when(first)
        def _():
            buf[0:CF_HALO, :] = jnp.zeros((CF_HALO, D), F32)

        @pl.when(jnp.logical_not(first))
        def _():
            buf[0:CF_HALO, :] = buf[tm:tm + CF_HALO, :]

        a = a_ref[...].astype(F32) + v_ref[0:1, :]
        g = g_ref[...].astype(F32) + v_ref[1:2, :]
        buf[CF_HALO:CF_HALO + tm, :] = a * _sigmoid(g)
        cv = jnp.zeros((tm, D), F32) + v_ref[2:3, :]
        for k in range(CF_TAPS):
            cv = cv + t_ref[k:k + 1, :] * buf[base + k:base + k + tm, :]
        cv_ref[...] = cv
        mu = jnp.mean(cv, axis=-1, keepdims=True)
        cc = cv - mu
        rstd = lax.rsqrt(jnp.mean(cc * cc, axis=-1, keepdims=True) + LN_EPS)
        ln = cc * rstd * v_ref[3:4, :] + v_ref[4:5, :]
        w_ref[...] = (ln * _sigmoid(ln)).astype(w_ref.dtype)

    return pl.pallas_call(body, name=name, out_shape=(SDS((s, D), BF), SDS((s, D), F32)), grid=(s // tm,),
                          in_specs=[_rows(tm, D, 0), _rows(tm, D, 1), _const((32, D)), _const((8, D))],
                          out_specs=(_rows(tm, D), _rows(tm, D)),
                          scratch_shapes=[pltpu.VMEM((tm + CF_HALO, D), F32)], compiler_params=_cp(1))(p, p, taps, vec)


def _cf_act_bwd(name, p, cv, dw, taps, vec):
    s = p.shape[0]
    tm = _tile(s, ROW_TILE_D)
    nt = s // tm
    rev = lambda col: pl.BlockSpec((tm, D), lambda i: (nt - 1 - i, col))

    def body(a_ref, g_ref, cv_ref, dw_ref, t_ref, v_ref, dp_ref, tacc_ref, acc_ref, nbuf):
        step = pl.program_id(0)

        @pl.when(step == 0)
        def _():
            acc_ref[...] = jnp.zeros_like(acc_ref)
            tacc_ref[...] = jnp.zeros_like(tacc_ref)

        a = a_ref[...].astype(F32) + v_ref[0:1, :]
        g = g_ref[...].astype(F32) + v_ref[1:2, :]
        sg = _sigmoid(g)
        u = a * sg
        cvv = cv_ref[...]
        mu = jnp.mean(cvv, axis=-1, keepdims=True)
        cc = cvv - mu
        rstd = lax.rsqrt(jnp.mean(cc * cc, axis=-1, keepdims=True) + LN_EPS)
        vhat = cc * rstd
        ln = vhat * v_ref[3:4, :] + v_ref[4:5, :]
        s2 = _sigmoid(ln)
        dln = dw_ref[...].astype(F32) * (s2 * (1.0 + ln * (1.0 - s2)))
        acc_ref[1:2, :] += jnp.sum(dln * vhat, axis=0, keepdims=True)
        acc_ref[2:3, :] += jnp.sum(dln, axis=0, keepdims=True)
        dvh = dln * v_ref[3:4, :]
        dcv = rstd * (dvh - jnp.mean(dvh, axis=-1, keepdims=True)
                      - vhat * jnp.mean(dvh * vhat, axis=-1, keepdims=True))
        acc_ref[0:1, :] += jnp.sum(dcv, axis=0, keepdims=True)
        _carry_up(nbuf, tm, step == 0, CF_HALO)
        nbuf[0:tm, :] = dcv
        du = jnp.zeros((tm, D), F32)
        for j in range(CF_TAPS):
            k = CF_TAPS - 1 - j
            sh = nbuf[j:j + tm, :]
            du = du + t_ref[k:k + 1, :] * sh
            tacc_ref[k:k + 1, :] += jnp.sum(u * sh, axis=0, keepdims=True)
        da = du * sg
        dg = du * a * sg * (1.0 - sg)
        dp_ref[:, 0:D] = da.astype(dp_ref.dtype)
        dp_ref[:, D:2 * D] = dg.astype(dp_ref.dtype)
        acc_ref[3:4, :] += jnp.sum(da, axis=0, keepdims=True)
        acc_ref[4:5, :] += jnp.sum(dg, axis=0, keepdims=True)

    return pl.pallas_call(
        body, name=name, out_shape=(SDS((s, 2 * D), BF), SDS((32, D), F32), SDS((8, D), F32)), grid=(nt,),
        in_specs=[rev(0), rev(1), rev(0), rev(0), _const((32, D)), _const((8, D))],
        out_specs=(pl.BlockSpec((tm, 2 * D), lambda i: (nt - 1 - i, 0)), _const((32, D)), _const((8, D))),
        scratch_shapes=[pltpu.VMEM((tm + CF_HALO, D), F32)], compiler_params=_cp(1))(p, p, cv, dw, taps, vec)


def _inv_count(row0, tm, window):
    t = row0 + lax.broadcasted_iota(jnp.int32, (tm, 1), 0)
    return 1.0 / jnp.minimum(t + 1, window).astype(F32)


def _pool_fwd(name, x, w, vec):
    s = x.shape[0]
    tm = _tile(s, ROW_TILE_D)
    G = POOL_GROUP

    def body(x_ref, w_ref, v_ref, pl_ref, m_ref, buf):
        i = pl.program_id(0)

        @pl.when(i == 0)
        def _():
            buf[0:POOL_HALO, :] = jnp.zeros((POOL_HALO, D), F32)

        @pl.when(i > 0)
        def _():
            buf[0:POOL_HALO, :] = buf[tm:tm + POOL_HALO, :]

        xv = x_ref[...]
        r = lax.rsqrt(jnp.mean(xv * xv, axis=-1, keepdims=True) + RMS_EPS)
        buf[POOL_HALO:POOL_HALO + tm, :] = xv * r * (v_ref[0:1, :] * (1.0 + v_ref[1:2, :])) + v_ref[2:3, :]
        for gi, win in enumerate(POOL_WINDOWS):
            cols = slice(gi * G, (gi + 1) * G)
            acc = buf[POOL_HALO:POOL_HALO + tm, cols]
            hg = acc
            for j in range(1, win):
                acc = acc + buf[POOL_HALO - j:POOL_HALO - j + tm, cols]
            pooled = (acc * _inv_count(i * tm, tm, win) - hg).astype(BF)
            pl_ref[:, cols] = pooled
            yg = jnp.dot(pooled, w_ref[gi], preferred_element_type=F32)
            m_ref[:, cols] = (yg + v_ref[3:4, cols]) * v_ref[4:5, cols]

    return pl.pallas_call(body, name=name, out_shape=(SDS((s, D), BF), SDS((s, D), F32)), grid=(s // tm,),
                          in_specs=[_rows(tm, D), _const((4, G, G)), _const((8, D))],
                          out_specs=(_rows(tm, D), _rows(tm, D)),
                          scratch_shapes=[pltpu.VMEM((tm + POOL_HALO, D), F32)], compiler_params=_cp(1))(x, w, vec)


def _pool_bwd(name, pooled, dm, w, vec):
    s = pooled.shape[0]
    tm = _tile(s, ROW_TILE_D)
    nt = s // tm
    G = POOL_GROUP
    rev = pl.BlockSpec((tm, D), lambda i: (nt - 1 - i, 0))

    def body(p_ref, dm_ref, w_ref, v_ref, dh_ref, dw_ref, acc_ref, nbuf):
        step = pl.program_id(0)
        row0 = (nt - 1 - step) * tm

        @pl.when(step == 0)
        def _():
            acc_ref[...] = jnp.zeros_like(acc_ref)
            dw_ref[...] = jnp.zeros_like(dw_ref)

        _carry_up(nbuf, tm, step == 0, POOL_HALO)
        dmv = dm_ref[...].astype(F32)
        acc_ref[0:1, :] += jnp.sum(dmv, axis=0, keepdims=True) * v_ref[1:2, :]
        dps = []
        for gi, win in enumerate(POOL_WINDOWS):
            cols = slice(gi * G, (gi + 1) * G)
            pg = p_ref[:, cols]
            yb = jnp.dot(pg, w_ref[gi], preferred_element_type=F32) + v_ref[0:1, cols]
            acc_ref[1:2, cols] += jnp.sum(dmv[:, cols] * yb, axis=0, keepdims=True)
            dy = (dmv[:, cols] * v_ref[1:2, cols]).astype(BF)
            dw_ref[gi] += lax.dot_general(pg, dy, TN, preferred_element_type=F32)
            dpg = lax.dot_general(dy, w_ref[gi], NT, preferred_element_type=F32)
            dps.append(dpg)
            nbuf[0:tm, cols] = dpg * _inv_count(row0, tm, win)
        for gi, win in enumerate(POOL_WINDOWS):
            cols = slice(gi * G, (gi + 1) * G)
            acc = nbuf[0:tm, cols]
            for j in range(1, win):
                acc = acc + nbuf[j:j + tm, cols]
            dh_ref[:, cols] = acc - dps[gi]

    return pl.pallas_call(
        body, name=name, out_shape=(SDS((s, D), F32), SDS((4, G, G), F32), SDS((8, D), F32)), grid=(nt,),
        in_specs=[rev, rev, _const((4, G, G)), _const((8, D))],
        out_specs=(rev, _const((4, G, G)), _const((8, D))),
        scratch_shapes=[pltpu.VMEM((tm + POOL_HALO, D), F32)], compiler_params=_cp(1))(pooled, dm, w, vec)


def _row_tile_2d(rows, width, bytes_per_row_elem=4, budget=2 * 1024 * 1024):
    t = max(8, budget // (width * bytes_per_row_elem))
    t = min(rows, 1 << (t.bit_length() - 1))
    while rows % t:
        t //= 2
    return t


def _add2(name, a, b, out_dtype):
    rows, width = a.shape
    tm = _row_tile_2d(rows, width)

    def body(a_ref, b_ref, o_ref):
        o_ref[...] = (a_ref[...].astype(F32) + b_ref[...].astype(F32)).astype(o_ref.dtype)

    return pl.pallas_call(body, name=name, out_shape=SDS((rows, width), out_dtype), grid=(rows // tm,),
                          in_specs=[_rows(tm, width), _rows(tm, width)], out_specs=_rows(tm, width),
                          compiler_params=_cp(1))(a, b)


def _add_slots(name, r):
    nl, _, k, n = r.shape
    tk = _row_tile_2d(k, n, 8)

    def body(r_ref, o_ref):
        o_ref[...] = ((r_ref[3].astype(F32) + r_ref[0].astype(F32)) + r_ref[1].astype(F32)) + r_ref[2].astype(F32)

    return pl.pallas_call(body, name=name, out_shape=SDS((nl, k, n), F32), grid=(nl, k // tk),
                          in_specs=[pl.BlockSpec((None, 4, tk, n), lambda l, i: (l, 0, i, 0))],
                          out_specs=pl.BlockSpec((None, tk, n), lambda l, i: (l, i, 0)),
                          compiler_params=_cp(2))(r)


def _adamw(name, w, g, m, v):
    rows, width = w.shape
    tm = _row_tile_2d(rows, width, 4, 1024 * 1024)
    c1 = 1.0 - ADAM_B1 ** ADAM_STEP
    c2 = 1.0 - ADAM_B2 ** ADAM_STEP

    def body(w_ref, g_ref, m_ref, v_ref, d_ref, nm_ref, nv_ref):
        gv = g_ref[...]
        nm = ADAM_B1 * m_ref[...] + (1.0 - ADAM_B1) * gv
        nv = ADAM_B2 * v_ref[...] + (1.0 - ADAM_B2) * (gv * gv)
        nm_ref[...] = nm
        nv_ref[...] = nv
        d_ref[...] = -ADAM_LR * ((nm / c1) / (jnp.sqrt(nv / c2) + ADAM_EPS) + ADAM_WD * w_ref[...])

    spec = _rows(tm, width)
    sds = SDS((rows, width), F32)
    return pl.pallas_call(body, name=name, out_shape=(sds, sds, sds), grid=(rows // tm,),
                          in_specs=[spec] * 4, out_specs=(spec,) * 3, compiler_params=_cp(1))(w, g, m, v)


def _mod_fwd(c16, w_mod, b_sh):
    n = w_mod.shape[2]
    tn = _tile(n, 512)

    def body(c_ref, w_ref, b_ref, o_ref):
        cv = c_ref[...]
        ca = (cv * _sigmoid(cv)).astype(BF)
        o_ref[...] = jnp.dot(ca, w_ref[...].astype(BF), preferred_element_type=F32) + b_ref[0:1, :]

    return pl.pallas_call(body, name="mod_fwd", out_shape=SDS((DEPTH, 16, n), F32), grid=(DEPTH, n // tn),
                          in_specs=[_const((16, D)), pl.BlockSpec((None, D, tn), lambda l, j: (l, 0, j)),
                                    pl.BlockSpec((None, 8, tn), lambda l, j: (l, 0, j))],
                          out_specs=pl.BlockSpec((None, 16, tn), lambda l, j: (l, 0, j)),
                          compiler_params=_cp(2))(c16, w_mod, b_sh)


def _mod_bwd(c16, dmod):
    n = dmod.shape[2]
    tn = _tile(n, 512)

    def body(c_ref, d_ref, o_ref):
        cv = c_ref[...]
        ca = (cv * _sigmoid(cv)).astype(BF)
        o_ref[...] = lax.dot_general(ca, d_ref[...].astype(BF), TN, preferred_element_type=F32)

    return pl.pallas_call(body, name="mod_bwd", out_shape=SDS((DEPTH, D, n), F32), grid=(DEPTH, n // tn),
                          in_specs=[_const((16, D)), pl.BlockSpec((None, 16, tn), lambda l, j: (l, 0, j))],
                          out_specs=pl.BlockSpec((None, D, tn), lambda l, j: (l, 0, j)),
                          compiler_params=_cp(2))(c16, dmod)


def _place():
    x, y, c = lax.axis_index("x"), lax.axis_index("y"), lax.axis_index("c")
    other_chips = [(1 - x, y), (x, 1 - y), (1 - x, 1 - y)]
    return x, y, c, other_chips


def _allgather_small(name, v, with_sum):
    m, n = v.shape

    def body(x_ref, out_ref, *rest):
        if with_sum:
            sum_ref, send_sems, recv_sems, local_sem = rest
        else:
            send_sems, recv_sems, local_sem = rest
        x, y, c, chips = _place()
        me, sibling = (x, y, c), (x, y, 1 - c)

        def rows(px, py, pc):
            return out_ref.at[pl.ds((4 * px + 2 * py + pc) * m, m), :]

        def copy(k, block, to, src=None):
            return pltpu.make_async_remote_copy(
                src_ref=rows(*block) if src is None else src, dst_ref=rows(*block),
                send_sem=send_sems.at[k], recv_sem=recv_sems.at[k], device_id=to, device_id_type=MESH)

        mine = pltpu.make_async_copy(x_ref, rows(*me), local_sem)
        mine.start()
        first = [copy(0, me, sibling, src=x_ref)]
        first += [copy(1 + j, me, (*chip, c), src=x_ref) for j, chip in enumerate(chips)]
        for cp in first:
            cp.start()
        passed = [copy(4 + j, (*chip, c), sibling) for j, chip in enumerate(chips)]
        for j, chip in enumerate(chips):
            copy(1 + j, (*chip, c), me).wait_recv()
            passed[j].start()
        copy(0, sibling, me).wait_recv()
        for j, chip in enumerate(chips):
            copy(4 + j, (*chip, 1 - c), me).wait_recv()
        for cp in first + passed:
            cp.wait_send()
        mine.wait()
        if with_sum:
            acc = out_ref[0:m, :]
            for k in range(1, N_DEV):
                acc = acc + out_ref[k * m:(k + 1) * m, :]
            sum_ref[...] = acc

    vm = pl.BlockSpec(memory_space=pltpu.VMEM)
    out_shape = [SDS((N_DEV * m, n), F32)] + ([SDS((m, n), F32)] if with_sum else [])
    res = pl.pallas_call(
        body, name=name, out_shape=tuple(out_shape), in_specs=[vm], out_specs=tuple([vm] * len(out_shape)),
        scratch_shapes=[pltpu.SemaphoreType.DMA((7,)), pltpu.SemaphoreType.DMA((7,)), pltpu.SemaphoreType.DMA],
        compiler_params=pltpu.CompilerParams(vmem_limit_bytes=VMEM_LIMIT_MB * 1024 * 1024))(v)
    return res if with_sum else res[0]


HBM = pl.BlockSpec(memory_space=pltpu.HBM)


def _sem_scratch(n_remote, n_local):
    return [pltpu.SemaphoreType.DMA((n_remote,)), pltpu.SemaphoreType.DMA((n_remote,)),
            pltpu.SemaphoreType.DMA((n_local,))]


DMA_PIECE_BYTES = 1 << 20


def _pieces(src, dst):
    *lead, rows, n = src.shape
    nsplit = max(1, min(rows // 16, (rows * n * jnp.dtype(src.dtype).itemsize) // DMA_PIECE_BYTES))
    while rows % nsplit or (rows // nsplit) % 16:
        nsplit -= 1
    size = rows // nsplit
    out = []
    for idx in itertools.product(*[range(d) for d in lead]):
        for i in range(nsplit):
            sl = tuple(idx) + (pl.ds(i * size, size),)
            out.append((src.at[sl], dst.at[sl]))
    return out


def _local_copies(src, dst, sem):
    return ([pltpu.make_async_copy(s_, d_, sem) for s_, d_ in _pieces(src, dst)],
            pltpu.make_async_copy(src, dst, sem))


def _remote_copies(src, dst, send_sem, recv_sem, to):
    mk = lambda s_, d_: pltpu.make_async_remote_copy(src_ref=s_, dst_ref=d_, send_sem=send_sem, recv_sem=recv_sem,
                                                     device_id=to, device_id_type=MESH)
    return [mk(s_, d_) for s_, d_ in _pieces(src, dst)], mk(src, dst)


def _gather_weights(shards):
    nq = len(shards)

    def body(*refs):
        ins, outs = refs[:nq], refs[nq:2 * nq]
        send_sems, recv_sems, local_sems = refs[2 * nq:]
        x, y, c, chips = _place()
        me_chip = 2 * x + y
        started, local_all, send_all, recv_all = [], [], [], []
        for q in range(nq):
            cps, whole = _local_copies(ins[q], outs[q].at[:, me_chip], local_sems.at[q])
            started += cps
            local_all.append(whole)
            for r, (px, py) in enumerate(chips):
                k = 3 * q + r
                cps, whole = _remote_copies(ins[q], outs[q].at[:, me_chip], send_sems.at[k], recv_sems.at[k], (px, py, c))
                started += cps
                send_all.append(whole)
                recv_all.append(_remote_copies(ins[q], outs[q].at[:, 2 * px + py], send_sems.at[k], recv_sems.at[k],
                                               (px, py, c))[1])
        for cp in started:
            cp.start()
        for cp in recv_all:
            cp.wait_recv()
        for cp in send_all:
            cp.wait_send()
        for cp in local_all:
            cp.wait()

    out_shape = tuple(SDS((s.shape[0], N_CHIPS) + s.shape[1:], s.dtype) for s in shards)
    return pl.pallas_call(body, name="gather_weights", out_shape=out_shape, in_specs=[HBM] * nq,
                          out_specs=tuple([HBM] * nq), scratch_shapes=_sem_scratch(3 * nq, nq))(*shards)


def _pair_exchange(gs):
    nq = len(gs)

    def body(*refs):
        ins, own, sib = refs[:nq], refs[nq:2 * nq], refs[2 * nq:3 * nq]
        send_sems, recv_sems, local_sems = refs[3 * nq:]
        x, y, c, _ = _place()
        started, local_all, remote_all = [], [], []
        for q in range(nq):
            cps, whole = _local_copies(ins[q].at[:, :, c], own[q], local_sems.at[q])
            started += cps
            local_all.append(whole)
            cps, whole = _remote_copies(ins[q].at[:, :, 1 - c], sib[q], send_sems.at[q], recv_sems.at[q], (x, y, 1 - c))
            started += cps
            remote_all.append(whole)
        for cp in started:
            cp.start()
        for cp in remote_all:
            cp.wait_recv()
        for cp in remote_all:
            cp.wait_send()
        for cp in local_all:
            cp.wait()

    half = tuple(SDS((g.shape[0], g.shape[1], g.shape[3], g.shape[4]), g.dtype) for g in gs)
    res = pl.pallas_call(body, name="grad_pair_exchange", out_shape=half + half, in_specs=[HBM] * nq,
                         out_specs=tuple([HBM] * (2 * nq)), scratch_shapes=_sem_scratch(nq, nq))(*gs)
    return res[:nq], res[nq:]


def _chip_exchange(gs):
    nq = len(gs)

    def body(*refs):
        ins, outs = refs[:nq], refs[nq:2 * nq]
        send_sems, recv_sems, local_sems = refs[2 * nq:]
        x, y, c, chips = _place()
        started, local_all, remote_all = [], [], []
        for q in range(nq):
            cps, whole = _local_copies(ins[q].at[:, 2 * x + y], outs[q].at[:, 3], local_sems.at[q])
            started += cps
            local_all.append(whole)
            for r, (px, py) in enumerate(chips):
                k = 3 * q + r
                cps, whole = _remote_copies(ins[q].at[:, 2 * px + py], outs[q].at[:, r], send_sems.at[k],
                                            recv_sems.at[k], (px, py, c))
                started += cps
                remote_all.append(whole)
        for cp in started:
            cp.start()
        for cp in remote_all:
            cp.wait_recv()
        for cp in remote_all:
            cp.wait_send()
        for cp in local_all:
            cp.wait()

    out_shape = tuple(SDS(g.shape, g.dtype) for g in gs)
    return pl.pallas_call(body, name="grad_chip_exchange", out_shape=out_shape, in_specs=[HBM] * nq,
                          out_specs=tuple([HBM] * nq), scratch_shapes=_sem_scratch(3 * nq, nq))(*gs)


def _pair_share(rs):
    nq = len(rs)

    def body(*refs):
        ins, outs = refs[:nq], refs[nq:2 * nq]
        send_sems, recv_sems, local_sems = refs[2 * nq:]
        x, y, c, _ = _place()
        started, local_all, send_all, recv_all = [], [], [], []
        for q in range(nq):
            cps, whole = _local_copies(ins[q], outs[q].at[:, c], local_sems.at[q])
            started += cps
            local_all.append(whole)
            cps, whole = _remote_copies(ins[q], outs[q].at[:, c], send_sems.at[q], recv_sems.at[q], (x, y, 1 - c))
            started += cps
            send_all.append(whole)
            recv_all.append(_remote_copies(ins[q], outs[q].at[:, 1 - c], send_sems.at[q], recv_sems.at[q],
                                           (x, y, 1 - c))[1])
        for cp in started:
            cp.start()
        for cp in recv_all:
            cp.wait_recv()
        for cp in send_all:
            cp.wait_send()
        for cp in local_all:
            cp.wait()

    out_shape = tuple(SDS((r.shape[0], 2) + r.shape[1:], r.dtype) for r in rs)
    return pl.pallas_call(body, name="grad_pair_share", out_shape=out_shape, in_specs=[HBM] * nq,
                          out_specs=tuple([HBM] * nq), scratch_shapes=_sem_scratch(nq, nq))(*rs)


def _pack(arrs, rows_multiple=8):
    flat = jnp.concatenate([a.astype(F32).reshape(-1) for a in arrs])
    pad = (-flat.shape[0]) % (128 * rows_multiple)
    return jnp.pad(flat, (0, pad)).reshape(-1, 128)


def _unpack(slab, shapes):
    flat = slab.reshape(-1)
    out, off = [], 0
    for shp in shapes:
        n = 1
        for d in shp:
            n *= d
        out.append(flat[off:off + n].reshape(shp))
        off += n
    return out


def _shard_last(a, chip, n):
    return lax.dynamic_slice_in_dim(a, chip * n, n, axis=a.ndim - 1)


def kernel(x, c, w_mod, b_mod, norm_g, sc_w_in, sc_conv, sc_w_out, pool_w, pool_b, pool_scale, cf_w_pw1, cf_b_pw1, cf_w_dw, cf_b_dw, cf_ln_g, cf_ln_b, cf_w_pw2, cf_b_pw2, ffn_w_up, ffn_conv, ffn_b_conv, ffn_w_down, loss_target, m_w_mod, m_b_mod, m_norm_g, m_sc_w_in, m_sc_conv, m_sc_w_out, m_pool_w, m_pool_b, m_pool_scale, m_cf_w_pw1, m_cf_b_pw1, m_cf_w_dw, m_cf_b_dw, m_cf_ln_g, m_cf_ln_b, m_cf_w_pw2, m_cf_b_pw2, m_ffn_w_up, m_ffn_conv, m_ffn_b_conv, m_ffn_w_down, v_w_mod, v_b_mod, v_norm_g, v_sc_w_in, v_sc_conv, v_sc_w_out, v_pool_w, v_pool_b, v_pool_scale, v_cf_w_pw1, v_cf_b_pw1, v_cf_w_dw, v_cf_b_dw, v_cf_ln_g, v_cf_ln_b, v_cf_w_pw2, v_cf_b_pw2, v_ffn_w_up, v_ffn_conv, v_ffn_b_conv, v_ffn_w_down):
    ax, ay, ac = lax.axis_index("x"), lax.axis_index("y"), lax.axis_index("c")
    chip = 2 * ax + ay
    dev = 4 * ax + 2 * ay + ac
    xs = x[0]
    target = loss_target[0]

    small_sharded = [norm_g, sc_conv, cf_b_pw1, cf_w_dw, cf_b_dw, cf_ln_g, cf_ln_b, cf_b_pw2, ffn_conv]
    slab = _pack([c] + small_sharded)
    gathered = _allgather_small("gather_small_params", slab, False).reshape(N_DEV, -1, 128)
    parts = [_unpack(gathered[d], [c.shape] + [a.shape for a in small_sharded]) for d in range(N_DEV)]
    c_all = jnp.concatenate([p[0] for p in parts], axis=0)
    full = [jnp.concatenate([parts[2 * j][1 + i] for j in range(N_CHIPS)], axis=-1)
            for i in range(len(small_sharded))]
    norm_g_f, sc_conv_f, cf_b_pw1_f, cf_w_dw_f, cf_b_dw_f, cf_ln_g_f, cf_ln_b_f, cf_b_pw2_f, ffn_conv_f = full
    c16 = jnp.pad(c_all, ((0, 8), (0, 0)))

    n_mod = w_mod.shape[2]
    b_sh = jnp.broadcast_to(_shard_last(b_mod, chip, n_mod)[:, None, :], (DEPTH, 8, n_mod))
    mod_part = _mod_fwd(c16, w_mod, b_sh)
    mod_g = _allgather_small("gather_mod", mod_part.reshape(DEPTH * 16, n_mod), False)
    mod_g = mod_g.reshape(N_DEV, DEPTH, 16, n_mod)
    mod_mine = jnp.concatenate(
        [lax.dynamic_index_in_dim(mod_g[2 * j], dev, axis=1, keepdims=False) for j in range(N_CHIPS)], axis=-1)
    mod = mod_mine.reshape(DEPTH, 6, D)

    shards = [sc_w_in.astype(BF), sc_w_out.astype(BF), pool_w[0].astype(BF), cf_w_pw1.astype(BF),
              cf_w_pw2.astype(BF), ffn_w_up.astype(BF), ffn_w_down.astype(BF)]
    w_in_f, w_out_f, pool_f, pw1_f, pw2_f, w_up_f, w_down_f = _gather_weights(shards)
    w_out_f = w_out_f.reshape(w_out_f.shape[0], D, D)
    pool_f = pool_f.reshape(4, POOL_GROUP, POOL_GROUP)
    pw2_f = pw2_f.reshape(1, D, D)
    w_down_f = w_down_f.reshape(DEPTH, F, D)

    saved = []
    xcur = xs
    for i in range(DEPTH):
        kind, j = i % 3, i // 3
        sh1, sc1, g1, sh2, sc2, g2 = [mod[i, k] for k in range(6)]
        st = {"x0": xcur}
        pro_vec = _vecs([norm_g_f[i, 0], sc1, sh1], D)
        if kind == 0:
            h = _pro_fwd(f"pro1_fwd_{i}", xcur, pro_vec)
            p = _mm_nn(f"sc_in_{i}", h, w_in_f, j, True, BF)
            z = _sc_act_fwd(f"sc_act_fwd_{i}", p, _vecs(list(sc_conv_f[j]), D))
            m = _mm_nn(f"sc_out_{i}", z, w_out_f, j, False, F32)
            st.update(h=h, p=p)
        elif kind == 1:
            pool_vec = _vecs([norm_g_f[i, 0], sc1, sh1, pool_b[j], pool_scale[j]], D)
            pooled, m = _pool_fwd(f"pool_fwd_{i}", xcur, pool_f, pool_vec)
            st.update(pooled=pooled)
        else:
            h = _pro_fwd(f"pro1_fwd_{i}", xcur, pro_vec)
            p = _mm_nn(f"cf_pw1_{i}", h, pw1_f, j, True, BF)
            taps = jnp.pad(cf_w_dw_f[j], ((0, 1), (0, 0)))
            cf_vec = _vecs([cf_b_pw1_f[j, :D], cf_b_pw1_f[j, D:], cf_b_dw_f[j], cf_ln_g_f[j], cf_ln_b_f[j]], D)
            wact, cv = _cf_act_fwd(f"cf_act_fwd_{i}", p, taps, cf_vec)
            m = _mm_nn(f"cf_pw2_{i}", wact, pw2_f, j, False, F32)
            m = _add_bias(f"cf_bias_{i}", m, cf_b_pw2_f[j])
            st.update(h=h, p=p, wact=wact, cv=cv, taps=taps, cf_vec=cf_vec)
        x1 = _epi_fwd(f"epi1_fwd_{i}", xcur, m, _vecs([g1, norm_g_f[i, 1]], D))
        st.update(m=m, x1=x1)
        h2 = _pro_fwd(f"pro2_fwd_{i}", x1, _vecs([norm_g_f[i, 2], sc2, sh2], D))
        up = _mm_nn(f"ffn_up_{i}", h2, w_up_f, i, True, BF)
        ffn_vec = _vecs(list(ffn_conv_f[i]) + [ffn_b_conv[i]], F)
        a = _ffn_act_fwd(f"ffn_act_fwd_{i}", up, ffn_vec)
        f = _mm_nn(f"ffn_down_{i}", a, w_down_f, i, False, F32, tm=512)
        xcur = _epi_fwd(f"epi2_fwd_{i}", x1, f, _vecs([g2, norm_g_f[i, 3]], D))
        st.update(h2=h2, up=up, f=f, ffn_vec=ffn_vec)
        saved.append(st)

    dy, loss_cols = _loss_fwd_bwd(xcur, target)
    loss = lax.psum(jnp.sum(loss_cols[0]), ("x", "y", "c"))

    dmod = [None] * DEPTH
    d_norm_g = [None] * DEPTH
    d_sc_conv = [None, None]
    d_ffn_conv, d_ffn_b = [None] * DEPTH, [None] * DEPTH
    g_w_in, g_w_out, g_w_up, g_w_down = [None, None], [None, None], [None] * DEPTH, [None] * DEPTH
    dxo = dy
    for i in reversed(range(DEPTH)):
        kind, j = i % 3, i // 3
        st = saved[i]
        sh1, sc1, g1, sh2, sc2, g2 = [mod[i, k] for k in range(6)]
        df, e2 = _epi_bwd(f"epi2_bwd_{i}", dxo, st["f"], _vecs([g2, norm_g_f[i, 3]], D))
        da = _mm_nt(f"ffn_down_dx_{i}", df, w_down_f, i, False, BF, tn=F // 2)
        a, dup, fsum = _ffn_act_bwd(f"ffn_act_bwd_{i}", st["up"], da, st["ffn_vec"])
        g_w_down[i] = _mm_tn(f"ffn_down_dw_{i}", a, df, False, tk=F // 2)
        dh2 = _mm_nt(f"ffn_up_dx_{i}", dup, w_up_f, i, True, F32)
        g_w_up[i] = _mm_tn(f"ffn_up_dw_{i}", st["h2"], dup, True)
        dx1, p2 = _pro_bwd(f"pro2_bwd_{i}", dh2, st["x1"], dxo, _vecs([norm_g_f[i, 2], sc2], D))
        dm, e1 = _epi_bwd(f"epi1_bwd_{i}", dx1, st["m"], _vecs([g1, norm_g_f[i, 1]], D))
        if kind == 0:
            dz = _mm_nt(f"sc_out_dx_{i}", dm, w_out_f, j, False, BF)
            z, dp, ssum = _sc_act_bwd(f"sc_act_bwd_{i}", st["p"], dz, _vecs(list(sc_conv_f[j]), D))
            g_w_out[j] = _mm_tn(f"sc_out_dw_{i}", z, dm, False)
            dh = _mm_nt(f"sc_in_dx_{i}", dp, w_in_f, j, True, F32)
            g_w_in[j] = _mm_tn(f"sc_in_dw_{i}", st["h"], dp, True)
            d_sc_conv[j] = ssum[0:3]
        elif kind == 1:
            dh, g_pool, psum = _pool_bwd(f"pool_bwd_{i}", st["pooled"], dm, pool_f, _vecs([pool_b[j], pool_scale[j]], D))
        else:
            dwact = _mm_nt(f"cf_pw2_dx_{i}", dm, pw2_f, j, False, BF)
            g_pw2 = _mm_tn(f"cf_pw2_dw_{i}", st["wact"], dm, False)
            dp, tsum, csum = _cf_act_bwd(f"cf_act_bwd_{i}", st["p"], st["cv"], dwact, st["taps"], st["cf_vec"])
            dh = _mm_nt(f"cf_pw1_dx_{i}", dp, pw1_f, j, True, F32)
            g_pw1 = _mm_tn(f"cf_pw1_dw_{i}", st["h"], dp, True)
            d_cf = dict(b_pw1=jnp.concatenate([csum[3], csum[4]])[None], w_dw=tsum[None, :CF_TAPS], b_dw=csum[0:1],
                        ln_g=csum[1:2], ln_b=csum[2:3], b_pw2=e1[2:3])
        dxo, p1 = _pro_bwd(f"pro1_bwd_{i}", dh, st["x0"], dx1, _vecs([norm_g_f[i, 0], sc1], D))
        dmod[i] = jnp.concatenate([p1[0], p1[1], e1[0], p2[0], p2[1], e2[0]])
        d_norm_g[i] = jnp.stack([p1[2], e1[1], p2[2], e2[1]])
        if kind == 1:
            d_pool_b, d_pool_scale = psum[0:1], psum[1:2]
        saved[i] = None
        st = None
        if i == 0:
            grad_x = dxo[None]
        d_ffn_conv[i], d_ffn_b[i] = fsum[1:4], fsum[0]

    small_shapes = [(DEPTH, 6 * D), (DEPTH, 4, D), (2, 3, D), (1, D), (1, D), (1, 2 * D), (1, CF_TAPS, D), (1, D),
                    (1, D), (1, D), (1, D), (DEPTH, 3, F), (DEPTH, F)]
    small = [jnp.stack(dmod), jnp.stack(d_norm_g), jnp.stack(d_sc_conv), d_pool_b, d_pool_scale, d_cf["b_pw1"],
             d_cf["w_dw"], d_cf["b_dw"], d_cf["ln_g"], d_cf["ln_b"], d_cf["b_pw2"], jnp.stack(d_ffn_conv),
             jnp.stack(d_ffn_b)]
    gsl, ssl = _allgather_small("reduce_small_grads", _pack(small), True)
    tot = _unpack(ssl, small_shapes)
    (gb_mod, gt_norm_g, gt_sc_conv, g_pool_b, g_pool_scale, gt_b_pw1, gt_w_dw, gt_b_dw, gt_ln_g, gt_ln_b, gt_b_pw2,
     gt_ffn_conv, g_ffn_b) = tot
    dmod_all = jnp.stack([_unpack(gsl.reshape(N_DEV, -1, 128)[d], small_shapes[:1])[0] for d in range(N_DEV)], axis=1)
    dmod_sh = jnp.pad(_shard_last(dmod_all, chip, n_mod), ((0, 0), (0, 8), (0, 0)))
    g_w_mod = _mod_bwd(c16, dmod_sh)
    g_norm_g = _shard_last(gt_norm_g, chip, D // 4)
    g_sc_conv = _shard_last(gt_sc_conv, chip, D // 4)
    g_b_pw1 = _shard_last(gt_b_pw1, chip, 2 * D // 4)
    g_w_dw = _shard_last(gt_w_dw, chip, D // 4)
    g_b_dw = _shard_last(gt_b_dw, chip, D // 4)
    g_ln_g = _shard_last(gt_ln_g, chip, D // 4)
    g_ln_b = _shard_last(gt_ln_b, chip, D // 4)
    g_b_pw2 = _shard_last(gt_b_pw2, chip, D // 4)
    g_ffn_conv = _shard_last(gt_ffn_conv, chip, F // 4)

    def halves(gl):
        g = jnp.stack(gl) if isinstance(gl, list) else gl
        if g.ndim == 3:
            g = g.reshape(g.shape[0], N_CHIPS, g.shape[1] // N_CHIPS, g.shape[2])
        nl, _, k, n = g.shape
        return g.reshape(nl, N_CHIPS, 2, k // 2, n)

    partial = [halves(g_w_in), halves(g_w_out), halves(g_pool.astype(BF)), halves([g_pw1]), halves([g_pw2]),
               halves(g_w_up), halves(g_w_down)]
    own, sib = _pair_exchange(partial)
    chip_sum = []
    for q in range(len(own)):
        shp = own[q].shape
        s2 = _add2(f"grad_pair_sum_{q}", own[q].reshape(-1, shp[-1]), sib[q].reshape(-1, shp[-1]), BF)
        chip_sum.append(s2.reshape(shp))
    pieces = _chip_exchange(chip_sum)
    reduced = [_add_slots(f"grad_chip_sum_{q}", pieces[q]) for q in range(len(pieces))]
    shared = _pair_share(reduced)
    big_w = [sc_w_in, sc_w_out, pool_w, cf_w_pw1, cf_w_pw2, ffn_w_up, ffn_w_down]
    big_g = [shared[q].reshape(big_w[q].shape) for q in range(len(big_w))]
    g_sc_w_in, g_sc_w_out, g_pool_w, g_cf_w_pw1, g_cf_w_pw2, g_ffn_w_up, g_ffn_w_down = big_g

    def adam_big(name, w, g, m, v):
        shp = w.shape
        two = lambda t: t.reshape(-1, shp[-1])
        return [o.reshape(shp) for o in _adamw(name, two(w), two(g), two(m), two(v))]

    grads = dict(w_mod=g_w_mod, b_mod=gb_mod, norm_g=g_norm_g, sc_w_in=g_sc_w_in, sc_conv=g_sc_conv,
                 sc_w_out=g_sc_w_out, pool_w=g_pool_w, pool_b=g_pool_b, pool_scale=g_pool_scale,
                 cf_w_pw1=g_cf_w_pw1, cf_b_pw1=g_b_pw1, cf_w_dw=g_w_dw, cf_b_dw=g_b_dw, cf_ln_g=g_ln_g,
                 cf_ln_b=g_ln_b, cf_w_pw2=g_cf_w_pw2, cf_b_pw2=g_b_pw2, ffn_w_up=g_ffn_w_up, ffn_conv=g_ffn_conv,
                 ffn_b_conv=g_ffn_b, ffn_w_down=g_ffn_w_down)
    weights = dict(w_mod=w_mod, b_mod=b_mod, norm_g=norm_g, sc_w_in=sc_w_in, sc_conv=sc_conv, sc_w_out=sc_w_out,
                   pool_w=pool_w, pool_b=pool_b, pool_scale=pool_scale, cf_w_pw1=cf_w_pw1, cf_b_pw1=cf_b_pw1,
                   cf_w_dw=cf_w_dw, cf_b_dw=cf_b_dw, cf_ln_g=cf_ln_g, cf_ln_b=cf_ln_b, cf_w_pw2=cf_w_pw2,
                   cf_b_pw2=cf_b_pw2, ffn_w_up=ffn_w_up, ffn_conv=ffn_conv, ffn_b_conv=ffn_b_conv,
                   ffn_w_down=ffn_w_down)
    m_in = dict(w_mod=m_w_mod, b_mod=m_b_mod, norm_g=m_norm_g, sc_w_in=m_sc_w_in, sc_conv=m_sc_conv,
                sc_w_out=m_sc_w_out, pool_w=m_pool_w, pool_b=m_pool_b, pool_scale=m_pool_scale,
                cf_w_pw1=m_cf_w_pw1, cf_b_pw1=m_cf_b_pw1, cf_w_dw=m_cf_w_dw, cf_b_dw=m_cf_b_dw, cf_ln_g=m_cf_ln_g,
                cf_ln_b=m_cf_ln_b, cf_w_pw2=m_cf_w_pw2, cf_b_pw2=m_cf_b_pw2, ffn_w_up=m_ffn_w_up,
                ffn_conv=m_ffn_conv, ffn_b_conv=m_ffn_b_conv, ffn_w_down=m_ffn_w_down)
    v_in = dict(w_mod=v_w_mod, b_mod=v_b_mod, norm_g=v_norm_g, sc_w_in=v_sc_w_in, sc_conv=v_sc_conv,
                sc_w_out=v_sc_w_out, pool_w=v_pool_w, pool_b=v_pool_b, pool_scale=v_pool_scale,
                cf_w_pw1=v_cf_w_pw1, cf_b_pw1=v_cf_b_pw1, cf_w_dw=v_cf_w_dw, cf_b_dw=v_cf_b_dw, cf_ln_g=v_cf_ln_g,
                cf_ln_b=v_cf_ln_b, cf_w_pw2=v_cf_w_pw2, cf_b_pw2=v_cf_b_pw2, ffn_w_up=v_ffn_w_up,
                ffn_conv=v_ffn_conv, ffn_b_conv=v_ffn_b_conv, ffn_w_down=v_ffn_w_down)
    names = list(weights)
    big_names = ["w_mod", "sc_w_in", "sc_w_out", "pool_w", "cf_w_pw1", "cf_w_pw2", "ffn_w_up", "ffn_w_down"]
    small_names = [n for n in names if n not in big_names]
    delta, new_m, new_v = {}, {}, {}
    for n in big_names:
        delta[n], new_m[n], new_v[n] = adam_big(f"adamw_{n}", weights[n], grads[n], m_in[n], v_in[n])
    grads = {n: grads[n].reshape(weights[n].shape) for n in names}
    sm_shapes = [weights[n].shape for n in small_names]
    sd, sm, sv = _adamw("adamw_small", _pack([weights[n] for n in small_names]), _pack([grads[n] for n in small_names]),
                        _pack([m_in[n] for n in small_names]), _pack([v_in[n] for n in small_names]))
    for n, d_, m_, v_ in zip(small_names, _unpack(sd, sm_shapes), _unpack(sm, sm_shapes), _unpack(sv, sm_shapes)):
        delta[n], new_m[n], new_v[n] = d_, m_, v_

    return (loss, grad_x, *[grads[n] for n in names], *[delta[n] for n in names], *[new_m[n] for n in names],
            *[new_v[n] for n in names])


def _add_bias(name, m, b):
    s = m.shape[0]
    tm = _tile(s, ROW_TILE_D)

    def body(m_ref, b_ref, o_ref):
        o_ref[...] = m_ref[...] + b_ref[0:1, :]

    return pl.pallas_call(body, name=name, out_shape=SDS((s, D), F32), grid=(s // tm,),
                          in_specs=[_rows(tm, D), _const((8, D))], out_specs=_rows(tm, D),
                          compiler_params=_cp(1))(m, _vecs([b], D))
```

```python
import itertools

import jax
import jax.numpy as jnp
from jax import lax
from jax.experimental import pallas as pl
from jax.experimental.pallas import tpu as pltpu

D = 1024
F = 2816
DEPTH = 4
POOL_WINDOWS = (2, 4, 8, 16)
POOL_GROUP = 256
CF_TAPS = 31
RMS_EPS = 1e-6
LN_EPS = 1e-5
ADAM_LR = 0.001
ADAM_B1 = 0.9
ADAM_B2 = 0.999
ADAM_EPS = 1e-08
ADAM_WD = 0.01
ADAM_STEP = 10

BF = jnp.bfloat16
F32 = jnp.float32
MESH = pl.DeviceIdType.MESH
SDS = jax.ShapeDtypeStruct
N_CHIPS = 4
N_DEV = 8
VMEM_LIMIT_MB = 56
ROW_TILE_D = 256
ROW_TILE_F = 256
HALO = 8
HALO_BF = 16
CF_HALO = 32
POOL_HALO = 16


def _cp(n_axes):
    return pltpu.CompilerParams(dimension_semantics=("arbitrary",) * n_axes,
                                vmem_limit_bytes=VMEM_LIMIT_MB * 1024 * 1024)


def _tile(n, pref):
    t = min(n, pref)
    assert n % t == 0, (n, pref)
    return t


def _const(shape):
    nd = len(shape)
    return pl.BlockSpec(shape, lambda *_: (0,) * nd)


def _vecs(rows, width):
    v = jnp.stack([r.reshape(width).astype(F32) for r in rows])
    pad = (-v.shape[0]) % 8
    return jnp.pad(v, ((0, pad), (0, 0)))


def _sigmoid(v):
    return 0.5 * jnp.tanh(0.5 * v) + 0.5


def _down(prev8, g, k):
    n = g.shape[0]
    return pltpu.roll(jnp.concatenate([prev8, g], axis=0), k, 0)[8:8 + n]


def _up(g, next8, k):
    n = g.shape[0]
    return pltpu.roll(jnp.concatenate([g, next8], axis=0), n + 8 - k, 0)[0:n]


def _fold8(v):
    acc = v[0:8]
    for r in range(8, v.shape[0], 8):
        acc = acc + v[r:r + 8]
    return acc


def _mm(name, a, b, out_sds, grid, a_spec, b_spec, o_spec, acc_shape, dn):
    nk = grid[2]

    def body(a_ref, b_ref, o_ref, *acc):
        part = lax.dot_general(a_ref[...], b_ref[...], dn, preferred_element_type=F32)
        if nk == 1:
            o_ref[...] = part.astype(o_ref.dtype)
        else:
            acc_ref = acc[0]
            k = pl.program_id(2)

            @pl.when(k == 0)
            def _():
                acc_ref[...] = part

            @pl.when(k > 0)
            def _():
                acc_ref[...] += part

            @pl.when(k == nk - 1)
            def _():
                o_ref[...] = acc_ref[...].astype(o_ref.dtype)

    scratch = [] if nk == 1 else [pltpu.VMEM(acc_shape, F32)]
    return pl.pallas_call(body, name=name, out_shape=out_sds, grid=grid, in_specs=[a_spec, b_spec],
                          out_specs=o_spec, scratch_shapes=scratch, compiler_params=_cp(3))(a, b)


NN = (((1,), (0,)), ((), ()))
NT = (((1,), (1,)), ((), ()))
TN = (((0,), (0,)), ((), ()))


def _mm_nn(name, a, w, layer, col_sharded, out_dtype, tm=1024, tn=None):
    m, k = a.shape
    tm = _tile(m, tm)
    if col_sharded:
        n4 = w.shape[3]
        tn = n4 if tn is None else tn
        tpc = n4 // tn
        n = N_CHIPS * n4
        b_spec = pl.BlockSpec((None, None, k, tn), lambda i, j, kk: (layer, j // tpc, 0, j % tpc))
    else:
        n = w.shape[2]
        tn = n if tn is None else tn
        b_spec = pl.BlockSpec((None, k, tn), lambda i, j, kk: (layer, 0, j))
    return _mm(name, a, w, SDS((m, n), out_dtype), (m // tm, n // tn, 1),
               pl.BlockSpec((tm, k), lambda i, j, kk: (i, 0)), b_spec,
               pl.BlockSpec((tm, tn), lambda i, j, kk: (i, j)), None, NN)


def _mm_nt(name, g, w, layer, col_sharded, out_dtype, tm=1024, tn=None):
    m, n = g.shape
    tm = _tile(m, tm)
    if col_sharded:
        kdim, n4 = w.shape[2], w.shape[3]
        tn = kdim if tn is None else tn
        tk = n4
        b_spec = pl.BlockSpec((None, None, tn, tk), lambda i, j, kk: (layer, kk, j, 0))
    else:
        kdim = w.shape[1]
        tn = kdim if tn is None else tn
        tk = n
        b_spec = pl.BlockSpec((None, tn, tk), lambda i, j, kk: (layer, j, kk))
    return _mm(name, g, w, SDS((m, kdim), out_dtype), (m // tm, kdim // tn, n // tk),
               pl.BlockSpec((tm, tk), lambda i, j, kk: (i, kk)), b_spec,
               pl.BlockSpec((tm, tn), lambda i, j, kk: (i, j)), (tm, tn), NT)


def _mm_tn(name, a, g, col_sharded, tk=None, tn=None, ts=512):
    s, k = a.shape
    n = g.shape[1]
    ts = _tile(s, ts)
    tk = k if tk is None else tk
    if col_sharded:
        n4 = n // N_CHIPS
        tn = n4 if tn is None else tn
        tpc = n4 // tn
        out_sds = SDS((N_CHIPS, k, n4), BF)
        o_spec = pl.BlockSpec((None, tk, tn), lambda i, j, ss: (j // tpc, i, j % tpc))
    else:
        tn = n if tn is None else tn
        out_sds = SDS((k, n), BF)
        o_spec = pl.BlockSpec((tk, tn), lambda i, j, ss: (i, j))
    return _mm(name, a, g, out_sds, (k // tk, n // tn, s // ts),
               pl.BlockSpec((ts, tk), lambda i, j, ss: (ss, i)),
               pl.BlockSpec((ts, tn), lambda i, j, ss: (ss, j)), o_spec, (tk, tn), TN)


def _rows(tm, width, col=0):
    return pl.BlockSpec((tm, width), lambda i: (i, col))


def _pro_fwd(name, x, vec):
    s = x.shape[0]
    tm = _tile(s, ROW_TILE_D)

    def body(x_ref, v_ref, h_ref):
        xv = x_ref[...]
        r = lax.rsqrt(jnp.mean(xv * xv, axis=-1, keepdims=True) + RMS_EPS)
        a = v_ref[0:1, :] * (1.0 + v_ref[1:2, :])
        h_ref[...] = (xv * r * a + v_ref[2:3, :]).astype(h_ref.dtype)

    return pl.pallas_call(body, name=name, out_shape=SDS((s, D), BF), grid=(s // tm,),
                          in_specs=[_rows(tm, D), _const((8, D))], out_specs=_rows(tm, D),
                          compiler_params=_cp(1))(x, vec)


def _epi_fwd(name, x, m, vec):
    s = x.shape[0]
    tm = _tile(s, ROW_TILE_D)

    def body(x_ref, m_ref, v_ref, o_ref):
        mv = m_ref[...]
        rm = lax.rsqrt(jnp.mean(mv * mv, axis=-1, keepdims=True) + RMS_EPS)
        o_ref[...] = x_ref[...] + v_ref[0:1, :] * (mv * rm * v_ref[1:2, :])

    return pl.pallas_call(body, name=name, out_shape=SDS((s, D), F32), grid=(s // tm,),
                          in_specs=[_rows(tm, D), _rows(tm, D), _const((8, D))], out_specs=_rows(tm, D),
                          compiler_params=_cp(1))(x, m, vec)


def _loss_fwd_bwd(y, t):
    s = y.shape[0]
    tm = _tile(s, ROW_TILE_D)

    def body(y_ref, t_ref, dy_ref, acc_ref):
        @pl.when(pl.program_id(0) == 0)
        def _():
            acc_ref[...] = jnp.zeros_like(acc_ref)

        e = y_ref[...] - t_ref[...]
        dy_ref[...] = e * (1.0 / D)
        acc_ref[0:1, :] += jnp.sum(e * e, axis=0, keepdims=True) * (0.5 / D)

    return pl.pallas_call(body, name="loss", out_shape=(SDS((s, D), F32), SDS((8, D), F32)), grid=(s // tm,),
                          in_specs=[_rows(tm, D), _rows(tm, D)], out_specs=(_rows(tm, D), _const((8, D))),
                          compiler_params=_cp(1))(y, t)


def _epi_bwd(name, dxo, m, vec):
    s = dxo.shape[0]
    tm = _tile(s, ROW_TILE_D)

    def body(d_ref, m_ref, v_ref, dm_ref, acc_ref):
        @pl.when(pl.program_id(0) == 0)
        def _():
            acc_ref[...] = jnp.zeros_like(acc_ref)

        mv = m_ref[...]
        dv = d_ref[...]
        g = v_ref[0:1, :]
        ngb = v_ref[1:2, :]
        rm = lax.rsqrt(jnp.mean(mv * mv, axis=-1, keepdims=True) + RMS_EPS)
        mn = mv * rm
        dmn = dv * (g * ngb)
        dm = rm * (dmn - mn * jnp.mean(dmn * mn, axis=-1, keepdims=True))
        dm_ref[...] = dm.astype(dm_ref.dtype)
        t = dv * mn
        acc_ref[0:1, :] += jnp.sum(t, axis=0, keepdims=True) * ngb
        acc_ref[1:2, :] += jnp.sum(t, axis=0, keepdims=True) * g
        acc_ref[2:3, :] += jnp.sum(dm, axis=0, keepdims=True)

    return pl.pallas_call(body, name=name, out_shape=(SDS((s, D), BF), SDS((8, D), F32)), grid=(s // tm,),
                          in_specs=[_rows(tm, D), _rows(tm, D), _const((8, D))],
                          out_specs=(_rows(tm, D), _const((8, D))), compiler_params=_cp(1))(dxo, m, vec)


def _pro_bwd(name, dh, x, dxo, vec):
    s = x.shape[0]
    tm = _tile(s, ROW_TILE_D)

    def body(dh_ref, x_ref, d_ref, v_ref, dx_ref, acc_ref):
        @pl.when(pl.program_id(0) == 0)
        def _():
            acc_ref[...] = jnp.zeros_like(acc_ref)

        xv = x_ref[...]
        dhv = dh_ref[...].astype(F32)
        ng = v_ref[0:1, :]
        sc1 = 1.0 + v_ref[1:2, :]
        r = lax.rsqrt(jnp.mean(xv * xv, axis=-1, keepdims=True) + RMS_EPS)
        xn = xv * r
        dxn = dhv * (ng * sc1)
        dx_ref[...] = d_ref[...] + r * (dxn - xn * jnp.mean(dxn * xn, axis=-1, keepdims=True))
        t = jnp.sum(dhv * xn, axis=0, keepdims=True)
        acc_ref[0:1, :] += jnp.sum(dhv, axis=0, keepdims=True)
        acc_ref[1:2, :] += t * ng
        acc_ref[2:3, :] += t * sc1

    return pl.pallas_call(body, name=name, out_shape=(SDS((s, D), F32), SDS((8, D), F32)), grid=(s // tm,),
                          in_specs=[_rows(tm, D), _rows(tm, D), _rows(tm, D), _const((8, D))],
                          out_specs=(_rows(tm, D), _const((8, D))), compiler_params=_cp(1))(dh, x, dxo, vec)


def _carry_down(buf, tm, first):
    @pl.when(first)
    def _():
        buf[0:HALO, :] = jnp.zeros((HALO, buf.shape[1]), F32)

    @pl.when(jnp.logical_not(first))
    def _():
        buf[0:HALO, :] = buf[tm:tm + HALO, :]


def _carry_up(buf, tm, first, halo=HALO):
    @pl.when(first)
    def _():
        buf[tm:tm + halo, :] = jnp.zeros((halo, buf.shape[1]), F32)

    @pl.when(jnp.logical_not(first))
    def _():
        buf[tm:tm + halo, :] = buf[0:halo, :]


def _ffn_act_fwd(name, up, vec):
    s = up.shape[0]
    tm = _tile(s, ROW_TILE_F)
    rb_n = _tile(tm, 64)
    cw = 256

    def body(g_ref, v_ref, w_ref, a_ref, carry):
        @pl.when(pl.program_id(0) == 0)
        def _():
            carry[...] = jnp.zeros_like(carry)

        for cc in range(F // cw):
            cols = slice(cc * cw, (cc + 1) * cw)
            w0, w1, w2, b = w_ref[0:1, cols], w_ref[1:2, cols], w_ref[2:3, cols], w_ref[3:4, cols]

            def rb_body(rb, prev8):
                rows = pl.ds(pl.multiple_of(rb * rb_n, rb_n), rb_n)
                g = g_ref[rows, cols].astype(F32)
                gc = w2 * g + w1 * _down(prev8, g, 1) + w0 * _down(prev8, g, 2) + b
                a_ref[rows, cols] = (gc * _sigmoid(gc) * v_ref[rows, cols].astype(F32)).astype(a_ref.dtype)
                return g[rb_n - 8:rb_n]

            carry[:, cols] = lax.fori_loop(0, tm // rb_n, rb_body, carry[:, cols])

    return pl.pallas_call(body, name=name, out_shape=SDS((s, F), BF), grid=(s // tm,),
                          in_specs=[_rows(tm, F, 0), _rows(tm, F, 1), _const((8, F))], out_specs=_rows(tm, F),
                          scratch_shapes=[pltpu.VMEM((HALO, F), F32)], compiler_params=_cp(1))(up, up, vec)


def _prev_halo_spec(nt, tm, width, col):
    per = tm // HALO_BF
    return pl.BlockSpec((HALO_BF, width), lambda i: (jnp.maximum((nt - 1 - i) * per - 1, 0), col))


def _ffn_act_bwd(name, up, da, vec):
    s = up.shape[0]
    tm = _tile(s, ROW_TILE_F)
    nt = s // tm
    rev = lambda col: pl.BlockSpec((tm, F), lambda i: (nt - 1 - i, col))

    rb_n = _tile(tm, 64)
    nrb = tm // rb_n
    cw = 128

    def body(g_ref, gp_ref, v_ref, da_ref, w_ref, a_ref, dup_ref, acc_ref, carry):
        step = pl.program_id(0)

        @pl.when(step == 0)
        def _():
            acc_ref[...] = jnp.zeros_like(acc_ref)
            carry[...] = jnp.zeros_like(carry)

        for cc in range(F // cw):
            cols = slice(cc * cw, (cc + 1) * cw)
            w0, w1, w2, b = w_ref[0:1, cols], w_ref[1:2, cols], w_ref[2:3, cols], w_ref[3:4, cols]
            halo = jnp.where(step < nt - 1, gp_ref[:, cols].astype(F32)[HALO_BF - 8:HALO_BF], 0.0)

            def rb_body(it, st):
                nxt8, ab, a0, a1, a2 = st
                rb = nrb - 1 - it
                r0 = pl.multiple_of(rb * rb_n, rb_n)
                rows = pl.ds(r0, rb_n)
                g = g_ref[rows, cols].astype(F32)
                ra = pl.multiple_of(jnp.maximum(r0 - HALO_BF, 0), HALO_BF)
                above = g_ref[pl.ds(ra, HALO_BF), cols].astype(F32)[HALO_BF - 8:HALO_BF]
                prev8 = jnp.where(rb == 0, halo, above)
                g1 = _down(prev8, g, 1)
                g2 = _down(prev8, g, 2)
                gc = w2 * g + w1 * g1 + w0 * g2 + b
                sg = _sigmoid(gc)
                sl = gc * sg
                val = v_ref[rows, cols].astype(F32)
                dav = da_ref[rows, cols].astype(F32)
                a_ref[rows, cols] = (sl * val).astype(a_ref.dtype)
                dup_ref[rows, F + cc * cw:F + (cc + 1) * cw] = (dav * sl).astype(dup_ref.dtype)
                dgc = (dav * val) * (sg + sl * (1.0 - sg))
                dup_ref[rows, cols] = (w2 * dgc + w1 * _up(dgc, nxt8, 1) + w0 * _up(dgc, nxt8, 2)).astype(dup_ref.dtype)
                return (dgc[0:8], ab + _fold8(dgc), a0 + _fold8(g2 * dgc), a1 + _fold8(g1 * dgc),
                        a2 + _fold8(g * dgc))

            z = jnp.zeros((8, cw), F32)
            nxt8, ab, a0, a1, a2 = lax.fori_loop(0, nrb, rb_body, (carry[:, cols], z, z, z, z))
            carry[:, cols] = nxt8
            acc_ref[0:1, cols] += jnp.sum(ab, axis=0, keepdims=True)
            acc_ref[1:2, cols] += jnp.sum(a0, axis=0, keepdims=True)
            acc_ref[2:3, cols] += jnp.sum(a1, axis=0, keepdims=True)
            acc_ref[3:4, cols] += jnp.sum(a2, axis=0, keepdims=True)

    return pl.pallas_call(
        body, name=name, out_shape=(SDS((s, F), BF), SDS((s, 2 * F), BF), SDS((8, F), F32)), grid=(nt,),
        in_specs=[rev(0), _prev_halo_spec(nt, tm, F, 0), rev(1), rev(0), _const((8, F))],
        out_specs=(rev(0), pl.BlockSpec((tm, 2 * F), lambda i: (nt - 1 - i, 0)), _const((8, F))),
        scratch_shapes=[pltpu.VMEM((HALO, F), F32)], compiler_params=_cp(1))(up, up, up, da, vec)


def _sc_act_fwd(name, p, vec):
    s = p.shape[0]
    tm = _tile(s, ROW_TILE_D)

    def body(b_ref, c_ref, h_ref, w_ref, z_ref, buf):
        _carry_down(buf, tm, pl.program_id(0) == 0)
        buf[HALO:HALO + tm, :] = c_ref[...].astype(F32) * h_ref[...].astype(F32)
        u = (w_ref[2:3, :] * buf[HALO:HALO + tm, :] + w_ref[1:2, :] * buf[HALO - 1:HALO - 1 + tm, :]
             + w_ref[0:1, :] * buf[HALO - 2:HALO - 2 + tm, :])
        z_ref[...] = (b_ref[...].astype(F32) * u).astype(z_ref.dtype)

    return pl.pallas_call(body, name=name, out_shape=SDS((s, D), BF), grid=(s // tm,),
                          in_specs=[_rows(tm, D, 0), _rows(tm, D, 1), _rows(tm, D, 2), _const((8, D))],
                          out_specs=_rows(tm, D), scratch_shapes=[pltpu.VMEM((tm + HALO, D), F32)],
                          compiler_params=_cp(1))(p, p, p, vec)


def _sc_act_bwd(name, p, dz, vec):
    s = p.shape[0]
    tm = _tile(s, ROW_TILE_D)
    nt = s // tm
    rev = lambda col: pl.BlockSpec((tm, D), lambda i: (nt - 1 - i, col))

    def body(b_ref, c_ref, h_ref, cp_ref, hp_ref, dz_ref, w_ref, z_ref, dp_ref, acc_ref, pbuf, nbuf):
        step = pl.program_id(0)

        @pl.when(step == 0)
        def _():
            acc_ref[...] = jnp.zeros_like(acc_ref)

        cg = c_ref[...].astype(F32)
        hin = h_ref[...].astype(F32)
        bg = b_ref[...].astype(F32)
        q = cg * hin
        pbuf[0:HALO_BF, :] = jnp.where(step < nt - 1, cp_ref[...].astype(F32) * hp_ref[...].astype(F32), 0.0)
        pbuf[HALO_BF:HALO_BF + tm, :] = q
        u = (w_ref[2:3, :] * q + w_ref[1:2, :] * pbuf[HALO_BF - 1:HALO_BF - 1 + tm, :]
             + w_ref[0:1, :] * pbuf[HALO_BF - 2:HALO_BF - 2 + tm, :])
        dzv = dz_ref[...].astype(F32)
        z_ref[...] = (bg * u).astype(z_ref.dtype)
        dp_ref[:, 0:D] = (dzv * u).astype(dp_ref.dtype)
        du = dzv * bg
        _carry_up(nbuf, tm, step == 0)
        nbuf[0:tm, :] = du
        d1 = nbuf[1:1 + tm, :]
        d2 = nbuf[2:2 + tm, :]
        dq = w_ref[2:3, :] * du + w_ref[1:2, :] * d1 + w_ref[0:1, :] * d2
        dp_ref[:, D:2 * D] = (dq * hin).astype(dp_ref.dtype)
        dp_ref[:, 2 * D:3 * D] = (dq * cg).astype(dp_ref.dtype)
        acc_ref[0:1, :] += jnp.sum(q * d2, axis=0, keepdims=True)
        acc_ref[1:2, :] += jnp.sum(q * d1, axis=0, keepdims=True)
        acc_ref[2:3, :] += jnp.sum(q * du, axis=0, keepdims=True)

    return pl.pallas_call(
        body, name=name, out_shape=(SDS((s, D), BF), SDS((s, 3 * D), BF), SDS((8, D), F32)), grid=(nt,),
        in_specs=[rev(0), rev(1), rev(2), _prev_halo_spec(nt, tm, D, 1), _prev_halo_spec(nt, tm, D, 2), rev(0),
                  _const((8, D))],
        out_specs=(rev(0), pl.BlockSpec((tm, 3 * D), lambda i: (nt - 1 - i, 0)), _const((8, D))),
        scratch_shapes=[pltpu.VMEM((tm + HALO_BF, D), F32), pltpu.VMEM((tm + HALO, D), F32)],
        compiler_params=_cp(1))(p, p, p, p, p, dz, vec)


CF_ROW_BLOCK = 32
CF_LANES = 128


def _tap_conv_block(ext, rb_n, tap_of_offset, t_ref, cols, init, u=None, accs=None):
    n = ext.shape[0]
    out = init
    for b in range(8):
        rolled = ext if b == 0 else pltpu.roll(ext, n - b, 0)
        for a in range(n // 8):
            k = tap_of_offset(8 * a + b)
            if k is None:
                continue
            sl = rolled[8 * a:8 * a + rb_n]
            out = out + t_ref[k:k + 1, cols] * sl
            if accs is not None:
                accs[k] = accs[k] + _fold8(u * sl)
    return out


def _cf_act_fwd(name, p, taps, vec):
    s = p.shape[0]
    tm = _tile(s, ROW_TILE_D)
    rb_n = _tile(tm, CF_ROW_BLOCK)
    base = CF_HALO - (CF_TAPS - 1)
    tap_of = lambda off: off - base if 0 <= off - base < CF_TAPS else None

    def body(a_ref, g_ref, t_ref, v_ref, w_ref, cv_ref, buf):
        first = pl.program_id(0) == 0

        @pl.when(first)
        def _():
            buf[0:CF_HALO, :] = jnp.zeros((CF_HALO, D), F32)

        @pl.when(jnp.logical_not(first))
        def _():
            buf[0:CF_HALO, :] = buf[tm:tm + CF_HALO, :]

        a = a_ref[...].astype(F32) + v_ref[0:1, :]
        g = g_ref[...].astype(F32) + v_ref[1:2, :]
        buf[CF_HALO:CF_HALO + tm, :] = a * _sigmoid(g)
        for ci in range(D // CF_LANES):
            cols = slice(ci * CF_LANES, (ci + 1) * CF_LANES)

            def rb_body(rb, carry):
                r0 = pl.multiple_of(rb * rb_n, rb_n)
                ext = buf[pl.ds(r0, rb_n + CF_HALO), cols]
                init = jnp.zeros((rb_n, CF_LANES), F32) + v_ref[2:3, cols]
                cv_ref[pl.ds(r0, rb_n), cols] = _tap_conv_block(ext, rb_n, tap_of, t_ref, cols, init)
                return carry

            lax.fori_loop(0, tm // rb_n, rb_body, 0)
        cv = cv_ref[...]
        mu = jnp.mean(cv, axis=-1, keepdims=True)
        cc = cv - mu
        rstd = lax.rsqrt(jnp.mean(cc * cc, axis=-1, keepdims=True) + LN_EPS)
        ln = cc * rstd * v_ref[3:4, :] + v_ref[4:5, :]
        w_ref[...] = (ln * _sigmoid(ln)).astype(w_ref.dtype)

    return pl.pallas_call(body, name=name, out_shape=(SDS((s, D), BF), SDS((s, D), F32)), grid=(s // tm,),
                          in_specs=[_rows(tm, D, 0), _rows(tm, D, 1), _const((32, D)), _const((8, D))],
                          out_specs=(_rows(tm, D), _rows(tm, D)),
                          scratch_shapes=[pltpu.VMEM((tm + CF_HALO, D), F32)], compiler_params=_cp(1))(p, p, taps, vec)


def _cf_act_bwd(name, p, cv, dw, taps, vec):
    s = p.shape[0]
    tm = _tile(s, ROW_TILE_D)
    nt = s // tm
    rev = lambda col: pl.BlockSpec((tm, D), lambda i: (nt - 1 - i, col))
    rb_n = _tile(tm, CF_ROW_BLOCK)
    tap_of = lambda off: CF_TAPS - 1 - off if off < CF_TAPS else None

    def body(a_ref, g_ref, cv_ref, dw_ref, t_ref, v_ref, dp_ref, tacc_ref, acc_ref, nbuf, ubuf, dubuf):
        step = pl.program_id(0)

        @pl.when(step == 0)
        def _():
            acc_ref[...] = jnp.zeros_like(acc_ref)
            tacc_ref[...] = jnp.zeros_like(tacc_ref)

        a = a_ref[...].astype(F32) + v_ref[0:1, :]
        g = g_ref[...].astype(F32) + v_ref[1:2, :]
        sg = _sigmoid(g)
        ubuf[...] = a * sg
        cvv = cv_ref[...]
        mu = jnp.mean(cvv, axis=-1, keepdims=True)
        cc = cvv - mu
        rstd = lax.rsqrt(jnp.mean(cc * cc, axis=-1, keepdims=True) + LN_EPS)
        vhat = cc * rstd
        ln = vhat * v_ref[3:4, :] + v_ref[4:5, :]
        s2 = _sigmoid(ln)
        dln = dw_ref[...].astype(F32) * (s2 * (1.0 + ln * (1.0 - s2)))
        acc_ref[1:2, :] += jnp.sum(dln * vhat, axis=0, keepdims=True)
        acc_ref[2:3, :] += jnp.sum(dln, axis=0, keepdims=True)
        dvh = dln * v_ref[3:4, :]
        dcv = rstd * (dvh - jnp.mean(dvh, axis=-1, keepdims=True)
                      - vhat * jnp.mean(dvh * vhat, axis=-1, keepdims=True))
        acc_ref[0:1, :] += jnp.sum(dcv, axis=0, keepdims=True)
        _carry_up(nbuf, tm, step == 0, CF_HALO)
        nbuf[0:tm, :] = dcv
        for ci in range(D // CF_LANES):
            cols = slice(ci * CF_LANES, (ci + 1) * CF_LANES)

            def rb_body(rb, accs):
                r0 = pl.multiple_of(rb * rb_n, rb_n)
                ext = nbuf[pl.ds(r0, rb_n + CF_HALO), cols]
                accs = list(accs)
                dubuf[pl.ds(r0, rb_n), cols] = _tap_conv_block(
                    ext, rb_n, tap_of, t_ref, cols, jnp.zeros((rb_n, CF_LANES), F32),
                    ubuf[pl.ds(r0, rb_n), cols], accs)
                return tuple(accs)

            z = jnp.zeros((8, CF_LANES), F32)
            accs = lax.fori_loop(0, tm // rb_n, rb_body, tuple([z] * CF_TAPS))
            for k in range(CF_TAPS):
                tacc_ref[k:k + 1, cols] += jnp.sum(accs[k], axis=0, keepdims=True)
        a = a_ref[...].astype(F32) + v_ref[0:1, :]
        sg = _sigmoid(g_ref[...].astype(F32) + v_ref[1:2, :])
        da = dubuf[...] * sg
        dg = da * a * (1.0 - sg)
        dp_ref[:, 0:D] = da.astype(dp_ref.dtype)
        dp_ref[:, D:2 * D] = dg.astype(dp_ref.dtype)
        acc_ref[3:4, :] += jnp.sum(da, axis=0, keepdims=True)
        acc_ref[4:5, :] += jnp.sum(dg, axis=0, keepdims=True)

    return pl.pallas_call(
        body, name=name, out_shape=(SDS((s, 2 * D), BF), SDS((32, D), F32), SDS((8, D), F32)), grid=(nt,),
        in_specs=[rev(0), rev(1), rev(0), rev(0), _const((32, D)), _const((8, D))],
        out_specs=(pl.BlockSpec((tm, 2 * D), lambda i: (nt - 1 - i, 0)), _const((32, D)), _const((8, D))),
        scratch_shapes=[pltpu.VMEM((tm + CF_HALO, D), F32), pltpu.VMEM((tm, D), F32), pltpu.VMEM((tm, D), F32)],
        compiler_params=_cp(1))(p, p, cv, dw, taps, vec)


def _inv_count(row0, tm, window):
    t = row0 + lax.broadcasted_iota(jnp.int32, (tm, 1), 0)
    return 1.0 / jnp.minimum(t + 1, window).astype(F32)


def _pool_fwd(name, x, w, vec):
    s = x.shape[0]
    tm = _tile(s, ROW_TILE_D)
    G = POOL_GROUP

    def body(x_ref, w_ref, v_ref, pl_ref, m_ref, buf):
        i = pl.program_id(0)

        @pl.when(i == 0)
        def _():
            buf[0:POOL_HALO, :] = jnp.zeros((POOL_HALO, D), F32)

        @pl.when(i > 0)
        def _():
            buf[0:POOL_HALO, :] = buf[tm:tm + POOL_HALO, :]

        xv = x_ref[...]
        r = lax.rsqrt(jnp.mean(xv * xv, axis=-1, keepdims=True) + RMS_EPS)
        buf[POOL_HALO:POOL_HALO + tm, :] = xv * r * (v_ref[0:1, :] * (1.0 + v_ref[1:2, :])) + v_ref[2:3, :]
        for gi, win in enumerate(POOL_WINDOWS):
            cols = slice(gi * G, (gi + 1) * G)
            acc = buf[POOL_HALO:POOL_HALO + tm, cols]
            hg = acc
            for j in range(1, win):
                acc = acc + buf[POOL_HALO - j:POOL_HALO - j + tm, cols]
            pooled = (acc * _inv_count(i * tm, tm, win) - hg).astype(BF)
            pl_ref[:, cols] = pooled
            yg = jnp.dot(pooled, w_ref[gi], preferred_element_type=F32)
            m_ref[:, cols] = (yg + v_ref[3:4, cols]) * v_ref[4:5, cols]

    return pl.pallas_call(body, name=name, out_shape=(SDS((s, D), BF), SDS((s, D), F32)), grid=(s // tm,),
                          in_specs=[_rows(tm, D), _const((4, G, G)), _const((8, D))],
                          out_specs=(_rows(tm, D), _rows(tm, D)),
                          scratch_shapes=[pltpu.VMEM((tm + POOL_HALO, D), F32)], compiler_params=_cp(1))(x, w, vec)


def _pool_bwd(name, pooled, dm, w, vec):
    s = pooled.shape[0]
    tm = _tile(s, ROW_TILE_D)
    nt = s // tm
    G = POOL_GROUP
    rev = pl.BlockSpec((tm, D), lambda i: (nt - 1 - i, 0))

    def body(p_ref, dm_ref, w_ref, v_ref, dh_ref, dw_ref, acc_ref, nbuf):
        step = pl.program_id(0)
        row0 = (nt - 1 - step) * tm

        @pl.when(step == 0)
        def _():
            acc_ref[...] = jnp.zeros_like(acc_ref)
            dw_ref[...] = jnp.zeros_like(dw_ref)

        _carry_up(nbuf, tm, step == 0, POOL_HALO)
        dmv = dm_ref[...].astype(F32)
        acc_ref[0:1, :] += jnp.sum(dmv, axis=0, keepdims=True) * v_ref[1:2, :]
        dps = []
        for gi, win in enumerate(POOL_WINDOWS):
            cols = slice(gi * G, (gi + 1) * G)
            pg = p_ref[:, cols]
            yb = jnp.dot(pg, w_ref[gi], preferred_element_type=F32) + v_ref[0:1, cols]
            acc_ref[1:2, cols] += jnp.sum(dmv[:, cols] * yb, axis=0, keepdims=True)
            dy = (dmv[:, cols] * v_ref[1:2, cols]).astype(BF)
            dw_ref[gi] += lax.dot_general(pg, dy, TN, preferred_element_type=F32)
            dpg = lax.dot_general(dy, w_ref[gi], NT, preferred_element_type=F32)
            dps.append(dpg)
            nbuf[0:tm, cols] = dpg * _inv_count(row0, tm, win)
        for gi, win in enumerate(POOL_WINDOWS):
            cols = slice(gi * G, (gi + 1) * G)
            acc = nbuf[0:tm, cols]
            for j in range(1, win):
                acc = acc + nbuf[j:j + tm, cols]
            dh_ref[:, cols] = acc - dps[gi]

    return pl.pallas_call(
        body, name=name, out_shape=(SDS((s, D), F32), SDS((4, G, G), F32), SDS((8, D), F32)), grid=(nt,),
        in_specs=[rev, rev, _const((4, G, G)), _const((8, D))],
        out_specs=(rev, _const((4, G, G)), _const((8, D))),
        scratch_shapes=[pltpu.VMEM((tm + POOL_HALO, D), F32)], compiler_params=_cp(1))(pooled, dm, w, vec)


def _row_tile_2d(rows, width, bytes_per_row_elem=4, budget=2 * 1024 * 1024):
    t = max(8, budget // (width * bytes_per_row_elem))
    t = min(rows, 1 << (t.bit_length() - 1))
    while rows % t:
        t //= 2
    return t


def _add_slots(name, r):
    nl, _, k, n = r.shape
    tk = _row_tile_2d(k, n, 16)

    def body(r_ref, o_ref):
        f = lambda i: r_ref[i].astype(F32)
        o_ref[...] = ((f(7) + f(6)) + (f(0) + f(1))) + ((f(2) + f(3)) + (f(4) + f(5)))

    return pl.pallas_call(body, name=name, out_shape=SDS((nl, k, n), F32), grid=(nl, k // tk),
                          in_specs=[pl.BlockSpec((None, N_DEV, tk, n), lambda l, i: (l, 0, i, 0))],
                          out_specs=pl.BlockSpec((None, tk, n), lambda l, i: (l, i, 0)),
                          compiler_params=_cp(2))(r)


def _adamw(name, w, g, m, v):
    rows, width = w.shape
    tm = _row_tile_2d(rows, width, 4, 1024 * 1024)
    c1 = 1.0 - ADAM_B1 ** ADAM_STEP
    c2 = 1.0 - ADAM_B2 ** ADAM_STEP

    def body(w_ref, g_ref, m_ref, v_ref, d_ref, nm_ref, nv_ref):
        gv = g_ref[...]
        nm = ADAM_B1 * m_ref[...] + (1.0 - ADAM_B1) * gv
        nv = ADAM_B2 * v_ref[...] + (1.0 - ADAM_B2) * (gv * gv)
        nm_ref[...] = nm
        nv_ref[...] = nv
        d_ref[...] = -ADAM_LR * ((nm / c1) / (jnp.sqrt(nv / c2) + ADAM_EPS) + ADAM_WD * w_ref[...])

    spec = _rows(tm, width)
    sds = SDS((rows, width), F32)
    return pl.pallas_call(body, name=name, out_shape=(sds, sds, sds), grid=(rows // tm,),
                          in_specs=[spec] * 4, out_specs=(spec,) * 3, compiler_params=_cp(1))(w, g, m, v)


def _mod_fwd(c16, w_mod, b_sh):
    n = w_mod.shape[2]
    tn = _tile(n, 512)

    def body(c_ref, w_ref, b_ref, o_ref):
        cv = c_ref[...]
        ca = (cv * _sigmoid(cv)).astype(BF)
        o_ref[...] = jnp.dot(ca, w_ref[...].astype(BF), preferred_element_type=F32) + b_ref[0:1, :]

    return pl.pallas_call(body, name="mod_fwd", out_shape=SDS((DEPTH, 16, n), F32), grid=(DEPTH, n // tn),
                          in_specs=[_const((16, D)), pl.BlockSpec((None, D, tn), lambda l, j: (l, 0, j)),
                                    pl.BlockSpec((None, 8, tn), lambda l, j: (l, 0, j))],
                          out_specs=pl.BlockSpec((None, 16, tn), lambda l, j: (l, 0, j)),
                          compiler_params=_cp(2))(c16, w_mod, b_sh)


def _mod_bwd(c16, dmod):
    n = dmod.shape[2]
    tn = _tile(n, 512)

    def body(c_ref, d_ref, o_ref):
        cv = c_ref[...]
        ca = (cv * _sigmoid(cv)).astype(BF)
        o_ref[...] = lax.dot_general(ca, d_ref[...].astype(BF), TN, preferred_element_type=F32)

    return pl.pallas_call(body, name="mod_bwd", out_shape=SDS((DEPTH, D, n), F32), grid=(DEPTH, n // tn),
                          in_specs=[_const((16, D)), pl.BlockSpec((None, 16, tn), lambda l, j: (l, 0, j))],
                          out_specs=pl.BlockSpec((None, D, tn), lambda l, j: (l, 0, j)),
                          compiler_params=_cp(2))(c16, dmod)


def _place():
    x, y, c = lax.axis_index("x"), lax.axis_index("y"), lax.axis_index("c")
    other_chips = [(1 - x, y), (x, 1 - y), (1 - x, 1 - y)]
    return x, y, c, other_chips


def _allgather_small(name, v, with_sum):
    m, n = v.shape

    def body(x_ref, out_ref, *rest):
        if with_sum:
            sum_ref, send_sems, recv_sems, local_sem = rest
        else:
            send_sems, recv_sems, local_sem = rest
        x, y, c, chips = _place()
        me, sibling = (x, y, c), (x, y, 1 - c)

        def rows(px, py, pc):
            return out_ref.at[pl.ds((4 * px + 2 * py + pc) * m, m), :]

        def copy(k, block, to, src=None):
            return pltpu.make_async_remote_copy(
                src_ref=rows(*block) if src is None else src, dst_ref=rows(*block),
                send_sem=send_sems.at[k], recv_sem=recv_sems.at[k], device_id=to, device_id_type=MESH)

        mine = pltpu.make_async_copy(x_ref, rows(*me), local_sem)
        mine.start()
        first = [copy(0, me, sibling, src=x_ref)]
        first += [copy(1 + j, me, (*chip, c), src=x_ref) for j, chip in enumerate(chips)]
        for cp in first:
            cp.start()
        passed = [copy(4 + j, (*chip, c), sibling) for j, chip in enumerate(chips)]
        for j, chip in enumerate(chips):
            copy(1 + j, (*chip, c), me).wait_recv()
            passed[j].start()
        copy(0, sibling, me).wait_recv()
        for j, chip in enumerate(chips):
            copy(4 + j, (*chip, 1 - c), me).wait_recv()
        for cp in first + passed:
            cp.wait_send()
        mine.wait()
        if with_sum:
            acc = out_ref[0:m, :]
            for k in range(1, N_DEV):
                acc = acc + out_ref[k * m:(k + 1) * m, :]
            sum_ref[...] = acc

    vm = pl.BlockSpec(memory_space=pltpu.VMEM)
    out_shape = [SDS((N_DEV * m, n), F32)] + ([SDS((m, n), F32)] if with_sum else [])
    res = pl.pallas_call(
        body, name=name, out_shape=tuple(out_shape), in_specs=[vm], out_specs=tuple([vm] * len(out_shape)),
        scratch_shapes=[pltpu.SemaphoreType.DMA((7,)), pltpu.SemaphoreType.DMA((7,)), pltpu.SemaphoreType.DMA],
        compiler_params=pltpu.CompilerParams(vmem_limit_bytes=VMEM_LIMIT_MB * 1024 * 1024))(v)
    return res if with_sum else res[0]


HBM = pl.BlockSpec(memory_space=pltpu.HBM)


def _sem_scratch(n_remote, n_local):
    return [pltpu.SemaphoreType.DMA((n_remote,)), pltpu.SemaphoreType.DMA((n_remote,)),
            pltpu.SemaphoreType.DMA((n_local,))]


DMA_PIECE_BYTES = 1 << 20


def _pieces(src, dst):
    *lead, rows, n = src.shape
    nsplit = max(1, min(rows // 16, (rows * n * jnp.dtype(src.dtype).itemsize) // DMA_PIECE_BYTES))
    while rows % nsplit or (rows // nsplit) % 16:
        nsplit -= 1
    size = rows // nsplit
    out = []
    for idx in itertools.product(*[range(d) for d in lead]):
        for i in range(nsplit):
            sl = tuple(idx) + (pl.ds(i * size, size),)
            out.append((src.at[sl], dst.at[sl]))
    return out


def _local_copies(src, dst, sem):
    return ([pltpu.make_async_copy(s_, d_, sem) for s_, d_ in _pieces(src, dst)],
            pltpu.make_async_copy(src, dst, sem))


def _remote_copies(src, dst, send_sem, recv_sem, to):
    mk = lambda s_, d_: pltpu.make_async_remote_copy(src_ref=s_, dst_ref=d_, send_sem=send_sem, recv_sem=recv_sem,
                                                     device_id=to, device_id_type=MESH)
    return [mk(s_, d_) for s_, d_ in _pieces(src, dst)], mk(src, dst)


def _gather_weights(shards):
    nq = len(shards)

    def body(*refs):
        ins, outs = refs[:nq], refs[nq:2 * nq]
        send_sems, recv_sems, local_sems = refs[2 * nq:]
        x, y, c, chips = _place()
        me_chip = 2 * x + y
        started, local_all, send_all, recv_all = [], [], [], []
        for q in range(nq):
            cps, whole = _local_copies(ins[q], outs[q].at[:, me_chip], local_sems.at[q])
            started += cps
            local_all.append(whole)
            for r, (px, py) in enumerate(chips):
                k = 3 * q + r
                cps, whole = _remote_copies(ins[q], outs[q].at[:, me_chip], send_sems.at[k], recv_sems.at[k], (px, py, c))
                started += cps
                send_all.append(whole)
                recv_all.append(_remote_copies(ins[q], outs[q].at[:, 2 * px + py], send_sems.at[k], recv_sems.at[k],
                                               (px, py, c))[1])
        for cp in started:
            cp.start()
        for cp in recv_all:
            cp.wait_recv()
        for cp in send_all:
            cp.wait_send()
        for cp in local_all:
            cp.wait()

    out_shape = tuple(SDS((s.shape[0], N_CHIPS) + s.shape[1:], s.dtype) for s in shards)
    return pl.pallas_call(body, name="gather_weights", out_shape=out_shape, in_specs=[HBM] * nq,
                          out_specs=tuple([HBM] * nq), scratch_shapes=_sem_scratch(3 * nq, nq))(*shards)


def _grad_exchange(gs):
    nq = len(gs)

    def body(*refs):
        ins, outs = refs[:nq], refs[nq:2 * nq]
        send_sems, recv_sems, local_sems = refs[2 * nq:]
        x, y, c, chips = _place()
        peers = [(2 * r + e, (px, py, c if e == 0 else 1 - c)) for r, (px, py) in enumerate(chips) for e in (0, 1)]
        peers.append((6, (x, y, 1 - c)))
        started, local_all, remote_all = [], [], []
        for q in range(nq):
            cps, whole = _local_copies(ins[q].at[:, 2 * x + y, c], outs[q].at[:, 7], local_sems.at[q])
            started += cps
            local_all.append(whole)
            for slot, (px, py, pc) in peers:
                k = 7 * q + slot
                cps, whole = _remote_copies(ins[q].at[:, 2 * px + py, pc], outs[q].at[:, slot], send_sems.at[k],
                                            recv_sems.at[k], (px, py, pc))
                started += cps
                remote_all.append(whole)
        for cp in started:
            cp.start()
        for cp in remote_all:
            cp.wait_recv()
        for cp in remote_all:
            cp.wait_send()
        for cp in local_all:
            cp.wait()

    out_shape = tuple(SDS((g.shape[0], N_DEV, g.shape[3], g.shape[4]), g.dtype) for g in gs)
    return pl.pallas_call(body, name="grad_exchange", out_shape=out_shape, in_specs=[HBM] * nq,
                          out_specs=tuple([HBM] * nq), scratch_shapes=_sem_scratch(7 * nq, nq))(*gs)


def _pair_share(rs):
    nq = len(rs)

    def body(*refs):
        ins, outs = refs[:nq], refs[nq:2 * nq]
        send_sems, recv_sems, local_sems = refs[2 * nq:]
        x, y, c, _ = _place()
        started, local_all, send_all, recv_all = [], [], [], []
        for q in range(nq):
            cps, whole = _local_copies(ins[q], outs[q].at[:, c], local_sems.at[q])
            started += cps
            local_all.append(whole)
            cps, whole = _remote_copies(ins[q], outs[q].at[:, c], send_sems.at[q], recv_sems.at[q], (x, y, 1 - c))
            started += cps
            send_all.append(whole)
            recv_all.append(_remote_copies(ins[q], outs[q].at[:, 1 - c], send_sems.at[q], recv_sems.at[q],
                                           (x, y, 1 - c))[1])
        for cp in started:
            cp.start()
        for cp in recv_all:
            cp.wait_recv()
        for cp in send_all:
            cp.wait_send()
        for cp in local_all:
            cp.wait()

    out_shape = tuple(SDS((r.shape[0], 2) + r.shape[1:], r.dtype) for r in rs)
    return pl.pallas_call(body, name="grad_pair_share", out_shape=out_shape, in_specs=[HBM] * nq,
                          out_specs=tuple([HBM] * nq), scratch_shapes=_sem_scratch(nq, nq))(*rs)


def _pack(arrs, rows_multiple=8):
    flat = jnp.concatenate([a.astype(F32).reshape(-1) for a in arrs])
    pad = (-flat.shape[0]) % (128 * rows_multiple)
    return jnp.pad(flat, (0, pad)).reshape(-1, 128)


def _unpack(slab, shapes):
    flat = slab.reshape(-1)
    out, off = [], 0
    for shp in shapes:
        n = 1
        for d in shp:
            n *= d
        out.append(flat[off:off + n].reshape(shp))
        off += n
    return out


def _shard_last(a, chip, n):
    return lax.dynamic_slice_in_dim(a, chip * n, n, axis=a.ndim - 1)


def kernel(x, c, w_mod, b_mod, norm_g, sc_w_in, sc_conv, sc_w_out, pool_w, pool_b, pool_scale, cf_w_pw1, cf_b_pw1, cf_w_dw, cf_b_dw, cf_ln_g, cf_ln_b, cf_w_pw2, cf_b_pw2, ffn_w_up, ffn_conv, ffn_b_conv, ffn_w_down, loss_target, m_w_mod, m_b_mod, m_norm_g, m_sc_w_in, m_sc_conv, m_sc_w_out, m_pool_w, m_pool_b, m_pool_scale, m_cf_w_pw1, m_cf_b_pw1, m_cf_w_dw, m_cf_b_dw, m_cf_ln_g, m_cf_ln_b, m_cf_w_pw2, m_cf_b_pw2, m_ffn_w_up, m_ffn_conv, m_ffn_b_conv, m_ffn_w_down, v_w_mod, v_b_mod, v_norm_g, v_sc_w_in, v_sc_conv, v_sc_w_out, v_pool_w, v_pool_b, v_pool_scale, v_cf_w_pw1, v_cf_b_pw1, v_cf_w_dw, v_cf_b_dw, v_cf_ln_g, v_cf_ln_b, v_cf_w_pw2, v_cf_b_pw2, v_ffn_w_up, v_ffn_conv, v_ffn_b_conv, v_ffn_w_down):
    ax, ay, ac = lax.axis_index("x"), lax.axis_index("y"), lax.axis_index("c")
    chip = 2 * ax + ay
    dev = 4 * ax + 2 * ay + ac
    xs = x[0]
    target = loss_target[0]

    small_sharded = [norm_g, sc_conv, cf_b_pw1, cf_w_dw, cf_b_dw, cf_ln_g, cf_ln_b, cf_b_pw2, ffn_conv]
    slab = _pack([c] + small_sharded)
    gathered = _allgather_small("gather_small_params", slab, False).reshape(N_DEV, -1, 128)
    parts = [_unpack(gathered[d], [c.shape] + [a.shape for a in small_sharded]) for d in range(N_DEV)]
    c_all = jnp.concatenate([p[0] for p in parts], axis=0)
    full = [jnp.concatenate([parts[2 * j][1 + i] for j in range(N_CHIPS)], axis=-1)
            for i in range(len(small_sharded))]
    norm_g_f, sc_conv_f, cf_b_pw1_f, cf_w_dw_f, cf_b_dw_f, cf_ln_g_f, cf_ln_b_f, cf_b_pw2_f, ffn_conv_f = full
    c16 = jnp.pad(c_all, ((0, 8), (0, 0)))

    n_mod = w_mod.shape[2]
    b_sh = jnp.broadcast_to(_shard_last(b_mod, chip, n_mod)[:, None, :], (DEPTH, 8, n_mod))
    mod_part = _mod_fwd(c16, w_mod, b_sh)
    mod_g = _allgather_small("gather_mod", mod_part.reshape(DEPTH * 16, n_mod), False)
    mod_g = mod_g.reshape(N_DEV, DEPTH, 16, n_mod)
    mod_mine = jnp.concatenate(
        [lax.dynamic_index_in_dim(mod_g[2 * j], dev, axis=1, keepdims=False) for j in range(N_CHIPS)], axis=-1)
    mod = mod_mine.reshape(DEPTH, 6, D)

    shards = [sc_w_in.astype(BF), sc_w_out.astype(BF), pool_w[0].astype(BF), cf_w_pw1.astype(BF),
              cf_w_pw2.astype(BF), ffn_w_up.astype(BF), ffn_w_down.astype(BF)]
    w_in_f, w_out_f, pool_f, pw1_f, pw2_f, w_up_f, w_down_f = _gather_weights(shards)
    w_out_f = w_out_f.reshape(w_out_f.shape[0], D, D)
    pool_f = pool_f.reshape(4, POOL_GROUP, POOL_GROUP)
    pw2_f = pw2_f.reshape(1, D, D)
    w_down_f = w_down_f.reshape(DEPTH, F, D)

    saved = []
    xcur = xs
    for i in range(DEPTH):
        kind, j = i % 3, i // 3
        sh1, sc1, g1, sh2, sc2, g2 = [mod[i, k] for k in range(6)]
        st = {"x0": xcur}
        pro_vec = _vecs([norm_g_f[i, 0], sc1, sh1], D)
        if kind == 0:
            h = _pro_fwd(f"pro1_fwd_{i}", xcur, pro_vec)
            p = _mm_nn(f"sc_in_{i}", h, w_in_f, j, True, BF)
            z = _sc_act_fwd(f"sc_act_fwd_{i}", p, _vecs(list(sc_conv_f[j]), D))
            m = _mm_nn(f"sc_out_{i}", z, w_out_f, j, False, F32)
            st.update(h=h, p=p)
        elif kind == 1:
            pool_vec = _vecs([norm_g_f[i, 0], sc1, sh1, pool_b[j], pool_scale[j]], D)
            pooled, m = _pool_fwd(f"pool_fwd_{i}", xcur, pool_f, pool_vec)
            st.update(pooled=pooled)
        else:
            h = _pro_fwd(f"pro1_fwd_{i}", xcur, pro_vec)
            p = _mm_nn(f"cf_pw1_{i}", h, pw1_f, j, True, BF)
            taps = jnp.pad(cf_w_dw_f[j], ((0, 1), (0, 0)))
            cf_vec = _vecs([cf_b_pw1_f[j, :D], cf_b_pw1_f[j, D:], cf_b_dw_f[j], cf_ln_g_f[j], cf_ln_b_f[j]], D)
            wact, cv = _cf_act_fwd(f"cf_act_fwd_{i}", p, taps, cf_vec)
            m = _mm_nn(f"cf_pw2_{i}", wact, pw2_f, j, False, F32)
            m = _add_bias(f"cf_bias_{i}", m, cf_b_pw2_f[j])
            st.update(h=h, p=p, wact=wact, cv=cv, taps=taps, cf_vec=cf_vec)
        x1 = _epi_fwd(f"epi1_fwd_{i}", xcur, m, _vecs([g1, norm_g_f[i, 1]], D))
        st.update(m=m, x1=x1)
        h2 = _pro_fwd(f"pro2_fwd_{i}", x1, _vecs([norm_g_f[i, 2], sc2, sh2], D))
        up = _mm_nn(f"ffn_up_{i}", h2, w_up_f, i, True, BF)
        ffn_vec = _vecs(list(ffn_conv_f[i]) + [ffn_b_conv[i]], F)
        a = _ffn_act_fwd(f"ffn_act_fwd_{i}", up, ffn_vec)
        f = _mm_nn(f"ffn_down_{i}", a, w_down_f, i, False, F32, tm=512)
        xcur = _epi_fwd(f"epi2_fwd_{i}", x1, f, _vecs([g2, norm_g_f[i, 3]], D))
        st.update(h2=h2, up=up, f=f, ffn_vec=ffn_vec)
        saved.append(st)

    dy, loss_cols = _loss_fwd_bwd(xcur, target)
    loss = lax.psum(jnp.sum(loss_cols[0]), ("x", "y", "c"))

    dmod = [None] * DEPTH
    d_norm_g = [None] * DEPTH
    d_sc_conv = [None, None]
    d_ffn_conv, d_ffn_b = [None] * DEPTH, [None] * DEPTH
    g_w_in, g_w_out, g_w_up, g_w_down = [None, None], [None, None], [None] * DEPTH, [None] * DEPTH
    dxo = dy
    for i in reversed(range(DEPTH)):
        kind, j = i % 3, i // 3
        st = saved[i]
        sh1, sc1, g1, sh2, sc2, g2 = [mod[i, k] for k in range(6)]
        df, e2 = _epi_bwd(f"epi2_bwd_{i}", dxo, st["f"], _vecs([g2, norm_g_f[i, 3]], D))
        da = _mm_nt(f"ffn_down_dx_{i}", df, w_down_f, i, False, BF, tn=F // 2)
        a, dup, fsum = _ffn_act_bwd(f"ffn_act_bwd_{i}", st["up"], da, st["ffn_vec"])
        g_w_down[i] = _mm_tn(f"ffn_down_dw_{i}", a, df, False, tk=F // 2)
        dh2 = _mm_nt(f"ffn_up_dx_{i}", dup, w_up_f, i, True, F32)
        g_w_up[i] = _mm_tn(f"ffn_up_dw_{i}", st["h2"], dup, True)
        dx1, p2 = _pro_bwd(f"pro2_bwd_{i}", dh2, st["x1"], dxo, _vecs([norm_g_f[i, 2], sc2], D))
        dm, e1 = _epi_bwd(f"epi1_bwd_{i}", dx1, st["m"], _vecs([g1, norm_g_f[i, 1]], D))
        if kind == 0:
            dz = _mm_nt(f"sc_out_dx_{i}", dm, w_out_f, j, False, BF)
            z, dp, ssum = _sc_act_bwd(f"sc_act_bwd_{i}", st["p"], dz, _vecs(list(sc_conv_f[j]), D))
            g_w_out[j] = _mm_tn(f"sc_out_dw_{i}", z, dm, False)
            dh = _mm_nt(f"sc_in_dx_{i}", dp, w_in_f, j, True, F32)
            g_w_in[j] = _mm_tn(f"sc_in_dw_{i}", st["h"], dp, True)
            d_sc_conv[j] = ssum[0:3]
        elif kind == 1:
            dh, g_pool, psum = _pool_bwd(f"pool_bwd_{i}", st["pooled"], dm, pool_f, _vecs([pool_b[j], pool_scale[j]], D))
        else:
            dwact = _mm_nt(f"cf_pw2_dx_{i}", dm, pw2_f, j, False, BF)
            g_pw2 = _mm_tn(f"cf_pw2_dw_{i}", st["wact"], dm, False)
            dp, tsum, csum = _cf_act_bwd(f"cf_act_bwd_{i}", st["p"], st["cv"], dwact, st["taps"], st["cf_vec"])
            dh = _mm_nt(f"cf_pw1_dx_{i}", dp, pw1_f, j, True, F32)
            g_pw1 = _mm_tn(f"cf_pw1_dw_{i}", st["h"], dp, True)
            d_cf = dict(b_pw1=jnp.concatenate([csum[3], csum[4]])[None], w_dw=tsum[None, :CF_TAPS], b_dw=csum[0:1],
                        ln_g=csum[1:2], ln_b=csum[2:3], b_pw2=e1[2:3])
        dxo, p1 = _pro_bwd(f"pro1_bwd_{i}", dh, st["x0"], dx1, _vecs([norm_g_f[i, 0], sc1], D))
        dmod[i] = jnp.concatenate([p1[0], p1[1], e1[0], p2[0], p2[1], e2[0]])
        d_norm_g[i] = jnp.stack([p1[2], e1[1], p2[2], e2[1]])
        if kind == 1:
            d_pool_b, d_pool_scale = psum[0:1], psum[1:2]
        saved[i] = None
        st = None
        if i == 0:
            grad_x = dxo[None]
        d_ffn_conv[i], d_ffn_b[i] = fsum[1:4], fsum[0]

    small_shapes = [(DEPTH, 6 * D), (DEPTH, 4, D), (2, 3, D), (1, D), (1, D), (1, 2 * D), (1, CF_TAPS, D), (1, D),
                    (1, D), (1, D), (1, D), (DEPTH, 3, F), (DEPTH, F)]
    small = [jnp.stack(dmod), jnp.stack(d_norm_g), jnp.stack(d_sc_conv), d_pool_b, d_pool_scale, d_cf["b_pw1"],
             d_cf["w_dw"], d_cf["b_dw"], d_cf["ln_g"], d_cf["ln_b"], d_cf["b_pw2"], jnp.stack(d_ffn_conv),
             jnp.stack(d_ffn_b)]
    gsl, ssl = _allgather_small("reduce_small_grads", _pack(small), True)
    tot = _unpack(ssl, small_shapes)
    (gb_mod, gt_norm_g, gt_sc_conv, g_pool_b, g_pool_scale, gt_b_pw1, gt_w_dw, gt_b_dw, gt_ln_g, gt_ln_b, gt_b_pw2,
     gt_ffn_conv, g_ffn_b) = tot
    dmod_all = jnp.stack([_unpack(gsl.reshape(N_DEV, -1, 128)[d], small_shapes[:1])[0] for d in range(N_DEV)], axis=1)
    dmod_sh = jnp.pad(_shard_last(dmod_all, chip, n_mod), ((0, 0), (0, 8), (0, 0)))
    g_w_mod = _mod_bwd(c16, dmod_sh)
    g_norm_g = _shard_last(gt_norm_g, chip, D // 4)
    g_sc_conv = _shard_last(gt_sc_conv, chip, D // 4)
    g_b_pw1 = _shard_last(gt_b_pw1, chip, 2 * D // 4)
    g_w_dw = _shard_last(gt_w_dw, chip, D // 4)
    g_b_dw = _shard_last(gt_b_dw, chip, D // 4)
    g_ln_g = _shard_last(gt_ln_g, chip, D // 4)
    g_ln_b = _shard_last(gt_ln_b, chip, D // 4)
    g_b_pw2 = _shard_last(gt_b_pw2, chip, D // 4)
    g_ffn_conv = _shard_last(gt_ffn_conv, chip, F // 4)

    def halves(gl):
        g = jnp.stack(gl) if isinstance(gl, list) else gl
        if g.ndim == 3:
            g = g.reshape(g.shape[0], N_CHIPS, g.shape[1] // N_CHIPS, g.shape[2])
        nl, _, k, n = g.shape
        return g.reshape(nl, N_CHIPS, 2, k // 2, n)

    partial = [halves(g_w_in), halves(g_w_out), halves(g_pool.astype(BF)), halves([g_pw1]), halves([g_pw2]),
               halves(g_w_up), halves(g_w_down)]
    pieces = _grad_exchange(partial)
    reduced = [_add_slots(f"grad_sum_{q}", pieces[q]) for q in range(len(pieces))]
    shared = _pair_share(reduced)
    big_w = [sc_w_in, sc_w_out, pool_w, cf_w_pw1, cf_w_pw2, ffn_w_up, ffn_w_down]
    big_g = [shared[q].reshape(big_w[q].shape) for q in range(len(big_w))]
    g_sc_w_in, g_sc_w_out, g_pool_w, g_cf_w_pw1, g_cf_w_pw2, g_ffn_w_up, g_ffn_w_down = big_g

    def adam_big(name, w, g, m, v):
        shp = w.shape
        two = lambda t: t.reshape(-1, shp[-1])
        return [o.reshape(shp) for o in _adamw(name, two(w), two(g), two(m), two(v))]

    grads = dict(w_mod=g_w_mod, b_mod=gb_mod, norm_g=g_norm_g, sc_w_in=g_sc_w_in, sc_conv=g_sc_conv,
                 sc_w_out=g_sc_w_out, pool_w=g_pool_w, pool_b=g_pool_b, pool_scale=g_pool_scale,
                 cf_w_pw1=g_cf_w_pw1, cf_b_pw1=g_b_pw1, cf_w_dw=g_w_dw, cf_b_dw=g_b_dw, cf_ln_g=g_ln_g,
                 cf_ln_b=g_ln_b, cf_w_pw2=g_cf_w_pw2, cf_b_pw2=g_b_pw2, ffn_w_up=g_ffn_w_up, ffn_conv=g_ffn_conv,
                 ffn_b_conv=g_ffn_b, ffn_w_down=g_ffn_w_down)
    weights = dict(w_mod=w_mod, b_mod=b_mod, norm_g=norm_g, sc_w_in=sc_w_in, sc_conv=sc_conv, sc_w_out=sc_w_out,
                   pool_w=pool_w, pool_b=pool_b, pool_scale=pool_scale, cf_w_pw1=cf_w_pw1, cf_b_pw1=cf_b_pw1,
                   cf_w_dw=cf_w_dw, cf_b_dw=cf_b_dw, cf_ln_g=cf_ln_g, cf_ln_b=cf_ln_b, cf_w_pw2=cf_w_pw2,
                   cf_b_pw2=cf_b_pw2, ffn_w_up=ffn_w_up, ffn_conv=ffn_conv, ffn_b_conv=ffn_b_conv,
                   ffn_w_down=ffn_w_down)
    m_in = dict(w_mod=m_w_mod, b_mod=m_b_mod, norm_g=m_norm_g, sc_w_in=m_sc_w_in, sc_conv=m_sc_conv,
                sc_w_out=m_sc_w_out, pool_w=m_pool_w, pool_b=m_pool_b, pool_scale=m_pool_scale,
                cf_w_pw1=m_cf_w_pw1, cf_b_pw1=m_cf_b_pw1, cf_w_dw=m_cf_w_dw, cf_b_dw=m_cf_b_dw, cf_ln_g=m_cf_ln_g,
                cf_ln_b=m_cf_ln_b, cf_w_pw2=m_cf_w_pw2, cf_b_pw2=m_cf_b_pw2, ffn_w_up=m_ffn_w_up,
                ffn_conv=m_ffn_conv, ffn_b_conv=m_ffn_b_conv, ffn_w_down=m_ffn_w_down)
    v_in = dict(w_mod=v_w_mod, b_mod=v_b_mod, norm_g=v_norm_g, sc_w_in=v_sc_w_in, sc_conv=v_sc_conv,
                sc_w_out=v_sc_w_out, pool_w=v_pool_w, pool_b=v_pool_b, pool_scale=v_pool_scale,
                cf_w_pw1=v_cf_w_pw1, cf_b_pw1=v_cf_b_pw1, cf_w_dw=v_cf_w_dw, cf_b_dw=v_cf_b_dw, cf_ln_g=v_cf_ln_g,
                cf_ln_b=v_cf_ln_b, cf_w_pw2=v_cf_w_pw2, cf_b_pw2=v_cf_b_pw2, ffn_w_up=v_ffn_w_up,
                ffn_conv=v_ffn_conv, ffn_b_conv=v_ffn_b_conv, ffn_w_down=v_ffn_w_down)
    names = list(weights)
    big_names = ["w_mod", "sc_w_in", "sc_w_out", "pool_w", "cf_w_pw1", "cf_w_pw2", "ffn_w_up", "ffn_w_down"]
    small_names = [n for n in names if n not in big_names]
    delta, new_m, new_v = {}, {}, {}
    for n in big_names:
        delta[n], new_m[n], new_v[n] = adam_big(f"adamw_{n}", weights[n], grads[n], m_in[n], v_in[n])
    grads = {n: grads[n].reshape(weights[n].shape) for n in names}
    sm_shapes = [weights[n].shape for n in small_names]
    sd, sm, sv = _adamw("adamw_small", _pack([weights[n] for n in small_names]), _pack([grads[n] for n in small_names]),
                        _pack([m_in[n] for n in small_names]), _pack([v_in[n] for n in small_names]))
    for n, d_, m_, v_ in zip(small_names, _unpack(sd, sm_shapes), _unpack(sm, sm_shapes), _unpack(sv, sm_shapes)):
        delta[n], new_m[n], new_v[n] = d_, m_, v_

    return (loss, grad_x, *[grads[n] for n in names], *[delta[n] for n in names], *[new_m[n] for n in names],
            *[new_v[n] for n in names])


def _add_bias(name, m, b):
    s = m.shape[0]
    tm = _tile(s, ROW_TILE_D)

    def body(m_ref, b_ref, o_ref):
        o_ref[...] = m_ref[...] + b_ref[0:1, :]

    return pl.pallas_call(body, name=name, out_shape=SDS((s, D), F32), grid=(s // tm,),
                          in_specs=[_rows(tm, D), _const((8, D))], out_specs=_rows(tm, D),
                          compiler_params=_cp(1))(m, _vecs([b], D))
```

```python
import itertools

import jax
import jax.numpy as jnp
from jax import lax
from jax.experimental import pallas as pl
from jax.experimental.pallas import tpu as pltpu

D = 1024
F = 2816
DEPTH = 4
POOL_WINDOWS = (2, 4, 8, 16)
POOL_GROUP = 256
CF_TAPS = 31
RMS_EPS = 1e-6
LN_EPS = 1e-5
ADAM_LR = 0.001
ADAM_B1 = 0.9
ADAM_B2 = 0.999
ADAM_EPS = 1e-08
ADAM_WD = 0.01
ADAM_STEP = 10

BF = jnp.bfloat16
F32 = jnp.float32
MESH = pl.DeviceIdType.MESH
SDS = jax.ShapeDtypeStruct
N_CHIPS = 4
N_DEV = 8
VMEM_LIMIT_MB = 56
ROW_TILE_D = 256
ROW_TILE_F = 256
HALO = 8
HALO_BF = 16
CF_HALO = 32
POOL_HALO = 16


def _cp(n_axes):
    return pltpu.CompilerParams(dimension_semantics=("arbitrary",) * n_axes,
                                vmem_limit_bytes=VMEM_LIMIT_MB * 1024 * 1024)


def _tile(n, pref):
    t = min(n, pref)
    assert n % t == 0, (n, pref)
    return t


def _const(shape):
    nd = len(shape)
    return pl.BlockSpec(shape, lambda *_: (0,) * nd)


def _vecs(rows, width):
    v = jnp.stack([r.reshape(width).astype(F32) for r in rows])
    pad = (-v.shape[0]) % 8
    return jnp.pad(v, ((0, pad), (0, 0)))


def _sigmoid(v):
    return 0.5 * jnp.tanh(0.5 * v) + 0.5


def _down(prev8, g, k):
    n = g.shape[0]
    return pltpu.roll(jnp.concatenate([prev8, g], axis=0), k, 0)[8:8 + n]


def _up(g, next8, k):
    n = g.shape[0]
    return pltpu.roll(jnp.concatenate([g, next8], axis=0), n + 8 - k, 0)[0:n]


def _fold8(v):
    acc = v[0:8]
    for r in range(8, v.shape[0], 8):
        acc = acc + v[r:r + 8]
    return acc


def _mm(name, a, b, out_sds, grid, a_spec, b_spec, o_spec, acc_shape, dn):
    nk = grid[2]

    def body(a_ref, b_ref, o_ref, *acc):
        part = lax.dot_general(a_ref[...], b_ref[...], dn, preferred_element_type=F32)
        if nk == 1:
            o_ref[...] = part.astype(o_ref.dtype)
        else:
            acc_ref = acc[0]
            k = pl.program_id(2)

            @pl.when(k == 0)
            def _():
                acc_ref[...] = part

            @pl.when(k > 0)
            def _():
                acc_ref[...] += part

            @pl.when(k == nk - 1)
            def _():
                o_ref[...] = acc_ref[...].astype(o_ref.dtype)

    scratch = [] if nk == 1 else [pltpu.VMEM(acc_shape, F32)]
    return pl.pallas_call(body, name=name, out_shape=out_sds, grid=grid, in_specs=[a_spec, b_spec],
                          out_specs=o_spec, scratch_shapes=scratch, compiler_params=_cp(3))(a, b)


NN = (((1,), (0,)), ((), ()))
NT = (((1,), (1,)), ((), ()))
TN = (((0,), (0,)), ((), ()))


def _mm_nn(name, a, w, layer, col_sharded, out_dtype, tm=1024, tn=None):
    m, k = a.shape
    tm = _tile(m, tm)
    if col_sharded:
        n4 = w.shape[3]
        tn = n4 if tn is None else tn
        tpc = n4 // tn
        n = N_CHIPS * n4
        b_spec = pl.BlockSpec((None, None, k, tn), lambda i, j, kk: (layer, j // tpc, 0, j % tpc))
    else:
        n = w.shape[2]
        tn = n if tn is None else tn
        b_spec = pl.BlockSpec((None, k, tn), lambda i, j, kk: (layer, 0, j))
    return _mm(name, a, w, SDS((m, n), out_dtype), (m // tm, n // tn, 1),
               pl.BlockSpec((tm, k), lambda i, j, kk: (i, 0)), b_spec,
               pl.BlockSpec((tm, tn), lambda i, j, kk: (i, j)), None, NN)


def _mm_nt(name, g, w, layer, col_sharded, out_dtype, tm=1024, tn=None):
    m, n = g.shape
    tm = _tile(m, tm)
    if col_sharded:
        kdim, n4 = w.shape[2], w.shape[3]
        tn = kdim if tn is None else tn
        tk = n4
        b_spec = pl.BlockSpec((None, None, tn, tk), lambda i, j, kk: (layer, kk, j, 0))
    else:
        kdim = w.shape[1]
        tn = kdim if tn is None else tn
        tk = n
        b_spec = pl.BlockSpec((None, tn, tk), lambda i, j, kk: (layer, j, kk))
    return _mm(name, g, w, SDS((m, kdim), out_dtype), (m // tm, kdim // tn, n // tk),
               pl.BlockSpec((tm, tk), lambda i, j, kk: (i, kk)), b_spec,
               pl.BlockSpec((tm, tn), lambda i, j, kk: (i, j)), (tm, tn), NT)


def _mm_tn(name, a, g, col_sharded, tk=None, tn=None, ts=2048):
    s, k = a.shape
    n = g.shape[1]
    ts = _tile(s, ts)
    tk = k if tk is None else tk
    if col_sharded:
        n4 = n // N_CHIPS
        tn = n4 if tn is None else tn
        tpc = n4 // tn
        out_sds = SDS((N_CHIPS, k, n4), BF)
        o_spec = pl.BlockSpec((None, tk, tn), lambda i, j, ss: (j // tpc, i, j % tpc))
    else:
        tn = n if tn is None else tn
        out_sds = SDS((k, n), BF)
        o_spec = pl.BlockSpec((tk, tn), lambda i, j, ss: (i, j))
    return _mm(name, a, g, out_sds, (k // tk, n // tn, s // ts),
               pl.BlockSpec((ts, tk), lambda i, j, ss: (ss, i)),
               pl.BlockSpec((ts, tn), lambda i, j, ss: (ss, j)), o_spec, (tk, tn), TN)


def _rows(tm, width, col=0):
    return pl.BlockSpec((tm, width), lambda i: (i, col))


def _pro_fwd(name, x, vec):
    s = x.shape[0]
    tm = _tile(s, ROW_TILE_D)

    def body(x_ref, v_ref, h_ref):
        xv = x_ref[...]
        r = lax.rsqrt(jnp.mean(xv * xv, axis=-1, keepdims=True) + RMS_EPS)
        a = v_ref[0:1, :] * (1.0 + v_ref[1:2, :])
        h_ref[...] = (xv * r * a + v_ref[2:3, :]).astype(h_ref.dtype)

    return pl.pallas_call(body, name=name, out_shape=SDS((s, D), BF), grid=(s // tm,),
                          in_specs=[_rows(tm, D), _const((8, D))], out_specs=_rows(tm, D),
                          compiler_params=_cp(1))(x, vec)


def _epi_pro_fwd(name, x, m, vec, with_next):
    s = x.shape[0]
    tm = _tile(s, ROW_TILE_D)

    def body(x_ref, m_ref, v_ref, o_ref, *h_ref):
        mv = m_ref[...] + v_ref[2:3, :]
        rm = lax.rsqrt(jnp.mean(mv * mv, axis=-1, keepdims=True) + RMS_EPS)
        xo = x_ref[...] + v_ref[0:1, :] * (mv * rm * v_ref[1:2, :])
        o_ref[...] = xo
        if with_next:
            r = lax.rsqrt(jnp.mean(xo * xo, axis=-1, keepdims=True) + RMS_EPS)
            a = v_ref[3:4, :] * (1.0 + v_ref[4:5, :])
            h_ref[0][...] = (xo * r * a + v_ref[5:6, :]).astype(BF)

    out_shape = (SDS((s, D), F32),) + ((SDS((s, D), BF),) if with_next else ())
    res = pl.pallas_call(body, name=name, out_shape=out_shape, grid=(s // tm,),
                         in_specs=[_rows(tm, D), _rows(tm, D), _const((8, D))],
                         out_specs=tuple([_rows(tm, D)] * len(out_shape)), compiler_params=_cp(1))(x, m, vec)
    return res if with_next else (res[0], None)


def _loss_fwd_bwd(y, t):
    s = y.shape[0]
    tm = _tile(s, ROW_TILE_D)

    def body(y_ref, t_ref, dy_ref, acc_ref):
        @pl.when(pl.program_id(0) == 0)
        def _():
            acc_ref[...] = jnp.zeros_like(acc_ref)

        e = y_ref[...] - t_ref[...]
        dy_ref[...] = e * (1.0 / D)
        acc_ref[0:1, :] += jnp.sum(e * e, axis=0, keepdims=True) * (0.5 / D)

    return pl.pallas_call(body, name="loss", out_shape=(SDS((s, D), F32), SDS((8, D), F32)), grid=(s // tm,),
                          in_specs=[_rows(tm, D), _rows(tm, D)], out_specs=(_rows(tm, D), _const((8, D))),
                          compiler_params=_cp(1))(y, t)


def _epi_bwd_rows(dv, mv, g, ngb, dm_ref, acc_ref, row):
    rm = lax.rsqrt(jnp.mean(mv * mv, axis=-1, keepdims=True) + RMS_EPS)
    mn = mv * rm
    dmn = dv * (g * ngb)
    dm = rm * (dmn - mn * jnp.mean(dmn * mn, axis=-1, keepdims=True))
    dm_ref[...] = dm.astype(dm_ref.dtype)
    t = jnp.sum(dv * mn, axis=0, keepdims=True)
    acc_ref[row:row + 1, :] += t * ngb
    acc_ref[row + 1:row + 2, :] += t * g
    acc_ref[row + 2:row + 3, :] += jnp.sum(dm, axis=0, keepdims=True)


def _epi_bwd(name, dxo, m, vec):
    s = dxo.shape[0]
    tm = _tile(s, ROW_TILE_D)

    def body(d_ref, m_ref, v_ref, dm_ref, acc_ref):
        @pl.when(pl.program_id(0) == 0)
        def _():
            acc_ref[...] = jnp.zeros_like(acc_ref)

        _epi_bwd_rows(d_ref[...], m_ref[...] + v_ref[2:3, :], v_ref[0:1, :], v_ref[1:2, :], dm_ref, acc_ref, 0)

    return pl.pallas_call(body, name=name, out_shape=(SDS((s, D), BF), SDS((8, D), F32)), grid=(s // tm,),
                          in_specs=[_rows(tm, D), _rows(tm, D), _const((8, D))],
                          out_specs=(_rows(tm, D), _const((8, D))), compiler_params=_cp(1))(dxo, m, vec)


def _pro_epi_bwd(name, dh, x, dxo, m, vec):
    s = x.shape[0]
    tm = _tile(s, ROW_TILE_D)

    def body(dh_ref, x_ref, d_ref, m_ref, v_ref, dx_ref, dm_ref, acc_ref):
        @pl.when(pl.program_id(0) == 0)
        def _():
            acc_ref[...] = jnp.zeros_like(acc_ref)

        dx = _pro_bwd_rows(dh_ref[...].astype(F32), x_ref[...], d_ref[...], v_ref[0:1, :], v_ref[1:2, :], acc_ref)
        dx_ref[...] = dx
        _epi_bwd_rows(dx, m_ref[...] + v_ref[4:5, :], v_ref[2:3, :], v_ref[3:4, :], dm_ref, acc_ref, 3)

    return pl.pallas_call(body, name=name, out_shape=(SDS((s, D), F32), SDS((s, D), BF), SDS((8, D), F32)),
                          grid=(s // tm,),
                          in_specs=[_rows(tm, D), _rows(tm, D), _rows(tm, D), _rows(tm, D), _const((8, D))],
                          out_specs=(_rows(tm, D), _rows(tm, D), _const((8, D))),
                          compiler_params=_cp(1))(dh, x, dxo, m, vec)


def _pro_bwd_rows(dhv, xv, dxo, ng, sc, acc_ref):
    sc1 = 1.0 + sc
    r = lax.rsqrt(jnp.mean(xv * xv, axis=-1, keepdims=True) + RMS_EPS)
    xn = xv * r
    dxn = dhv * (ng * sc1)
    t = jnp.sum(dhv * xn, axis=0, keepdims=True)
    acc_ref[0:1, :] += jnp.sum(dhv, axis=0, keepdims=True)
    acc_ref[1:2, :] += t * ng
    acc_ref[2:3, :] += t * sc1
    return dxo + r * (dxn - xn * jnp.mean(dxn * xn, axis=-1, keepdims=True))


def _pro_bwd(name, dh, x, dxo, vec):
    s = x.shape[0]
    tm = _tile(s, ROW_TILE_D)

    def body(dh_ref, x_ref, d_ref, v_ref, dx_ref, acc_ref):
        @pl.when(pl.program_id(0) == 0)
        def _():
            acc_ref[...] = jnp.zeros_like(acc_ref)

        dx_ref[...] = _pro_bwd_rows(dh_ref[...].astype(F32), x_ref[...], d_ref[...], v_ref[0:1, :], v_ref[1:2, :],
                                    acc_ref)

    return pl.pallas_call(body, name=name, out_shape=(SDS((s, D), F32), SDS((8, D), F32)), grid=(s // tm,),
                          in_specs=[_rows(tm, D), _rows(tm, D), _rows(tm, D), _const((8, D))],
                          out_specs=(_rows(tm, D), _const((8, D))), compiler_params=_cp(1))(dh, x, dxo, vec)


def _carry_down(buf, tm, first):
    @pl.when(first)
    def _():
        buf[0:HALO, :] = jnp.zeros((HALO, buf.shape[1]), F32)

    @pl.when(jnp.logical_not(first))
    def _():
        buf[0:HALO, :] = buf[tm:tm + HALO, :]


def _carry_up(buf, tm, first, halo=HALO):
    @pl.when(first)
    def _():
        buf[tm:tm + halo, :] = jnp.zeros((halo, buf.shape[1]), F32)

    @pl.when(jnp.logical_not(first))
    def _():
        buf[tm:tm + halo, :] = buf[0:halo, :]


def _ffn_act_fwd(name, up, vec):
    s = up.shape[0]
    tm = _tile(s, ROW_TILE_F)
    rb_n = _tile(tm, 64)
    cw = 256

    def body(g_ref, v_ref, w_ref, a_ref, carry):
        @pl.when(pl.program_id(0) == 0)
        def _():
            carry[...] = jnp.zeros_like(carry)

        for cc in range(F // cw):
            cols = slice(cc * cw, (cc + 1) * cw)
            w0, w1, w2, b = w_ref[0:1, cols], w_ref[1:2, cols], w_ref[2:3, cols], w_ref[3:4, cols]

            def rb_body(rb, prev8):
                rows = pl.ds(pl.multiple_of(rb * rb_n, rb_n), rb_n)
                g = g_ref[rows, cols].astype(F32)
                gc = w2 * g + w1 * _down(prev8, g, 1) + w0 * _down(prev8, g, 2) + b
                a_ref[rows, cols] = (gc * _sigmoid(gc) * v_ref[rows, cols].astype(F32)).astype(a_ref.dtype)
                return g[rb_n - 8:rb_n]

            carry[:, cols] = lax.fori_loop(0, tm // rb_n, rb_body, carry[:, cols])

    return pl.pallas_call(body, name=name, out_shape=SDS((s, F), BF), grid=(s // tm,),
                          in_specs=[_rows(tm, F, 0), _rows(tm, F, 1), _const((8, F))], out_specs=_rows(tm, F),
                          scratch_shapes=[pltpu.VMEM((HALO, F), F32)], compiler_params=_cp(1))(up, up, vec)


def _prev_halo_spec(nt, tm, width, col):
    per = tm // HALO_BF
    return pl.BlockSpec((HALO_BF, width), lambda i: (jnp.maximum((nt - 1 - i) * per - 1, 0), col))


def _ffn_act_bwd(name, up, da, vec):
    s = up.shape[0]
    tm = _tile(s, ROW_TILE_F)
    nt = s // tm
    rev = lambda col: pl.BlockSpec((tm, F), lambda i: (nt - 1 - i, col))

    rb_n = _tile(tm, 64)
    nrb = tm // rb_n
    cw = 128

    def body(g_ref, gp_ref, v_ref, da_ref, w_ref, a_ref, dup_ref, acc_ref, carry):
        step = pl.program_id(0)

        @pl.when(step == 0)
        def _():
            acc_ref[...] = jnp.zeros_like(acc_ref)
            carry[...] = jnp.zeros_like(carry)

        for cc in range(F // cw):
            cols = slice(cc * cw, (cc + 1) * cw)
            w0, w1, w2, b = w_ref[0:1, cols], w_ref[1:2, cols], w_ref[2:3, cols], w_ref[3:4, cols]
            halo = jnp.where(step < nt - 1, gp_ref[:, cols].astype(F32)[HALO_BF - 8:HALO_BF], 0.0)

            def rb_body(it, st):
                nxt8, ab, a0, a1, a2 = st
                rb = nrb - 1 - it
                r0 = pl.multiple_of(rb * rb_n, rb_n)
                rows = pl.ds(r0, rb_n)
                g = g_ref[rows, cols].astype(F32)
                ra = pl.multiple_of(jnp.maximum(r0 - HALO_BF, 0), HALO_BF)
                above = g_ref[pl.ds(ra, HALO_BF), cols].astype(F32)[HALO_BF - 8:HALO_BF]
                prev8 = jnp.where(rb == 0, halo, above)
                g1 = _down(prev8, g, 1)
                g2 = _down(prev8, g, 2)
                gc = w2 * g + w1 * g1 + w0 * g2 + b
                sg = _sigmoid(gc)
                sl = gc * sg
                val = v_ref[rows, cols].astype(F32)
                dav = da_ref[rows, cols].astype(F32)
                a_ref[rows, cols] = (sl * val).astype(a_ref.dtype)
                dup_ref[rows, F + cc * cw:F + (cc + 1) * cw] = (dav * sl).astype(dup_ref.dtype)
                dgc = (dav * val) * (sg + sl * (1.0 - sg))
                dup_ref[rows, cols] = (w2 * dgc + w1 * _up(dgc, nxt8, 1) + w0 * _up(dgc, nxt8, 2)).astype(dup_ref.dtype)
                return (dgc[0:8], ab + _fold8(dgc), a0 + _fold8(g2 * dgc), a1 + _fold8(g1 * dgc),
                        a2 + _fold8(g * dgc))

            z = jnp.zeros((8, cw), F32)
            nxt8, ab, a0, a1, a2 = lax.fori_loop(0, nrb, rb_body, (carry[:, cols], z, z, z, z))
            carry[:, cols] = nxt8
            acc_ref[0:1, cols] += jnp.sum(ab, axis=0, keepdims=True)
            acc_ref[1:2, cols] += jnp.sum(a0, axis=0, keepdims=True)
            acc_ref[2:3, cols] += jnp.sum(a1, axis=0, keepdims=True)
            acc_ref[3:4, cols] += jnp.sum(a2, axis=0, keepdims=True)

    return pl.pallas_call(
        body, name=name, out_shape=(SDS((s, F), BF), SDS((s, 2 * F), BF), SDS((8, F), F32)), grid=(nt,),
        in_specs=[rev(0), _prev_halo_spec(nt, tm, F, 0), rev(1), rev(0), _const((8, F))],
        out_specs=(rev(0), pl.BlockSpec((tm, 2 * F), lambda i: (nt - 1 - i, 0)), _const((8, F))),
        scratch_shapes=[pltpu.VMEM((HALO, F), F32)], compiler_params=_cp(1))(up, up, up, da, vec)


def _sc_act_fwd(name, p, vec):
    s = p.shape[0]
    tm = _tile(s, ROW_TILE_D)

    def body(b_ref, c_ref, h_ref, w_ref, z_ref, buf):
        _carry_down(buf, tm, pl.program_id(0) == 0)
        buf[HALO:HALO + tm, :] = c_ref[...].astype(F32) * h_ref[...].astype(F32)
        u = (w_ref[2:3, :] * buf[HALO:HALO + tm, :] + w_ref[1:2, :] * buf[HALO - 1:HALO - 1 + tm, :]
             + w_ref[0:1, :] * buf[HALO - 2:HALO - 2 + tm, :])
        z_ref[...] = (b_ref[...].astype(F32) * u).astype(z_ref.dtype)

    return pl.pallas_call(body, name=name, out_shape=SDS((s, D), BF), grid=(s // tm,),
                          in_specs=[_rows(tm, D, 0), _rows(tm, D, 1), _rows(tm, D, 2), _const((8, D))],
                          out_specs=_rows(tm, D), scratch_shapes=[pltpu.VMEM((tm + HALO, D), F32)],
                          compiler_params=_cp(1))(p, p, p, vec)


def _sc_act_bwd(name, p, dz, vec):
    s = p.shape[0]
    tm = _tile(s, ROW_TILE_D)
    nt = s // tm
    rev = lambda col: pl.BlockSpec((tm, D), lambda i: (nt - 1 - i, col))

    def body(b_ref, c_ref, h_ref, cp_ref, hp_ref, dz_ref, w_ref, z_ref, dp_ref, acc_ref, pbuf, nbuf):
        step = pl.program_id(0)

        @pl.when(step == 0)
        def _():
            acc_ref[...] = jnp.zeros_like(acc_ref)

        cg = c_ref[...].astype(F32)
        hin = h_ref[...].astype(F32)
        bg = b_ref[...].astype(F32)
        q = cg * hin
        pbuf[0:HALO_BF, :] = jnp.where(step < nt - 1, cp_ref[...].astype(F32) * hp_ref[...].astype(F32), 0.0)
        pbuf[HALO_BF:HALO_BF + tm, :] = q
        u = (w_ref[2:3, :] * q + w_ref[1:2, :] * pbuf[HALO_BF - 1:HALO_BF - 1 + tm, :]
             + w_ref[0:1, :] * pbuf[HALO_BF - 2:HALO_BF - 2 + tm, :])
        dzv = dz_ref[...].astype(F32)
        z_ref[...] = (bg * u).astype(z_ref.dtype)
        dp_ref[:, 0:D] = (dzv * u).astype(dp_ref.dtype)
        du = dzv * bg
        _carry_up(nbuf, tm, step == 0)
        nbuf[0:tm, :] = du
        d1 = nbuf[1:1 + tm, :]
        d2 = nbuf[2:2 + tm, :]
        dq = w_ref[2:3, :] * du + w_ref[1:2, :] * d1 + w_ref[0:1, :] * d2
        dp_ref[:, D:2 * D] = (dq * hin).astype(dp_ref.dtype)
        dp_ref[:, 2 * D:3 * D] = (dq * cg).astype(dp_ref.dtype)
        acc_ref[0:1, :] += jnp.sum(q * d2, axis=0, keepdims=True)
        acc_ref[1:2, :] += jnp.sum(q * d1, axis=0, keepdims=True)
        acc_ref[2:3, :] += jnp.sum(q * du, axis=0, keepdims=True)

    return pl.pallas_call(
        body, name=name, out_shape=(SDS((s, D), BF), SDS((s, 3 * D), BF), SDS((8, D), F32)), grid=(nt,),
        in_specs=[rev(0), rev(1), rev(2), _prev_halo_spec(nt, tm, D, 1), _prev_halo_spec(nt, tm, D, 2), rev(0),
                  _const((8, D))],
        out_specs=(rev(0), pl.BlockSpec((tm, 3 * D), lambda i: (nt - 1 - i, 0)), _const((8, D))),
        scratch_shapes=[pltpu.VMEM((tm + HALO_BF, D), F32), pltpu.VMEM((tm + HALO, D), F32)],
        compiler_params=_cp(1))(p, p, p, p, p, dz, vec)


CF_ROW_BLOCK = 32
CF_LANES = 128


def _tap_conv_block(ext, rb_n, tap_of_offset, t_ref, cols, init, u=None, accs=None):
    n = ext.shape[0]
    out = init
    for b in range(8):
        rolled = ext if b == 0 else pltpu.roll(ext, n - b, 0)
        for a in range(n // 8):
            k = tap_of_offset(8 * a + b)
            if k is None:
                continue
            sl = rolled[8 * a:8 * a + rb_n]
            out = out + t_ref[k:k + 1, cols] * sl
            if accs is not None:
                accs[k] = accs[k] + _fold8(u * sl)
    return out


def _cf_act_fwd(name, p, taps, vec):
    s = p.shape[0]
    tm = _tile(s, ROW_TILE_D)
    rb_n = _tile(tm, CF_ROW_BLOCK)
    base = CF_HALO - (CF_TAPS - 1)
    tap_of = lambda off: off - base if 0 <= off - base < CF_TAPS else None

    def body(a_ref, g_ref, t_ref, v_ref, w_ref, cv_ref, buf):
        first = pl.program_id(0) == 0

        @pl.when(first)
        def _():
            buf[0:CF_HALO, :] = jnp.zeros((CF_HALO, D), F32)

        @pl.when(jnp.logical_not(first))
        def _():
            buf[0:CF_HALO, :] = buf[tm:tm + CF_HALO, :]

        a = a_ref[...].astype(F32) + v_ref[0:1, :]
        g = g_ref[...].astype(F32) + v_ref[1:2, :]
        buf[CF_HALO:CF_HALO + tm, :] = a * _sigmoid(g)
        for ci in range(D // CF_LANES):
            cols = slice(ci * CF_LANES, (ci + 1) * CF_LANES)

            def rb_body(rb, carry):
                r0 = pl.multiple_of(rb * rb_n, rb_n)
                ext = buf[pl.ds(r0, rb_n + CF_HALO), cols]
                init = jnp.zeros((rb_n, CF_LANES), F32) + v_ref[2:3, cols]
                cv_ref[pl.ds(r0, rb_n), cols] = _tap_conv_block(ext, rb_n, tap_of, t_ref, cols, init)
                return carry

            lax.fori_loop(0, tm // rb_n, rb_body, 0)
        cv = cv_ref[...]
        mu = jnp.mean(cv, axis=-1, keepdims=True)
        cc = cv - mu
        rstd = lax.rsqrt(jnp.mean(cc * cc, axis=-1, keepdims=True) + LN_EPS)
        ln = cc * rstd * v_ref[3:4, :] + v_ref[4:5, :]
        w_ref[...] = (ln * _sigmoid(ln)).astype(w_ref.dtype)

    return pl.pallas_call(body, name=name, out_shape=(SDS((s, D), BF), SDS((s, D), F32)), grid=(s // tm,),
                          in_specs=[_rows(tm, D, 0), _rows(tm, D, 1), _const((32, D)), _const((8, D))],
                          out_specs=(_rows(tm, D), _rows(tm, D)),
                          scratch_shapes=[pltpu.VMEM((tm + CF_HALO, D), F32)], compiler_params=_cp(1))(p, p, taps, vec)


def _cf_act_bwd(name, p, cv, dw, taps, vec):
    s = p.shape[0]
    tm = _tile(s, ROW_TILE_D)
    nt = s // tm
    rev = lambda col: pl.BlockSpec((tm, D), lambda i: (nt - 1 - i, col))
    rb_n = _tile(tm, CF_ROW_BLOCK)
    tap_of = lambda off: CF_TAPS - 1 - off if off < CF_TAPS else None

    def body(a_ref, g_ref, cv_ref, dw_ref, t_ref, v_ref, dp_ref, tacc_ref, acc_ref, nbuf, ubuf, dubuf):
        step = pl.program_id(0)

        @pl.when(step == 0)
        def _():
            acc_ref[...] = jnp.zeros_like(acc_ref)
            tacc_ref[...] = jnp.zeros_like(tacc_ref)

        a = a_ref[...].astype(F32) + v_ref[0:1, :]
        g = g_ref[...].astype(F32) + v_ref[1:2, :]
        sg = _sigmoid(g)
        ubuf[...] = a * sg
        cvv = cv_ref[...]
        mu = jnp.mean(cvv, axis=-1, keepdims=True)
        cc = cvv - mu
        rstd = lax.rsqrt(jnp.mean(cc * cc, axis=-1, keepdims=True) + LN_EPS)
        vhat = cc * rstd
        ln = vhat * v_ref[3:4, :] + v_ref[4:5, :]
        s2 = _sigmoid(ln)
        dln = dw_ref[...].astype(F32) * (s2 * (1.0 + ln * (1.0 - s2)))
        acc_ref[1:2, :] += jnp.sum(dln * vhat, axis=0, keepdims=True)
        acc_ref[2:3, :] += jnp.sum(dln, axis=0, keepdims=True)
        dvh = dln * v_ref[3:4, :]
        dcv = rstd * (dvh - jnp.mean(dvh, axis=-1, keepdims=True)
                      - vhat * jnp.mean(dvh * vhat, axis=-1, keepdims=True))
        acc_ref[0:1, :] += jnp.sum(dcv, axis=0, keepdims=True)
        _carry_up(nbuf, tm, step == 0, CF_HALO)
        nbuf[0:tm, :] = dcv
        for ci in range(D // CF_LANES):
            cols = slice(ci * CF_LANES, (ci + 1) * CF_LANES)

            def rb_body(rb, accs):
                r0 = pl.multiple_of(rb * rb_n, rb_n)
                ext = nbuf[pl.ds(r0, rb_n + CF_HALO), cols]
                accs = list(accs)
                dubuf[pl.ds(r0, rb_n), cols] = _tap_conv_block(
                    ext, rb_n, tap_of, t_ref, cols, jnp.zeros((rb_n, CF_LANES), F32),
                    ubuf[pl.ds(r0, rb_n), cols], accs)
                return tuple(accs)

            z = jnp.zeros((8, CF_LANES), F32)
            accs = lax.fori_loop(0, tm // rb_n, rb_body, tuple([z] * CF_TAPS))
            for k in range(CF_TAPS):
                tacc_ref[k:k + 1, cols] += jnp.sum(accs[k], axis=0, keepdims=True)
        a = a_ref[...].astype(F32) + v_ref[0:1, :]
        sg = _sigmoid(g_ref[...].astype(F32) + v_ref[1:2, :])
        da = dubuf[...] * sg
        dg = da * a * (1.0 - sg)
        dp_ref[:, 0:D] = da.astype(dp_ref.dtype)
        dp_ref[:, D:2 * D] = dg.astype(dp_ref.dtype)
        acc_ref[3:4, :] += jnp.sum(da, axis=0, keepdims=True)
        acc_ref[4:5, :] += jnp.sum(dg, axis=0, keepdims=True)

    return pl.pallas_call(
        body, name=name, out_shape=(SDS((s, 2 * D), BF), SDS((32, D), F32), SDS((8, D), F32)), grid=(nt,),
        in_specs=[rev(0), rev(1), rev(0), rev(0), _const((32, D)), _const((8, D))],
        out_specs=(pl.BlockSpec((tm, 2 * D), lambda i: (nt - 1 - i, 0)), _const((32, D)), _const((8, D))),
        scratch_shapes=[pltpu.VMEM((tm + CF_HALO, D), F32), pltpu.VMEM((tm, D), F32), pltpu.VMEM((tm, D), F32)],
        compiler_params=_cp(1))(p, p, cv, dw, taps, vec)


def _inv_count(row0, tm, window):
    t = row0 + lax.broadcasted_iota(jnp.int32, (tm, 1), 0)
    return 1.0 / jnp.minimum(t + 1, window).astype(F32)


def _pool_fwd(name, x, w, vec):
    s = x.shape[0]
    tm = _tile(s, ROW_TILE_D)
    G = POOL_GROUP

    def body(x_ref, w_ref, v_ref, pl_ref, m_ref, buf):
        i = pl.program_id(0)

        @pl.when(i == 0)
        def _():
            buf[0:POOL_HALO, :] = jnp.zeros((POOL_HALO, D), F32)

        @pl.when(i > 0)
        def _():
            buf[0:POOL_HALO, :] = buf[tm:tm + POOL_HALO, :]

        xv = x_ref[...]
        r = lax.rsqrt(jnp.mean(xv * xv, axis=-1, keepdims=True) + RMS_EPS)
        buf[POOL_HALO:POOL_HALO + tm, :] = xv * r * (v_ref[0:1, :] * (1.0 + v_ref[1:2, :])) + v_ref[2:3, :]
        for gi, win in enumerate(POOL_WINDOWS):
            cols = slice(gi * G, (gi + 1) * G)
            acc = buf[POOL_HALO:POOL_HALO + tm, cols]
            hg = acc
            for j in range(1, win):
                acc = acc + buf[POOL_HALO - j:POOL_HALO - j + tm, cols]
            pooled = (acc * _inv_count(i * tm, tm, win) - hg).astype(BF)
            pl_ref[:, cols] = pooled
            yg = jnp.dot(pooled, w_ref[gi], preferred_element_type=F32)
            m_ref[:, cols] = (yg + v_ref[3:4, cols]) * v_ref[4:5, cols]

    return pl.pallas_call(body, name=name, out_shape=(SDS((s, D), BF), SDS((s, D), F32)), grid=(s // tm,),
                          in_specs=[_rows(tm, D), _const((4, G, G)), _const((8, D))],
                          out_specs=(_rows(tm, D), _rows(tm, D)),
                          scratch_shapes=[pltpu.VMEM((tm + POOL_HALO, D), F32)], compiler_params=_cp(1))(x, w, vec)


def _pool_bwd(name, pooled, dm, w, vec):
    s = pooled.shape[0]
    tm = _tile(s, ROW_TILE_D)
    nt = s // tm
    G = POOL_GROUP
    rev = pl.BlockSpec((tm, D), lambda i: (nt - 1 - i, 0))

    def body(p_ref, dm_ref, w_ref, v_ref, dh_ref, dw_ref, acc_ref, nbuf):
        step = pl.program_id(0)
        row0 = (nt - 1 - step) * tm

        @pl.when(step == 0)
        def _():
            acc_ref[...] = jnp.zeros_like(acc_ref)
            dw_ref[...] = jnp.zeros_like(dw_ref)

        _carry_up(nbuf, tm, step == 0, POOL_HALO)
        dmv = dm_ref[...].astype(F32)
        acc_ref[0:1, :] += jnp.sum(dmv, axis=0, keepdims=True) * v_ref[1:2, :]
        dps = []
        for gi, win in enumerate(POOL_WINDOWS):
            cols = slice(gi * G, (gi + 1) * G)
            pg = p_ref[:, cols]
            yb = jnp.dot(pg, w_ref[gi], preferred_element_type=F32) + v_ref[0:1, cols]
            acc_ref[1:2, cols] += jnp.sum(dmv[:, cols] * yb, axis=0, keepdims=True)
            dy = (dmv[:, cols] * v_ref[1:2, cols]).astype(BF)
            dw_ref[gi] += lax.dot_general(pg, dy, TN, preferred_element_type=F32)
            dpg = lax.dot_general(dy, w_ref[gi], NT, preferred_element_type=F32)
            dps.append(dpg)
            nbuf[0:tm, cols] = dpg * _inv_count(row0, tm, win)
        for gi, win in enumerate(POOL_WINDOWS):
            cols = slice(gi * G, (gi + 1) * G)
            acc = nbuf[0:tm, cols]
            for j in range(1, win):
                acc = acc + nbuf[j:j + tm, cols]
            dh_ref[:, cols] = acc - dps[gi]

    return pl.pallas_call(
        body, name=name, out_shape=(SDS((s, D), F32), SDS((4, G, G), F32), SDS((8, D), F32)), grid=(nt,),
        in_specs=[rev, rev, _const((4, G, G)), _const((8, D))],
        out_specs=(rev, _const((4, G, G)), _const((8, D))),
        scratch_shapes=[pltpu.VMEM((tm + POOL_HALO, D), F32)], compiler_params=_cp(1))(pooled, dm, w, vec)


def _row_tile_2d(rows, width, bytes_per_row_elem=4, budget=2 * 1024 * 1024):
    t = max(8, budget // (width * bytes_per_row_elem))
    t = min(rows, 1 << (t.bit_length() - 1))
    while rows % t:
        t //= 2
    return t


def _add_slots(name, r):
    nl, _, k, n = r.shape
    tk = _row_tile_2d(k, n, 16)

    def body(r_ref, o_ref):
        f = lambda i: r_ref[i].astype(F32)
        o_ref[...] = ((f(7) + f(6)) + (f(0) + f(1))) + ((f(2) + f(3)) + (f(4) + f(5)))

    return pl.pallas_call(body, name=name, out_shape=SDS((nl, k, n), F32), grid=(nl, k // tk),
                          in_specs=[pl.BlockSpec((None, N_DEV, tk, n), lambda l, i: (l, 0, i, 0))],
                          out_specs=pl.BlockSpec((None, tk, n), lambda l, i: (l, i, 0)),
                          compiler_params=_cp(2))(r)


def _adamw(name, w, g, m, v):
    rows, width = w.shape
    tm = _row_tile_2d(rows, width, 4, 1024 * 1024)
    c1 = 1.0 - ADAM_B1 ** ADAM_STEP
    c2 = 1.0 - ADAM_B2 ** ADAM_STEP

    def body(w_ref, g_ref, m_ref, v_ref, d_ref, nm_ref, nv_ref):
        gv = g_ref[...]
        nm = ADAM_B1 * m_ref[...] + (1.0 - ADAM_B1) * gv
        nv = ADAM_B2 * v_ref[...] + (1.0 - ADAM_B2) * (gv * gv)
        nm_ref[...] = nm
        nv_ref[...] = nv
        d_ref[...] = -ADAM_LR * ((nm / c1) / (jnp.sqrt(nv / c2) + ADAM_EPS) + ADAM_WD * w_ref[...])

    spec = _rows(tm, width)
    sds = SDS((rows, width), F32)
    return pl.pallas_call(body, name=name, out_shape=(sds, sds, sds), grid=(rows // tm,),
                          in_specs=[spec] * 4, out_specs=(spec,) * 3, compiler_params=_cp(1))(w, g, m, v)


def _mod_fwd(c16, w_mod, b_sh):
    n = w_mod.shape[2]
    tn = _tile(n, 512)

    def body(c_ref, w_ref, b_ref, o_ref):
        cv = c_ref[...]
        ca = (cv * _sigmoid(cv)).astype(BF)
        o_ref[...] = jnp.dot(ca, w_ref[...].astype(BF), preferred_element_type=F32) + b_ref[0:1, :]

    return pl.pallas_call(body, name="mod_fwd", out_shape=SDS((DEPTH, 16, n), F32), grid=(DEPTH, n // tn),
                          in_specs=[_const((16, D)), pl.BlockSpec((None, D, tn), lambda l, j: (l, 0, j)),
                                    pl.BlockSpec((None, 8, tn), lambda l, j: (l, 0, j))],
                          out_specs=pl.BlockSpec((None, 16, tn), lambda l, j: (l, 0, j)),
                          compiler_params=_cp(2))(c16, w_mod, b_sh)


def _mod_bwd(c16, dmod):
    n = dmod.shape[2]
    tn = _tile(n, 512)

    def body(c_ref, d_ref, o_ref):
        cv = c_ref[...]
        ca = (cv * _sigmoid(cv)).astype(BF)
        o_ref[...] = lax.dot_general(ca, d_ref[...].astype(BF), TN, preferred_element_type=F32)

    return pl.pallas_call(body, name="mod_bwd", out_shape=SDS((DEPTH, D, n), F32), grid=(DEPTH, n // tn),
                          in_specs=[_const((16, D)), pl.BlockSpec((None, 16, tn), lambda l, j: (l, 0, j))],
                          out_specs=pl.BlockSpec((None, D, tn), lambda l, j: (l, 0, j)),
                          compiler_params=_cp(2))(c16, dmod)


def _place():
    x, y, c = lax.axis_index("x"), lax.axis_index("y"), lax.axis_index("c")
    other_chips = [(1 - x, y), (x, 1 - y), (1 - x, 1 - y)]
    return x, y, c, other_chips


def _allgather_small(name, v, with_sum):
    m, n = v.shape

    def body(x_ref, out_ref, *rest):
        if with_sum:
            sum_ref, send_sems, recv_sems, local_sem = rest
        else:
            send_sems, recv_sems, local_sem = rest
        x, y, c, chips = _place()
        me, sibling = (x, y, c), (x, y, 1 - c)

        def rows(px, py, pc):
            return out_ref.at[pl.ds((4 * px + 2 * py + pc) * m, m), :]

        def copy(k, block, to, src=None):
            return pltpu.make_async_remote_copy(
                src_ref=rows(*block) if src is None else src, dst_ref=rows(*block),
                send_sem=send_sems.at[k], recv_sem=recv_sems.at[k], device_id=to, device_id_type=MESH)

        mine = pltpu.make_async_copy(x_ref, rows(*me), local_sem)
        mine.start()
        first = [copy(0, me, sibling, src=x_ref)]
        first += [copy(1 + j, me, (*chip, c), src=x_ref) for j, chip in enumerate(chips)]
        for cp in first:
            cp.start()
        passed = [copy(4 + j, (*chip, c), sibling) for j, chip in enumerate(chips)]
        for j, chip in enumerate(chips):
            copy(1 + j, (*chip, c), me).wait_recv()
            passed[j].start()
        copy(0, sibling, me).wait_recv()
        for j, chip in enumerate(chips):
            copy(4 + j, (*chip, 1 - c), me).wait_recv()
        for cp in first + passed:
            cp.wait_send()
        mine.wait()
        if with_sum:
            acc = out_ref[0:m, :]
            for k in range(1, N_DEV):
                acc = acc + out_ref[k * m:(k + 1) * m, :]
            sum_ref[...] = acc

    vm = pl.BlockSpec(memory_space=pltpu.VMEM)
    out_shape = [SDS((N_DEV * m, n), F32)] + ([SDS((m, n), F32)] if with_sum else [])
    res = pl.pallas_call(
        body, name=name, out_shape=tuple(out_shape), in_specs=[vm], out_specs=tuple([vm] * len(out_shape)),
        scratch_shapes=[pltpu.SemaphoreType.DMA((7,)), pltpu.SemaphoreType.DMA((7,)), pltpu.SemaphoreType.DMA],
        compiler_params=pltpu.CompilerParams(vmem_limit_bytes=VMEM_LIMIT_MB * 1024 * 1024))(v)
    return res if with_sum else res[0]


HBM = pl.BlockSpec(memory_space=pltpu.HBM)


def _sem_scratch(n_remote, n_local):
    return [pltpu.SemaphoreType.DMA((n_remote,)), pltpu.SemaphoreType.DMA((n_remote,)),
            pltpu.SemaphoreType.DMA((n_local,))]


DMA_PIECE_BYTES = 1 << 20


def _pieces(src, dst):
    *lead, rows, n = src.shape
    nsplit = max(1, min(rows // 16, (rows * n * jnp.dtype(src.dtype).itemsize) // DMA_PIECE_BYTES))
    while rows % nsplit or (rows // nsplit) % 16:
        nsplit -= 1
    size = rows // nsplit
    out = []
    for idx in itertools.product(*[range(d) for d in lead]):
        for i in range(nsplit):
            sl = tuple(idx) + (pl.ds(i * size, size),)
            out.append((src.at[sl], dst.at[sl]))
    return out


def _local_copies(src, dst, sem):
    return ([pltpu.make_async_copy(s_, d_, sem) for s_, d_ in _pieces(src, dst)],
            pltpu.make_async_copy(src, dst, sem))


def _remote_copies(src, dst, send_sem, recv_sem, to):
    mk = lambda s_, d_: pltpu.make_async_remote_copy(src_ref=s_, dst_ref=d_, send_sem=send_sem, recv_sem=recv_sem,
                                                     device_id=to, device_id_type=MESH)
    return [mk(s_, d_) for s_, d_ in _pieces(src, dst)], mk(src, dst)


def _gather_weights(shards):
    nq = len(shards)

    def body(*refs):
        ins, outs = refs[:nq], refs[nq:2 * nq]
        send_sems, recv_sems, local_sems = refs[2 * nq:]
        x, y, c, chips = _place()
        me_chip = 2 * x + y
        started, local_all, send_all, recv_all = [], [], [], []
        for q in range(nq):
            cps, whole = _local_copies(ins[q], outs[q].at[:, me_chip], local_sems.at[q])
            started += cps
            local_all.append(whole)
            for r, (px, py) in enumerate(chips):
                k = 3 * q + r
                cps, whole = _remote_copies(ins[q], outs[q].at[:, me_chip], send_sems.at[k], recv_sems.at[k], (px, py, c))
                started += cps
                send_all.append(whole)
                recv_all.append(_remote_copies(ins[q], outs[q].at[:, 2 * px + py], send_sems.at[k], recv_sems.at[k],
                                               (px, py, c))[1])
        for cp in started:
            cp.start()
        for cp in recv_all:
            cp.wait_recv()
        for cp in send_all:
            cp.wait_send()
        for cp in local_all:
            cp.wait()

    out_shape = tuple(SDS((s.shape[0], N_CHIPS) + s.shape[1:], s.dtype) for s in shards)
    return pl.pallas_call(body, name="gather_weights", out_shape=out_shape, in_specs=[HBM] * nq,
                          out_specs=tuple([HBM] * nq), scratch_shapes=_sem_scratch(3 * nq, nq))(*shards)


def _grad_exchange(gs):
    nq = len(gs)

    def body(*refs):
        ins, outs = refs[:nq], refs[nq:2 * nq]
        send_sems, recv_sems, local_sems = refs[2 * nq:]
        x, y, c, chips = _place()
        peers = [(2 * r + e, (px, py, c if e == 0 else 1 - c)) for r, (px, py) in enumerate(chips) for e in (0, 1)]
        peers.append((6, (x, y, 1 - c)))
        started, local_all, remote_all = [], [], []
        for q in range(nq):
            cps, whole = _local_copies(ins[q].at[:, 2 * x + y, c], outs[q].at[:, 7], local_sems.at[q])
            started += cps
            local_all.append(whole)
            for slot, (px, py, pc) in peers:
                k = 7 * q + slot
                cps, whole = _remote_copies(ins[q].at[:, 2 * px + py, pc], outs[q].at[:, slot], send_sems.at[k],
                                            recv_sems.at[k], (px, py, pc))
                started += cps
                remote_all.append(whole)
        for cp in started:
            cp.start()
        for cp in remote_all:
            cp.wait_recv()
        for cp in remote_all:
            cp.wait_send()
        for cp in local_all:
            cp.wait()

    out_shape = tuple(SDS((g.shape[0], N_DEV, g.shape[3], g.shape[4]), g.dtype) for g in gs)
    return pl.pallas_call(body, name="grad_exchange", out_shape=out_shape, in_specs=[HBM] * nq,
                          out_specs=tuple([HBM] * nq), scratch_shapes=_sem_scratch(7 * nq, nq))(*gs)


def _pair_share(rs):
    nq = len(rs)

    def body(*refs):
        ins, outs = refs[:nq], refs[nq:2 * nq]
        send_sems, recv_sems, local_sems = refs[2 * nq:]
        x, y, c, _ = _place()
        started, local_all, send_all, recv_all = [], [], [], []
        for q in range(nq):
            cps, whole = _local_copies(ins[q], outs[q].at[:, c], local_sems.at[q])
            started += cps
            local_all.append(whole)
            cps, whole = _remote_copies(ins[q], outs[q].at[:, c], send_sems.at[q], recv_sems.at[q], (x, y, 1 - c))
            started += cps
            send_all.append(whole)
            recv_all.append(_remote_copies(ins[q], outs[q].at[:, 1 - c], send_sems.at[q], recv_sems.at[q],
                                           (x, y, 1 - c))[1])
        for cp in started:
            cp.start()
        for cp in recv_all:
            cp.wait_recv()
        for cp in send_all:
            cp.wait_send()
        for cp in local_all:
            cp.wait()

    out_shape = tuple(SDS((r.shape[0], 2) + r.shape[1:], r.dtype) for r in rs)
    return pl.pallas_call(body, name="grad_pair_share", out_shape=out_shape, in_specs=[HBM] * nq,
                          out_specs=tuple([HBM] * nq), scratch_shapes=_sem_scratch(nq, nq))(*rs)


def _pack(arrs, rows_multiple=8):
    flat = jnp.concatenate([a.astype(F32).reshape(-1) for a in arrs])
    pad = (-flat.shape[0]) % (128 * rows_multiple)
    return jnp.pad(flat, (0, pad)).reshape(-1, 128)


def _unpack(slab, shapes):
    flat = slab.reshape(-1)
    out, off = [], 0
    for shp in shapes:
        n = 1
        for d in shp:
            n *= d
        out.append(flat[off:off + n].reshape(shp))
        off += n
    return out


def _shard_last(a, chip, n):
    return lax.dynamic_slice_in_dim(a, chip * n, n, axis=a.ndim - 1)


def kernel(x, c, w_mod, b_mod, norm_g, sc_w_in, sc_conv, sc_w_out, pool_w, pool_b, pool_scale, cf_w_pw1, cf_b_pw1, cf_w_dw, cf_b_dw, cf_ln_g, cf_ln_b, cf_w_pw2, cf_b_pw2, ffn_w_up, ffn_conv, ffn_b_conv, ffn_w_down, loss_target, m_w_mod, m_b_mod, m_norm_g, m_sc_w_in, m_sc_conv, m_sc_w_out, m_pool_w, m_pool_b, m_pool_scale, m_cf_w_pw1, m_cf_b_pw1, m_cf_w_dw, m_cf_b_dw, m_cf_ln_g, m_cf_ln_b, m_cf_w_pw2, m_cf_b_pw2, m_ffn_w_up, m_ffn_conv, m_ffn_b_conv, m_ffn_w_down, v_w_mod, v_b_mod, v_norm_g, v_sc_w_in, v_sc_conv, v_sc_w_out, v_pool_w, v_pool_b, v_pool_scale, v_cf_w_pw1, v_cf_b_pw1, v_cf_w_dw, v_cf_b_dw, v_cf_ln_g, v_cf_ln_b, v_cf_w_pw2, v_cf_b_pw2, v_ffn_w_up, v_ffn_conv, v_ffn_b_conv, v_ffn_w_down):
    ax, ay, ac = lax.axis_index("x"), lax.axis_index("y"), lax.axis_index("c")
    chip = 2 * ax + ay
    dev = 4 * ax + 2 * ay + ac
    xs = x[0]
    target = loss_target[0]

    small_sharded = [norm_g, sc_conv, cf_b_pw1, cf_w_dw, cf_b_dw, cf_ln_g, cf_ln_b, cf_b_pw2, ffn_conv]
    slab = _pack([c] + small_sharded)
    gathered = _allgather_small("gather_small_params", slab, False).reshape(N_DEV, -1, 128)
    parts = [_unpack(gathered[d], [c.shape] + [a.shape for a in small_sharded]) for d in range(N_DEV)]
    c_all = jnp.concatenate([p[0] for p in parts], axis=0)
    full = [jnp.concatenate([parts[2 * j][1 + i] for j in range(N_CHIPS)], axis=-1)
            for i in range(len(small_sharded))]
    norm_g_f, sc_conv_f, cf_b_pw1_f, cf_w_dw_f, cf_b_dw_f, cf_ln_g_f, cf_ln_b_f, cf_b_pw2_f, ffn_conv_f = full
    c16 = jnp.pad(c_all, ((0, 8), (0, 0)))

    n_mod = w_mod.shape[2]
    b_sh = jnp.broadcast_to(_shard_last(b_mod, chip, n_mod)[:, None, :], (DEPTH, 8, n_mod))
    mod_part = _mod_fwd(c16, w_mod, b_sh)
    mod_g = _allgather_small("gather_mod", mod_part.reshape(DEPTH * 16, n_mod), False)
    mod_g = mod_g.reshape(N_DEV, DEPTH, 16, n_mod)
    mod_mine = jnp.concatenate(
        [lax.dynamic_index_in_dim(mod_g[2 * j], dev, axis=1, keepdims=False) for j in range(N_CHIPS)], axis=-1)
    mod = mod_mine.reshape(DEPTH, 6, D)

    shards = [sc_w_in.astype(BF), sc_w_out.astype(BF), pool_w[0].astype(BF), cf_w_pw1.astype(BF),
              cf_w_pw2.astype(BF), ffn_w_up.astype(BF), ffn_w_down.astype(BF)]
    w_in_f, w_out_f, pool_f, pw1_f, pw2_f, w_up_f, w_down_f = _gather_weights(shards)
    w_out_f = w_out_f.reshape(w_out_f.shape[0], D, D)
    pool_f = pool_f.reshape(4, POOL_GROUP, POOL_GROUP)
    pw2_f = pw2_f.reshape(1, D, D)
    w_down_f = w_down_f.reshape(DEPTH, F, D)

    zero_d = jnp.zeros((D,), F32)
    mods = [[mod[i, k] for k in range(6)] for i in range(DEPTH)]
    saved = []
    xcur = xs
    h_next = None
    for i in range(DEPTH):
        kind, j = i % 3, i // 3
        sh1, sc1, g1, sh2, sc2, g2 = mods[i]
        st = {"x0": xcur}
        m_bias = zero_d
        if kind != 1:
            h = h_next if h_next is not None else _pro_fwd(f"pro1_fwd_{i}", xcur, _vecs([norm_g_f[i, 0], sc1, sh1], D))
        if kind == 0:
            p = _mm_nn(f"sc_in_{i}", h, w_in_f, j, True, BF)
            z = _sc_act_fwd(f"sc_act_fwd_{i}", p, _vecs(list(sc_conv_f[j]), D))
            m = _mm_nn(f"sc_out_{i}", z, w_out_f, j, False, F32)
            st.update(h=h, p=p)
        elif kind == 1:
            pool_vec = _vecs([norm_g_f[i, 0], sc1, sh1, pool_b[j], pool_scale[j]], D)
            pooled, m = _pool_fwd(f"pool_fwd_{i}", xcur, pool_f, pool_vec)
            st.update(pooled=pooled)
        else:
            p = _mm_nn(f"cf_pw1_{i}", h, pw1_f, j, True, BF)
            taps = jnp.pad(cf_w_dw_f[j], ((0, 1), (0, 0)))
            cf_vec = _vecs([cf_b_pw1_f[j, :D], cf_b_pw1_f[j, D:], cf_b_dw_f[j], cf_ln_g_f[j], cf_ln_b_f[j]], D)
            wact, cv = _cf_act_fwd(f"cf_act_fwd_{i}", p, taps, cf_vec)
            m = _mm_nn(f"cf_pw2_{i}", wact, pw2_f, j, False, F32)
            m_bias = cf_b_pw2_f[j]
            st.update(h=h, p=p, wact=wact, cv=cv, taps=taps, cf_vec=cf_vec)
        x1, h2 = _epi_pro_fwd(f"epi1_fwd_{i}", xcur, m,
                              _vecs([g1, norm_g_f[i, 1], m_bias, norm_g_f[i, 2], sc2, sh2], D), True)
        st.update(m=m, x1=x1, m_bias=m_bias)
        up = _mm_nn(f"ffn_up_{i}", h2, w_up_f, i, True, BF)
        ffn_vec = _vecs(list(ffn_conv_f[i]) + [ffn_b_conv[i]], F)
        a = _ffn_act_fwd(f"ffn_act_fwd_{i}", up, ffn_vec)
        f = _mm_nn(f"ffn_down_{i}", a, w_down_f, i, False, F32, tm=512)
        nxt = i + 1
        fuse_next = nxt < DEPTH and nxt % 3 != 1
        rows = [g2, norm_g_f[i, 3], zero_d]
        if fuse_next:
            rows += [norm_g_f[nxt, 0], mods[nxt][1], mods[nxt][0]]
        xcur, h_next = _epi_pro_fwd(f"epi2_fwd_{i}", x1, f, _vecs(rows, D), fuse_next)
        st.update(h2=h2, up=up, f=f, ffn_vec=ffn_vec)
        saved.append(st)

    dy, loss_cols = _loss_fwd_bwd(xcur, target)
    loss = lax.psum(jnp.sum(loss_cols[0]), ("x", "y", "c"))

    dmod = [None] * DEPTH
    d_norm_g = [None] * DEPTH
    d_sc_conv = [None, None]
    d_ffn_conv, d_ffn_b = [None] * DEPTH, [None] * DEPTH
    g_w_in, g_w_out, g_w_up, g_w_down = [None, None], [None, None], [None] * DEPTH, [None] * DEPTH
    dxo = dy
    last = DEPTH - 1
    pend = _epi_bwd(f"epi2_bwd_{last}", dy, saved[last]["f"], _vecs([mods[last][5], norm_g_f[last, 3], zero_d], D))
    for i in reversed(range(DEPTH)):
        kind, j = i % 3, i // 3
        st = saved[i]
        sh1, sc1, g1, sh2, sc2, g2 = mods[i]
        df, e2 = pend
        da = _mm_nt(f"ffn_down_dx_{i}", df, w_down_f, i, False, BF, tn=F // 2)
        a, dup, fsum = _ffn_act_bwd(f"ffn_act_bwd_{i}", st["up"], da, st["ffn_vec"])
        g_w_down[i] = _mm_tn(f"ffn_down_dw_{i}", a, df, False, tk=F // 2)
        dh2 = _mm_nt(f"ffn_up_dx_{i}", dup, w_up_f, i, True, F32)
        g_w_up[i] = _mm_tn(f"ffn_up_dw_{i}", st["h2"], dup, True)
        dx1, dm, s21 = _pro_epi_bwd(f"pro2_epi1_bwd_{i}", dh2, st["x1"], dxo, st["m"],
                                    _vecs([norm_g_f[i, 2], sc2, g1, norm_g_f[i, 1], st["m_bias"]], D))
        p2, e1 = s21[0:3], s21[3:6]
        if kind == 0:
            dz = _mm_nt(f"sc_out_dx_{i}", dm, w_out_f, j, False, BF)
            z, dp, ssum = _sc_act_bwd(f"sc_act_bwd_{i}", st["p"], dz, _vecs(list(sc_conv_f[j]), D))
            g_w_out[j] = _mm_tn(f"sc_out_dw_{i}", z, dm, False)
            dh = _mm_nt(f"sc_in_dx_{i}", dp, w_in_f, j, True, F32)
            g_w_in[j] = _mm_tn(f"sc_in_dw_{i}", st["h"], dp, True)
            d_sc_conv[j] = ssum[0:3]
        elif kind == 1:
            dh, g_pool, psum = _pool_bwd(f"pool_bwd_{i}", st["pooled"], dm, pool_f, _vecs([pool_b[j], pool_scale[j]], D))
        else:
            dwact = _mm_nt(f"cf_pw2_dx_{i}", dm, pw2_f, j, False, BF)
            g_pw2 = _mm_tn(f"cf_pw2_dw_{i}", st["wact"], dm, False)
            dp, tsum, csum = _cf_act_bwd(f"cf_act_bwd_{i}", st["p"], st["cv"], dwact, st["taps"], st["cf_vec"])
            dh = _mm_nt(f"cf_pw1_dx_{i}", dp, pw1_f, j, True, F32)
            g_pw1 = _mm_tn(f"cf_pw1_dw_{i}", st["h"], dp, True)
            d_cf = dict(b_pw1=jnp.concatenate([csum[3], csum[4]])[None], w_dw=tsum[None, :CF_TAPS], b_dw=csum[0:1],
                        ln_g=csum[1:2], ln_b=csum[2:3], b_pw2=e1[2:3])
        if i > 0:
            prev = i - 1
            dxo, df_prev, s12 = _pro_epi_bwd(f"pro1_epi2_bwd_{i}", dh, st["x0"], dx1, saved[prev]["f"],
                                             _vecs([norm_g_f[i, 0], sc1, mods[prev][5], norm_g_f[prev, 3], zero_d], D))
            p1, pend = s12[0:3], (df_prev, s12[3:6])
        else:
            dxo, p1 = _pro_bwd(f"pro1_bwd_{i}", dh, st["x0"], dx1, _vecs([norm_g_f[i, 0], sc1], D))
        dmod[i] = jnp.concatenate([p1[0], p1[1], e1[0], p2[0], p2[1], e2[0]])
        d_norm_g[i] = jnp.stack([p1[2], e1[1], p2[2], e2[1]])
        if kind == 1:
            d_pool_b, d_pool_scale = psum[0:1], psum[1:2]
        saved[i] = None
        st = None
        if i == 0:
            grad_x = dxo[None]
        d_ffn_conv[i], d_ffn_b[i] = fsum[1:4], fsum[0]

    small_shapes = [(DEPTH, 6 * D), (DEPTH, 4, D), (2, 3, D), (1, D), (1, D), (1, 2 * D), (1, CF_TAPS, D), (1, D),
                    (1, D), (1, D), (1, D), (DEPTH, 3, F), (DEPTH, F)]
    small = [jnp.stack(dmod), jnp.stack(d_norm_g), jnp.stack(d_sc_conv), d_pool_b, d_pool_scale, d_cf["b_pw1"],
             d_cf["w_dw"], d_cf["b_dw"], d_cf["ln_g"], d_cf["ln_b"], d_cf["b_pw2"], jnp.stack(d_ffn_conv),
             jnp.stack(d_ffn_b)]
    gsl, ssl = _allgather_small("reduce_small_grads", _pack(small), True)
    tot = _unpack(ssl, small_shapes)
    (gb_mod, gt_norm_g, gt_sc_conv, g_pool_b, g_pool_scale, gt_b_pw1, gt_w_dw, gt_b_dw, gt_ln_g, gt_ln_b, gt_b_pw2,
     gt_ffn_conv, g_ffn_b) = tot
    dmod_all = jnp.stack([_unpack(gsl.reshape(N_DEV, -1, 128)[d], small_shapes[:1])[0] for d in range(N_DEV)], axis=1)
    dmod_sh = jnp.pad(_shard_last(dmod_all, chip, n_mod), ((0, 0), (0, 8), (0, 0)))
    g_w_mod = _mod_bwd(c16, dmod_sh)
    g_norm_g = _shard_last(gt_norm_g, chip, D // 4)
    g_sc_conv = _shard_last(gt_sc_conv, chip, D // 4)
    g_b_pw1 = _shard_last(gt_b_pw1, chip, 2 * D // 4)
    g_w_dw = _shard_last(gt_w_dw, chip, D // 4)
    g_b_dw = _shard_last(gt_b_dw, chip, D // 4)
    g_ln_g = _shard_last(gt_ln_g, chip, D // 4)
    g_ln_b = _shard_last(gt_ln_b, chip, D // 4)
    g_b_pw2 = _shard_last(gt_b_pw2, chip, D // 4)
    g_ffn_conv = _shard_last(gt_ffn_conv, chip, F // 4)

    def halves(gl):
        g = jnp.stack(gl) if isinstance(gl, list) else gl
        if g.ndim == 3:
            g = g.reshape(g.shape[0], N_CHIPS, g.shape[1] // N_CHIPS, g.shape[2])
        nl, _, k, n = g.shape
        return g.reshape(nl, N_CHIPS, 2, k // 2, n)

    partial = [halves(g_w_in), halves(g_w_out), halves(g_pool.astype(BF)), halves([g_pw1]), halves([g_pw2]),
               halves(g_w_up), halves(g_w_down)]
    pieces = _grad_exchange(partial)
    reduced = [_add_slots(f"grad_sum_{q}", pieces[q]) for q in range(len(pieces))]
    shared = _pair_share(reduced)
    big_w = [sc_w_in, sc_w_out, pool_w, cf_w_pw1, cf_w_pw2, ffn_w_up, ffn_w_down]
    big_g = [shared[q].reshape(big_w[q].shape) for q in range(len(big_w))]
    g_sc_w_in, g_sc_w_out, g_pool_w, g_cf_w_pw1, g_cf_w_pw2, g_ffn_w_up, g_ffn_w_down = big_g

    def adam_big(name, w, g, m, v):
        shp = w.shape
        two = lambda t: t.reshape(-1, shp[-1])
        return [o.reshape(shp) for o in _adamw(name, two(w), two(g), two(m), two(v))]

    grads = dict(w_mod=g_w_mod, b_mod=gb_mod, norm_g=g_norm_g, sc_w_in=g_sc_w_in, sc_conv=g_sc_conv,
                 sc_w_out=g_sc_w_out, pool_w=g_pool_w, pool_b=g_pool_b, pool_scale=g_pool_scale,
                 cf_w_pw1=g_cf_w_pw1, cf_b_pw1=g_b_pw1, cf_w_dw=g_w_dw, cf_b_dw=g_b_dw, cf_ln_g=g_ln_g,
                 cf_ln_b=g_ln_b, cf_w_pw2=g_cf_w_pw2, cf_b_pw2=g_b_pw2, ffn_w_up=g_ffn_w_up, ffn_conv=g_ffn_conv,
                 ffn_b_conv=g_ffn_b, ffn_w_down=g_ffn_w_down)
    weights = dict(w_mod=w_mod, b_mod=b_mod, norm_g=norm_g, sc_w_in=sc_w_in, sc_conv=sc_conv, sc_w_out=sc_w_out,
                   pool_w=pool_w, pool_b=pool_b, pool_scale=pool_scale, cf_w_pw1=cf_w_pw1, cf_b_pw1=cf_b_pw1,
                   cf_w_dw=cf_w_dw, cf_b_dw=cf_b_dw, cf_ln_g=cf_ln_g, cf_ln_b=cf_ln_b, cf_w_pw2=cf_w_pw2,
                   cf_b_pw2=cf_b_pw2, ffn_w_up=ffn_w_up, ffn_conv=ffn_conv, ffn_b_conv=ffn_b_conv,
                   ffn_w_down=ffn_w_down)
    m_in = dict(w_mod=m_w_mod, b_mod=m_b_mod, norm_g=m_norm_g, sc_w_in=m_sc_w_in, sc_conv=m_sc_conv,
                sc_w_out=m_sc_w_out, pool_w=m_pool_w, pool_b=m_pool_b, pool_scale=m_pool_scale,
                cf_w_pw1=m_cf_w_pw1, cf_b_pw1=m_cf_b_pw1, cf_w_dw=m_cf_w_dw, cf_b_dw=m_cf_b_dw, cf_ln_g=m_cf_ln_g,
                cf_ln_b=m_cf_ln_b, cf_w_pw2=m_cf_w_pw2, cf_b_pw2=m_cf_b_pw2, ffn_w_up=m_ffn_w_up,
                ffn_conv=m_ffn_conv, ffn_b_conv=m_ffn_b_conv, ffn_w_down=m_ffn_w_down)
    v_in = dict(w_mod=v_w_mod, b_mod=v_b_mod, norm_g=v_norm_g, sc_w_in=v_sc_w_in, sc_conv=v_sc_conv,
                sc_w_out=v_sc_w_out, pool_w=v_pool_w, pool_b=v_pool_b, pool_scale=v_pool_scale,
                cf_w_pw1=v_cf_w_pw1, cf_b_pw1=v_cf_b_pw1, cf_w_dw=v_cf_w_dw, cf_b_dw=v_cf_b_dw, cf_ln_g=v_cf_ln_g,
                cf_ln_b=v_cf_ln_b, cf_w_pw2=v_cf_w_pw2, cf_b_pw2=v_cf_b_pw2, ffn_w_up=v_ffn_w_up,
                ffn_conv=v_ffn_conv, ffn_b_conv=v_ffn_b_conv, ffn_w_down=v_ffn_w_down)
    names = list(weights)
    big_names = ["w_mod", "sc_w_in", "sc_w_out", "pool_w", "cf_w_pw1", "cf_w_pw2", "ffn_w_up", "ffn_w_down"]
    small_names = [n for n in names if n not in big_names]
    delta, new_m, new_v = {}, {}, {}
    for n in big_names:
        delta[n], new_m[n], new_v[n] = adam_big(f"adamw_{n}", weights[n], grads[n], m_in[n], v_in[n])
    grads = {n: grads[n].reshape(weights[n].shape) for n in names}
    sm_shapes = [weights[n].shape for n in small_names]
    sd, sm, sv = _adamw("adamw_small", _pack([weights[n] for n in small_names]), _pack([grads[n] for n in small_names]),
                        _pack([m_in[n] for n in small_names]), _pack([v_in[n] for n in small_names]))
    for n, d_, m_, v_ in zip(small_names, _unpack(sd, sm_shapes), _unpack(sm, sm_shapes), _unpack(sv, sm_shapes)):
        delta[n], new_m[n], new_v[n] = d_, m_, v_

    return (loss, grad_x, *[grads[n] for n in names], *[delta[n] for n in names], *[new_m[n] for n in names],
            *[new_v[n] for n in names])
```

```python
import itertools

import jax
import jax.numpy as jnp
from jax import lax
from jax.experimental import pallas as pl
from jax.experimental.pallas import tpu as pltpu

D = 1024
F = 2816
DEPTH = 4
POOL_WINDOWS = (2, 4, 8, 16)
POOL_GROUP = 256
CF_TAPS = 31
RMS_EPS = 1e-6
LN_EPS = 1e-5
ADAM_LR = 0.001
ADAM_B1 = 0.9
ADAM_B2 = 0.999
ADAM_EPS = 1e-08
ADAM_WD = 0.01
ADAM_STEP = 10

BF = jnp.bfloat16
F32 = jnp.float32
MESH = pl.DeviceIdType.MESH
SDS = jax.ShapeDtypeStruct
N_CHIPS = 4
N_DEV = 8
VMEM_LIMIT_MB = 56
ROW_TILE_D = 256
ROW_TILE_F = 256
HALO = 8
HALO_BF = 16
CF_HALO = 32
POOL_HALO = 16


def _cp(n_axes):
    return pltpu.CompilerParams(dimension_semantics=("arbitrary",) * n_axes,
                                vmem_limit_bytes=VMEM_LIMIT_MB * 1024 * 1024)


def _tile(n, pref):
    t = min(n, pref)
    assert n % t == 0, (n, pref)
    return t


def _const(shape):
    nd = len(shape)
    return pl.BlockSpec(shape, lambda *_: (0,) * nd)


def _vecs(rows, width):
    v = jnp.stack([r.reshape(width).astype(F32) for r in rows])
    pad = (-v.shape[0]) % 8
    return jnp.pad(v, ((0, pad), (0, 0)))


def _sigmoid(v):
    return 0.5 * jnp.tanh(0.5 * v) + 0.5


def _down(prev8, g, k):
    n = g.shape[0]
    return pltpu.roll(jnp.concatenate([prev8, g], axis=0), k, 0)[8:8 + n]


def _up(g, next8, k):
    n = g.shape[0]
    return pltpu.roll(jnp.concatenate([g, next8], axis=0), n + 8 - k, 0)[0:n]


def _fold8(v):
    acc = v[0:8]
    for r in range(8, v.shape[0], 8):
        acc = acc + v[r:r + 8]
    return acc


def _mm(name, a, b, out_sds, grid, a_spec, b_spec, o_spec, acc_shape, dn):
    nk = grid[2]

    def body(a_ref, b_ref, o_ref, *acc):
        part = lax.dot_general(a_ref[...], b_ref[...], dn, preferred_element_type=F32)
        if nk == 1:
            o_ref[...] = part.astype(o_ref.dtype)
        else:
            acc_ref = acc[0]
            k = pl.program_id(2)

            @pl.when(k == 0)
            def _():
                acc_ref[...] = part

            @pl.when(k > 0)
            def _():
                acc_ref[...] += part

            @pl.when(k == nk - 1)
            def _():
                o_ref[...] = acc_ref[...].astype(o_ref.dtype)

    scratch = [] if nk == 1 else [pltpu.VMEM(acc_shape, F32)]
    return pl.pallas_call(body, name=name, out_shape=out_sds, grid=grid, in_specs=[a_spec, b_spec],
                          out_specs=o_spec, scratch_shapes=scratch, compiler_params=_cp(3))(a, b)


NN = (((1,), (0,)), ((), ()))
NT = (((1,), (1,)), ((), ()))
TN = (((0,), (0,)), ((), ()))


def _mm_nn(name, a, w, layer, col_sharded, out_dtype, tm=1024, tn=None):
    m, k = a.shape
    tm = _tile(m, tm)
    if col_sharded:
        n4 = w.shape[3]
        tn = n4 if tn is None else tn
        tpc = n4 // tn
        n = N_CHIPS * n4
        b_spec = pl.BlockSpec((None, None, k, tn), lambda i, j, kk: (layer, j // tpc, 0, j % tpc))
    else:
        n = w.shape[2]
        tn = n if tn is None else tn
        b_spec = pl.BlockSpec((None, k, tn), lambda i, j, kk: (layer, 0, j))
    return _mm(name, a, w, SDS((m, n), out_dtype), (m // tm, n // tn, 1),
               pl.BlockSpec((tm, k), lambda i, j, kk: (i, 0)), b_spec,
               pl.BlockSpec((tm, tn), lambda i, j, kk: (i, j)), None, NN)


def _mm_nt(name, g, w, layer, col_sharded, out_dtype, tm=1024, tn=None):
    m, n = g.shape
    if col_sharded:
        return _mm_nt_staged(name, g, w, layer, out_dtype)
    tm = _tile(m, tm)
    kdim = w.shape[1]
    tn = kdim if tn is None else tn
    return _mm(name, g, w, SDS((m, kdim), out_dtype), (m // tm, kdim // tn, 1),
               pl.BlockSpec((tm, n), lambda i, j, kk: (i, 0)),
               pl.BlockSpec((None, tn, n), lambda i, j, kk: (layer, j, 0)),
               pl.BlockSpec((tm, tn), lambda i, j, kk: (i, j)), None, NT)


def _mm_nt_staged(name, g, w, layer, out_dtype, tm=512):
    m, n = g.shape
    kdim, n4 = w.shape[2], w.shape[3]
    tm = _tile(m, tm)

    def body(g_ref, w_hbm, o_ref, wbuf, sems):
        @pl.when(pl.program_id(0) == 0)
        def _():
            cps = [pltpu.make_async_copy(w_hbm.at[layer, j], wbuf.at[:, pl.ds(j * n4, n4)], sems.at[j])
                   for j in range(N_CHIPS)]
            for cp in cps:
                cp.start()
            for cp in cps:
                cp.wait()

        o_ref[...] = lax.dot_general(g_ref[...], wbuf[...], NT, preferred_element_type=F32).astype(o_ref.dtype)

    return pl.pallas_call(body, name=name, out_shape=SDS((m, kdim), out_dtype), grid=(m // tm,),
                          in_specs=[pl.BlockSpec((tm, n), lambda i: (i, 0)), pl.BlockSpec(memory_space=pltpu.HBM)],
                          out_specs=pl.BlockSpec((tm, kdim), lambda i: (i, 0)),
                          scratch_shapes=[pltpu.VMEM((kdim, n), BF), pltpu.SemaphoreType.DMA((N_CHIPS,))],
                          compiler_params=_cp(1))(g, w)


def _mm_tn(name, a, g, col_sharded, tk=None, tn=None, ts=2048):
    s, k = a.shape
    n = g.shape[1]
    ts = _tile(s, ts)
    tk = k if tk is None else tk
    if col_sharded:
        n4 = n // N_CHIPS
        tn = n4 if tn is None else tn
        tpc = n4 // tn
        out_sds = SDS((N_CHIPS, k, n4), BF)
        o_spec = pl.BlockSpec((None, tk, tn), lambda i, j, ss: (j // tpc, i, j % tpc))
    else:
        tn = n if tn is None else tn
        out_sds = SDS((k, n), BF)
        o_spec = pl.BlockSpec((tk, tn), lambda i, j, ss: (i, j))
    return _mm(name, a, g, out_sds, (k // tk, n // tn, s // ts),
               pl.BlockSpec((ts, tk), lambda i, j, ss: (ss, i)),
               pl.BlockSpec((ts, tn), lambda i, j, ss: (ss, j)), o_spec, (tk, tn), TN)


def _rows(tm, width, col=0):
    return pl.BlockSpec((tm, width), lambda i: (i, col))


def _pro_fwd(name, x, vec):
    s = x.shape[0]
    tm = _tile(s, ROW_TILE_D)

    def body(x_ref, v_ref, h_ref):
        xv = x_ref[...]
        r = lax.rsqrt(jnp.mean(xv * xv, axis=-1, keepdims=True) + RMS_EPS)
        a = v_ref[0:1, :] * (1.0 + v_ref[1:2, :])
        h_ref[...] = (xv * r * a + v_ref[2:3, :]).astype(h_ref.dtype)

    return pl.pallas_call(body, name=name, out_shape=SDS((s, D), BF), grid=(s // tm,),
                          in_specs=[_rows(tm, D), _const((8, D))], out_specs=_rows(tm, D),
                          compiler_params=_cp(1))(x, vec)


def _epi_pro_fwd(name, x, m, vec, with_next):
    s = x.shape[0]
    tm = _tile(s, ROW_TILE_D)

    def body(x_ref, m_ref, v_ref, o_ref, *h_ref):
        mv = m_ref[...] + v_ref[2:3, :]
        rm = lax.rsqrt(jnp.mean(mv * mv, axis=-1, keepdims=True) + RMS_EPS)
        xo = x_ref[...] + v_ref[0:1, :] * (mv * rm * v_ref[1:2, :])
        o_ref[...] = xo
        if with_next:
            r = lax.rsqrt(jnp.mean(xo * xo, axis=-1, keepdims=True) + RMS_EPS)
            a = v_ref[3:4, :] * (1.0 + v_ref[4:5, :])
            h_ref[0][...] = (xo * r * a + v_ref[5:6, :]).astype(BF)

    out_shape = (SDS((s, D), F32),) + ((SDS((s, D), BF),) if with_next else ())
    res = pl.pallas_call(body, name=name, out_shape=out_shape, grid=(s // tm,),
                         in_specs=[_rows(tm, D), _rows(tm, D), _const((8, D))],
                         out_specs=tuple([_rows(tm, D)] * len(out_shape)), compiler_params=_cp(1))(x, m, vec)
    return res if with_next else (res[0], None)


def _loss_fwd_bwd(y, t):
    s = y.shape[0]
    tm = _tile(s, ROW_TILE_D)

    def body(y_ref, t_ref, dy_ref, acc_ref):
        @pl.when(pl.program_id(0) == 0)
        def _():
            acc_ref[...] = jnp.zeros_like(acc_ref)

        e = y_ref[...] - t_ref[...]
        dy_ref[...] = e * (1.0 / D)
        acc_ref[0:1, :] += jnp.sum(e * e, axis=0, keepdims=True) * (0.5 / D)

    return pl.pallas_call(body, name="loss", out_shape=(SDS((s, D), F32), SDS((8, D), F32)), grid=(s // tm,),
                          in_specs=[_rows(tm, D), _rows(tm, D)], out_specs=(_rows(tm, D), _const((8, D))),
                          compiler_params=_cp(1))(y, t)


def _epi_bwd_rows(dv, mv, g, ngb, dm_ref, acc_ref, row):
    rm = lax.rsqrt(jnp.mean(mv * mv, axis=-1, keepdims=True) + RMS_EPS)
    mn = mv * rm
    dmn = dv * (g * ngb)
    dm = rm * (dmn - mn * jnp.mean(dmn * mn, axis=-1, keepdims=True))
    dm_ref[...] = dm.astype(dm_ref.dtype)
    t = jnp.sum(dv * mn, axis=0, keepdims=True)
    acc_ref[row:row + 1, :] += t * ngb
    acc_ref[row + 1:row + 2, :] += t * g
    acc_ref[row + 2:row + 3, :] += jnp.sum(dm, axis=0, keepdims=True)


def _epi_bwd(name, dxo, m, vec):
    s = dxo.shape[0]
    tm = _tile(s, ROW_TILE_D)

    def body(d_ref, m_ref, v_ref, dm_ref, acc_ref):
        @pl.when(pl.program_id(0) == 0)
        def _():
            acc_ref[...] = jnp.zeros_like(acc_ref)

        _epi_bwd_rows(d_ref[...], m_ref[...] + v_ref[2:3, :], v_ref[0:1, :], v_ref[1:2, :], dm_ref, acc_ref, 0)

    return pl.pallas_call(body, name=name, out_shape=(SDS((s, D), BF), SDS((8, D), F32)), grid=(s // tm,),
                          in_specs=[_rows(tm, D), _rows(tm, D), _const((8, D))],
                          out_specs=(_rows(tm, D), _const((8, D))), compiler_params=_cp(1))(dxo, m, vec)


def _pro_epi_bwd(name, dh, x, dxo, m, vec):
    s = x.shape[0]
    tm = _tile(s, ROW_TILE_D)

    def body(dh_ref, x_ref, d_ref, m_ref, v_ref, dx_ref, dm_ref, acc_ref):
        @pl.when(pl.program_id(0) == 0)
        def _():
            acc_ref[...] = jnp.zeros_like(acc_ref)

        dx = _pro_bwd_rows(dh_ref[...].astype(F32), x_ref[...], d_ref[...], v_ref[0:1, :], v_ref[1:2, :], acc_ref)
        dx_ref[...] = dx
        _epi_bwd_rows(dx, m_ref[...] + v_ref[4:5, :], v_ref[2:3, :], v_ref[3:4, :], dm_ref, acc_ref, 3)

    return pl.pallas_call(body, name=name, out_shape=(SDS((s, D), F32), SDS((s, D), BF), SDS((8, D), F32)),
                          grid=(s // tm,),
                          in_specs=[_rows(tm, D), _rows(tm, D), _rows(tm, D), _rows(tm, D), _const((8, D))],
                          out_specs=(_rows(tm, D), _rows(tm, D), _const((8, D))),
                          compiler_params=_cp(1))(dh, x, dxo, m, vec)


def _pro_bwd_rows(dhv, xv, dxo, ng, sc, acc_ref):
    sc1 = 1.0 + sc
    r = lax.rsqrt(jnp.mean(xv * xv, axis=-1, keepdims=True) + RMS_EPS)
    xn = xv * r
    dxn = dhv * (ng * sc1)
    t = jnp.sum(dhv * xn, axis=0, keepdims=True)
    acc_ref[0:1, :] += jnp.sum(dhv, axis=0, keepdims=True)
    acc_ref[1:2, :] += t * ng
    acc_ref[2:3, :] += t * sc1
    return dxo + r * (dxn - xn * jnp.mean(dxn * xn, axis=-1, keepdims=True))


def _pro_bwd(name, dh, x, dxo, vec):
    s = x.shape[0]
    tm = _tile(s, ROW_TILE_D)

    def body(dh_ref, x_ref, d_ref, v_ref, dx_ref, acc_ref):
        @pl.when(pl.program_id(0) == 0)
        def _():
            acc_ref[...] = jnp.zeros_like(acc_ref)

        dx_ref[...] = _pro_bwd_rows(dh_ref[...].astype(F32), x_ref[...], d_ref[...], v_ref[0:1, :], v_ref[1:2, :],
                                    acc_ref)

    return pl.pallas_call(body, name=name, out_shape=(SDS((s, D), F32), SDS((8, D), F32)), grid=(s // tm,),
                          in_specs=[_rows(tm, D), _rows(tm, D), _rows(tm, D), _const((8, D))],
                          out_specs=(_rows(tm, D), _const((8, D))), compiler_params=_cp(1))(dh, x, dxo, vec)


def _carry_up(buf, tm, first, halo=HALO):
    @pl.when(first)
    def _():
        buf[tm:tm + halo, :] = jnp.zeros((halo, buf.shape[1]), F32)

    @pl.when(jnp.logical_not(first))
    def _():
        buf[tm:tm + halo, :] = buf[0:halo, :]


def _ffn_act_fwd(name, up, vec):
    s = up.shape[0]
    tm = _tile(s, ROW_TILE_F)
    rb_n = _tile(tm, 64)
    cw = 256

    def body(g_ref, v_ref, w_ref, a_ref, carry):
        @pl.when(pl.program_id(0) == 0)
        def _():
            carry[...] = jnp.zeros_like(carry)

        for cc in range(F // cw):
            cols = slice(cc * cw, (cc + 1) * cw)
            w0, w1, w2, b = w_ref[0:1, cols], w_ref[1:2, cols], w_ref[2:3, cols], w_ref[3:4, cols]

            def rb_body(rb, prev8):
                rows = pl.ds(pl.multiple_of(rb * rb_n, rb_n), rb_n)
                g = g_ref[rows, cols].astype(F32)
                gc = w2 * g + w1 * _down(prev8, g, 1) + w0 * _down(prev8, g, 2) + b
                a_ref[rows, cols] = (gc * _sigmoid(gc) * v_ref[rows, cols].astype(F32)).astype(a_ref.dtype)
                return g[rb_n - 8:rb_n]

            carry[:, cols] = lax.fori_loop(0, tm // rb_n, rb_body, carry[:, cols])

    return pl.pallas_call(body, name=name, out_shape=SDS((s, F), BF), grid=(s // tm,),
                          in_specs=[_rows(tm, F, 0), _rows(tm, F, 1), _const((8, F))], out_specs=_rows(tm, F),
                          scratch_shapes=[pltpu.VMEM((HALO, F), F32)], compiler_params=_cp(1))(up, up, vec)


def _prev_halo_spec(nt, tm, width, col):
    per = tm // HALO_BF
    return pl.BlockSpec((HALO_BF, width), lambda i: (jnp.maximum((nt - 1 - i) * per - 1, 0), col))


def _ffn_act_bwd(name, up, da, vec):
    s = up.shape[0]
    tm = _tile(s, ROW_TILE_F)
    nt = s // tm
    rev = lambda col: pl.BlockSpec((tm, F), lambda i: (nt - 1 - i, col))

    rb_n = _tile(tm, 64)
    nrb = tm // rb_n
    cw = 128

    def body(g_ref, gp_ref, v_ref, da_ref, w_ref, dup_ref, acc_ref, carry):
        step = pl.program_id(0)

        @pl.when(step == 0)
        def _():
            acc_ref[...] = jnp.zeros_like(acc_ref)
            carry[...] = jnp.zeros_like(carry)

        for cc in range(F // cw):
            cols = slice(cc * cw, (cc + 1) * cw)
            w0, w1, w2, b = w_ref[0:1, cols], w_ref[1:2, cols], w_ref[2:3, cols], w_ref[3:4, cols]
            halo = jnp.where(step < nt - 1, gp_ref[:, cols].astype(F32)[HALO_BF - 8:HALO_BF], 0.0)

            def rb_body(it, st):
                nxt8, ab, a0, a1, a2 = st
                rb = nrb - 1 - it
                r0 = pl.multiple_of(rb * rb_n, rb_n)
                rows = pl.ds(r0, rb_n)
                g = g_ref[rows, cols].astype(F32)
                ra = pl.multiple_of(jnp.maximum(r0 - HALO_BF, 0), HALO_BF)
                above = g_ref[pl.ds(ra, HALO_BF), cols].astype(F32)[HALO_BF - 8:HALO_BF]
                prev8 = jnp.where(rb == 0, halo, above)
                g1 = _down(prev8, g, 1)
                g2 = _down(prev8, g, 2)
                gc = w2 * g + w1 * g1 + w0 * g2 + b
                sg = _sigmoid(gc)
                sl = gc * sg
                val = v_ref[rows, cols].astype(F32)
                dav = da_ref[rows, cols].astype(F32)
                dup_ref[rows, F + cc * cw:F + (cc + 1) * cw] = (dav * sl).astype(dup_ref.dtype)
                dgc = (dav * val) * (sg + sl * (1.0 - sg))
                dup_ref[rows, cols] = (w2 * dgc + w1 * _up(dgc, nxt8, 1) + w0 * _up(dgc, nxt8, 2)).astype(dup_ref.dtype)
                return (dgc[0:8], ab + _fold8(dgc), a0 + _fold8(g2 * dgc), a1 + _fold8(g1 * dgc),
                        a2 + _fold8(g * dgc))

            z = jnp.zeros((8, cw), F32)
            nxt8, ab, a0, a1, a2 = lax.fori_loop(0, nrb, rb_body, (carry[:, cols], z, z, z, z))
            carry[:, cols] = nxt8
            acc_ref[0:1, cols] += jnp.sum(ab, axis=0, keepdims=True)
            acc_ref[1:2, cols] += jnp.sum(a0, axis=0, keepdims=True)
            acc_ref[2:3, cols] += jnp.sum(a1, axis=0, keepdims=True)
            acc_ref[3:4, cols] += jnp.sum(a2, axis=0, keepdims=True)

    return pl.pallas_call(
        body, name=name, out_shape=(SDS((s, 2 * F), BF), SDS((8, F), F32)), grid=(nt,),
        in_specs=[rev(0), _prev_halo_spec(nt, tm, F, 0), rev(1), rev(0), _const((8, F))],
        out_specs=(pl.BlockSpec((tm, 2 * F), lambda i: (nt - 1 - i, 0)), _const((8, F))),
        scratch_shapes=[pltpu.VMEM((HALO, F), F32)], compiler_params=_cp(1))(up, up, up, da, vec)


def _sc_act_fwd(name, p, vec):
    s = p.shape[0]
    tm = _tile(s, ROW_TILE_D)

    rb_n = _tile(tm, 64)
    cw = 256

    def body(b_ref, c_ref, h_ref, w_ref, z_ref, carry):
        @pl.when(pl.program_id(0) == 0)
        def _():
            carry[...] = jnp.zeros_like(carry)

        for cc in range(D // cw):
            cols = slice(cc * cw, (cc + 1) * cw)
            w0, w1, w2 = w_ref[0:1, cols], w_ref[1:2, cols], w_ref[2:3, cols]

            def rb_body(rb, prev8):
                rows = pl.ds(pl.multiple_of(rb * rb_n, rb_n), rb_n)
                q = c_ref[rows, cols].astype(F32) * h_ref[rows, cols].astype(F32)
                u = w2 * q + w1 * _down(prev8, q, 1) + w0 * _down(prev8, q, 2)
                z_ref[rows, cols] = (b_ref[rows, cols].astype(F32) * u).astype(z_ref.dtype)
                return q[rb_n - 8:rb_n]

            carry[:, cols] = lax.fori_loop(0, tm // rb_n, rb_body, carry[:, cols])

    return pl.pallas_call(body, name=name, out_shape=SDS((s, D), BF), grid=(s // tm,),
                          in_specs=[_rows(tm, D, 0), _rows(tm, D, 1), _rows(tm, D, 2), _const((8, D))],
                          out_specs=_rows(tm, D), scratch_shapes=[pltpu.VMEM((HALO, D), F32)],
                          compiler_params=_cp(1))(p, p, p, vec)


def _sc_act_bwd(name, p, dz, vec):
    s = p.shape[0]
    tm = _tile(s, ROW_TILE_D)
    nt = s // tm
    rev = lambda col: pl.BlockSpec((tm, D), lambda i: (nt - 1 - i, col))

    rb_n = _tile(tm, 64)
    nrb = tm // rb_n
    cw = 128

    def body(b_ref, c_ref, h_ref, cp_ref, hp_ref, dz_ref, w_ref, z_ref, dp_ref, acc_ref, carry):
        step = pl.program_id(0)

        @pl.when(step == 0)
        def _():
            acc_ref[...] = jnp.zeros_like(acc_ref)
            carry[...] = jnp.zeros_like(carry)

        for cc in range(D // cw):
            cols = slice(cc * cw, (cc + 1) * cw)
            w0, w1, w2 = w_ref[0:1, cols], w_ref[1:2, cols], w_ref[2:3, cols]
            halo = jnp.where(step < nt - 1,
                             (cp_ref[:, cols].astype(F32) * hp_ref[:, cols].astype(F32))[HALO_BF - 8:HALO_BF], 0.0)

            def rb_body(it, st):
                nxt8, a0, a1, a2 = st
                rb = nrb - 1 - it
                r0 = pl.multiple_of(rb * rb_n, rb_n)
                rows = pl.ds(r0, rb_n)
                cg = c_ref[rows, cols].astype(F32)
                hin = h_ref[rows, cols].astype(F32)
                bg = b_ref[rows, cols].astype(F32)
                q = cg * hin
                up_rows = pl.ds(pl.multiple_of(jnp.maximum(r0 - HALO_BF, 0), HALO_BF), HALO_BF)
                above = (c_ref[up_rows, cols].astype(F32) * h_ref[up_rows, cols].astype(F32))[HALO_BF - 8:HALO_BF]
                prev8 = jnp.where(rb == 0, halo, above)
                q1 = _down(prev8, q, 1)
                q2 = _down(prev8, q, 2)
                u = w2 * q + w1 * q1 + w0 * q2
                dzv = dz_ref[rows, cols].astype(F32)
                z_ref[rows, cols] = (bg * u).astype(z_ref.dtype)
                dp_ref[rows, cols] = (dzv * u).astype(dp_ref.dtype)
                du = dzv * bg
                dq = w2 * du + w1 * _up(du, nxt8, 1) + w0 * _up(du, nxt8, 2)
                dp_ref[rows, D + cc * cw:D + (cc + 1) * cw] = (dq * hin).astype(dp_ref.dtype)
                dp_ref[rows, 2 * D + cc * cw:2 * D + (cc + 1) * cw] = (dq * cg).astype(dp_ref.dtype)
                return du[0:8], a0 + _fold8(q2 * du), a1 + _fold8(q1 * du), a2 + _fold8(q * du)

            z = jnp.zeros((8, cw), F32)
            nxt8, a0, a1, a2 = lax.fori_loop(0, nrb, rb_body, (carry[:, cols], z, z, z))
            carry[:, cols] = nxt8
            acc_ref[0:1, cols] += jnp.sum(a0, axis=0, keepdims=True)
            acc_ref[1:2, cols] += jnp.sum(a1, axis=0, keepdims=True)
            acc_ref[2:3, cols] += jnp.sum(a2, axis=0, keepdims=True)

    return pl.pallas_call(
        body, name=name, out_shape=(SDS((s, D), BF), SDS((s, 3 * D), BF), SDS((8, D), F32)), grid=(nt,),
        in_specs=[rev(0), rev(1), rev(2), _prev_halo_spec(nt, tm, D, 1), _prev_halo_spec(nt, tm, D, 2), rev(0),
                  _const((8, D))],
        out_specs=(rev(0), pl.BlockSpec((tm, 3 * D), lambda i: (nt - 1 - i, 0)), _const((8, D))),
        scratch_shapes=[pltpu.VMEM((HALO, D), F32)], compiler_params=_cp(1))(p, p, p, p, p, dz, vec)


CF_ROW_BLOCK = 32
CF_LANES = 128


def _tap_conv_block(ext, rb_n, tap_of_offset, t_ref, cols, init, u=None, accs=None):
    n = ext.shape[0]
    out = init
    for b in range(8):
        rolled = ext if b == 0 else pltpu.roll(ext, n - b, 0)
        for a in range(n // 8):
            k = tap_of_offset(8 * a + b)
            if k is None:
                continue
            sl = rolled[8 * a:8 * a + rb_n]
            out = out + t_ref[k:k + 1, cols] * sl
            if accs is not None:
                accs[k] = accs[k] + _fold8(u * sl)
    return out


def _cf_act_fwd(name, p, taps, vec):
    s = p.shape[0]
    tm = _tile(s, ROW_TILE_D)
    rb_n = _tile(tm, CF_ROW_BLOCK)
    base = CF_HALO - (CF_TAPS - 1)
    tap_of = lambda off: off - base if 0 <= off - base < CF_TAPS else None

    def body(a_ref, g_ref, t_ref, v_ref, w_ref, cv_ref, buf):
        first = pl.program_id(0) == 0

        @pl.when(first)
        def _():
            buf[0:CF_HALO, :] = jnp.zeros((CF_HALO, D), F32)

        @pl.when(jnp.logical_not(first))
        def _():
            buf[0:CF_HALO, :] = buf[tm:tm + CF_HALO, :]

        a = a_ref[...].astype(F32) + v_ref[0:1, :]
        g = g_ref[...].astype(F32) + v_ref[1:2, :]
        buf[CF_HALO:CF_HALO + tm, :] = a * _sigmoid(g)
        for ci in range(D // CF_LANES):
            cols = slice(ci * CF_LANES, (ci + 1) * CF_LANES)

            def rb_body(rb, carry):
                r0 = pl.multiple_of(rb * rb_n, rb_n)
                ext = buf[pl.ds(r0, rb_n + CF_HALO), cols]
                init = jnp.zeros((rb_n, CF_LANES), F32) + v_ref[2:3, cols]
                cv_ref[pl.ds(r0, rb_n), cols] = _tap_conv_block(ext, rb_n, tap_of, t_ref, cols, init)
                return carry

            lax.fori_loop(0, tm // rb_n, rb_body, 0)
        cv = cv_ref[...]
        mu = jnp.mean(cv, axis=-1, keepdims=True)
        cc = cv - mu
        rstd = lax.rsqrt(jnp.mean(cc * cc, axis=-1, keepdims=True) + LN_EPS)
        ln = cc * rstd * v_ref[3:4, :] + v_ref[4:5, :]
        w_ref[...] = (ln * _sigmoid(ln)).astype(w_ref.dtype)

    return pl.pallas_call(body, name=name, out_shape=(SDS((s, D), BF), SDS((s, D), F32)), grid=(s // tm,),
                          in_specs=[_rows(tm, D, 0), _rows(tm, D, 1), _const((32, D)), _const((8, D))],
                          out_specs=(_rows(tm, D), _rows(tm, D)),
                          scratch_shapes=[pltpu.VMEM((tm + CF_HALO, D), F32)], compiler_params=_cp(1))(p, p, taps, vec)


def _cf_act_bwd(name, p, cv, dw, taps, vec):
    s = p.shape[0]
    tm = _tile(s, ROW_TILE_D)
    nt = s // tm
    rev = lambda col: pl.BlockSpec((tm, D), lambda i: (nt - 1 - i, col))
    rb_n = _tile(tm, CF_ROW_BLOCK)
    tap_of = lambda off: CF_TAPS - 1 - off if off < CF_TAPS else None

    def body(a_ref, g_ref, cv_ref, dw_ref, t_ref, v_ref, dp_ref, tacc_ref, acc_ref, nbuf, ubuf, dubuf):
        step = pl.program_id(0)

        @pl.when(step == 0)
        def _():
            acc_ref[...] = jnp.zeros_like(acc_ref)
            tacc_ref[...] = jnp.zeros_like(tacc_ref)

        a = a_ref[...].astype(F32) + v_ref[0:1, :]
        g = g_ref[...].astype(F32) + v_ref[1:2, :]
        sg = _sigmoid(g)
        ubuf[...] = a * sg
        cvv = cv_ref[...]
        mu = jnp.mean(cvv, axis=-1, keepdims=True)
        cc = cvv - mu
        rstd = lax.rsqrt(jnp.mean(cc * cc, axis=-1, keepdims=True) + LN_EPS)
        vhat = cc * rstd
        ln = vhat * v_ref[3:4, :] + v_ref[4:5, :]
        s2 = _sigmoid(ln)
        dln = dw_ref[...].astype(F32) * (s2 * (1.0 + ln * (1.0 - s2)))
        acc_ref[1:2, :] += jnp.sum(dln * vhat, axis=0, keepdims=True)
        acc_ref[2:3, :] += jnp.sum(dln, axis=0, keepdims=True)
        dvh = dln * v_ref[3:4, :]
        dcv = rstd * (dvh - jnp.mean(dvh, axis=-1, keepdims=True)
                      - vhat * jnp.mean(dvh * vhat, axis=-1, keepdims=True))
        acc_ref[0:1, :] += jnp.sum(dcv, axis=0, keepdims=True)
        _carry_up(nbuf, tm, step == 0, CF_HALO)
        nbuf[0:tm, :] = dcv
        for ci in range(D // CF_LANES):
            cols = slice(ci * CF_LANES, (ci + 1) * CF_LANES)

            def rb_body(rb, accs):
                r0 = pl.multiple_of(rb * rb_n, rb_n)
                ext = nbuf[pl.ds(r0, rb_n + CF_HALO), cols]
                accs = list(accs)
                dubuf[pl.ds(r0, rb_n), cols] = _tap_conv_block(
                    ext, rb_n, tap_of, t_ref, cols, jnp.zeros((rb_n, CF_LANES), F32),
                    ubuf[pl.ds(r0, rb_n), cols], accs)
                return tuple(accs)

            z = jnp.zeros((8, CF_LANES), F32)
            accs = lax.fori_loop(0, tm // rb_n, rb_body, tuple([z] * CF_TAPS))
            for k in range(CF_TAPS):
                tacc_ref[k:k + 1, cols] += jnp.sum(accs[k], axis=0, keepdims=True)
        a = a_ref[...].astype(F32) + v_ref[0:1, :]
        sg = _sigmoid(g_ref[...].astype(F32) + v_ref[1:2, :])
        da = dubuf[...] * sg
        dg = da * a * (1.0 - sg)
        dp_ref[:, 0:D] = da.astype(dp_ref.dtype)
        dp_ref[:, D:2 * D] = dg.astype(dp_ref.dtype)
        acc_ref[3:4, :] += jnp.sum(da, axis=0, keepdims=True)
        acc_ref[4:5, :] += jnp.sum(dg, axis=0, keepdims=True)

    return pl.pallas_call(
        body, name=name, out_shape=(SDS((s, 2 * D), BF), SDS((32, D), F32), SDS((8, D), F32)), grid=(nt,),
        in_specs=[rev(0), rev(1), rev(0), rev(0), _const((32, D)), _const((8, D))],
        out_specs=(pl.BlockSpec((tm, 2 * D), lambda i: (nt - 1 - i, 0)), _const((32, D)), _const((8, D))),
        scratch_shapes=[pltpu.VMEM((tm + CF_HALO, D), F32), pltpu.VMEM((tm, D), F32), pltpu.VMEM((tm, D), F32)],
        compiler_params=_cp(1))(p, p, cv, dw, taps, vec)


def _inv_count(row0, tm, window):
    t = row0 + lax.broadcasted_iota(jnp.int32, (tm, 1), 0)
    return 1.0 / jnp.minimum(t + 1, window).astype(F32)


def _pool_fwd(name, x, w, vec):
    s = x.shape[0]
    tm = _tile(s, ROW_TILE_D)
    G = POOL_GROUP

    def body(x_ref, w_ref, v_ref, pl_ref, m_ref, buf):
        i = pl.program_id(0)

        @pl.when(i == 0)
        def _():
            buf[0:POOL_HALO, :] = jnp.zeros((POOL_HALO, D), F32)

        @pl.when(i > 0)
        def _():
            buf[0:POOL_HALO, :] = buf[tm:tm + POOL_HALO, :]

        xv = x_ref[...]
        r = lax.rsqrt(jnp.mean(xv * xv, axis=-1, keepdims=True) + RMS_EPS)
        buf[POOL_HALO:POOL_HALO + tm, :] = xv * r * (v_ref[0:1, :] * (1.0 + v_ref[1:2, :])) + v_ref[2:3, :]
        for gi, win in enumerate(POOL_WINDOWS):
            cols = slice(gi * G, (gi + 1) * G)
            acc = buf[POOL_HALO:POOL_HALO + tm, cols]
            hg = acc
            for j in range(1, win):
                acc = acc + buf[POOL_HALO - j:POOL_HALO - j + tm, cols]
            pooled = (acc * _inv_count(i * tm, tm, win) - hg).astype(BF)
            pl_ref[:, cols] = pooled
            yg = jnp.dot(pooled, w_ref[gi], preferred_element_type=F32)
            m_ref[:, cols] = (yg + v_ref[3:4, cols]) * v_ref[4:5, cols]

    return pl.pallas_call(body, name=name, out_shape=(SDS((s, D), BF), SDS((s, D), F32)), grid=(s // tm,),
                          in_specs=[_rows(tm, D), _const((4, G, G)), _const((8, D))],
                          out_specs=(_rows(tm, D), _rows(tm, D)),
                          scratch_shapes=[pltpu.VMEM((tm + POOL_HALO, D), F32)], compiler_params=_cp(1))(x, w, vec)


def _pool_bwd(name, pooled, dm, w, vec):
    s = pooled.shape[0]
    tm = _tile(s, ROW_TILE_D)
    nt = s // tm
    G = POOL_GROUP
    rev = pl.BlockSpec((tm, D), lambda i: (nt - 1 - i, 0))

    def body(p_ref, dm_ref, w_ref, v_ref, dh_ref, dw_ref, acc_ref, nbuf):
        step = pl.program_id(0)
        row0 = (nt - 1 - step) * tm

        @pl.when(step == 0)
        def _():
            acc_ref[...] = jnp.zeros_like(acc_ref)
            dw_ref[...] = jnp.zeros_like(dw_ref)

        _carry_up(nbuf, tm, step == 0, POOL_HALO)
        dmv = dm_ref[...].astype(F32)
        acc_ref[0:1, :] += jnp.sum(dmv, axis=0, keepdims=True) * v_ref[1:2, :]
        dps = []
        for gi, win in enumerate(POOL_WINDOWS):
            cols = slice(gi * G, (gi + 1) * G)
            pg = p_ref[:, cols]
            yb = jnp.dot(pg, w_ref[gi], preferred_element_type=F32) + v_ref[0:1, cols]
            acc_ref[1:2, cols] += jnp.sum(dmv[:, cols] * yb, axis=0, keepdims=True)
            dy = (dmv[:, cols] * v_ref[1:2, cols]).astype(BF)
            dw_ref[gi] += lax.dot_general(pg, dy, TN, preferred_element_type=F32)
            dpg = lax.dot_general(dy, w_ref[gi], NT, preferred_element_type=F32)
            dps.append(dpg)
            nbuf[0:tm, cols] = dpg * _inv_count(row0, tm, win)
        for gi, win in enumerate(POOL_WINDOWS):
            cols = slice(gi * G, (gi + 1) * G)
            acc = nbuf[0:tm, cols]
            for j in range(1, win):
                acc = acc + nbuf[j:j + tm, cols]
            dh_ref[:, cols] = acc - dps[gi]

    return pl.pallas_call(
        body, name=name, out_shape=(SDS((s, D), F32), SDS((4, G, G), F32), SDS((8, D), F32)), grid=(nt,),
        in_specs=[rev, rev, _const((4, G, G)), _const((8, D))],
        out_specs=(rev, _const((4, G, G)), _const((8, D))),
        scratch_shapes=[pltpu.VMEM((tm + POOL_HALO, D), F32)], compiler_params=_cp(1))(pooled, dm, w, vec)


def _row_tile_2d(rows, width, bytes_per_row_elem=4, budget=2 * 1024 * 1024):
    t = max(8, budget // (width * bytes_per_row_elem))
    t = min(rows, 1 << (t.bit_length() - 1))
    while rows % t:
        t //= 2
    return t


def _add_slots(name, r):
    nl, _, k, n = r.shape
    tk = _row_tile_2d(k, n, 16)

    def body(r_ref, o_ref):
        f = lambda i: r_ref[i].astype(F32)
        o_ref[...] = ((f(7) + f(6)) + (f(0) + f(1))) + ((f(2) + f(3)) + (f(4) + f(5)))

    return pl.pallas_call(body, name=name, out_shape=SDS((nl, k, n), F32), grid=(nl, k // tk),
                          in_specs=[pl.BlockSpec((None, N_DEV, tk, n), lambda l, i: (l, 0, i, 0))],
                          out_specs=pl.BlockSpec((None, tk, n), lambda l, i: (l, i, 0)),
                          compiler_params=_cp(2))(r)


def _adamw(name, w, g, m, v):
    rows, width = w.shape
    tm = _row_tile_2d(rows, width, 4, 1024 * 1024)
    c1 = 1.0 - ADAM_B1 ** ADAM_STEP
    c2 = 1.0 - ADAM_B2 ** ADAM_STEP

    def body(w_ref, g_ref, m_ref, v_ref, d_ref, nm_ref, nv_ref):
        gv = g_ref[...]
        nm = ADAM_B1 * m_ref[...] + (1.0 - ADAM_B1) * gv
        nv = ADAM_B2 * v_ref[...] + (1.0 - ADAM_B2) * (gv * gv)
        nm_ref[...] = nm
        nv_ref[...] = nv
        d_ref[...] = -ADAM_LR * ((nm / c1) / (jnp.sqrt(nv / c2) + ADAM_EPS) + ADAM_WD * w_ref[...])

    spec = _rows(tm, width)
    sds = SDS((rows, width), F32)
    return pl.pallas_call(body, name=name, out_shape=(sds, sds, sds), grid=(rows // tm,),
                          in_specs=[spec] * 4, out_specs=(spec,) * 3, compiler_params=_cp(1))(w, g, m, v)


def _mod_fwd(c16, w_mod, b_sh):
    n = w_mod.shape[2]
    tn = _tile(n, 512)

    def body(c_ref, w_ref, b_ref, o_ref):
        cv = c_ref[...]
        ca = (cv * _sigmoid(cv)).astype(BF)
        o_ref[...] = jnp.dot(ca, w_ref[...].astype(BF), preferred_element_type=F32) + b_ref[0:1, :]

    return pl.pallas_call(body, name="mod_fwd", out_shape=SDS((DEPTH, 16, n), F32), grid=(DEPTH, n // tn),
                          in_specs=[_const((16, D)), pl.BlockSpec((None, D, tn), lambda l, j: (l, 0, j)),
                                    pl.BlockSpec((None, 8, tn), lambda l, j: (l, 0, j))],
                          out_specs=pl.BlockSpec((None, 16, tn), lambda l, j: (l, 0, j)),
                          compiler_params=_cp(2))(c16, w_mod, b_sh)


def _mod_bwd(c16, dmod):
    n = dmod.shape[2]
    tn = _tile(n, 512)

    def body(c_ref, d_ref, o_ref):
        cv = c_ref[...]
        ca = (cv * _sigmoid(cv)).astype(BF)
        o_ref[...] = lax.dot_general(ca, d_ref[...].astype(BF), TN, preferred_element_type=F32)

    return pl.pallas_call(body, name="mod_bwd", out_shape=SDS((DEPTH, D, n), F32), grid=(DEPTH, n // tn),
                          in_specs=[_const((16, D)), pl.BlockSpec((None, 16, tn), lambda l, j: (l, 0, j))],
                          out_specs=pl.BlockSpec((None, D, tn), lambda l, j: (l, 0, j)),
                          compiler_params=_cp(2))(c16, dmod)


def _place():
    x, y, c = lax.axis_index("x"), lax.axis_index("y"), lax.axis_index("c")
    other_chips = [(1 - x, y), (x, 1 - y), (1 - x, 1 - y)]
    return x, y, c, other_chips


def _allgather_small(name, v, with_sum):
    m, n = v.shape

    def body(x_ref, out_ref, *rest):
        if with_sum:
            sum_ref, send_sems, recv_sems, local_sem = rest
        else:
            send_sems, recv_sems, local_sem = rest
        x, y, c, chips = _place()
        me, sibling = (x, y, c), (x, y, 1 - c)

        def rows(px, py, pc):
            return out_ref.at[pl.ds((4 * px + 2 * py + pc) * m, m), :]

        def copy(k, block, to, src=None):
            return pltpu.make_async_remote_copy(
                src_ref=rows(*block) if src is None else src, dst_ref=rows(*block),
                send_sem=send_sems.at[k], recv_sem=recv_sems.at[k], device_id=to, device_id_type=MESH)

        mine = pltpu.make_async_copy(x_ref, rows(*me), local_sem)
        mine.start()
        first = [copy(0, me, sibling, src=x_ref)]
        first += [copy(1 + j, me, (*chip, c), src=x_ref) for j, chip in enumerate(chips)]
        for cp in first:
            cp.start()
        passed = [copy(4 + j, (*chip, c), sibling) for j, chip in enumerate(chips)]
        for j, chip in enumerate(chips):
            copy(1 + j, (*chip, c), me).wait_recv()
            passed[j].start()
        copy(0, sibling, me).wait_recv()
        for j, chip in enumerate(chips):
            copy(4 + j, (*chip, 1 - c), me).wait_recv()
        for cp in first + passed:
            cp.wait_send()
        mine.wait()
        if with_sum:
            acc = out_ref[0:m, :]
            for k in range(1, N_DEV):
                acc = acc + out_ref[k * m:(k + 1) * m, :]
            sum_ref[...] = acc

    vm = pl.BlockSpec(memory_space=pltpu.VMEM)
    out_shape = [SDS((N_DEV * m, n), F32)] + ([SDS((m, n), F32)] if with_sum else [])
    res = pl.pallas_call(
        body, name=name, out_shape=tuple(out_shape), in_specs=[vm], out_specs=tuple([vm] * len(out_shape)),
        scratch_shapes=[pltpu.SemaphoreType.DMA((7,)), pltpu.SemaphoreType.DMA((7,)), pltpu.SemaphoreType.DMA],
        compiler_params=pltpu.CompilerParams(vmem_limit_bytes=VMEM_LIMIT_MB * 1024 * 1024))(v)
    return res if with_sum else res[0]


HBM = pl.BlockSpec(memory_space=pltpu.HBM)


def _sem_scratch(n_remote, n_local):
    return [pltpu.SemaphoreType.DMA((n_remote,)), pltpu.SemaphoreType.DMA((n_remote,)),
            pltpu.SemaphoreType.DMA((n_local,))]


DMA_PIECE_BYTES = 1 << 20


def _pieces(src, dst):
    *lead, rows, n = src.shape
    nsplit = max(1, min(rows // 16, (rows * n * jnp.dtype(src.dtype).itemsize) // DMA_PIECE_BYTES))
    while rows % nsplit or (rows // nsplit) % 16:
        nsplit -= 1
    size = rows // nsplit
    out = []
    for idx in itertools.product(*[range(d) for d in lead]):
        for i in range(nsplit):
            sl = tuple(idx) + (pl.ds(i * size, size),)
            out.append((src.at[sl], dst.at[sl]))
    return out


def _local_copies(src, dst, sem):
    return ([pltpu.make_async_copy(s_, d_, sem) for s_, d_ in _pieces(src, dst)],
            pltpu.make_async_copy(src, dst, sem))


def _remote_copies(src, dst, send_sem, recv_sem, to):
    mk = lambda s_, d_: pltpu.make_async_remote_copy(src_ref=s_, dst_ref=d_, send_sem=send_sem, recv_sem=recv_sem,
                                                     device_id=to, device_id_type=MESH)
    return [mk(s_, d_) for s_, d_ in _pieces(src, dst)], mk(src, dst)


def _gather_weights(shards):
    nq = len(shards)

    def body(*refs):
        ins, outs = refs[:nq], refs[nq:2 * nq]
        send_sems, recv_sems, local_sems = refs[2 * nq:]
        x, y, c, chips = _place()
        me_chip = 2 * x + y
        started, local_all, send_all, recv_all = [], [], [], []
        for q in range(nq):
            cps, whole = _local_copies(ins[q], outs[q].at[:, me_chip], local_sems.at[q])
            started += cps
            local_all.append(whole)
            for r, (px, py) in enumerate(chips):
                k = 3 * q + r
                cps, whole = _remote_copies(ins[q], outs[q].at[:, me_chip], send_sems.at[k], recv_sems.at[k], (px, py, c))
                started += cps
                send_all.append(whole)
                recv_all.append(_remote_copies(ins[q], outs[q].at[:, 2 * px + py], send_sems.at[k], recv_sems.at[k],
                                               (px, py, c))[1])
        for cp in started:
            cp.start()
        for cp in recv_all:
            cp.wait_recv()
        for cp in send_all:
            cp.wait_send()
        for cp in local_all:
            cp.wait()

    out_shape = tuple(SDS((s.shape[0], N_CHIPS) + s.shape[1:], s.dtype) for s in shards)
    return pl.pallas_call(body, name="gather_weights", out_shape=out_shape, in_specs=[HBM] * nq,
                          out_specs=tuple([HBM] * nq), scratch_shapes=_sem_scratch(3 * nq, nq))(*shards)


def _grad_exchange(gs):
    nq = len(gs)

    def body(*refs):
        ins, outs = refs[:nq], refs[nq:2 * nq]
        send_sems, recv_sems, local_sems = refs[2 * nq:]
        x, y, c, chips = _place()
        peers = [(2 * r + e, (px, py, c if e == 0 else 1 - c)) for r, (px, py) in enumerate(chips) for e in (0, 1)]
        peers.append((6, (x, y, 1 - c)))
        started, local_all, remote_all = [], [], []
        for q in range(nq):
            cps, whole = _local_copies(ins[q].at[:, 2 * x + y, c], outs[q].at[:, 7], local_sems.at[q])
            started += cps
            local_all.append(whole)
            for slot, (px, py, pc) in peers:
                k = 7 * q + slot
                cps, whole = _remote_copies(ins[q].at[:, 2 * px + py, pc], outs[q].at[:, slot], send_sems.at[k],
                                            recv_sems.at[k], (px, py, pc))
                started += cps
                remote_all.append(whole)
        for cp in started:
            cp.start()
        for cp in remote_all:
            cp.wait_recv()
        for cp in remote_all:
            cp.wait_send()
        for cp in local_all:
            cp.wait()

    out_shape = tuple(SDS((g.shape[0], N_DEV, g.shape[3], g.shape[4]), g.dtype) for g in gs)
    return pl.pallas_call(body, name="grad_exchange", out_shape=out_shape, in_specs=[HBM] * nq,
                          out_specs=tuple([HBM] * nq), scratch_shapes=_sem_scratch(7 * nq, nq))(*gs)


def _pair_share(rs):
    nq = len(rs)

    def body(*refs):
        ins, outs = refs[:nq], refs[nq:2 * nq]
        send_sems, recv_sems, local_sems = refs[2 * nq:]
        x, y, c, _ = _place()
        started, local_all, send_all, recv_all = [], [], [], []
        for q in range(nq):
            cps, whole = _local_copies(ins[q], outs[q].at[:, c], local_sems.at[q])
            started += cps
            local_all.append(whole)
            cps, whole = _remote_copies(ins[q], outs[q].at[:, c], send_sems.at[q], recv_sems.at[q], (x, y, 1 - c))
            started += cps
            send_all.append(whole)
            recv_all.append(_remote_copies(ins[q], outs[q].at[:, 1 - c], send_sems.at[q], recv_sems.at[q],
                                           (x, y, 1 - c))[1])
        for cp in started:
            cp.start()
        for cp in recv_all:
            cp.wait_recv()
        for cp in send_all:
            cp.wait_send()
        for cp in local_all:
            cp.wait()

    out_shape = tuple(SDS((r.shape[0], 2) + r.shape[1:], r.dtype) for r in rs)
    return pl.pallas_call(body, name="grad_pair_share", out_shape=out_shape, in_specs=[HBM] * nq,
                          out_specs=tuple([HBM] * nq), scratch_shapes=_sem_scratch(nq, nq))(*rs)


def _pack(arrs, rows_multiple=8):
    flat = jnp.concatenate([a.astype(F32).reshape(-1) for a in arrs])
    pad = (-flat.shape[0]) % (128 * rows_multiple)
    return jnp.pad(flat, (0, pad)).reshape(-1, 128)


def _unpack(slab, shapes):
    flat = slab.reshape(-1)
    out, off = [], 0
    for shp in shapes:
        n = 1
        for d in shp:
            n *= d
        out.append(flat[off:off + n].reshape(shp))
        off += n
    return out


def _shard_last(a, chip, n):
    return lax.dynamic_slice_in_dim(a, chip * n, n, axis=a.ndim - 1)


def kernel(x, c, w_mod, b_mod, norm_g, sc_w_in, sc_conv, sc_w_out, pool_w, pool_b, pool_scale, cf_w_pw1, cf_b_pw1, cf_w_dw, cf_b_dw, cf_ln_g, cf_ln_b, cf_w_pw2, cf_b_pw2, ffn_w_up, ffn_conv, ffn_b_conv, ffn_w_down, loss_target, m_w_mod, m_b_mod, m_norm_g, m_sc_w_in, m_sc_conv, m_sc_w_out, m_pool_w, m_pool_b, m_pool_scale, m_cf_w_pw1, m_cf_b_pw1, m_cf_w_dw, m_cf_b_dw, m_cf_ln_g, m_cf_ln_b, m_cf_w_pw2, m_cf_b_pw2, m_ffn_w_up, m_ffn_conv, m_ffn_b_conv, m_ffn_w_down, v_w_mod, v_b_mod, v_norm_g, v_sc_w_in, v_sc_conv, v_sc_w_out, v_pool_w, v_pool_b, v_pool_scale, v_cf_w_pw1, v_cf_b_pw1, v_cf_w_dw, v_cf_b_dw, v_cf_ln_g, v_cf_ln_b, v_cf_w_pw2, v_cf_b_pw2, v_ffn_w_up, v_ffn_conv, v_ffn_b_conv, v_ffn_w_down):
    ax, ay, ac = lax.axis_index("x"), lax.axis_index("y"), lax.axis_index("c")
    chip = 2 * ax + ay
    dev = 4 * ax + 2 * ay + ac
    xs = x[0]
    target = loss_target[0]

    small_sharded = [norm_g, sc_conv, cf_b_pw1, cf_w_dw, cf_b_dw, cf_ln_g, cf_ln_b, cf_b_pw2, ffn_conv]
    slab = _pack([c] + small_sharded)
    gathered = _allgather_small("gather_small_params", slab, False).reshape(N_DEV, -1, 128)
    parts = [_unpack(gathered[d], [c.shape] + [a.shape for a in small_sharded]) for d in range(N_DEV)]
    c_all = jnp.concatenate([p[0] for p in parts], axis=0)
    full = [jnp.concatenate([parts[2 * j][1 + i] for j in range(N_CHIPS)], axis=-1)
            for i in range(len(small_sharded))]
    norm_g_f, sc_conv_f, cf_b_pw1_f, cf_w_dw_f, cf_b_dw_f, cf_ln_g_f, cf_ln_b_f, cf_b_pw2_f, ffn_conv_f = full
    c16 = jnp.pad(c_all, ((0, 8), (0, 0)))

    n_mod = w_mod.shape[2]
    b_sh = jnp.broadcast_to(_shard_last(b_mod, chip, n_mod)[:, None, :], (DEPTH, 8, n_mod))
    mod_part = _mod_fwd(c16, w_mod, b_sh)
    mod_g = _allgather_small("gather_mod", mod_part.reshape(DEPTH * 16, n_mod), False)
    mod_g = mod_g.reshape(N_DEV, DEPTH, 16, n_mod)
    mod_mine = jnp.concatenate(
        [lax.dynamic_index_in_dim(mod_g[2 * j], dev, axis=1, keepdims=False) for j in range(N_CHIPS)], axis=-1)
    mod = mod_mine.reshape(DEPTH, 6, D)

    shards = [sc_w_in.astype(BF), sc_w_out.astype(BF), pool_w[0].astype(BF), cf_w_pw1.astype(BF),
              cf_w_pw2.astype(BF), ffn_w_up.astype(BF), ffn_w_down.astype(BF)]
    w_in_f, w_out_f, pool_f, pw1_f, pw2_f, w_up_f, w_down_f = _gather_weights(shards)
    w_out_f = w_out_f.reshape(w_out_f.shape[0], D, D)
    pool_f = pool_f.reshape(4, POOL_GROUP, POOL_GROUP)
    pw2_f = pw2_f.reshape(1, D, D)
    w_down_f = w_down_f.reshape(DEPTH, F, D)

    zero_d = jnp.zeros((D,), F32)
    mods = [[mod[i, k] for k in range(6)] for i in range(DEPTH)]
    saved = []
    xcur = xs
    h_next = None
    for i in range(DEPTH):
        kind, j = i % 3, i // 3
        sh1, sc1, g1, sh2, sc2, g2 = mods[i]
        st = {"x0": xcur}
        m_bias = zero_d
        if kind != 1:
            h = h_next if h_next is not None else _pro_fwd(f"pro1_fwd_{i}", xcur, _vecs([norm_g_f[i, 0], sc1, sh1], D))
        if kind == 0:
            p = _mm_nn(f"sc_in_{i}", h, w_in_f, j, True, BF)
            z = _sc_act_fwd(f"sc_act_fwd_{i}", p, _vecs(list(sc_conv_f[j]), D))
            m = _mm_nn(f"sc_out_{i}", z, w_out_f, j, False, F32)
            st.update(h=h, p=p)
        elif kind == 1:
            pool_vec = _vecs([norm_g_f[i, 0], sc1, sh1, pool_b[j], pool_scale[j]], D)
            pooled, m = _pool_fwd(f"pool_fwd_{i}", xcur, pool_f, pool_vec)
            st.update(pooled=pooled)
        else:
            p = _mm_nn(f"cf_pw1_{i}", h, pw1_f, j, True, BF)
            taps = jnp.pad(cf_w_dw_f[j], ((0, 1), (0, 0)))
            cf_vec = _vecs([cf_b_pw1_f[j, :D], cf_b_pw1_f[j, D:], cf_b_dw_f[j], cf_ln_g_f[j], cf_ln_b_f[j]], D)
            wact, cv = _cf_act_fwd(f"cf_act_fwd_{i}", p, taps, cf_vec)
            m = _mm_nn(f"cf_pw2_{i}", wact, pw2_f, j, False, F32)
            m_bias = cf_b_pw2_f[j]
            st.update(h=h, p=p, wact=wact, cv=cv, taps=taps, cf_vec=cf_vec)
        x1, h2 = _epi_pro_fwd(f"epi1_fwd_{i}", xcur, m,
                              _vecs([g1, norm_g_f[i, 1], m_bias, norm_g_f[i, 2], sc2, sh2], D), True)
        st.update(m=m, x1=x1, m_bias=m_bias)
        up = _mm_nn(f"ffn_up_{i}", h2, w_up_f, i, True, BF)
        ffn_vec = _vecs(list(ffn_conv_f[i]) + [ffn_b_conv[i]], F)
        a = _ffn_act_fwd(f"ffn_act_fwd_{i}", up, ffn_vec)
        f = _mm_nn(f"ffn_down_{i}", a, w_down_f, i, False, F32, tm=512)
        nxt = i + 1
        fuse_next = nxt < DEPTH and nxt % 3 != 1
        rows = [g2, norm_g_f[i, 3], zero_d]
        if fuse_next:
            rows += [norm_g_f[nxt, 0], mods[nxt][1], mods[nxt][0]]
        xcur, h_next = _epi_pro_fwd(f"epi2_fwd_{i}", x1, f, _vecs(rows, D), fuse_next)
        st.update(h2=h2, up=up, a=a, f=f, ffn_vec=ffn_vec)
        saved.append(st)

    dy, loss_cols = _loss_fwd_bwd(xcur, target)
    loss = lax.psum(jnp.sum(loss_cols[0]), ("x", "y", "c"))

    dmod = [None] * DEPTH
    d_norm_g = [None] * DEPTH
    d_sc_conv = [None, None]
    d_ffn_conv, d_ffn_b = [None] * DEPTH, [None] * DEPTH
    g_w_in, g_w_out, g_w_up, g_w_down = [None, None], [None, None], [None] * DEPTH, [None] * DEPTH
    dxo = dy
    last = DEPTH - 1
    pend = _epi_bwd(f"epi2_bwd_{last}", dy, saved[last]["f"], _vecs([mods[last][5], norm_g_f[last, 3], zero_d], D))
    for i in reversed(range(DEPTH)):
        kind, j = i % 3, i // 3
        st = saved[i]
        sh1, sc1, g1, sh2, sc2, g2 = mods[i]
        df, e2 = pend
        da = _mm_nt(f"ffn_down_dx_{i}", df, w_down_f, i, False, BF, tn=F // 2)
        dup, fsum = _ffn_act_bwd(f"ffn_act_bwd_{i}", st["up"], da, st["ffn_vec"])
        g_w_down[i] = _mm_tn(f"ffn_down_dw_{i}", st["a"], df, False, tk=F // 2)
        dh2 = _mm_nt(f"ffn_up_dx_{i}", dup, w_up_f, i, True, F32)
        g_w_up[i] = _mm_tn(f"ffn_up_dw_{i}", st["h2"], dup, True)
        dx1, dm, s21 = _pro_epi_bwd(f"pro2_epi1_bwd_{i}", dh2, st["x1"], dxo, st["m"],
                                    _vecs([norm_g_f[i, 2], sc2, g1, norm_g_f[i, 1], st["m_bias"]], D))
        p2, e1 = s21[0:3], s21[3:6]
        if kind == 0:
            dz = _mm_nt(f"sc_out_dx_{i}", dm, w_out_f, j, False, BF)
            z, dp, ssum = _sc_act_bwd(f"sc_act_bwd_{i}", st["p"], dz, _vecs(list(sc_conv_f[j]), D))
            g_w_out[j] = _mm_tn(f"sc_out_dw_{i}", z, dm, False)
            dh = _mm_nt(f"sc_in_dx_{i}", dp, w_in_f, j, True, F32)
            g_w_in[j] = _mm_tn(f"sc_in_dw_{i}", st["h"], dp, True)
            d_sc_conv[j] = ssum[0:3]
        elif kind == 1:
            dh, g_pool, psum = _pool_bwd(f"pool_bwd_{i}", st["pooled"], dm, pool_f, _vecs([pool_b[j], pool_scale[j]], D))
        else:
            dwact = _mm_nt(f"cf_pw2_dx_{i}", dm, pw2_f, j, False, BF)
            g_pw2 = _mm_tn(f"cf_pw2_dw_{i}", st["wact"], dm, False)
            dp, tsum, csum = _cf_act_bwd(f"cf_act_bwd_{i}", st["p"], st["cv"], dwact, st["taps"], st["cf_vec"])
            dh = _mm_nt(f"cf_pw1_dx_{i}", dp, pw1_f, j, True, F32)
            g_pw1 = _mm_tn(f"cf_pw1_dw_{i}", st["h"], dp, True)
            d_cf = dict(b_pw1=jnp.concatenate([csum[3], csum[4]])[None], w_dw=tsum[None, :CF_TAPS], b_dw=csum[0:1],
                        ln_g=csum[1:2], ln_b=csum[2:3], b_pw2=e1[2:3])
        if i > 0:
            prev = i - 1
            dxo, df_prev, s12 = _pro_epi_bwd(f"pro1_epi2_bwd_{i}", dh, st["x0"], dx1, saved[prev]["f"],
                                             _vecs([norm_g_f[i, 0], sc1, mods[prev][5], norm_g_f[prev, 3], zero_d], D))
            p1, pend = s12[0:3], (df_prev, s12[3:6])
        else:
            dxo, p1 = _pro_bwd(f"pro1_bwd_{i}", dh, st["x0"], dx1, _vecs([norm_g_f[i, 0], sc1], D))
        dmod[i] = jnp.concatenate([p1[0], p1[1], e1[0], p2[0], p2[1], e2[0]])
        d_norm_g[i] = jnp.stack([p1[2], e1[1], p2[2], e2[1]])
        if kind == 1:
            d_pool_b, d_pool_scale = psum[0:1], psum[1:2]
        saved[i] = None
        st = None
        if i == 0:
            grad_x = dxo[None]
        d_ffn_conv[i], d_ffn_b[i] = fsum[1:4], fsum[0]

    small_shapes = [(DEPTH, 6 * D), (DEPTH, 4, D), (2, 3, D), (1, D), (1, D), (1, 2 * D), (1, CF_TAPS, D), (1, D),
                    (1, D), (1, D), (1, D), (DEPTH, 3, F), (DEPTH, F)]
    small = [jnp.stack(dmod), jnp.stack(d_norm_g), jnp.stack(d_sc_conv), d_pool_b, d_pool_scale, d_cf["b_pw1"],
             d_cf["w_dw"], d_cf["b_dw"], d_cf["ln_g"], d_cf["ln_b"], d_cf["b_pw2"], jnp.stack(d_ffn_conv),
             jnp.stack(d_ffn_b)]
    gsl, ssl = _allgather_small("reduce_small_grads", _pack(small), True)
    tot = _unpack(ssl, small_shapes)
    (gb_mod, gt_norm_g, gt_sc_conv, g_pool_b, g_pool_scale, gt_b_pw1, gt_w_dw, gt_b_dw, gt_ln_g, gt_ln_b, gt_b_pw2,
     gt_ffn_conv, g_ffn_b) = tot
    dmod_all = jnp.stack([_unpack(gsl.reshape(N_DEV, -1, 128)[d], small_shapes[:1])[0] for d in range(N_DEV)], axis=1)
    dmod_sh = jnp.pad(_shard_last(dmod_all, chip, n_mod), ((0, 0), (0, 8), (0, 0)))
    g_w_mod = _mod_bwd(c16, dmod_sh)
    g_norm_g = _shard_last(gt_norm_g, chip, D // 4)
    g_sc_conv = _shard_last(gt_sc_conv, chip, D // 4)
    g_b_pw1 = _shard_last(gt_b_pw1, chip, 2 * D // 4)
    g_w_dw = _shard_last(gt_w_dw, chip, D // 4)
    g_b_dw = _shard_last(gt_b_dw, chip, D // 4)
    g_ln_g = _shard_last(gt_ln_g, chip, D // 4)
    g_ln_b = _shard_last(gt_ln_b, chip, D // 4)
    g_b_pw2 = _shard_last(gt_b_pw2, chip, D // 4)
    g_ffn_conv = _shard_last(gt_ffn_conv, chip, F // 4)

    def halves(gl):
        g = jnp.stack(gl) if isinstance(gl, list) else gl
        if g.ndim == 3:
            g = g.reshape(g.shape[0], N_CHIPS, g.shape[1] // N_CHIPS, g.shape[2])
        nl, _, k, n = g.shape
        return g.reshape(nl, N_CHIPS, 2, k // 2, n)

    partial = [halves(g_w_in), halves(g_w_out), halves(g_pool.astype(BF)), halves([g_pw1]), halves([g_pw2]),
               halves(g_w_up), halves(g_w_down)]
    pieces = _grad_exchange(partial)
    reduced = [_add_slots(f"grad_sum_{q}", pieces[q]) for q in range(len(pieces))]
    shared = _pair_share(reduced)
    big_w = [sc_w_in, sc_w_out, pool_w, cf_w_pw1, cf_w_pw2, ffn_w_up, ffn_w_down]
    big_g = [shared[q].reshape(big_w[q].shape) for q in range(len(big_w))]
    g_sc_w_in, g_sc_w_out, g_pool_w, g_cf_w_pw1, g_cf_w_pw2, g_ffn_w_up, g_ffn_w_down = big_g

    def adam_big(name, w, g, m, v):
        shp = w.shape
        two = lambda t: t.reshape(-1, shp[-1])
        return [o.reshape(shp) for o in _adamw(name, two(w), two(g), two(m), two(v))]

    grads = dict(w_mod=g_w_mod, b_mod=gb_mod, norm_g=g_norm_g, sc_w_in=g_sc_w_in, sc_conv=g_sc_conv,
                 sc_w_out=g_sc_w_out, pool_w=g_pool_w, pool_b=g_pool_b, pool_scale=g_pool_scale,
                 cf_w_pw1=g_cf_w_pw1, cf_b_pw1=g_b_pw1, cf_w_dw=g_w_dw, cf_b_dw=g_b_dw, cf_ln_g=g_ln_g,
                 cf_ln_b=g_ln_b, cf_w_pw2=g_cf_w_pw2, cf_b_pw2=g_b_pw2, ffn_w_up=g_ffn_w_up, ffn_conv=g_ffn_conv,
                 ffn_b_conv=g_ffn_b, ffn_w_down=g_ffn_w_down)
    weights = dict(w_mod=w_mod, b_mod=b_mod, norm_g=norm_g, sc_w_in=sc_w_in, sc_conv=sc_conv, sc_w_out=sc_w_out,
                   pool_w=pool_w, pool_b=pool_b, pool_scale=pool_scale, cf_w_pw1=cf_w_pw1, cf_b_pw1=cf_b_pw1,
                   cf_w_dw=cf_w_dw, cf_b_dw=cf_b_dw, cf_ln_g=cf_ln_g, cf_ln_b=cf_ln_b, cf_w_pw2=cf_w_pw2,
                   cf_b_pw2=cf_b_pw2, ffn_w_up=ffn_w_up, ffn_conv=ffn_conv, ffn_b_conv=ffn_b_conv,
                   ffn_w_down=ffn_w_down)
    m_in = dict(w_mod=m_w_mod, b_mod=m_b_mod, norm_g=m_norm_g, sc_w_in=m_sc_w_in, sc_conv=m_sc_conv,
                sc_w_out=m_sc_w_out, pool_w=m_pool_w, pool_b=m_pool_b, pool_scale=m_pool_scale,
                cf_w_pw1=m_cf_w_pw1, cf_b_pw1=m_cf_b_pw1, cf_w_dw=m_cf_w_dw, cf_b_dw=m_cf_b_dw, cf_ln_g=m_cf_ln_g,
                cf_ln_b=m_cf_ln_b, cf_w_pw2=m_cf_w_pw2, cf_b_pw2=m_cf_b_pw2, ffn_w_up=m_ffn_w_up,
                ffn_conv=m_ffn_conv, ffn_b_conv=m_ffn_b_conv, ffn_w_down=m_ffn_w_down)
    v_in = dict(w_mod=v_w_mod, b_mod=v_b_mod, norm_g=v_norm_g, sc_w_in=v_sc_w_in, sc_conv=v_sc_conv,
                sc_w_out=v_sc_w_out, pool_w=v_pool_w, pool_b=v_pool_b, pool_scale=v_pool_scale,
                cf_w_pw1=v_cf_w_pw1, cf_b_pw1=v_cf_b_pw1, cf_w_dw=v_cf_w_dw, cf_b_dw=v_cf_b_dw, cf_ln_g=v_cf_ln_g,
                cf_ln_b=v_cf_ln_b, cf_w_pw2=v_cf_w_pw2, cf_b_pw2=v_cf_b_pw2, ffn_w_up=v_ffn_w_up,
                ffn_conv=v_ffn_conv, ffn_b_conv=v_ffn_b_conv, ffn_w_down=v_ffn_w_down)
    names = list(weights)
    big_names = ["w_mod", "sc_w_in", "sc_w_out", "pool_w", "cf_w_pw1", "cf_w_pw2", "ffn_w_up", "ffn_w_down"]
    small_names = [n for n in names if n not in big_names]
    delta, new_m, new_v = {}, {}, {}
    for n in big_names:
        delta[n], new_m[n], new_v[n] = adam_big(f"adamw_{n}", weights[n], grads[n], m_in[n], v_in[n])
    grads = {n: grads[n].reshape(weights[n].shape) for n in names}
    sm_shapes = [weights[n].shape for n in small_names]
    sd, sm, sv = _adamw("adamw_small", _pack([weights[n] for n in small_names]), _pack([grads[n] for n in small_names]),
                        _pack([m_in[n] for n in small_names]), _pack([v_in[n] for n in small_names]))
    for n, d_, m_, v_ in zip(small_names, _unpack(sd, sm_shapes), _unpack(sm, sm_shapes), _unpack(sv, sm_shapes)):
        delta[n], new_m[n], new_v[n] = d_, m_, v_

    return (loss, grad_x, *[grads[n] for n in names], *[delta[n] for n in names], *[new_m[n] for n in names],
            *[new_v[n] for n in names])
```

```python
import itertools

import jax
import jax.numpy as jnp
from jax import lax
from jax.experimental import pallas as pl
from jax.experimental.pallas import tpu as pltpu
from jax.experimental.pallas import tpu_sc as plsc

D = 1024
F = 2816
DEPTH = 4
POOL_WINDOWS = (2, 4, 8, 16)
POOL_GROUP = 256
CF_TAPS = 31
RMS_EPS = 1e-6
LN_EPS = 1e-5
ADAM_LR = 0.001
ADAM_B1 = 0.9
ADAM_B2 = 0.999
ADAM_EPS = 1e-08
ADAM_WD = 0.01
ADAM_STEP = 10

BF = jnp.bfloat16
F32 = jnp.float32
MESH = pl.DeviceIdType.MESH
SDS = jax.ShapeDtypeStruct
N_CHIPS = 4
N_DEV = 8
SEQUENCER_GATHER_ID = 1
VMEM_LIMIT_MB = 56
ROW_TILE_D = 256
ROW_TILE_F = 256
HALO = 8
HALO_BF = 16
CF_HALO = 32
POOL_HALO = 16


def _cp(n_axes):
    return pltpu.CompilerParams(dimension_semantics=("arbitrary",) * n_axes,
                                vmem_limit_bytes=VMEM_LIMIT_MB * 1024 * 1024)


def _tile(n, pref):
    t = min(n, pref)
    assert n % t == 0, (n, pref)
    return t


def _const(shape):
    nd = len(shape)
    return pl.BlockSpec(shape, lambda *_: (0,) * nd)


def _vecs(rows, width):
    v = jnp.stack([r.reshape(width).astype(F32) for r in rows])
    pad = (-v.shape[0]) % 8
    return jnp.pad(v, ((0, pad), (0, 0)))


def _sigmoid(v):
    return 0.5 * jnp.tanh(0.5 * v) + 0.5


def _down(prev8, g, k):
    n = g.shape[0]
    return pltpu.roll(jnp.concatenate([prev8, g], axis=0), k, 0)[8:8 + n]


def _up(g, next8, k):
    n = g.shape[0]
    return pltpu.roll(jnp.concatenate([g, next8], axis=0), n + 8 - k, 0)[0:n]


def _fold8(v):
    acc = v[0:8]
    for r in range(8, v.shape[0], 8):
        acc = acc + v[r:r + 8]
    return acc


def _mm(name, a, b, out_sds, grid, a_spec, b_spec, o_spec, acc_shape, dn):
    nk = grid[2]

    def body(a_ref, b_ref, o_ref, *acc):
        part = lax.dot_general(a_ref[...], b_ref[...], dn, preferred_element_type=F32)
        if nk == 1:
            o_ref[...] = part.astype(o_ref.dtype)
        else:
            acc_ref = acc[0]
            k = pl.program_id(2)

            @pl.when(k == 0)
            def _():
                acc_ref[...] = part

            @pl.when(k > 0)
            def _():
                acc_ref[...] += part

            @pl.when(k == nk - 1)
            def _():
                o_ref[...] = acc_ref[...].astype(o_ref.dtype)

    scratch = [] if nk == 1 else [pltpu.VMEM(acc_shape, F32)]
    return pl.pallas_call(body, name=name, out_shape=out_sds, grid=grid, in_specs=[a_spec, b_spec],
                          out_specs=o_spec, scratch_shapes=scratch, compiler_params=_cp(3))(a, b)


NN = (((1,), (0,)), ((), ()))
NT = (((1,), (1,)), ((), ()))
TN = (((0,), (0,)), ((), ()))


def _mm_nn(name, a, w, layer, col_sharded, out_dtype, tm=1024, tn=None):
    m, k = a.shape
    tm = _tile(m, tm)
    if col_sharded:
        n4 = w.shape[3]
        tn = n4 if tn is None else tn
        tpc = n4 // tn
        n = N_CHIPS * n4
        b_spec = pl.BlockSpec((None, None, k, tn), lambda i, j, kk: (layer, j // tpc, 0, j % tpc))
    else:
        n = w.shape[2]
        tn = n if tn is None else tn
        b_spec = pl.BlockSpec((None, k, tn), lambda i, j, kk: (layer, 0, j))
    return _mm(name, a, w, SDS((m, n), out_dtype), (m // tm, n // tn, 1),
               pl.BlockSpec((tm, k), lambda i, j, kk: (i, 0)), b_spec,
               pl.BlockSpec((tm, tn), lambda i, j, kk: (i, j)), None, NN)


def _mm_nt(name, g, w, layer, col_sharded, out_dtype, tm=1024, tn=None):
    m, n = g.shape
    if col_sharded:
        return _mm_nt_staged(name, g, w, layer, out_dtype)
    tm = _tile(m, tm)
    kdim = w.shape[1]
    tn = kdim if tn is None else tn
    return _mm(name, g, w, SDS((m, kdim), out_dtype), (m // tm, kdim // tn, 1),
               pl.BlockSpec((tm, n), lambda i, j, kk: (i, 0)),
               pl.BlockSpec((None, tn, n), lambda i, j, kk: (layer, j, 0)),
               pl.BlockSpec((tm, tn), lambda i, j, kk: (i, j)), None, NT)


def _mm_nt_staged(name, g, w, layer, out_dtype, tm=512):
    m, n = g.shape
    kdim, n4 = w.shape[2], w.shape[3]
    tm = _tile(m, tm)

    def body(g_ref, w_hbm, o_ref, wbuf, sems):
        @pl.when(pl.program_id(0) == 0)
        def _():
            cps = [pltpu.make_async_copy(w_hbm.at[layer, j], wbuf.at[:, pl.ds(j * n4, n4)], sems.at[j])
                   for j in range(N_CHIPS)]
            for cp in cps:
                cp.start()
            for cp in cps:
                cp.wait()

        o_ref[...] = lax.dot_general(g_ref[...], wbuf[...], NT, preferred_element_type=F32).astype(o_ref.dtype)

    return pl.pallas_call(body, name=name, out_shape=SDS((m, kdim), out_dtype), grid=(m // tm,),
                          in_specs=[pl.BlockSpec((tm, n), lambda i: (i, 0)), pl.BlockSpec(memory_space=pltpu.HBM)],
                          out_specs=pl.BlockSpec((tm, kdim), lambda i: (i, 0)),
                          scratch_shapes=[pltpu.VMEM((kdim, n), BF), pltpu.SemaphoreType.DMA((N_CHIPS,))],
                          compiler_params=_cp(1))(g, w)


def _mm_tn(name, a, g, col_sharded, tk=None, tn=None, ts=2048):
    s, k = a.shape
    n = g.shape[1]
    ts = _tile(s, ts)
    tk = k if tk is None else tk
    if col_sharded:
        n4 = n // N_CHIPS
        tn = n4 if tn is None else tn
        tpc = n4 // tn
        out_sds = SDS((N_CHIPS, k, n4), BF)
        o_spec = pl.BlockSpec((None, tk, tn), lambda i, j, ss: (j // tpc, i, j % tpc))
    else:
        tn = n if tn is None else tn
        out_sds = SDS((k, n), BF)
        o_spec = pl.BlockSpec((tk, tn), lambda i, j, ss: (i, j))
    return _mm(name, a, g, out_sds, (k // tk, n // tn, s // ts),
               pl.BlockSpec((ts, tk), lambda i, j, ss: (ss, i)),
               pl.BlockSpec((ts, tn), lambda i, j, ss: (ss, j)), o_spec, (tk, tn), TN)


def _rows(tm, width, col=0):
    return pl.BlockSpec((tm, width), lambda i: (i, col))


def _pro_fwd(name, x, vec):
    s = x.shape[0]
    tm = _tile(s, ROW_TILE_D)

    def body(x_ref, v_ref, h_ref):
        xv = x_ref[...]
        r = lax.rsqrt(jnp.mean(xv * xv, axis=-1, keepdims=True) + RMS_EPS)
        a = v_ref[0:1, :] * (1.0 + v_ref[1:2, :])
        h_ref[...] = (xv * r * a + v_ref[2:3, :]).astype(h_ref.dtype)

    return pl.pallas_call(body, name=name, out_shape=SDS((s, D), BF), grid=(s // tm,),
                          in_specs=[_rows(tm, D), _const((8, D))], out_specs=_rows(tm, D),
                          compiler_params=_cp(1))(x, vec)


def _epi_pro_fwd(name, x, m, vec, with_next):
    s = x.shape[0]
    tm = _tile(s, ROW_TILE_D)

    def body(x_ref, m_ref, v_ref, o_ref, *h_ref):
        mv = m_ref[...] + v_ref[2:3, :]
        rm = lax.rsqrt(jnp.mean(mv * mv, axis=-1, keepdims=True) + RMS_EPS)
        xo = x_ref[...] + v_ref[0:1, :] * (mv * rm * v_ref[1:2, :])
        o_ref[...] = xo
        if with_next:
            r = lax.rsqrt(jnp.mean(xo * xo, axis=-1, keepdims=True) + RMS_EPS)
            a = v_ref[3:4, :] * (1.0 + v_ref[4:5, :])
            h_ref[0][...] = (xo * r * a + v_ref[5:6, :]).astype(BF)

    out_shape = (SDS((s, D), F32),) + ((SDS((s, D), BF),) if with_next else ())
    res = pl.pallas_call(body, name=name, out_shape=out_shape, grid=(s // tm,),
                         in_specs=[_rows(tm, D), _rows(tm, D), _const((8, D))],
                         out_specs=tuple([_rows(tm, D)] * len(out_shape)), compiler_params=_cp(1))(x, m, vec)
    return res if with_next else (res[0], None)


def _loss_fwd_bwd(y, t):
    s = y.shape[0]
    tm = _tile(s, ROW_TILE_D)

    def body(y_ref, t_ref, dy_ref, acc_ref):
        @pl.when(pl.program_id(0) == 0)
        def _():
            acc_ref[...] = jnp.zeros_like(acc_ref)

        e = y_ref[...] - t_ref[...]
        dy_ref[...] = e * (1.0 / D)
        acc_ref[0:1, :] += jnp.sum(e * e, axis=0, keepdims=True) * (0.5 / D)

    return pl.pallas_call(body, name="loss", out_shape=(SDS((s, D), F32), SDS((8, D), F32)), grid=(s // tm,),
                          in_specs=[_rows(tm, D), _rows(tm, D)], out_specs=(_rows(tm, D), _const((8, D))),
                          compiler_params=_cp(1))(y, t)


def _epi_bwd_rows(dv, mv, g, ngb, dm_ref, acc_ref, row):
    rm = lax.rsqrt(jnp.mean(mv * mv, axis=-1, keepdims=True) + RMS_EPS)
    mn = mv * rm
    dmn = dv * (g * ngb)
    dm = rm * (dmn - mn * jnp.mean(dmn * mn, axis=-1, keepdims=True))
    dm_ref[...] = dm.astype(dm_ref.dtype)
    t = jnp.sum(dv * mn, axis=0, keepdims=True)
    acc_ref[row:row + 1, :] += t * ngb
    acc_ref[row + 1:row + 2, :] += t * g
    acc_ref[row + 2:row + 3, :] += jnp.sum(dm, axis=0, keepdims=True)


def _epi_bwd(name, dxo, m, vec):
    s = dxo.shape[0]
    tm = _tile(s, ROW_TILE_D)

    def body(d_ref, m_ref, v_ref, dm_ref, acc_ref):
        @pl.when(pl.program_id(0) == 0)
        def _():
            acc_ref[...] = jnp.zeros_like(acc_ref)

        _epi_bwd_rows(d_ref[...], m_ref[...] + v_ref[2:3, :], v_ref[0:1, :], v_ref[1:2, :], dm_ref, acc_ref, 0)

    return pl.pallas_call(body, name=name, out_shape=(SDS((s, D), BF), SDS((8, D), F32)), grid=(s // tm,),
                          in_specs=[_rows(tm, D), _rows(tm, D), _const((8, D))],
                          out_specs=(_rows(tm, D), _const((8, D))), compiler_params=_cp(1))(dxo, m, vec)


def _pro_epi_bwd(name, dh, x, dxo, m, vec):
    s = x.shape[0]
    tm = _tile(s, ROW_TILE_D)

    def body(dh_ref, x_ref, d_ref, m_ref, v_ref, dx_ref, dm_ref, acc_ref):
        @pl.when(pl.program_id(0) == 0)
        def _():
            acc_ref[...] = jnp.zeros_like(acc_ref)

        dx = _pro_bwd_rows(dh_ref[...].astype(F32), x_ref[...], d_ref[...], v_ref[0:1, :], v_ref[1:2, :], acc_ref)
        dx_ref[...] = dx
        _epi_bwd_rows(dx, m_ref[...] + v_ref[4:5, :], v_ref[2:3, :], v_ref[3:4, :], dm_ref, acc_ref, 3)

    return pl.pallas_call(body, name=name, out_shape=(SDS((s, D), F32), SDS((s, D), BF), SDS((8, D), F32)),
                          grid=(s // tm,),
                          in_specs=[_rows(tm, D), _rows(tm, D), _rows(tm, D), _rows(tm, D), _const((8, D))],
                          out_specs=(_rows(tm, D), _rows(tm, D), _const((8, D))),
                          compiler_params=_cp(1))(dh, x, dxo, m, vec)


def _pro_bwd_rows(dhv, xv, dxo, ng, sc, acc_ref):
    sc1 = 1.0 + sc
    r = lax.rsqrt(jnp.mean(xv * xv, axis=-1, keepdims=True) + RMS_EPS)
    xn = xv * r
    dxn = dhv * (ng * sc1)
    t = jnp.sum(dhv * xn, axis=0, keepdims=True)
    acc_ref[0:1, :] += jnp.sum(dhv, axis=0, keepdims=True)
    acc_ref[1:2, :] += t * ng
    acc_ref[2:3, :] += t * sc1
    return dxo + r * (dxn - xn * jnp.mean(dxn * xn, axis=-1, keepdims=True))


def _pro_bwd(name, dh, x, dxo, vec):
    s = x.shape[0]
    tm = _tile(s, ROW_TILE_D)

    def body(dh_ref, x_ref, d_ref, v_ref, dx_ref, acc_ref):
        @pl.when(pl.program_id(0) == 0)
        def _():
            acc_ref[...] = jnp.zeros_like(acc_ref)

        dx_ref[...] = _pro_bwd_rows(dh_ref[...].astype(F32), x_ref[...], d_ref[...], v_ref[0:1, :], v_ref[1:2, :],
                                    acc_ref)

    return pl.pallas_call(body, name=name, out_shape=(SDS((s, D), F32), SDS((8, D), F32)), grid=(s // tm,),
                          in_specs=[_rows(tm, D), _rows(tm, D), _rows(tm, D), _const((8, D))],
                          out_specs=(_rows(tm, D), _const((8, D))), compiler_params=_cp(1))(dh, x, dxo, vec)


def _carry_up(buf, tm, first, halo=HALO):
    @pl.when(first)
    def _():
        buf[tm:tm + halo, :] = jnp.zeros((halo, buf.shape[1]), F32)

    @pl.when(jnp.logical_not(first))
    def _():
        buf[tm:tm + halo, :] = buf[0:halo, :]


def _ffn_act_fwd(name, up, vec):
    s = up.shape[0]
    tm = _tile(s, ROW_TILE_F)
    rb_n = _tile(tm, 64)
    cw = 256

    def body(g_ref, v_ref, w_ref, a_ref, carry):
        @pl.when(pl.program_id(0) == 0)
        def _():
            carry[...] = jnp.zeros_like(carry)

        for cc in range(F // cw):
            cols = slice(cc * cw, (cc + 1) * cw)
            w0, w1, w2, b = w_ref[0:1, cols], w_ref[1:2, cols], w_ref[2:3, cols], w_ref[3:4, cols]

            def rb_body(rb, prev8):
                rows = pl.ds(pl.multiple_of(rb * rb_n, rb_n), rb_n)
                g = g_ref[rows, cols].astype(F32)
                gc = w2 * g + w1 * _down(prev8, g, 1) + w0 * _down(prev8, g, 2) + b
                a_ref[rows, cols] = (gc * _sigmoid(gc) * v_ref[rows, cols].astype(F32)).astype(a_ref.dtype)
                return g[rb_n - 8:rb_n]

            carry[:, cols] = lax.fori_loop(0, tm // rb_n, rb_body, carry[:, cols])

    return pl.pallas_call(body, name=name, out_shape=SDS((s, F), BF), grid=(s // tm,),
                          in_specs=[_rows(tm, F, 0), _rows(tm, F, 1), _const((8, F))], out_specs=_rows(tm, F),
                          scratch_shapes=[pltpu.VMEM((HALO, F), F32)], compiler_params=_cp(1))(up, up, vec)


def _prev_halo_spec(nt, tm, width, col):
    per = tm // HALO_BF
    return pl.BlockSpec((HALO_BF, width), lambda i: (jnp.maximum((nt - 1 - i) * per - 1, 0), col))


def _ffn_act_bwd(name, up, da, vec):
    s = up.shape[0]
    tm = _tile(s, ROW_TILE_F)
    nt = s // tm
    rev = lambda col: pl.BlockSpec((tm, F), lambda i: (nt - 1 - i, col))

    rb_n = _tile(tm, 64)
    nrb = tm // rb_n
    cw = 128

    def body(g_ref, gp_ref, v_ref, da_ref, w_ref, dup_ref, acc_ref, carry):
        step = pl.program_id(0)

        @pl.when(step == 0)
        def _():
            acc_ref[...] = jnp.zeros_like(acc_ref)
            carry[...] = jnp.zeros_like(carry)

        for cc in range(F // cw):
            cols = slice(cc * cw, (cc + 1) * cw)
            w0, w1, w2, b = w_ref[0:1, cols], w_ref[1:2, cols], w_ref[2:3, cols], w_ref[3:4, cols]
            halo = jnp.where(step < nt - 1, gp_ref[:, cols].astype(F32)[HALO_BF - 8:HALO_BF], 0.0)

            def rb_body(it, st):
                nxt8, ab, a0, a1, a2 = st
                rb = nrb - 1 - it
                r0 = pl.multiple_of(rb * rb_n, rb_n)
                rows = pl.ds(r0, rb_n)
                g = g_ref[rows, cols].astype(F32)
                ra = pl.multiple_of(jnp.maximum(r0 - HALO_BF, 0), HALO_BF)
                above = g_ref[pl.ds(ra, HALO_BF), cols].astype(F32)[HALO_BF - 8:HALO_BF]
                prev8 = jnp.where(rb == 0, halo, above)
                g1 = _down(prev8, g, 1)
                g2 = _down(prev8, g, 2)
                gc = w2 * g + w1 * g1 + w0 * g2 + b
                sg = _sigmoid(gc)
                sl = gc * sg
                val = v_ref[rows, cols].astype(F32)
                dav = da_ref[rows, cols].astype(F32)
                dup_ref[rows, F + cc * cw:F + (cc + 1) * cw] = (dav * sl).astype(dup_ref.dtype)
                dgc = (dav * val) * (sg + sl * (1.0 - sg))
                dup_ref[rows, cols] = (w2 * dgc + w1 * _up(dgc, nxt8, 1) + w0 * _up(dgc, nxt8, 2)).astype(dup_ref.dtype)
                return (dgc[0:8], ab + _fold8(dgc), a0 + _fold8(g2 * dgc), a1 + _fold8(g1 * dgc),
                        a2 + _fold8(g * dgc))

            z = jnp.zeros((8, cw), F32)
            nxt8, ab, a0, a1, a2 = lax.fori_loop(0, nrb, rb_body, (carry[:, cols], z, z, z, z))
            carry[:, cols] = nxt8
            acc_ref[0:1, cols] += jnp.sum(ab, axis=0, keepdims=True)
            acc_ref[1:2, cols] += jnp.sum(a0, axis=0, keepdims=True)
            acc_ref[2:3, cols] += jnp.sum(a1, axis=0, keepdims=True)
            acc_ref[3:4, cols] += jnp.sum(a2, axis=0, keepdims=True)

    return pl.pallas_call(
        body, name=name, out_shape=(SDS((s, 2 * F), BF), SDS((8, F), F32)), grid=(nt,),
        in_specs=[rev(0), _prev_halo_spec(nt, tm, F, 0), rev(1), rev(0), _const((8, F))],
        out_specs=(pl.BlockSpec((tm, 2 * F), lambda i: (nt - 1 - i, 0)), _const((8, F))),
        scratch_shapes=[pltpu.VMEM((HALO, F), F32)], compiler_params=_cp(1))(up, up, up, da, vec)


def _sc_act_fwd(name, p, vec):
    s = p.shape[0]
    tm = _tile(s, ROW_TILE_D)

    rb_n = _tile(tm, 64)
    cw = 256

    def body(b_ref, c_ref, h_ref, w_ref, z_ref, carry):
        @pl.when(pl.program_id(0) == 0)
        def _():
            carry[...] = jnp.zeros_like(carry)

        for cc in range(D // cw):
            cols = slice(cc * cw, (cc + 1) * cw)
            w0, w1, w2 = w_ref[0:1, cols], w_ref[1:2, cols], w_ref[2:3, cols]

            def rb_body(rb, prev8):
                rows = pl.ds(pl.multiple_of(rb * rb_n, rb_n), rb_n)
                q = c_ref[rows, cols].astype(F32) * h_ref[rows, cols].astype(F32)
                u = w2 * q + w1 * _down(prev8, q, 1) + w0 * _down(prev8, q, 2)
                z_ref[rows, cols] = (b_ref[rows, cols].astype(F32) * u).astype(z_ref.dtype)
                return q[rb_n - 8:rb_n]

            carry[:, cols] = lax.fori_loop(0, tm // rb_n, rb_body, carry[:, cols])

    return pl.pallas_call(body, name=name, out_shape=SDS((s, D), BF), grid=(s // tm,),
                          in_specs=[_rows(tm, D, 0), _rows(tm, D, 1), _rows(tm, D, 2), _const((8, D))],
                          out_specs=_rows(tm, D), scratch_shapes=[pltpu.VMEM((HALO, D), F32)],
                          compiler_params=_cp(1))(p, p, p, vec)


def _sc_act_bwd(name, p, dz, vec):
    s = p.shape[0]
    tm = _tile(s, ROW_TILE_D)
    nt = s // tm
    rev = lambda col: pl.BlockSpec((tm, D), lambda i: (nt - 1 - i, col))

    rb_n = _tile(tm, 64)
    nrb = tm // rb_n
    cw = 128

    def body(b_ref, c_ref, h_ref, cp_ref, hp_ref, dz_ref, w_ref, z_ref, dp_ref, acc_ref, carry):
        step = pl.program_id(0)

        @pl.when(step == 0)
        def _():
            acc_ref[...] = jnp.zeros_like(acc_ref)
            carry[...] = jnp.zeros_like(carry)

        for cc in range(D // cw):
            cols = slice(cc * cw, (cc + 1) * cw)
            w0, w1, w2 = w_ref[0:1, cols], w_ref[1:2, cols], w_ref[2:3, cols]
            halo = jnp.where(step < nt - 1,
                             (cp_ref[:, cols].astype(F32) * hp_ref[:, cols].astype(F32))[HALO_BF - 8:HALO_BF], 0.0)

            def rb_body(it, st):
                nxt8, a0, a1, a2 = st
                rb = nrb - 1 - it
                r0 = pl.multiple_of(rb * rb_n, rb_n)
                rows = pl.ds(r0, rb_n)
                cg = c_ref[rows, cols].astype(F32)
                hin = h_ref[rows, cols].astype(F32)
                bg = b_ref[rows, cols].astype(F32)
                q = cg * hin
                up_rows = pl.ds(pl.multiple_of(jnp.maximum(r0 - HALO_BF, 0), HALO_BF), HALO_BF)
                above = (c_ref[up_rows, cols].astype(F32) * h_ref[up_rows, cols].astype(F32))[HALO_BF - 8:HALO_BF]
                prev8 = jnp.where(rb == 0, halo, above)
                q1 = _down(prev8, q, 1)
                q2 = _down(prev8, q, 2)
                u = w2 * q + w1 * q1 + w0 * q2
                dzv = dz_ref[rows, cols].astype(F32)
                z_ref[rows, cols] = (bg * u).astype(z_ref.dtype)
                dp_ref[rows, cols] = (dzv * u).astype(dp_ref.dtype)
                du = dzv * bg
                dq = w2 * du + w1 * _up(du, nxt8, 1) + w0 * _up(du, nxt8, 2)
                dp_ref[rows, D + cc * cw:D + (cc + 1) * cw] = (dq * hin).astype(dp_ref.dtype)
                dp_ref[rows, 2 * D + cc * cw:2 * D + (cc + 1) * cw] = (dq * cg).astype(dp_ref.dtype)
                return du[0:8], a0 + _fold8(q2 * du), a1 + _fold8(q1 * du), a2 + _fold8(q * du)

            z = jnp.zeros((8, cw), F32)
            nxt8, a0, a1, a2 = lax.fori_loop(0, nrb, rb_body, (carry[:, cols], z, z, z))
            carry[:, cols] = nxt8
            acc_ref[0:1, cols] += jnp.sum(a0, axis=0, keepdims=True)
            acc_ref[1:2, cols] += jnp.sum(a1, axis=0, keepdims=True)
            acc_ref[2:3, cols] += jnp.sum(a2, axis=0, keepdims=True)

    return pl.pallas_call(
        body, name=name, out_shape=(SDS((s, D), BF), SDS((s, 3 * D), BF), SDS((8, D), F32)), grid=(nt,),
        in_specs=[rev(0), rev(1), rev(2), _prev_halo_spec(nt, tm, D, 1), _prev_halo_spec(nt, tm, D, 2), rev(0),
                  _const((8, D))],
        out_specs=(rev(0), pl.BlockSpec((tm, 3 * D), lambda i: (nt - 1 - i, 0)), _const((8, D))),
        scratch_shapes=[pltpu.VMEM((HALO, D), F32)], compiler_params=_cp(1))(p, p, p, p, p, dz, vec)


CF_ROW_BLOCK = 32
CF_LANES = 128


def _tap_conv_block(ext, rb_n, tap_of_offset, t_ref, cols, init, u=None, accs=None):
    n = ext.shape[0]
    out = init
    for b in range(8):
        rolled = ext if b == 0 else pltpu.roll(ext, n - b, 0)
        for a in range(n // 8):
            k = tap_of_offset(8 * a + b)
            if k is None:
                continue
            sl = rolled[8 * a:8 * a + rb_n]
            out = out + t_ref[k:k + 1, cols] * sl
            if accs is not None:
                accs[k] = accs[k] + _fold8(u * sl)
    return out


def _cf_act_fwd(name, p, taps, vec):
    s = p.shape[0]
    tm = _tile(s, ROW_TILE_D)
    rb_n = _tile(tm, CF_ROW_BLOCK)
    base = CF_HALO - (CF_TAPS - 1)
    tap_of = lambda off: off - base if 0 <= off - base < CF_TAPS else None

    def body(a_ref, g_ref, t_ref, v_ref, w_ref, cv_ref, buf):
        first = pl.program_id(0) == 0

        @pl.when(first)
        def _():
            buf[0:CF_HALO, :] = jnp.zeros((CF_HALO, D), F32)

        @pl.when(jnp.logical_not(first))
        def _():
            buf[0:CF_HALO, :] = buf[tm:tm + CF_HALO, :]

        a = a_ref[...].astype(F32) + v_ref[0:1, :]
        g = g_ref[...].astype(F32) + v_ref[1:2, :]
        buf[CF_HALO:CF_HALO + tm, :] = a * _sigmoid(g)
        for ci in range(D // CF_LANES):
            cols = slice(ci * CF_LANES, (ci + 1) * CF_LANES)

            def rb_body(rb, carry):
                r0 = pl.multiple_of(rb * rb_n, rb_n)
                ext = buf[pl.ds(r0, rb_n + CF_HALO), cols]
                init = jnp.zeros((rb_n, CF_LANES), F32) + v_ref[2:3, cols]
                cv_ref[pl.ds(r0, rb_n), cols] = _tap_conv_block(ext, rb_n, tap_of, t_ref, cols, init)
                return carry

            lax.fori_loop(0, tm // rb_n, rb_body, 0)
        cv = cv_ref[...]
        mu = jnp.mean(cv, axis=-1, keepdims=True)
        cc = cv - mu
        rstd = lax.rsqrt(jnp.mean(cc * cc, axis=-1, keepdims=True) + LN_EPS)
        ln = cc * rstd * v_ref[3:4, :] + v_ref[4:5, :]
        w_ref[...] = (ln * _sigmoid(ln)).astype(w_ref.dtype)

    return pl.pallas_call(body, name=name, out_shape=(SDS((s, D), BF), SDS((s, D), F32)), grid=(s // tm,),
                          in_specs=[_rows(tm, D, 0), _rows(tm, D, 1), _const((32, D)), _const((8, D))],
                          out_specs=(_rows(tm, D), _rows(tm, D)),
                          scratch_shapes=[pltpu.VMEM((tm + CF_HALO, D), F32)], compiler_params=_cp(1))(p, p, taps, vec)


def _cf_act_bwd(name, p, cv, dw, taps, vec):
    s = p.shape[0]
    tm = _tile(s, ROW_TILE_D)
    nt = s // tm
    rev = lambda col: pl.BlockSpec((tm, D), lambda i: (nt - 1 - i, col))
    rb_n = _tile(tm, CF_ROW_BLOCK)
    tap_of = lambda off: CF_TAPS - 1 - off if off < CF_TAPS else None

    def body(a_ref, g_ref, cv_ref, dw_ref, t_ref, v_ref, dp_ref, tacc_ref, acc_ref, nbuf, ubuf, dubuf):
        step = pl.program_id(0)

        @pl.when(step == 0)
        def _():
            acc_ref[...] = jnp.zeros_like(acc_ref)
            tacc_ref[...] = jnp.zeros_like(tacc_ref)

        a = a_ref[...].astype(F32) + v_ref[0:1, :]
        g = g_ref[...].astype(F32) + v_ref[1:2, :]
        sg = _sigmoid(g)
        ubuf[...] = a * sg
        cvv = cv_ref[...]
        mu = jnp.mean(cvv, axis=-1, keepdims=True)
        cc = cvv - mu
        rstd = lax.rsqrt(jnp.mean(cc * cc, axis=-1, keepdims=True) + LN_EPS)
        vhat = cc * rstd
        ln = vhat * v_ref[3:4, :] + v_ref[4:5, :]
        s2 = _sigmoid(ln)
        dln = dw_ref[...].astype(F32) * (s2 * (1.0 + ln * (1.0 - s2)))
        acc_ref[1:2, :] += jnp.sum(dln * vhat, axis=0, keepdims=True)
        acc_ref[2:3, :] += jnp.sum(dln, axis=0, keepdims=True)
        dvh = dln * v_ref[3:4, :]
        dcv = rstd * (dvh - jnp.mean(dvh, axis=-1, keepdims=True)
                      - vhat * jnp.mean(dvh * vhat, axis=-1, keepdims=True))
        acc_ref[0:1, :] += jnp.sum(dcv, axis=0, keepdims=True)
        _carry_up(nbuf, tm, step == 0, CF_HALO)
        nbuf[0:tm, :] = dcv
        for ci in range(D // CF_LANES):
            cols = slice(ci * CF_LANES, (ci + 1) * CF_LANES)

            def rb_body(rb, accs):
                r0 = pl.multiple_of(rb * rb_n, rb_n)
                ext = nbuf[pl.ds(r0, rb_n + CF_HALO), cols]
                accs = list(accs)
                dubuf[pl.ds(r0, rb_n), cols] = _tap_conv_block(
                    ext, rb_n, tap_of, t_ref, cols, jnp.zeros((rb_n, CF_LANES), F32),
                    ubuf[pl.ds(r0, rb_n), cols], accs)
                return tuple(accs)

            z = jnp.zeros((8, CF_LANES), F32)
            accs = lax.fori_loop(0, tm // rb_n, rb_body, tuple([z] * CF_TAPS))
            for k in range(CF_TAPS):
                tacc_ref[k:k + 1, cols] += jnp.sum(accs[k], axis=0, keepdims=True)
        a = a_ref[...].astype(F32) + v_ref[0:1, :]
        sg = _sigmoid(g_ref[...].astype(F32) + v_ref[1:2, :])
        da = dubuf[...] * sg
        dg = da * a * (1.0 - sg)
        dp_ref[:, 0:D] = da.astype(dp_ref.dtype)
        dp_ref[:, D:2 * D] = dg.astype(dp_ref.dtype)
        acc_ref[3:4, :] += jnp.sum(da, axis=0, keepdims=True)
        acc_ref[4:5, :] += jnp.sum(dg, axis=0, keepdims=True)

    return pl.pallas_call(
        body, name=name, out_shape=(SDS((s, 2 * D), BF), SDS((32, D), F32), SDS((8, D), F32)), grid=(nt,),
        in_specs=[rev(0), rev(1), rev(0), rev(0), _const((32, D)), _const((8, D))],
        out_specs=(pl.BlockSpec((tm, 2 * D), lambda i: (nt - 1 - i, 0)), _const((32, D)), _const((8, D))),
        scratch_shapes=[pltpu.VMEM((tm + CF_HALO, D), F32), pltpu.VMEM((tm, D), F32), pltpu.VMEM((tm, D), F32)],
        compiler_params=_cp(1))(p, p, cv, dw, taps, vec)


def _inv_count(row0, tm, window):
    t = row0 + lax.broadcasted_iota(jnp.int32, (tm, 1), 0)
    return 1.0 / jnp.minimum(t + 1, window).astype(F32)


def _pool_fwd(name, x, w, vec):
    s = x.shape[0]
    tm = _tile(s, ROW_TILE_D)
    G = POOL_GROUP

    def body(x_ref, w_ref, v_ref, pl_ref, m_ref, buf):
        i = pl.program_id(0)

        @pl.when(i == 0)
        def _():
            buf[0:POOL_HALO, :] = jnp.zeros((POOL_HALO, D), F32)

        @pl.when(i > 0)
        def _():
            buf[0:POOL_HALO, :] = buf[tm:tm + POOL_HALO, :]

        xv = x_ref[...]
        r = lax.rsqrt(jnp.mean(xv * xv, axis=-1, keepdims=True) + RMS_EPS)
        buf[POOL_HALO:POOL_HALO + tm, :] = xv * r * (v_ref[0:1, :] * (1.0 + v_ref[1:2, :])) + v_ref[2:3, :]
        for gi, win in enumerate(POOL_WINDOWS):
            cols = slice(gi * G, (gi + 1) * G)
            acc = buf[POOL_HALO:POOL_HALO + tm, cols]
            hg = acc
            for j in range(1, win):
                acc = acc + buf[POOL_HALO - j:POOL_HALO - j + tm, cols]
            pooled = (acc * _inv_count(i * tm, tm, win) - hg).astype(BF)
            pl_ref[:, cols] = pooled
            yg = jnp.dot(pooled, w_ref[gi], preferred_element_type=F32)
            m_ref[:, cols] = (yg + v_ref[3:4, cols]) * v_ref[4:5, cols]

    return pl.pallas_call(body, name=name, out_shape=(SDS((s, D), BF), SDS((s, D), F32)), grid=(s // tm,),
                          in_specs=[_rows(tm, D), _const((4, G, G)), _const((8, D))],
                          out_specs=(_rows(tm, D), _rows(tm, D)),
                          scratch_shapes=[pltpu.VMEM((tm + POOL_HALO, D), F32)], compiler_params=_cp(1))(x, w, vec)


def _pool_bwd(name, pooled, dm, w, vec):
    s = pooled.shape[0]
    tm = _tile(s, ROW_TILE_D)
    nt = s // tm
    G = POOL_GROUP
    rev = pl.BlockSpec((tm, D), lambda i: (nt - 1 - i, 0))

    def body(p_ref, dm_ref, w_ref, v_ref, dh_ref, dw_ref, acc_ref, nbuf):
        step = pl.program_id(0)
        row0 = (nt - 1 - step) * tm

        @pl.when(step == 0)
        def _():
            acc_ref[...] = jnp.zeros_like(acc_ref)
            dw_ref[...] = jnp.zeros_like(dw_ref)

        _carry_up(nbuf, tm, step == 0, POOL_HALO)
        dmv = dm_ref[...].astype(F32)
        acc_ref[0:1, :] += jnp.sum(dmv, axis=0, keepdims=True) * v_ref[1:2, :]
        dps = []
        for gi, win in enumerate(POOL_WINDOWS):
            cols = slice(gi * G, (gi + 1) * G)
            pg = p_ref[:, cols]
            yb = jnp.dot(pg, w_ref[gi], preferred_element_type=F32) + v_ref[0:1, cols]
            acc_ref[1:2, cols] += jnp.sum(dmv[:, cols] * yb, axis=0, keepdims=True)
            dy = (dmv[:, cols] * v_ref[1:2, cols]).astype(BF)
            dw_ref[gi] += lax.dot_general(pg, dy, TN, preferred_element_type=F32)
            dpg = lax.dot_general(dy, w_ref[gi], NT, preferred_element_type=F32)
            dps.append(dpg)
            nbuf[0:tm, cols] = dpg * _inv_count(row0, tm, win)
        for gi, win in enumerate(POOL_WINDOWS):
            cols = slice(gi * G, (gi + 1) * G)
            acc = nbuf[0:tm, cols]
            for j in range(1, win):
                acc = acc + nbuf[j:j + tm, cols]
            dh_ref[:, cols] = acc - dps[gi]

    return pl.pallas_call(
        body, name=name, out_shape=(SDS((s, D), F32), SDS((4, G, G), F32), SDS((8, D), F32)), grid=(nt,),
        in_specs=[rev, rev, _const((4, G, G)), _const((8, D))],
        out_specs=(rev, _const((4, G, G)), _const((8, D))),
        scratch_shapes=[pltpu.VMEM((tm + POOL_HALO, D), F32)], compiler_params=_cp(1))(pooled, dm, w, vec)


def _row_tile_2d(rows, width, bytes_per_row_elem=4, budget=2 * 1024 * 1024):
    t = max(8, budget // (width * bytes_per_row_elem))
    t = min(rows, 1 << (t.bit_length() - 1))
    while rows % t:
        t //= 2
    return t


def _add_slots(name, r):
    nl, _, k, n = r.shape
    tk = _row_tile_2d(k, n, 16)

    def body(r_ref, o_ref):
        f = lambda i: r_ref[i].astype(F32)
        o_ref[...] = ((f(7) + f(6)) + (f(0) + f(1))) + ((f(2) + f(3)) + (f(4) + f(5)))

    return pl.pallas_call(body, name=name, out_shape=SDS((nl, k, n), F32), grid=(nl, k // tk),
                          in_specs=[pl.BlockSpec((None, N_DEV, tk, n), lambda l, i: (l, 0, i, 0))],
                          out_specs=pl.BlockSpec((None, tk, n), lambda l, i: (l, i, 0)),
                          compiler_params=_cp(2))(r)


def _adamw(name, w, g, m, v):
    rows, width = w.shape
    tm = _row_tile_2d(rows, width, 4, 1024 * 1024)
    c1 = 1.0 - ADAM_B1 ** ADAM_STEP
    c2 = 1.0 - ADAM_B2 ** ADAM_STEP

    def body(w_ref, g_ref, m_ref, v_ref, d_ref, nm_ref, nv_ref):
        gv = g_ref[...]
        nm = ADAM_B1 * m_ref[...] + (1.0 - ADAM_B1) * gv
        nv = ADAM_B2 * v_ref[...] + (1.0 - ADAM_B2) * (gv * gv)
        nm_ref[...] = nm
        nv_ref[...] = nv
        d_ref[...] = -ADAM_LR * ((nm / c1) / (jnp.sqrt(nv / c2) + ADAM_EPS) + ADAM_WD * w_ref[...])

    spec = _rows(tm, width)
    sds = SDS((rows, width), F32)
    return pl.pallas_call(body, name=name, out_shape=(sds, sds, sds), grid=(rows // tm,),
                          in_specs=[spec] * 4, out_specs=(spec,) * 3, compiler_params=_cp(1))(w, g, m, v)


def _mod_fwd(c16, w_mod, b_sh):
    n = w_mod.shape[2]
    tn = _tile(n, 512)

    def body(c_ref, w_ref, b_ref, o_ref):
        cv = c_ref[...]
        ca = (cv * _sigmoid(cv)).astype(BF)
        o_ref[...] = jnp.dot(ca, w_ref[...].astype(BF), preferred_element_type=F32) + b_ref[0:1, :]

    return pl.pallas_call(body, name="mod_fwd", out_shape=SDS((DEPTH, 16, n), F32), grid=(DEPTH, n // tn),
                          in_specs=[_const((16, D)), pl.BlockSpec((None, D, tn), lambda l, j: (l, 0, j)),
                                    pl.BlockSpec((None, 8, tn), lambda l, j: (l, 0, j))],
                          out_specs=pl.BlockSpec((None, 16, tn), lambda l, j: (l, 0, j)),
                          compiler_params=_cp(2))(c16, w_mod, b_sh)


def _mod_bwd(c16, dmod):
    n = dmod.shape[2]
    tn = _tile(n, 512)

    def body(c_ref, d_ref, o_ref):
        cv = c_ref[...]
        ca = (cv * _sigmoid(cv)).astype(BF)
        o_ref[...] = lax.dot_general(ca, d_ref[...].astype(BF), TN, preferred_element_type=F32)

    return pl.pallas_call(body, name="mod_bwd", out_shape=SDS((DEPTH, D, n), F32), grid=(DEPTH, n // tn),
                          in_specs=[_const((16, D)), pl.BlockSpec((None, 16, tn), lambda l, j: (l, 0, j))],
                          out_specs=pl.BlockSpec((None, D, tn), lambda l, j: (l, 0, j)),
                          compiler_params=_cp(2))(c16, dmod)


def _place():
    x, y, c = lax.axis_index("x"), lax.axis_index("y"), lax.axis_index("c")
    other_chips = [(1 - x, y), (x, 1 - y), (1 - x, 1 - y)]
    return x, y, c, other_chips


def _allgather_small(name, v, with_sum):
    m, n = v.shape

    def body(x_ref, out_ref, *rest):
        if with_sum:
            sum_ref, send_sems, recv_sems, local_sem = rest
        else:
            send_sems, recv_sems, local_sem = rest
        x, y, c, chips = _place()
        me, sibling = (x, y, c), (x, y, 1 - c)

        def rows(px, py, pc):
            return out_ref.at[pl.ds((4 * px + 2 * py + pc) * m, m), :]

        def copy(k, block, to, src=None):
            return pltpu.make_async_remote_copy(
                src_ref=rows(*block) if src is None else src, dst_ref=rows(*block),
                send_sem=send_sems.at[k], recv_sem=recv_sems.at[k], device_id=to, device_id_type=MESH)

        mine = pltpu.make_async_copy(x_ref, rows(*me), local_sem)
        mine.start()
        first = [copy(0, me, sibling, src=x_ref)]
        first += [copy(1 + j, me, (*chip, c), src=x_ref) for j, chip in enumerate(chips)]
        for cp in first:
            cp.start()
        passed = [copy(4 + j, (*chip, c), sibling) for j, chip in enumerate(chips)]
        for j, chip in enumerate(chips):
            copy(1 + j, (*chip, c), me).wait_recv()
            passed[j].start()
        copy(0, sibling, me).wait_recv()
        for j, chip in enumerate(chips):
            copy(4 + j, (*chip, 1 - c), me).wait_recv()
        for cp in first + passed:
            cp.wait_send()
        mine.wait()
        if with_sum:
            acc = out_ref[0:m, :]
            for k in range(1, N_DEV):
                acc = acc + out_ref[k * m:(k + 1) * m, :]
            sum_ref[...] = acc

    vm = pl.BlockSpec(memory_space=pltpu.VMEM)
    out_shape = [SDS((N_DEV * m, n), F32)] + ([SDS((m, n), F32)] if with_sum else [])
    res = pl.pallas_call(
        body, name=name, out_shape=tuple(out_shape), in_specs=[vm], out_specs=tuple([vm] * len(out_shape)),
        scratch_shapes=[pltpu.SemaphoreType.DMA((7,)), pltpu.SemaphoreType.DMA((7,)), pltpu.SemaphoreType.DMA],
        compiler_params=pltpu.CompilerParams(vmem_limit_bytes=VMEM_LIMIT_MB * 1024 * 1024))(v)
    return res if with_sum else res[0]


HBM = pl.BlockSpec(memory_space=pltpu.HBM)


def _sem_scratch(n_remote, n_local):
    return [pltpu.SemaphoreType.DMA((n_remote,)), pltpu.SemaphoreType.DMA((n_remote,)),
            pltpu.SemaphoreType.DMA((n_local,))]


DMA_PIECE_BYTES = 1 << 20


def _pieces(src, dst):
    *lead, rows, n = src.shape
    nsplit = max(1, min(rows // 16, (rows * n * jnp.dtype(src.dtype).itemsize) // DMA_PIECE_BYTES))
    while rows % nsplit or (rows // nsplit) % 16:
        nsplit -= 1
    size = rows // nsplit
    out = []
    for idx in itertools.product(*[range(d) for d in lead]):
        for i in range(nsplit):
            sl = tuple(idx) + (pl.ds(i * size, size),)
            out.append((src.at[sl], dst.at[sl]))
    return out


def _local_copies(src, dst, sem):
    return ([pltpu.make_async_copy(s_, d_, sem) for s_, d_ in _pieces(src, dst)],
            pltpu.make_async_copy(src, dst, sem))


def _remote_copies(src, dst, send_sem, recv_sem, to):
    mk = lambda s_, d_: pltpu.make_async_remote_copy(src_ref=s_, dst_ref=d_, send_sem=send_sem, recv_sem=recv_sem,
                                                     device_id=to, device_id_type=MESH)
    return [mk(s_, d_) for s_, d_ in _pieces(src, dst)], mk(src, dst)


def _gather_weights(shards, on_sequencer=False):
    nq = len(shards)

    def exchange(ins, outs, send_sems, recv_sems, local_sems, own_barrier):
        x, y, c, chips = _place()
        if own_barrier:
            barrier = pltpu.get_barrier_semaphore()
            for px, py in chips:
                pl.semaphore_signal(barrier, inc=1, device_id=(px, py, c), device_id_type=MESH)
            pl.semaphore_wait(barrier, len(chips))
        me_chip = 2 * x + y
        started, local_all, send_all, recv_all = [], [], [], []
        for q in range(nq):
            cps, whole = _local_copies(ins[q], outs[q].at[:, me_chip], local_sems.at[q])
            started += cps
            local_all.append(whole)
            for r, (px, py) in enumerate(chips):
                k = 3 * q + r
                cps, whole = _remote_copies(ins[q], outs[q].at[:, me_chip], send_sems.at[k], recv_sems.at[k], (px, py, c))
                started += cps
                send_all.append(whole)
                recv_all.append(_remote_copies(ins[q], outs[q].at[:, 2 * px + py], send_sems.at[k], recv_sems.at[k],
                                               (px, py, c))[1])
        for cp in started:
            cp.start()
        for cp in recv_all:
            cp.wait_recv()
        for cp in send_all:
            cp.wait_send()
        for cp in local_all:
            cp.wait()

    out_shape = tuple(SDS((s.shape[0], N_CHIPS) + s.shape[1:], s.dtype) for s in shards)
    if not on_sequencer:
        def body(*refs):
            exchange(refs[:nq], refs[nq:2 * nq], *refs[2 * nq:], own_barrier=False)

        return pl.pallas_call(body, name="gather_weights", out_shape=out_shape, in_specs=[HBM] * nq,
                              out_specs=tuple([HBM] * nq), scratch_shapes=_sem_scratch(3 * nq, nq))(*shards)

    in_refs = [jax.new_ref(s, memory_space=pltpu.MemorySpace.HBM) for s in shards]
    out_refs = [jax.empty_ref(o, memory_space=pltpu.MemorySpace.HBM) for o in out_shape]

    @pl.kernel(mesh=plsc.ScalarSubcoreMesh(axis_name="sequencer", num_cores=1), name="gather_weights_behind",
               scratch_types=tuple(_sem_scratch(3 * nq, nq)),
               compiler_params=pltpu.CompilerParams(collective_id=SEQUENCER_GATHER_ID))
    def launch(send_sems, recv_sems, local_sems):
        exchange(in_refs, out_refs, send_sems, recv_sems, local_sems, own_barrier=True)

    launch()
    return [r[...] for r in out_refs]


def _grad_exchange(gs):
    nq = len(gs)

    def body(*refs):
        ins, outs = refs[:nq], refs[nq:2 * nq]
        send_sems, recv_sems, local_sems = refs[2 * nq:]
        x, y, c, chips = _place()
        peers = [(2 * r + e, (px, py, c if e == 0 else 1 - c)) for r, (px, py) in enumerate(chips) for e in (0, 1)]
        peers.append((6, (x, y, 1 - c)))
        started, local_all, remote_all = [], [], []
        for q in range(nq):
            cps, whole = _local_copies(ins[q].at[:, 2 * x + y, c], outs[q].at[:, 7], local_sems.at[q])
            started += cps
            local_all.append(whole)
            for slot, (px, py, pc) in peers:
                k = 7 * q + slot
                cps, whole = _remote_copies(ins[q].at[:, 2 * px + py, pc], outs[q].at[:, slot], send_sems.at[k],
                                            recv_sems.at[k], (px, py, pc))
                started += cps
                remote_all.append(whole)
        for cp in started:
            cp.start()
        for cp in remote_all:
            cp.wait_recv()
        for cp in remote_all:
            cp.wait_send()
        for cp in local_all:
            cp.wait()

    out_shape = tuple(SDS((g.shape[0], N_DEV, g.shape[3], g.shape[4]), g.dtype) for g in gs)
    return pl.pallas_call(body, name="grad_exchange", out_shape=out_shape, in_specs=[HBM] * nq,
                          out_specs=tuple([HBM] * nq), scratch_shapes=_sem_scratch(7 * nq, nq))(*gs)


def _pair_share(rs):
    nq = len(rs)

    def body(*refs):
        ins, outs = refs[:nq], refs[nq:2 * nq]
        send_sems, recv_sems, local_sems = refs[2 * nq:]
        x, y, c, _ = _place()
        started, local_all, send_all, recv_all = [], [], [], []
        for q in range(nq):
            cps, whole = _local_copies(ins[q], outs[q].at[:, c], local_sems.at[q])
            started += cps
            local_all.append(whole)
            cps, whole = _remote_copies(ins[q], outs[q].at[:, c], send_sems.at[q], recv_sems.at[q], (x, y, 1 - c))
            started += cps
            send_all.append(whole)
            recv_all.append(_remote_copies(ins[q], outs[q].at[:, 1 - c], send_sems.at[q], recv_sems.at[q],
                                           (x, y, 1 - c))[1])
        for cp in started:
            cp.start()
        for cp in recv_all:
            cp.wait_recv()
        for cp in send_all:
            cp.wait_send()
        for cp in local_all:
            cp.wait()

    out_shape = tuple(SDS((r.shape[0], 2) + r.shape[1:], r.dtype) for r in rs)
    return pl.pallas_call(body, name="grad_pair_share", out_shape=out_shape, in_specs=[HBM] * nq,
                          out_specs=tuple([HBM] * nq), scratch_shapes=_sem_scratch(nq, nq))(*rs)


def _pack(arrs, rows_multiple=8):
    flat = jnp.concatenate([a.astype(F32).reshape(-1) for a in arrs])
    pad = (-flat.shape[0]) % (128 * rows_multiple)
    return jnp.pad(flat, (0, pad)).reshape(-1, 128)


def _unpack(slab, shapes):
    flat = slab.reshape(-1)
    out, off = [], 0
    for shp in shapes:
        n = 1
        for d in shp:
            n *= d
        out.append(flat[off:off + n].reshape(shp))
        off += n
    return out


def _shard_last(a, chip, n):
    return lax.dynamic_slice_in_dim(a, chip * n, n, axis=a.ndim - 1)


def kernel(x, c, w_mod, b_mod, norm_g, sc_w_in, sc_conv, sc_w_out, pool_w, pool_b, pool_scale, cf_w_pw1, cf_b_pw1, cf_w_dw, cf_b_dw, cf_ln_g, cf_ln_b, cf_w_pw2, cf_b_pw2, ffn_w_up, ffn_conv, ffn_b_conv, ffn_w_down, loss_target, m_w_mod, m_b_mod, m_norm_g, m_sc_w_in, m_sc_conv, m_sc_w_out, m_pool_w, m_pool_b, m_pool_scale, m_cf_w_pw1, m_cf_b_pw1, m_cf_w_dw, m_cf_b_dw, m_cf_ln_g, m_cf_ln_b, m_cf_w_pw2, m_cf_b_pw2, m_ffn_w_up, m_ffn_conv, m_ffn_b_conv, m_ffn_w_down, v_w_mod, v_b_mod, v_norm_g, v_sc_w_in, v_sc_conv, v_sc_w_out, v_pool_w, v_pool_b, v_pool_scale, v_cf_w_pw1, v_cf_b_pw1, v_cf_w_dw, v_cf_b_dw, v_cf_ln_g, v_cf_ln_b, v_cf_w_pw2, v_cf_b_pw2, v_ffn_w_up, v_ffn_conv, v_ffn_b_conv, v_ffn_w_down):
    ax, ay, ac = lax.axis_index("x"), lax.axis_index("y"), lax.axis_index("c")
    chip = 2 * ax + ay
    dev = 4 * ax + 2 * ay + ac
    xs = x[0]
    target = loss_target[0]

    small_sharded = [norm_g, sc_conv, cf_b_pw1, cf_w_dw, cf_b_dw, cf_ln_g, cf_ln_b, cf_b_pw2, ffn_conv]
    slab = _pack([c] + small_sharded)
    gathered = _allgather_small("gather_small_params", slab, False).reshape(N_DEV, -1, 128)
    parts = [_unpack(gathered[d], [c.shape] + [a.shape for a in small_sharded]) for d in range(N_DEV)]
    c_all = jnp.concatenate([p[0] for p in parts], axis=0)
    full = [jnp.concatenate([parts[2 * j][1 + i] for j in range(N_CHIPS)], axis=-1)
            for i in range(len(small_sharded))]
    norm_g_f, sc_conv_f, cf_b_pw1_f, cf_w_dw_f, cf_b_dw_f, cf_ln_g_f, cf_ln_b_f, cf_b_pw2_f, ffn_conv_f = full
    c16 = jnp.pad(c_all, ((0, 8), (0, 0)))

    n_mod = w_mod.shape[2]
    b_sh = jnp.broadcast_to(_shard_last(b_mod, chip, n_mod)[:, None, :], (DEPTH, 8, n_mod))
    mod_part = _mod_fwd(c16, w_mod, b_sh)
    mod_g = _allgather_small("gather_mod", mod_part.reshape(DEPTH * 16, n_mod), False)
    mod_g = mod_g.reshape(N_DEV, DEPTH, 16, n_mod)
    mod_mine = jnp.concatenate(
        [lax.dynamic_index_in_dim(mod_g[2 * j], dev, axis=1, keepdims=False) for j in range(N_CHIPS)], axis=-1)
    mod = mod_mine.reshape(DEPTH, 6, D)

    bf = lambda a: a.astype(BF)
    rows = lambda w: w.reshape(w.shape[0], w.shape[1] * w.shape[2], w.shape[3])
    w_in_0, w_out_0, w_up_0, w_down_0 = _gather_weights(
        [bf(sc_w_in[0:1]), bf(sc_w_out[0:1]), bf(ffn_w_up[0:1]), bf(ffn_w_down[0:1])])
    w_in_r, w_out_r, pool_f, pw1_f, pw2_f, w_up_r, w_down_r = _gather_weights(
        [bf(sc_w_in[1:]), bf(sc_w_out[1:]), bf(pool_w[0]), bf(cf_w_pw1), bf(cf_w_pw2), bf(ffn_w_up[1:]),
         bf(ffn_w_down[1:])], on_sequencer=True)
    pool_f = pool_f.reshape(4, POOL_GROUP, POOL_GROUP)
    pw1_f, pw2_f = (pw1_f, 0), (rows(pw2_f), 0)
    w_in_f = {0: (w_in_0, 0), 1: (w_in_r, 0)}
    w_out_f = {0: (rows(w_out_0), 0), 1: (rows(w_out_r), 0)}
    w_up_f = {i: (w_up_0, 0) if i == 0 else (w_up_r, i - 1) for i in range(DEPTH)}
    w_down_f = {i: (rows(w_down_0), 0) if i == 0 else (rows(w_down_r), i - 1) for i in range(DEPTH)}

    zero_d = jnp.zeros((D,), F32)
    mods = [[mod[i, k] for k in range(6)] for i in range(DEPTH)]
    saved = []
    xcur = xs
    h_next = None
    for i in range(DEPTH):
        kind, j = i % 3, i // 3
        sh1, sc1, g1, sh2, sc2, g2 = mods[i]
        st = {"x0": xcur}
        m_bias = zero_d
        if kind != 1:
            h = h_next if h_next is not None else _pro_fwd(f"pro1_fwd_{i}", xcur, _vecs([norm_g_f[i, 0], sc1, sh1], D))
        if kind == 0:
            p = _mm_nn(f"sc_in_{i}", h, *w_in_f[j],True, BF)
            z = _sc_act_fwd(f"sc_act_fwd_{i}", p, _vecs(list(sc_conv_f[j]), D))
            m = _mm_nn(f"sc_out_{i}", z, *w_out_f[j],False, F32)
            st.update(h=h, p=p)
        elif kind == 1:
            pool_vec = _vecs([norm_g_f[i, 0], sc1, sh1, pool_b[j], pool_scale[j]], D)
            pooled, m = _pool_fwd(f"pool_fwd_{i}", xcur, pool_f, pool_vec)
            st.update(pooled=pooled)
        else:
            p = _mm_nn(f"cf_pw1_{i}", h, *pw1_f,True, BF)
            taps = jnp.pad(cf_w_dw_f[j], ((0, 1), (0, 0)))
            cf_vec = _vecs([cf_b_pw1_f[j, :D], cf_b_pw1_f[j, D:], cf_b_dw_f[j], cf_ln_g_f[j], cf_ln_b_f[j]], D)
            wact, cv = _cf_act_fwd(f"cf_act_fwd_{i}", p, taps, cf_vec)
            m = _mm_nn(f"cf_pw2_{i}", wact, *pw2_f,False, F32)
            m_bias = cf_b_pw2_f[j]
            st.update(h=h, p=p, wact=wact, cv=cv, taps=taps, cf_vec=cf_vec)
        x1, h2 = _epi_pro_fwd(f"epi1_fwd_{i}", xcur, m,
                              _vecs([g1, norm_g_f[i, 1], m_bias, norm_g_f[i, 2], sc2, sh2], D), True)
        st.update(m=m, x1=x1, m_bias=m_bias)
        up = _mm_nn(f"ffn_up_{i}", h2, *w_up_f[i],True, BF)
        ffn_vec = _vecs(list(ffn_conv_f[i]) + [ffn_b_conv[i]], F)
        a = _ffn_act_fwd(f"ffn_act_fwd_{i}", up, ffn_vec)
        f = _mm_nn(f"ffn_down_{i}", a, *w_down_f[i],False, F32, tm=512)
        nxt = i + 1
        fuse_next = nxt < DEPTH and nxt % 3 != 1
        rows = [g2, norm_g_f[i, 3], zero_d]
        if fuse_next:
            rows += [norm_g_f[nxt, 0], mods[nxt][1], mods[nxt][0]]
        xcur, h_next = _epi_pro_fwd(f"epi2_fwd_{i}", x1, f, _vecs(rows, D), fuse_next)
        st.update(h2=h2, up=up, a=a, f=f, ffn_vec=ffn_vec)
        saved.append(st)

    dy, loss_cols = _loss_fwd_bwd(xcur, target)
    loss = lax.psum(jnp.sum(loss_cols[0]), ("x", "y", "c"))

    dmod = [None] * DEPTH
    d_norm_g = [None] * DEPTH
    d_sc_conv = [None, None]
    d_ffn_conv, d_ffn_b = [None] * DEPTH, [None] * DEPTH
    g_w_in, g_w_out, g_w_up, g_w_down = [None, None], [None, None], [None] * DEPTH, [None] * DEPTH
    dxo = dy
    last = DEPTH - 1
    pend = _epi_bwd(f"epi2_bwd_{last}", dy, saved[last]["f"], _vecs([mods[last][5], norm_g_f[last, 3], zero_d], D))
    for i in reversed(range(DEPTH)):
        kind, j = i % 3, i // 3
        st = saved[i]
        sh1, sc1, g1, sh2, sc2, g2 = mods[i]
        df, e2 = pend
        da = _mm_nt(f"ffn_down_dx_{i}", df, *w_down_f[i],False, BF, tn=F // 2)
        dup, fsum = _ffn_act_bwd(f"ffn_act_bwd_{i}", st["up"], da, st["ffn_vec"])
        g_w_down[i] = _mm_tn(f"ffn_down_dw_{i}", st["a"], df, False, tk=F // 2)
        dh2 = _mm_nt(f"ffn_up_dx_{i}", dup, *w_up_f[i],True, F32)
        g_w_up[i] = _mm_tn(f"ffn_up_dw_{i}", st["h2"], dup, True)
        dx1, dm, s21 = _pro_epi_bwd(f"pro2_epi1_bwd_{i}", dh2, st["x1"], dxo, st["m"],
                                    _vecs([norm_g_f[i, 2], sc2, g1, norm_g_f[i, 1], st["m_bias"]], D))
        p2, e1 = s21[0:3], s21[3:6]
        if kind == 0:
            dz = _mm_nt(f"sc_out_dx_{i}", dm, *w_out_f[j],False, BF)
            z, dp, ssum = _sc_act_bwd(f"sc_act_bwd_{i}", st["p"], dz, _vecs(list(sc_conv_f[j]), D))
            g_w_out[j] = _mm_tn(f"sc_out_dw_{i}", z, dm, False)
            dh = _mm_nt(f"sc_in_dx_{i}", dp, *w_in_f[j],True, F32)
            g_w_in[j] = _mm_tn(f"sc_in_dw_{i}", st["h"], dp, True)
            d_sc_conv[j] = ssum[0:3]
        elif kind == 1:
            dh, g_pool, psum = _pool_bwd(f"pool_bwd_{i}", st["pooled"], dm, pool_f, _vecs([pool_b[j], pool_scale[j]], D))
        else:
            dwact = _mm_nt(f"cf_pw2_dx_{i}", dm, *pw2_f,False, BF)
            g_pw2 = _mm_tn(f"cf_pw2_dw_{i}", st["wact"], dm, False)
            dp, tsum, csum = _cf_act_bwd(f"cf_act_bwd_{i}", st["p"], st["cv"], dwact, st["taps"], st["cf_vec"])
            dh = _mm_nt(f"cf_pw1_dx_{i}", dp, *pw1_f,True, F32)
            g_pw1 = _mm_tn(f"cf_pw1_dw_{i}", st["h"], dp, True)
            d_cf = dict(b_pw1=jnp.concatenate([csum[3], csum[4]])[None], w_dw=tsum[None, :CF_TAPS], b_dw=csum[0:1],
                        ln_g=csum[1:2], ln_b=csum[2:3], b_pw2=e1[2:3])
        if i > 0:
            prev = i - 1
            dxo, df_prev, s12 = _pro_epi_bwd(f"pro1_epi2_bwd_{i}", dh, st["x0"], dx1, saved[prev]["f"],
                                             _vecs([norm_g_f[i, 0], sc1, mods[prev][5], norm_g_f[prev, 3], zero_d], D))
            p1, pend = s12[0:3], (df_prev, s12[3:6])
        else:
            dxo, p1 = _pro_bwd(f"pro1_bwd_{i}", dh, st["x0"], dx1, _vecs([norm_g_f[i, 0], sc1], D))
        dmod[i] = jnp.concatenate([p1[0], p1[1], e1[0], p2[0], p2[1], e2[0]])
        d_norm_g[i] = jnp.stack([p1[2], e1[1], p2[2], e2[1]])
        if kind == 1:
            d_pool_b, d_pool_scale = psum[0:1], psum[1:2]
        saved[i] = None
        st = None
        if i == 0:
            grad_x = dxo[None]
        d_ffn_conv[i], d_ffn_b[i] = fsum[1:4], fsum[0]

    small_shapes = [(DEPTH, 6 * D), (DEPTH, 4, D), (2, 3, D), (1, D), (1, D), (1, 2 * D), (1, CF_TAPS, D), (1, D),
                    (1, D), (1, D), (1, D), (DEPTH, 3, F), (DEPTH, F)]
    small = [jnp.stack(dmod), jnp.stack(d_norm_g), jnp.stack(d_sc_conv), d_pool_b, d_pool_scale, d_cf["b_pw1"],
             d_cf["w_dw"], d_cf["b_dw"], d_cf["ln_g"], d_cf["ln_b"], d_cf["b_pw2"], jnp.stack(d_ffn_conv),
             jnp.stack(d_ffn_b)]
    gsl, ssl = _allgather_small("reduce_small_grads", _pack(small), True)
    tot = _unpack(ssl, small_shapes)
    (gb_mod, gt_norm_g, gt_sc_conv, g_pool_b, g_pool_scale, gt_b_pw1, gt_w_dw, gt_b_dw, gt_ln_g, gt_ln_b, gt_b_pw2,
     gt_ffn_conv, g_ffn_b) = tot
    dmod_all = jnp.stack([_unpack(gsl.reshape(N_DEV, -1, 128)[d], small_shapes[:1])[0] for d in range(N_DEV)], axis=1)
    dmod_sh = jnp.pad(_shard_last(dmod_all, chip, n_mod), ((0, 0), (0, 8), (0, 0)))
    g_w_mod = _mod_bwd(c16, dmod_sh)
    g_norm_g = _shard_last(gt_norm_g, chip, D // 4)
    g_sc_conv = _shard_last(gt_sc_conv, chip, D // 4)
    g_b_pw1 = _shard_last(gt_b_pw1, chip, 2 * D // 4)
    g_w_dw = _shard_last(gt_w_dw, chip, D // 4)
    g_b_dw = _shard_last(gt_b_dw, chip, D // 4)
    g_ln_g = _shard_last(gt_ln_g, chip, D // 4)
    g_ln_b = _shard_last(gt_ln_b, chip, D // 4)
    g_b_pw2 = _shard_last(gt_b_pw2, chip, D // 4)
    g_ffn_conv = _shard_last(gt_ffn_conv, chip, F // 4)

    def halves(gl):
        g = jnp.stack(gl) if isinstance(gl, list) else gl
        if g.ndim == 3:
            g = g.reshape(g.shape[0], N_CHIPS, g.shape[1] // N_CHIPS, g.shape[2])
        nl, _, k, n = g.shape
        return g.reshape(nl, N_CHIPS, 2, k // 2, n)

    partial = [halves(g_w_in), halves(g_w_out), halves(g_pool.astype(BF)), halves([g_pw1]), halves([g_pw2]),
               halves(g_w_up), halves(g_w_down)]
    pieces = _grad_exchange(partial)
    reduced = [_add_slots(f"grad_sum_{q}", pieces[q]) for q in range(len(pieces))]
    shared = _pair_share(reduced)
    big_w = [sc_w_in, sc_w_out, pool_w, cf_w_pw1, cf_w_pw2, ffn_w_up, ffn_w_down]
    big_g = [shared[q].reshape(big_w[q].shape) for q in range(len(big_w))]
    g_sc_w_in, g_sc_w_out, g_pool_w, g_cf_w_pw1, g_cf_w_pw2, g_ffn_w_up, g_ffn_w_down = big_g

    def adam_big(name, w, g, m, v):
        shp = w.shape
        two = lambda t: t.reshape(-1, shp[-1])
        return [o.reshape(shp) for o in _adamw(name, two(w), two(g), two(m), two(v))]

    grads = dict(w_mod=g_w_mod, b_mod=gb_mod, norm_g=g_norm_g, sc_w_in=g_sc_w_in, sc_conv=g_sc_conv,
                 sc_w_out=g_sc_w_out, pool_w=g_pool_w, pool_b=g_pool_b, pool_scale=g_pool_scale,
                 cf_w_pw1=g_cf_w_pw1, cf_b_pw1=g_b_pw1, cf_w_dw=g_w_dw, cf_b_dw=g_b_dw, cf_ln_g=g_ln_g,
                 cf_ln_b=g_ln_b, cf_w_pw2=g_cf_w_pw2, cf_b_pw2=g_b_pw2, ffn_w_up=g_ffn_w_up, ffn_conv=g_ffn_conv,
                 ffn_b_conv=g_ffn_b, ffn_w_down=g_ffn_w_down)
    weights = dict(w_mod=w_mod, b_mod=b_mod, norm_g=norm_g, sc_w_in=sc_w_in, sc_conv=sc_conv, sc_w_out=sc_w_out,
                   pool_w=pool_w, pool_b=pool_b, pool_scale=pool_scale, cf_w_pw1=cf_w_pw1, cf_b_pw1=cf_b_pw1,
                   cf_w_dw=cf_w_dw, cf_b_dw=cf_b_dw, cf_ln_g=cf_ln_g, cf_ln_b=cf_ln_b, cf_w_pw2=cf_w_pw2,
                   cf_b_pw2=cf_b_pw2, ffn_w_up=ffn_w_up, ffn_conv=ffn_conv, ffn_b_conv=ffn_b_conv,
                   ffn_w_down=ffn_w_down)
    m_in = dict(w_mod=m_w_mod, b_mod=m_b_mod, norm_g=m_norm_g, sc_w_in=m_sc_w_in, sc_conv=m_sc_conv,
                sc_w_out=m_sc_w_out, pool_w=m_pool_w, pool_b=m_pool_b, pool_scale=m_pool_scale,
                cf_w_pw1=m_cf_w_pw1, cf_b_pw1=m_cf_b_pw1, cf_w_dw=m_cf_w_dw, cf_b_dw=m_cf_b_dw, cf_ln_g=m_cf_ln_g,
                cf_ln_b=m_cf_ln_b, cf_w_pw2=m_cf_w_pw2, cf_b_pw2=m_cf_b_pw2, ffn_w_up=m_ffn_w_up,
                ffn_conv=m_ffn_conv, ffn_b_conv=m_ffn_b_conv, ffn_w_down=m_ffn_w_down)
    v_in = dict(w_mod=v_w_mod, b_mod=v_b_mod, norm_g=v_norm_g, sc_w_in=v_sc_w_in, sc_conv=v_sc_conv,
                sc_w_out=v_sc_w_out, pool_w=v_pool_w, pool_b=v_pool_b, pool_scale=v_pool_scale,
                cf_w_pw1=v_cf_w_pw1, cf_b_pw1=v_cf_b_pw1, cf_w_dw=v_cf_w_dw, cf_b_dw=v_cf_b_dw, cf_ln_g=v_cf_ln_g,
                cf_ln_b=v_cf_ln_b, cf_w_pw2=v_cf_w_pw2, cf_b_pw2=v_cf_b_pw2, ffn_w_up=v_ffn_w_up,
                ffn_conv=v_ffn_conv, ffn_b_conv=v_ffn_b_conv, ffn_w_down=v_ffn_w_down)
    names = list(weights)
    big_names = ["w_mod", "sc_w_in", "sc_w_out", "pool_w", "cf_w_pw1", "cf_w_pw2", "ffn_w_up", "ffn_w_down"]
    small_names = [n for n in names if n not in big_names]
    delta, new_m, new_v = {}, {}, {}
    for n in big_names:
        delta[n], new_m[n], new_v[n] = adam_big(f"adamw_{n}", weights[n], grads[n], m_in[n], v_in[n])
    grads = {n: grads[n].reshape(weights[n].shape) for n in names}
    sm_shapes = [weights[n].shape for n in small_names]
    sd, sm, sv = _adamw("adamw_small", _pack([weights[n] for n in small_names]), _pack([grads[n] for n in small_names]),
                        _pack([m_in[n] for n in small_names]), _pack([v_in[n] for n in small_names]))
    for n, d_, m_, v_ in zip(small_names, _unpack(sd, sm_shapes), _unpack(sm, sm_shapes), _unpack(sv, sm_shapes)):
        delta[n], new_m[n], new_v[n] = d_, m_, v_

    return (loss, grad_x, *[grads[n] for n in names], *[delta[n] for n in names], *[new_m[n] for n in names],
            *[new_v[n] for n in names])
```

```python
import itertools

import jax
import jax.numpy as jnp
from jax import lax
from jax.experimental import pallas as pl
from jax.experimental.pallas import tpu as pltpu
from jax.experimental.pallas import tpu_sc as plsc

D = 1024
F = 2816
DEPTH = 4
POOL_WINDOWS = (2, 4, 8, 16)
POOL_GROUP = 256
CF_TAPS = 31
RMS_EPS = 1e-6
LN_EPS = 1e-5
ADAM_LR = 0.001
ADAM_B1 = 0.9
ADAM_B2 = 0.999
ADAM_EPS = 1e-08
ADAM_WD = 0.01
ADAM_STEP = 10

BF = jnp.bfloat16
F32 = jnp.float32
MESH = pl.DeviceIdType.MESH
SDS = jax.ShapeDtypeStruct
N_CHIPS = 4
N_DEV = 8
GATHER_FIRST_ID, GATHER_REST_ID, GRAD_REST_ID = 1, 2, 3
VMEM_LIMIT_MB = 56
ROW_TILE_D = 256
ROW_TILE_F = 256
HALO = 8
HALO_BF = 16
CF_HALO = 32
POOL_HALO = 16


def _cp(n_axes):
    return pltpu.CompilerParams(dimension_semantics=("arbitrary",) * n_axes,
                                vmem_limit_bytes=VMEM_LIMIT_MB * 1024 * 1024)


def _tile(n, pref):
    t = min(n, pref)
    assert n % t == 0, (n, pref)
    return t


def _const(shape):
    nd = len(shape)
    return pl.BlockSpec(shape, lambda *_: (0,) * nd)


def _vecs(rows, width):
    v = jnp.stack([r.reshape(width).astype(F32) for r in rows])
    pad = (-v.shape[0]) % 8
    return jnp.pad(v, ((0, pad), (0, 0)))


def _sigmoid(v):
    return 0.5 * jnp.tanh(0.5 * v) + 0.5


def _down(prev8, g, k):
    n = g.shape[0]
    return pltpu.roll(jnp.concatenate([prev8, g], axis=0), k, 0)[8:8 + n]


def _up(g, next8, k):
    n = g.shape[0]
    return pltpu.roll(jnp.concatenate([g, next8], axis=0), n + 8 - k, 0)[0:n]


def _fold8(v):
    acc = v[0:8]
    for r in range(8, v.shape[0], 8):
        acc = acc + v[r:r + 8]
    return acc


def _mm(name, a, b, out_sds, grid, a_spec, b_spec, o_spec, acc_shape, dn):
    nk = grid[2]

    def body(a_ref, b_ref, o_ref, *acc):
        part = lax.dot_general(a_ref[...], b_ref[...], dn, preferred_element_type=F32)
        if nk == 1:
            o_ref[...] = part.astype(o_ref.dtype)
        else:
            acc_ref = acc[0]
            k = pl.program_id(2)

            @pl.when(k == 0)
            def _():
                acc_ref[...] = part

            @pl.when(k > 0)
            def _():
                acc_ref[...] += part

            @pl.when(k == nk - 1)
            def _():
                o_ref[...] = acc_ref[...].astype(o_ref.dtype)

    scratch = [] if nk == 1 else [pltpu.VMEM(acc_shape, F32)]
    return pl.pallas_call(body, name=name, out_shape=out_sds, grid=grid, in_specs=[a_spec, b_spec],
                          out_specs=o_spec, scratch_shapes=scratch, compiler_params=_cp(3))(a, b)


NN = (((1,), (0,)), ((), ()))
NT = (((1,), (1,)), ((), ()))
TN = (((0,), (0,)), ((), ()))


def _mm_nn(name, a, w, layer, col_sharded, out_dtype, tm=1024, tn=None):
    m, k = a.shape
    tm = _tile(m, tm)
    if col_sharded:
        n4 = w.shape[3]
        tn = n4 if tn is None else tn
        tpc = n4 // tn
        n = N_CHIPS * n4
        b_spec = pl.BlockSpec((None, None, k, tn), lambda i, j, kk: (layer, j // tpc, 0, j % tpc))
    else:
        n = w.shape[2]
        tn = n if tn is None else tn
        b_spec = pl.BlockSpec((None, k, tn), lambda i, j, kk: (layer, 0, j))
    return _mm(name, a, w, SDS((m, n), out_dtype), (m // tm, n // tn, 1),
               pl.BlockSpec((tm, k), lambda i, j, kk: (i, 0)), b_spec,
               pl.BlockSpec((tm, tn), lambda i, j, kk: (i, j)), None, NN)


def _mm_nt(name, g, w, layer, col_sharded, out_dtype, tm=1024, tn=None):
    m, n = g.shape
    if col_sharded:
        return _mm_nt_staged(name, g, w, layer, out_dtype)
    tm = _tile(m, tm)
    kdim = w.shape[1]
    tn = kdim if tn is None else tn
    return _mm(name, g, w, SDS((m, kdim), out_dtype), (m // tm, kdim // tn, 1),
               pl.BlockSpec((tm, n), lambda i, j, kk: (i, 0)),
               pl.BlockSpec((None, tn, n), lambda i, j, kk: (layer, j, 0)),
               pl.BlockSpec((tm, tn), lambda i, j, kk: (i, j)), None, NT)


def _mm_nt_staged(name, g, w, layer, out_dtype, tm=512):
    m, n = g.shape
    kdim, n4 = w.shape[2], w.shape[3]
    tm = _tile(m, tm)

    def body(g_ref, w_hbm, o_ref, wbuf, sems):
        @pl.when(pl.program_id(0) == 0)
        def _():
            cps = [pltpu.make_async_copy(w_hbm.at[layer, j], wbuf.at[:, pl.ds(j * n4, n4)], sems.at[j])
                   for j in range(N_CHIPS)]
            for cp in cps:
                cp.start()
            for cp in cps:
                cp.wait()

        o_ref[...] = lax.dot_general(g_ref[...], wbuf[...], NT, preferred_element_type=F32).astype(o_ref.dtype)

    return pl.pallas_call(body, name=name, out_shape=SDS((m, kdim), out_dtype), grid=(m // tm,),
                          in_specs=[pl.BlockSpec((tm, n), lambda i: (i, 0)), pl.BlockSpec(memory_space=pltpu.HBM)],
                          out_specs=pl.BlockSpec((tm, kdim), lambda i: (i, 0)),
                          scratch_shapes=[pltpu.VMEM((kdim, n), BF), pltpu.SemaphoreType.DMA((N_CHIPS,))],
                          compiler_params=_cp(1))(g, w)


def _mm_tn(name, a, g, col_sharded, tk=None, tn=None, ts=2048):
    s, k = a.shape
    n = g.shape[1]
    ts = _tile(s, ts)
    tk = k if tk is None else tk
    if col_sharded:
        n4 = n // N_CHIPS
        tn = n4 if tn is None else tn
        tpc = n4 // tn
        out_sds = SDS((N_CHIPS, k, n4), BF)
        o_spec = pl.BlockSpec((None, tk, tn), lambda i, j, ss: (j // tpc, i, j % tpc))
    else:
        tn = n if tn is None else tn
        out_sds = SDS((k, n), BF)
        o_spec = pl.BlockSpec((tk, tn), lambda i, j, ss: (i, j))
    return _mm(name, a, g, out_sds, (k // tk, n // tn, s // ts),
               pl.BlockSpec((ts, tk), lambda i, j, ss: (ss, i)),
               pl.BlockSpec((ts, tn), lambda i, j, ss: (ss, j)), o_spec, (tk, tn), TN)


def _rows(tm, width, col=0):
    return pl.BlockSpec((tm, width), lambda i: (i, col))


def _pro_fwd(name, x, vec):
    s = x.shape[0]
    tm = _tile(s, ROW_TILE_D)

    def body(x_ref, v_ref, h_ref):
        xv = x_ref[...]
        r = lax.rsqrt(jnp.mean(xv * xv, axis=-1, keepdims=True) + RMS_EPS)
        a = v_ref[0:1, :] * (1.0 + v_ref[1:2, :])
        h_ref[...] = (xv * r * a + v_ref[2:3, :]).astype(h_ref.dtype)

    return pl.pallas_call(body, name=name, out_shape=SDS((s, D), BF), grid=(s // tm,),
                          in_specs=[_rows(tm, D), _const((8, D))], out_specs=_rows(tm, D),
                          compiler_params=_cp(1))(x, vec)


def _epi_pro_fwd(name, x, m, vec, with_next):
    s = x.shape[0]
    tm = _tile(s, ROW_TILE_D)

    def body(x_ref, m_ref, v_ref, o_ref, *h_ref):
        mv = m_ref[...] + v_ref[2:3, :]
        rm = lax.rsqrt(jnp.mean(mv * mv, axis=-1, keepdims=True) + RMS_EPS)
        xo = x_ref[...] + v_ref[0:1, :] * (mv * rm * v_ref[1:2, :])
        o_ref[...] = xo
        if with_next:
            r = lax.rsqrt(jnp.mean(xo * xo, axis=-1, keepdims=True) + RMS_EPS)
            a = v_ref[3:4, :] * (1.0 + v_ref[4:5, :])
            h_ref[0][...] = (xo * r * a + v_ref[5:6, :]).astype(BF)

    out_shape = (SDS((s, D), F32),) + ((SDS((s, D), BF),) if with_next else ())
    res = pl.pallas_call(body, name=name, out_shape=out_shape, grid=(s // tm,),
                         in_specs=[_rows(tm, D), _rows(tm, D), _const((8, D))],
                         out_specs=tuple([_rows(tm, D)] * len(out_shape)), compiler_params=_cp(1))(x, m, vec)
    return res if with_next else (res[0], None)


def _loss_fwd_bwd(y, t):
    s = y.shape[0]
    tm = _tile(s, ROW_TILE_D)

    def body(y_ref, t_ref, dy_ref, acc_ref):
        @pl.when(pl.program_id(0) == 0)
        def _():
            acc_ref[...] = jnp.zeros_like(acc_ref)

        e = y_ref[...] - t_ref[...]
        dy_ref[...] = e * (1.0 / D)
        acc_ref[0:1, :] += jnp.sum(e * e, axis=0, keepdims=True) * (0.5 / D)

    return pl.pallas_call(body, name="loss", out_shape=(SDS((s, D), F32), SDS((8, D), F32)), grid=(s // tm,),
                          in_specs=[_rows(tm, D), _rows(tm, D)], out_specs=(_rows(tm, D), _const((8, D))),
                          compiler_params=_cp(1))(y, t)


def _epi_bwd_rows(dv, mv, g, ngb, dm_ref, acc_ref, row):
    rm = lax.rsqrt(jnp.mean(mv * mv, axis=-1, keepdims=True) + RMS_EPS)
    mn = mv * rm
    dmn = dv * (g * ngb)
    dm = rm * (dmn - mn * jnp.mean(dmn * mn, axis=-1, keepdims=True))
    dm_ref[...] = dm.astype(dm_ref.dtype)
    t = jnp.sum(dv * mn, axis=0, keepdims=True)
    acc_ref[row:row + 1, :] += t * ngb
    acc_ref[row + 1:row + 2, :] += t * g
    acc_ref[row + 2:row + 3, :] += jnp.sum(dm, axis=0, keepdims=True)


def _epi_bwd(name, dxo, m, vec):
    s = dxo.shape[0]
    tm = _tile(s, ROW_TILE_D)

    def body(d_ref, m_ref, v_ref, dm_ref, acc_ref):
        @pl.when(pl.program_id(0) == 0)
        def _():
            acc_ref[...] = jnp.zeros_like(acc_ref)

        _epi_bwd_rows(d_ref[...], m_ref[...] + v_ref[2:3, :], v_ref[0:1, :], v_ref[1:2, :], dm_ref, acc_ref, 0)

    return pl.pallas_call(body, name=name, out_shape=(SDS((s, D), BF), SDS((8, D), F32)), grid=(s // tm,),
                          in_specs=[_rows(tm, D), _rows(tm, D), _const((8, D))],
                          out_specs=(_rows(tm, D), _const((8, D))), compiler_params=_cp(1))(dxo, m, vec)


def _pro_epi_bwd(name, dh, x, dxo, m, vec):
    s = x.shape[0]
    tm = _tile(s, ROW_TILE_D)

    def body(dh_ref, x_ref, d_ref, m_ref, v_ref, dx_ref, dm_ref, acc_ref):
        @pl.when(pl.program_id(0) == 0)
        def _():
            acc_ref[...] = jnp.zeros_like(acc_ref)

        dx = _pro_bwd_rows(dh_ref[...].astype(F32), x_ref[...], d_ref[...], v_ref[0:1, :], v_ref[1:2, :], acc_ref)
        dx_ref[...] = dx
        _epi_bwd_rows(dx, m_ref[...] + v_ref[4:5, :], v_ref[2:3, :], v_ref[3:4, :], dm_ref, acc_ref, 3)

    return pl.pallas_call(body, name=name, out_shape=(SDS((s, D), F32), SDS((s, D), BF), SDS((8, D), F32)),
                          grid=(s // tm,),
                          in_specs=[_rows(tm, D), _rows(tm, D), _rows(tm, D), _rows(tm, D), _const((8, D))],
                          out_specs=(_rows(tm, D), _rows(tm, D), _const((8, D))),
                          compiler_params=_cp(1))(dh, x, dxo, m, vec)


def _pro_bwd_rows(dhv, xv, dxo, ng, sc, acc_ref):
    sc1 = 1.0 + sc
    r = lax.rsqrt(jnp.mean(xv * xv, axis=-1, keepdims=True) + RMS_EPS)
    xn = xv * r
    dxn = dhv * (ng * sc1)
    t = jnp.sum(dhv * xn, axis=0, keepdims=True)
    acc_ref[0:1, :] += jnp.sum(dhv, axis=0, keepdims=True)
    acc_ref[1:2, :] += t * ng
    acc_ref[2:3, :] += t * sc1
    return dxo + r * (dxn - xn * jnp.mean(dxn * xn, axis=-1, keepdims=True))


def _pro_bwd(name, dh, x, dxo, vec):
    s = x.shape[0]
    tm = _tile(s, ROW_TILE_D)

    def body(dh_ref, x_ref, d_ref, v_ref, dx_ref, acc_ref):
        @pl.when(pl.program_id(0) == 0)
        def _():
            acc_ref[...] = jnp.zeros_like(acc_ref)

        dx_ref[...] = _pro_bwd_rows(dh_ref[...].astype(F32), x_ref[...], d_ref[...], v_ref[0:1, :], v_ref[1:2, :],
                                    acc_ref)

    return pl.pallas_call(body, name=name, out_shape=(SDS((s, D), F32), SDS((8, D), F32)), grid=(s // tm,),
                          in_specs=[_rows(tm, D), _rows(tm, D), _rows(tm, D), _const((8, D))],
                          out_specs=(_rows(tm, D), _const((8, D))), compiler_params=_cp(1))(dh, x, dxo, vec)


def _carry_up(buf, tm, first, halo=HALO):
    @pl.when(first)
    def _():
        buf[tm:tm + halo, :] = jnp.zeros((halo, buf.shape[1]), F32)

    @pl.when(jnp.logical_not(first))
    def _():
        buf[tm:tm + halo, :] = buf[0:halo, :]


def _ffn_act_fwd(name, up, vec):
    s = up.shape[0]
    tm = _tile(s, ROW_TILE_F)
    rb_n = _tile(tm, 64)
    cw = 256

    def body(g_ref, v_ref, w_ref, a_ref, carry):
        @pl.when(pl.program_id(0) == 0)
        def _():
            carry[...] = jnp.zeros_like(carry)

        for cc in range(F // cw):
            cols = slice(cc * cw, (cc + 1) * cw)
            w0, w1, w2, b = w_ref[0:1, cols], w_ref[1:2, cols], w_ref[2:3, cols], w_ref[3:4, cols]

            def rb_body(rb, prev8):
                rows = pl.ds(pl.multiple_of(rb * rb_n, rb_n), rb_n)
                g = g_ref[rows, cols].astype(F32)
                gc = w2 * g + w1 * _down(prev8, g, 1) + w0 * _down(prev8, g, 2) + b
                a_ref[rows, cols] = (gc * _sigmoid(gc) * v_ref[rows, cols].astype(F32)).astype(a_ref.dtype)
                return g[rb_n - 8:rb_n]

            carry[:, cols] = lax.fori_loop(0, tm // rb_n, rb_body, carry[:, cols])

    return pl.pallas_call(body, name=name, out_shape=SDS((s, F), BF), grid=(s // tm,),
                          in_specs=[_rows(tm, F, 0), _rows(tm, F, 1), _const((8, F))], out_specs=_rows(tm, F),
                          scratch_shapes=[pltpu.VMEM((HALO, F), F32)], compiler_params=_cp(1))(up, up, vec)


def _prev_halo_spec(nt, tm, width, col):
    per = tm // HALO_BF
    return pl.BlockSpec((HALO_BF, width), lambda i: (jnp.maximum((nt - 1 - i) * per - 1, 0), col))


def _ffn_act_bwd(name, up, da, vec):
    s = up.shape[0]
    tm = _tile(s, ROW_TILE_F)
    nt = s // tm
    rev = lambda col: pl.BlockSpec((tm, F), lambda i: (nt - 1 - i, col))

    rb_n = _tile(tm, 64)
    nrb = tm // rb_n
    cw = 128

    def body(g_ref, gp_ref, v_ref, da_ref, w_ref, dup_ref, acc_ref, carry):
        step = pl.program_id(0)

        @pl.when(step == 0)
        def _():
            acc_ref[...] = jnp.zeros_like(acc_ref)
            carry[...] = jnp.zeros_like(carry)

        for cc in range(F // cw):
            cols = slice(cc * cw, (cc + 1) * cw)
            w0, w1, w2, b = w_ref[0:1, cols], w_ref[1:2, cols], w_ref[2:3, cols], w_ref[3:4, cols]
            halo = jnp.where(step < nt - 1, gp_ref[:, cols].astype(F32)[HALO_BF - 8:HALO_BF], 0.0)

            def rb_body(it, st):
                nxt8, ab, a0, a1, a2 = st
                rb = nrb - 1 - it
                r0 = pl.multiple_of(rb * rb_n, rb_n)
                rows = pl.ds(r0, rb_n)
                g = g_ref[rows, cols].astype(F32)
                ra = pl.multiple_of(jnp.maximum(r0 - HALO_BF, 0), HALO_BF)
                above = g_ref[pl.ds(ra, HALO_BF), cols].astype(F32)[HALO_BF - 8:HALO_BF]
                prev8 = jnp.where(rb == 0, halo, above)
                g1 = _down(prev8, g, 1)
                g2 = _down(prev8, g, 2)
                gc = w2 * g + w1 * g1 + w0 * g2 + b
                sg = _sigmoid(gc)
                sl = gc * sg
                val = v_ref[rows, cols].astype(F32)
                dav = da_ref[rows, cols].astype(F32)
                dup_ref[rows, F + cc * cw:F + (cc + 1) * cw] = (dav * sl).astype(dup_ref.dtype)
                dgc = (dav * val) * (sg + sl * (1.0 - sg))
                dup_ref[rows, cols] = (w2 * dgc + w1 * _up(dgc, nxt8, 1) + w0 * _up(dgc, nxt8, 2)).astype(dup_ref.dtype)
                return (dgc[0:8], ab + _fold8(dgc), a0 + _fold8(g2 * dgc), a1 + _fold8(g1 * dgc),
                        a2 + _fold8(g * dgc))

            z = jnp.zeros((8, cw), F32)
            nxt8, ab, a0, a1, a2 = lax.fori_loop(0, nrb, rb_body, (carry[:, cols], z, z, z, z))
            carry[:, cols] = nxt8
            acc_ref[0:1, cols] += jnp.sum(ab, axis=0, keepdims=True)
            acc_ref[1:2, cols] += jnp.sum(a0, axis=0, keepdims=True)
            acc_ref[2:3, cols] += jnp.sum(a1, axis=0, keepdims=True)
            acc_ref[3:4, cols] += jnp.sum(a2, axis=0, keepdims=True)

    return pl.pallas_call(
        body, name=name, out_shape=(SDS((s, 2 * F), BF), SDS((8, F), F32)), grid=(nt,),
        in_specs=[rev(0), _prev_halo_spec(nt, tm, F, 0), rev(1), rev(0), _const((8, F))],
        out_specs=(pl.BlockSpec((tm, 2 * F), lambda i: (nt - 1 - i, 0)), _const((8, F))),
        scratch_shapes=[pltpu.VMEM((HALO, F), F32)], compiler_params=_cp(1))(up, up, up, da, vec)


def _sc_act_fwd(name, p, vec):
    s = p.shape[0]
    tm = _tile(s, ROW_TILE_D)

    rb_n = _tile(tm, 64)
    cw = 256

    def body(b_ref, c_ref, h_ref, w_ref, z_ref, carry):
        @pl.when(pl.program_id(0) == 0)
        def _():
            carry[...] = jnp.zeros_like(carry)

        for cc in range(D // cw):
            cols = slice(cc * cw, (cc + 1) * cw)
            w0, w1, w2 = w_ref[0:1, cols], w_ref[1:2, cols], w_ref[2:3, cols]

            def rb_body(rb, prev8):
                rows = pl.ds(pl.multiple_of(rb * rb_n, rb_n), rb_n)
                q = c_ref[rows, cols].astype(F32) * h_ref[rows, cols].astype(F32)
                u = w2 * q + w1 * _down(prev8, q, 1) + w0 * _down(prev8, q, 2)
                z_ref[rows, cols] = (b_ref[rows, cols].astype(F32) * u).astype(z_ref.dtype)
                return q[rb_n - 8:rb_n]

            carry[:, cols] = lax.fori_loop(0, tm // rb_n, rb_body, carry[:, cols])

    return pl.pallas_call(body, name=name, out_shape=SDS((s, D), BF), grid=(s // tm,),
                          in_specs=[_rows(tm, D, 0), _rows(tm, D, 1), _rows(tm, D, 2), _const((8, D))],
                          out_specs=_rows(tm, D), scratch_shapes=[pltpu.VMEM((HALO, D), F32)],
                          compiler_params=_cp(1))(p, p, p, vec)


def _sc_act_bwd(name, p, dz, vec):
    s = p.shape[0]
    tm = _tile(s, ROW_TILE_D)
    nt = s // tm
    rev = lambda col: pl.BlockSpec((tm, D), lambda i: (nt - 1 - i, col))

    rb_n = _tile(tm, 64)
    nrb = tm // rb_n
    cw = 128

    def body(b_ref, c_ref, h_ref, cp_ref, hp_ref, dz_ref, w_ref, z_ref, dp_ref, acc_ref, carry):
        step = pl.program_id(0)

        @pl.when(step == 0)
        def _():
            acc_ref[...] = jnp.zeros_like(acc_ref)
            carry[...] = jnp.zeros_like(carry)

        for cc in range(D // cw):
            cols = slice(cc * cw, (cc + 1) * cw)
            w0, w1, w2 = w_ref[0:1, cols], w_ref[1:2, cols], w_ref[2:3, cols]
            halo = jnp.where(step < nt - 1,
                             (cp_ref[:, cols].astype(F32) * hp_ref[:, cols].astype(F32))[HALO_BF - 8:HALO_BF], 0.0)

            def rb_body(it, st):
                nxt8, a0, a1, a2 = st
                rb = nrb - 1 - it
                r0 = pl.multiple_of(rb * rb_n, rb_n)
                rows = pl.ds(r0, rb_n)
                cg = c_ref[rows, cols].astype(F32)
                hin = h_ref[rows, cols].astype(F32)
                bg = b_ref[rows, cols].astype(F32)
                q = cg * hin
                up_rows = pl.ds(pl.multiple_of(jnp.maximum(r0 - HALO_BF, 0), HALO_BF), HALO_BF)
                above = (c_ref[up_rows, cols].astype(F32) * h_ref[up_rows, cols].astype(F32))[HALO_BF - 8:HALO_BF]
                prev8 = jnp.where(rb == 0, halo, above)
                q1 = _down(prev8, q, 1)
                q2 = _down(prev8, q, 2)
                u = w2 * q + w1 * q1 + w0 * q2
                dzv = dz_ref[rows, cols].astype(F32)
                z_ref[rows, cols] = (bg * u).astype(z_ref.dtype)
                dp_ref[rows, cols] = (dzv * u).astype(dp_ref.dtype)
                du = dzv * bg
                dq = w2 * du + w1 * _up(du, nxt8, 1) + w0 * _up(du, nxt8, 2)
                dp_ref[rows, D + cc * cw:D + (cc + 1) * cw] = (dq * hin).astype(dp_ref.dtype)
                dp_ref[rows, 2 * D + cc * cw:2 * D + (cc + 1) * cw] = (dq * cg).astype(dp_ref.dtype)
                return du[0:8], a0 + _fold8(q2 * du), a1 + _fold8(q1 * du), a2 + _fold8(q * du)

            z = jnp.zeros((8, cw), F32)
            nxt8, a0, a1, a2 = lax.fori_loop(0, nrb, rb_body, (carry[:, cols], z, z, z))
            carry[:, cols] = nxt8
            acc_ref[0:1, cols] += jnp.sum(a0, axis=0, keepdims=True)
            acc_ref[1:2, cols] += jnp.sum(a1, axis=0, keepdims=True)
            acc_ref[2:3, cols] += jnp.sum(a2, axis=0, keepdims=True)

    return pl.pallas_call(
        body, name=name, out_shape=(SDS((s, D), BF), SDS((s, 3 * D), BF), SDS((8, D), F32)), grid=(nt,),
        in_specs=[rev(0), rev(1), rev(2), _prev_halo_spec(nt, tm, D, 1), _prev_halo_spec(nt, tm, D, 2), rev(0),
                  _const((8, D))],
        out_specs=(rev(0), pl.BlockSpec((tm, 3 * D), lambda i: (nt - 1 - i, 0)), _const((8, D))),
        scratch_shapes=[pltpu.VMEM((HALO, D), F32)], compiler_params=_cp(1))(p, p, p, p, p, dz, vec)


CF_ROW_BLOCK = 32
CF_LANES = 128


def _tap_conv_block(ext, rb_n, tap_of_offset, t_ref, cols, init, u=None, accs=None):
    n = ext.shape[0]
    out = init
    for b in range(8):
        rolled = ext if b == 0 else pltpu.roll(ext, n - b, 0)
        for a in range(n // 8):
            k = tap_of_offset(8 * a + b)
            if k is None:
                continue
            sl = rolled[8 * a:8 * a + rb_n]
            out = out + t_ref[k:k + 1, cols] * sl
            if accs is not None:
                accs[k] = accs[k] + _fold8(u * sl)
    return out


def _cf_act_fwd(name, p, taps, vec):
    s = p.shape[0]
    tm = _tile(s, ROW_TILE_D)
    rb_n = _tile(tm, CF_ROW_BLOCK)
    base = CF_HALO - (CF_TAPS - 1)
    tap_of = lambda off: off - base if 0 <= off - base < CF_TAPS else None

    def body(a_ref, g_ref, t_ref, v_ref, w_ref, cv_ref, buf):
        first = pl.program_id(0) == 0

        @pl.when(first)
        def _():
            buf[0:CF_HALO, :] = jnp.zeros((CF_HALO, D), F32)

        @pl.when(jnp.logical_not(first))
        def _():
            buf[0:CF_HALO, :] = buf[tm:tm + CF_HALO, :]

        a = a_ref[...].astype(F32) + v_ref[0:1, :]
        g = g_ref[...].astype(F32) + v_ref[1:2, :]
        buf[CF_HALO:CF_HALO + tm, :] = a * _sigmoid(g)
        for ci in range(D // CF_LANES):
            cols = slice(ci * CF_LANES, (ci + 1) * CF_LANES)

            def rb_body(rb, carry):
                r0 = pl.multiple_of(rb * rb_n, rb_n)
                ext = buf[pl.ds(r0, rb_n + CF_HALO), cols]
                init = jnp.zeros((rb_n, CF_LANES), F32) + v_ref[2:3, cols]
                cv_ref[pl.ds(r0, rb_n), cols] = _tap_conv_block(ext, rb_n, tap_of, t_ref, cols, init)
                return carry

            lax.fori_loop(0, tm // rb_n, rb_body, 0)
        cv = cv_ref[...]
        mu = jnp.mean(cv, axis=-1, keepdims=True)
        cc = cv - mu
        rstd = lax.rsqrt(jnp.mean(cc * cc, axis=-1, keepdims=True) + LN_EPS)
        ln = cc * rstd * v_ref[3:4, :] + v_ref[4:5, :]
        w_ref[...] = (ln * _sigmoid(ln)).astype(w_ref.dtype)

    return pl.pallas_call(body, name=name, out_shape=(SDS((s, D), BF), SDS((s, D), F32)), grid=(s // tm,),
                          in_specs=[_rows(tm, D, 0), _rows(tm, D, 1), _const((32, D)), _const((8, D))],
                          out_specs=(_rows(tm, D), _rows(tm, D)),
                          scratch_shapes=[pltpu.VMEM((tm + CF_HALO, D), F32)], compiler_params=_cp(1))(p, p, taps, vec)


def _cf_act_bwd(name, p, cv, dw, taps, vec):
    s = p.shape[0]
    tm = _tile(s, ROW_TILE_D)
    nt = s // tm
    rev = lambda col: pl.BlockSpec((tm, D), lambda i: (nt - 1 - i, col))
    rb_n = _tile(tm, CF_ROW_BLOCK)
    tap_of = lambda off: CF_TAPS - 1 - off if off < CF_TAPS else None

    def body(a_ref, g_ref, cv_ref, dw_ref, t_ref, v_ref, dp_ref, tacc_ref, acc_ref, nbuf, ubuf, dubuf):
        step = pl.program_id(0)

        @pl.when(step == 0)
        def _():
            acc_ref[...] = jnp.zeros_like(acc_ref)
            tacc_ref[...] = jnp.zeros_like(tacc_ref)

        a = a_ref[...].astype(F32) + v_ref[0:1, :]
        g = g_ref[...].astype(F32) + v_ref[1:2, :]
        sg = _sigmoid(g)
        ubuf[...] = a * sg
        cvv = cv_ref[...]
        mu = jnp.mean(cvv, axis=-1, keepdims=True)
        cc = cvv - mu
        rstd = lax.rsqrt(jnp.mean(cc * cc, axis=-1, keepdims=True) + LN_EPS)
        vhat = cc * rstd
        ln = vhat * v_ref[3:4, :] + v_ref[4:5, :]
        s2 = _sigmoid(ln)
        dln = dw_ref[...].astype(F32) * (s2 * (1.0 + ln * (1.0 - s2)))
        acc_ref[1:2, :] += jnp.sum(dln * vhat, axis=0, keepdims=True)
        acc_ref[2:3, :] += jnp.sum(dln, axis=0, keepdims=True)
        dvh = dln * v_ref[3:4, :]
        dcv = rstd * (dvh - jnp.mean(dvh, axis=-1, keepdims=True)
                      - vhat * jnp.mean(dvh * vhat, axis=-1, keepdims=True))
        acc_ref[0:1, :] += jnp.sum(dcv, axis=0, keepdims=True)
        _carry_up(nbuf, tm, step == 0, CF_HALO)
        nbuf[0:tm, :] = dcv
        for ci in range(D // CF_LANES):
            cols = slice(ci * CF_LANES, (ci + 1) * CF_LANES)

            def rb_body(rb, accs):
                r0 = pl.multiple_of(rb * rb_n, rb_n)
                ext = nbuf[pl.ds(r0, rb_n + CF_HALO), cols]
                accs = list(accs)
                dubuf[pl.ds(r0, rb_n), cols] = _tap_conv_block(
                    ext, rb_n, tap_of, t_ref, cols, jnp.zeros((rb_n, CF_LANES), F32),
                    ubuf[pl.ds(r0, rb_n), cols], accs)
                return tuple(accs)

            z = jnp.zeros((8, CF_LANES), F32)
            accs = lax.fori_loop(0, tm // rb_n, rb_body, tuple([z] * CF_TAPS))
            for k in range(CF_TAPS):
                tacc_ref[k:k + 1, cols] += jnp.sum(accs[k], axis=0, keepdims=True)
        a = a_ref[...].astype(F32) + v_ref[0:1, :]
        sg = _sigmoid(g_ref[...].astype(F32) + v_ref[1:2, :])
        da = dubuf[...] * sg
        dg = da * a * (1.0 - sg)
        dp_ref[:, 0:D] = da.astype(dp_ref.dtype)
        dp_ref[:, D:2 * D] = dg.astype(dp_ref.dtype)
        acc_ref[3:4, :] += jnp.sum(da, axis=0, keepdims=True)
        acc_ref[4:5, :] += jnp.sum(dg, axis=0, keepdims=True)

    return pl.pallas_call(
        body, name=name, out_shape=(SDS((s, 2 * D), BF), SDS((32, D), F32), SDS((8, D), F32)), grid=(nt,),
        in_specs=[rev(0), rev(1), rev(0), rev(0), _const((32, D)), _const((8, D))],
        out_specs=(pl.BlockSpec((tm, 2 * D), lambda i: (nt - 1 - i, 0)), _const((32, D)), _const((8, D))),
        scratch_shapes=[pltpu.VMEM((tm + CF_HALO, D), F32), pltpu.VMEM((tm, D), F32), pltpu.VMEM((tm, D), F32)],
        compiler_params=_cp(1))(p, p, cv, dw, taps, vec)


def _inv_count(row0, tm, window):
    t = row0 + lax.broadcasted_iota(jnp.int32, (tm, 1), 0)
    return 1.0 / jnp.minimum(t + 1, window).astype(F32)


def _pool_fwd(name, x, w, vec):
    s = x.shape[0]
    tm = _tile(s, ROW_TILE_D)
    G = POOL_GROUP

    def body(x_ref, w_ref, v_ref, pl_ref, m_ref, buf):
        i = pl.program_id(0)

        @pl.when(i == 0)
        def _():
            buf[0:POOL_HALO, :] = jnp.zeros((POOL_HALO, D), F32)

        @pl.when(i > 0)
        def _():
            buf[0:POOL_HALO, :] = buf[tm:tm + POOL_HALO, :]

        xv = x_ref[...]
        r = lax.rsqrt(jnp.mean(xv * xv, axis=-1, keepdims=True) + RMS_EPS)
        buf[POOL_HALO:POOL_HALO + tm, :] = xv * r * (v_ref[0:1, :] * (1.0 + v_ref[1:2, :])) + v_ref[2:3, :]
        for gi, win in enumerate(POOL_WINDOWS):
            cols = slice(gi * G, (gi + 1) * G)
            acc = buf[POOL_HALO:POOL_HALO + tm, cols]
            hg = acc
            for j in range(1, win):
                acc = acc + buf[POOL_HALO - j:POOL_HALO - j + tm, cols]
            pooled = (acc * _inv_count(i * tm, tm, win) - hg).astype(BF)
            pl_ref[:, cols] = pooled
            yg = jnp.dot(pooled, w_ref[gi], preferred_element_type=F32)
            m_ref[:, cols] = (yg + v_ref[3:4, cols]) * v_ref[4:5, cols]

    return pl.pallas_call(body, name=name, out_shape=(SDS((s, D), BF), SDS((s, D), F32)), grid=(s // tm,),
                          in_specs=[_rows(tm, D), _const((4, G, G)), _const((8, D))],
                          out_specs=(_rows(tm, D), _rows(tm, D)),
                          scratch_shapes=[pltpu.VMEM((tm + POOL_HALO, D), F32)], compiler_params=_cp(1))(x, w, vec)


def _pool_bwd(name, pooled, dm, w, vec):
    s = pooled.shape[0]
    tm = _tile(s, ROW_TILE_D)
    nt = s // tm
    G = POOL_GROUP
    rev = pl.BlockSpec((tm, D), lambda i: (nt - 1 - i, 0))

    def body(p_ref, dm_ref, w_ref, v_ref, dh_ref, dw_ref, acc_ref, nbuf):
        step = pl.program_id(0)
        row0 = (nt - 1 - step) * tm

        @pl.when(step == 0)
        def _():
            acc_ref[...] = jnp.zeros_like(acc_ref)
            dw_ref[...] = jnp.zeros_like(dw_ref)

        _carry_up(nbuf, tm, step == 0, POOL_HALO)
        dmv = dm_ref[...].astype(F32)
        acc_ref[0:1, :] += jnp.sum(dmv, axis=0, keepdims=True) * v_ref[1:2, :]
        dps = []
        for gi, win in enumerate(POOL_WINDOWS):
            cols = slice(gi * G, (gi + 1) * G)
            pg = p_ref[:, cols]
            yb = jnp.dot(pg, w_ref[gi], preferred_element_type=F32) + v_ref[0:1, cols]
            acc_ref[1:2, cols] += jnp.sum(dmv[:, cols] * yb, axis=0, keepdims=True)
            dy = (dmv[:, cols] * v_ref[1:2, cols]).astype(BF)
            dw_ref[gi] += lax.dot_general(pg, dy, TN, preferred_element_type=F32)
            dpg = lax.dot_general(dy, w_ref[gi], NT, preferred_element_type=F32)
            dps.append(dpg)
            nbuf[0:tm, cols] = dpg * _inv_count(row0, tm, win)
        for gi, win in enumerate(POOL_WINDOWS):
            cols = slice(gi * G, (gi + 1) * G)
            acc = nbuf[0:tm, cols]
            for j in range(1, win):
                acc = acc + nbuf[j:j + tm, cols]
            dh_ref[:, cols] = acc - dps[gi]

    return pl.pallas_call(
        body, name=name, out_shape=(SDS((s, D), F32), SDS((4, G, G), F32), SDS((8, D), F32)), grid=(nt,),
        in_specs=[rev, rev, _const((4, G, G)), _const((8, D))],
        out_specs=(rev, _const((4, G, G)), _const((8, D))),
        scratch_shapes=[pltpu.VMEM((tm + POOL_HALO, D), F32)], compiler_params=_cp(1))(pooled, dm, w, vec)


def _row_tile_2d(rows, width, bytes_per_row_elem=4, budget=2 * 1024 * 1024):
    t = max(8, budget // (width * bytes_per_row_elem))
    t = min(rows, 1 << (t.bit_length() - 1))
    while rows % t:
        t //= 2
    return t


def _add_slots(name, r):
    nl, _, k, n = r.shape
    tk = _row_tile_2d(k, n, 16)

    def body(r_ref, o_ref):
        f = lambda i: r_ref[i].astype(F32)
        o_ref[...] = ((f(7) + f(6)) + (f(0) + f(1))) + ((f(2) + f(3)) + (f(4) + f(5)))

    return pl.pallas_call(body, name=name, out_shape=SDS((nl, k, n), F32), grid=(nl, k // tk),
                          in_specs=[pl.BlockSpec((None, N_DEV, tk, n), lambda l, i: (l, 0, i, 0))],
                          out_specs=pl.BlockSpec((None, tk, n), lambda l, i: (l, i, 0)),
                          compiler_params=_cp(2))(r)


def _adamw(name, w, g, m, v):
    rows, width = w.shape
    tm = _row_tile_2d(rows, width, 4, 1024 * 1024)
    c1 = 1.0 - ADAM_B1 ** ADAM_STEP
    c2 = 1.0 - ADAM_B2 ** ADAM_STEP

    def body(w_ref, g_ref, m_ref, v_ref, d_ref, nm_ref, nv_ref):
        gv = g_ref[...]
        nm = ADAM_B1 * m_ref[...] + (1.0 - ADAM_B1) * gv
        nv = ADAM_B2 * v_ref[...] + (1.0 - ADAM_B2) * (gv * gv)
        nm_ref[...] = nm
        nv_ref[...] = nv
        d_ref[...] = -ADAM_LR * ((nm / c1) / (jnp.sqrt(nv / c2) + ADAM_EPS) + ADAM_WD * w_ref[...])

    spec = _rows(tm, width)
    sds = SDS((rows, width), F32)
    return pl.pallas_call(body, name=name, out_shape=(sds, sds, sds), grid=(rows // tm,),
                          in_specs=[spec] * 4, out_specs=(spec,) * 3, compiler_params=_cp(1))(w, g, m, v)


def _mod_fwd(c16, w_mod, b_sh):
    n = w_mod.shape[2]
    tn = _tile(n, 512)

    def body(c_ref, w_ref, b_ref, o_ref):
        cv = c_ref[...]
        ca = (cv * _sigmoid(cv)).astype(BF)
        o_ref[...] = jnp.dot(ca, w_ref[...].astype(BF), preferred_element_type=F32) + b_ref[0:1, :]

    return pl.pallas_call(body, name="mod_fwd", out_shape=SDS((DEPTH, 16, n), F32), grid=(DEPTH, n // tn),
                          in_specs=[_const((16, D)), pl.BlockSpec((None, D, tn), lambda l, j: (l, 0, j)),
                                    pl.BlockSpec((None, 8, tn), lambda l, j: (l, 0, j))],
                          out_specs=pl.BlockSpec((None, 16, tn), lambda l, j: (l, 0, j)),
                          compiler_params=_cp(2))(c16, w_mod, b_sh)


def _mod_bwd(c16, dmod):
    n = dmod.shape[2]
    tn = _tile(n, 512)

    def body(c_ref, d_ref, o_ref):
        cv = c_ref[...]
        ca = (cv * _sigmoid(cv)).astype(BF)
        o_ref[...] = lax.dot_general(ca, d_ref[...].astype(BF), TN, preferred_element_type=F32)

    return pl.pallas_call(body, name="mod_bwd", out_shape=SDS((DEPTH, D, n), F32), grid=(DEPTH, n // tn),
                          in_specs=[_const((16, D)), pl.BlockSpec((None, 16, tn), lambda l, j: (l, 0, j))],
                          out_specs=pl.BlockSpec((None, D, tn), lambda l, j: (l, 0, j)),
                          compiler_params=_cp(2))(c16, dmod)


def _place():
    x, y, c = lax.axis_index("x"), lax.axis_index("y"), lax.axis_index("c")
    other_chips = [(1 - x, y), (x, 1 - y), (1 - x, 1 - y)]
    return x, y, c, other_chips


def _allgather_small(name, v, with_sum):
    m, n = v.shape

    def body(x_ref, out_ref, *rest):
        if with_sum:
            sum_ref, send_sems, recv_sems, local_sem = rest
        else:
            send_sems, recv_sems, local_sem = rest
        x, y, c, chips = _place()
        me, sibling = (x, y, c), (x, y, 1 - c)

        def rows(px, py, pc):
            return out_ref.at[pl.ds((4 * px + 2 * py + pc) * m, m), :]

        def copy(k, block, to, src=None):
            return pltpu.make_async_remote_copy(
                src_ref=rows(*block) if src is None else src, dst_ref=rows(*block),
                send_sem=send_sems.at[k], recv_sem=recv_sems.at[k], device_id=to, device_id_type=MESH)

        mine = pltpu.make_async_copy(x_ref, rows(*me), local_sem)
        mine.start()
        first = [copy(0, me, sibling, src=x_ref)]
        first += [copy(1 + j, me, (*chip, c), src=x_ref) for j, chip in enumerate(chips)]
        for cp in first:
            cp.start()
        passed = [copy(4 + j, (*chip, c), sibling) for j, chip in enumerate(chips)]
        for j, chip in enumerate(chips):
            copy(1 + j, (*chip, c), me).wait_recv()
            passed[j].start()
        copy(0, sibling, me).wait_recv()
        for j, chip in enumerate(chips):
            copy(4 + j, (*chip, 1 - c), me).wait_recv()
        for cp in first + passed:
            cp.wait_send()
        mine.wait()
        if with_sum:
            acc = out_ref[0:m, :]
            for k in range(1, N_DEV):
                acc = acc + out_ref[k * m:(k + 1) * m, :]
            sum_ref[...] = acc

    vm = pl.BlockSpec(memory_space=pltpu.VMEM)
    out_shape = [SDS((N_DEV * m, n), F32)] + ([SDS((m, n), F32)] if with_sum else [])
    res = pl.pallas_call(
        body, name=name, out_shape=tuple(out_shape), in_specs=[vm], out_specs=tuple([vm] * len(out_shape)),
        scratch_shapes=[pltpu.SemaphoreType.DMA((7,)), pltpu.SemaphoreType.DMA((7,)), pltpu.SemaphoreType.DMA],
        compiler_params=pltpu.CompilerParams(vmem_limit_bytes=VMEM_LIMIT_MB * 1024 * 1024))(v)
    return res if with_sum else res[0]


HBM = pl.BlockSpec(memory_space=pltpu.HBM)


def _sem_scratch(n_remote, n_local):
    return [pltpu.SemaphoreType.DMA((n_remote,)), pltpu.SemaphoreType.DMA((n_remote,)),
            pltpu.SemaphoreType.DMA((n_local,))]


DMA_PIECE_BYTES = 1 << 20


def _pieces(src, dst):
    *lead, rows, n = src.shape
    nsplit = max(1, min(rows // 16, (rows * n * jnp.dtype(src.dtype).itemsize) // DMA_PIECE_BYTES))
    while rows % nsplit or (rows // nsplit) % 16:
        nsplit -= 1
    size = rows // nsplit
    out = []
    for idx in itertools.product(*[range(d) for d in lead]):
        for i in range(nsplit):
            sl = tuple(idx) + (pl.ds(i * size, size),)
            out.append((src.at[sl], dst.at[sl]))
    return out


def _local_copies(src, dst, sem):
    return ([pltpu.make_async_copy(s_, d_, sem) for s_, d_ in _pieces(src, dst)],
            pltpu.make_async_copy(src, dst, sem))


def _remote_copies(src, dst, send_sem, recv_sem, to):
    mk = lambda s_, d_: pltpu.make_async_remote_copy(src_ref=s_, dst_ref=d_, send_sem=send_sem, recv_sem=recv_sem,
                                                     device_id=to, device_id_type=MESH)
    return [mk(s_, d_) for s_, d_ in _pieces(src, dst)], mk(src, dst)


def _gather_weights(name, shards, sequencer_id=None):
    nq = len(shards)

    def exchange(ins, outs, send_sems, recv_sems, local_sems, own_barrier):
        x, y, c, chips = _place()
        if own_barrier:
            barrier = pltpu.get_barrier_semaphore()
            for px, py in chips:
                pl.semaphore_signal(barrier, inc=1, device_id=(px, py, c), device_id_type=MESH)
            pl.semaphore_wait(barrier, len(chips))
        me_chip = 2 * x + y
        started, local_all, send_all, recv_all = [], [], [], []
        for q in range(nq):
            cps, whole = _local_copies(ins[q], outs[q].at[:, me_chip], local_sems.at[q])
            started += cps
            local_all.append(whole)
            for r, (px, py) in enumerate(chips):
                k = 3 * q + r
                cps, whole = _remote_copies(ins[q], outs[q].at[:, me_chip], send_sems.at[k], recv_sems.at[k], (px, py, c))
                started += cps
                send_all.append(whole)
                recv_all.append(_remote_copies(ins[q], outs[q].at[:, 2 * px + py], send_sems.at[k], recv_sems.at[k],
                                               (px, py, c))[1])
        for cp in started:
            cp.start()
        for cp in recv_all:
            cp.wait_recv()
        for cp in send_all:
            cp.wait_send()
        for cp in local_all:
            cp.wait()

    out_shape = tuple(SDS((s.shape[0], N_CHIPS) + s.shape[1:], s.dtype) for s in shards)
    return _launch_exchange(exchange, name, shards, out_shape, _sem_scratch(3 * nq, nq), sequencer_id)


def _launch_exchange(exchange, name, arrays, out_shape, sems, sequencer_id):
    n = len(arrays)
    if sequencer_id is None:
        def body(*refs):
            exchange(refs[:n], refs[n:2 * n], *refs[2 * n:], own_barrier=False)

        return pl.pallas_call(body, name=name, out_shape=out_shape, in_specs=[HBM] * n,
                              out_specs=tuple([HBM] * n), scratch_shapes=sems)(*arrays)

    in_refs = [jax.new_ref(a, memory_space=pltpu.MemorySpace.HBM) for a in arrays]
    out_refs = [jax.empty_ref(o, memory_space=pltpu.MemorySpace.HBM) for o in out_shape]

    @pl.kernel(mesh=plsc.ScalarSubcoreMesh(axis_name="sequencer", num_cores=1), name=name, scratch_types=tuple(sems),
               compiler_params=pltpu.CompilerParams(collective_id=sequencer_id))
    def launch(send_sems, recv_sems, local_sems):
        exchange(in_refs, out_refs, send_sems, recv_sems, local_sems, own_barrier=True)

    launch()
    return [r[...] for r in out_refs]


def _grad_exchange(name, gs, sequencer_id=None):
    nq = len(gs)

    def exchange(ins, outs, send_sems, recv_sems, local_sems, own_barrier):
        x, y, c, chips = _place()
        peers = [(2 * r + e, (px, py, c if e == 0 else 1 - c)) for r, (px, py) in enumerate(chips) for e in (0, 1)]
        peers.append((6, (x, y, 1 - c)))
        if own_barrier:
            barrier = pltpu.get_barrier_semaphore()
            for _, peer in peers:
                pl.semaphore_signal(barrier, inc=1, device_id=peer, device_id_type=MESH)
            pl.semaphore_wait(barrier, len(peers))
        started, local_all, remote_all = [], [], []
        for q in range(nq):
            cps, whole = _local_copies(ins[q].at[:, 2 * x + y, c], outs[q].at[:, 7], local_sems.at[q])
            started += cps
            local_all.append(whole)
            for slot, (px, py, pc) in peers:
                k = 7 * q + slot
                cps, whole = _remote_copies(ins[q].at[:, 2 * px + py, pc], outs[q].at[:, slot], send_sems.at[k],
                                            recv_sems.at[k], (px, py, pc))
                started += cps
                remote_all.append(whole)
        for cp in started:
            cp.start()
        for cp in remote_all:
            cp.wait_recv()
        for cp in remote_all:
            cp.wait_send()
        for cp in local_all:
            cp.wait()

    out_shape = tuple(SDS((g.shape[0], N_DEV, g.shape[3], g.shape[4]), g.dtype) for g in gs)
    return _launch_exchange(exchange, name, gs, out_shape, _sem_scratch(7 * nq, nq), sequencer_id)


def _pair_share(rs):
    nq = len(rs)

    def body(*refs):
        ins, outs = refs[:nq], refs[nq:2 * nq]
        send_sems, recv_sems, local_sems = refs[2 * nq:]
        x, y, c, _ = _place()
        started, local_all, send_all, recv_all = [], [], [], []
        for q in range(nq):
            cps, whole = _local_copies(ins[q], outs[q].at[:, c], local_sems.at[q])
            started += cps
            local_all.append(whole)
            cps, whole = _remote_copies(ins[q], outs[q].at[:, c], send_sems.at[q], recv_sems.at[q], (x, y, 1 - c))
            started += cps
            send_all.append(whole)
            recv_all.append(_remote_copies(ins[q], outs[q].at[:, 1 - c], send_sems.at[q], recv_sems.at[q],
                                           (x, y, 1 - c))[1])
        for cp in started:
            cp.start()
        for cp in recv_all:
            cp.wait_recv()
        for cp in send_all:
            cp.wait_send()
        for cp in local_all:
            cp.wait()

    out_shape = tuple(SDS((r.shape[0], 2) + r.shape[1:], r.dtype) for r in rs)
    return pl.pallas_call(body, name="grad_pair_share", out_shape=out_shape, in_specs=[HBM] * nq,
                          out_specs=tuple([HBM] * nq), scratch_shapes=_sem_scratch(nq, nq))(*rs)


def _pack(arrs, rows_multiple=8):
    flat = jnp.concatenate([a.astype(F32).reshape(-1) for a in arrs])
    pad = (-flat.shape[0]) % (128 * rows_multiple)
    return jnp.pad(flat, (0, pad)).reshape(-1, 128)


def _unpack(slab, shapes):
    flat = slab.reshape(-1)
    out, off = [], 0
    for shp in shapes:
        n = 1
        for d in shp:
            n *= d
        out.append(flat[off:off + n].reshape(shp))
        off += n
    return out


def _shard_last(a, chip, n):
    return lax.dynamic_slice_in_dim(a, chip * n, n, axis=a.ndim - 1)


def kernel(x, c, w_mod, b_mod, norm_g, sc_w_in, sc_conv, sc_w_out, pool_w, pool_b, pool_scale, cf_w_pw1, cf_b_pw1, cf_w_dw, cf_b_dw, cf_ln_g, cf_ln_b, cf_w_pw2, cf_b_pw2, ffn_w_up, ffn_conv, ffn_b_conv, ffn_w_down, loss_target, m_w_mod, m_b_mod, m_norm_g, m_sc_w_in, m_sc_conv, m_sc_w_out, m_pool_w, m_pool_b, m_pool_scale, m_cf_w_pw1, m_cf_b_pw1, m_cf_w_dw, m_cf_b_dw, m_cf_ln_g, m_cf_ln_b, m_cf_w_pw2, m_cf_b_pw2, m_ffn_w_up, m_ffn_conv, m_ffn_b_conv, m_ffn_w_down, v_w_mod, v_b_mod, v_norm_g, v_sc_w_in, v_sc_conv, v_sc_w_out, v_pool_w, v_pool_b, v_pool_scale, v_cf_w_pw1, v_cf_b_pw1, v_cf_w_dw, v_cf_b_dw, v_cf_ln_g, v_cf_ln_b, v_cf_w_pw2, v_cf_b_pw2, v_ffn_w_up, v_ffn_conv, v_ffn_b_conv, v_ffn_w_down):
    ax, ay, ac = lax.axis_index("x"), lax.axis_index("y"), lax.axis_index("c")
    chip = 2 * ax + ay
    dev = 4 * ax + 2 * ay + ac
    xs = x[0]
    target = loss_target[0]

    small_sharded = [norm_g, sc_conv, cf_b_pw1, cf_w_dw, cf_b_dw, cf_ln_g, cf_ln_b, cf_b_pw2, ffn_conv]
    slab = _pack([c] + small_sharded)
    gathered = _allgather_small("gather_small_params", slab, False).reshape(N_DEV, -1, 128)
    parts = [_unpack(gathered[d], [c.shape] + [a.shape for a in small_sharded]) for d in range(N_DEV)]
    c_all = jnp.concatenate([p[0] for p in parts], axis=0)
    full = [jnp.concatenate([parts[2 * j][1 + i] for j in range(N_CHIPS)], axis=-1)
            for i in range(len(small_sharded))]
    norm_g_f, sc_conv_f, cf_b_pw1_f, cf_w_dw_f, cf_b_dw_f, cf_ln_g_f, cf_ln_b_f, cf_b_pw2_f, ffn_conv_f = full
    c16 = jnp.pad(c_all, ((0, 8), (0, 0)))

    n_mod = w_mod.shape[2]
    b_sh = jnp.broadcast_to(_shard_last(b_mod, chip, n_mod)[:, None, :], (DEPTH, 8, n_mod))
    mod_part = _mod_fwd(c16, w_mod, b_sh)
    mod_g = _allgather_small("gather_mod", mod_part.reshape(DEPTH * 16, n_mod), False)
    mod_g = mod_g.reshape(N_DEV, DEPTH, 16, n_mod)
    mod_mine = jnp.concatenate(
        [lax.dynamic_index_in_dim(mod_g[2 * j], dev, axis=1, keepdims=False) for j in range(N_CHIPS)], axis=-1)
    mod = mod_mine.reshape(DEPTH, 6, D)

    bf = lambda a: a.astype(BF)
    rows = lambda w: w.reshape(w.shape[0], w.shape[1] * w.shape[2], w.shape[3])
    after = lambda a, done: a + (done[(0,) * done.ndim] * 0).astype(a.dtype)
    (w_in_0,) = _gather_weights("gather_weights", [bf(sc_w_in[0:1])])
    w_out_0, w_up_0, w_down_0 = _gather_weights(
        "gather_weights_first", [after(bf(sc_w_out[0:1]), w_in_0), bf(ffn_w_up[0:1]), bf(ffn_w_down[0:1])],
        GATHER_FIRST_ID)
    w_in_r, w_out_r, pool_f, pw1_f, pw2_f, w_up_r, w_down_r = _gather_weights(
        "gather_weights_rest", [after(bf(sc_w_in[1:]), w_down_0), bf(sc_w_out[1:]), bf(pool_w[0]), bf(cf_w_pw1),
                                bf(cf_w_pw2), bf(ffn_w_up[1:]), bf(ffn_w_down[1:])], GATHER_REST_ID)
    pool_f = pool_f.reshape(4, POOL_GROUP, POOL_GROUP)
    pw1_f, pw2_f = (pw1_f, 0), (rows(pw2_f), 0)
    w_in_f = {0: (w_in_0, 0), 1: (w_in_r, 0)}
    w_out_f = {0: (rows(w_out_0), 0), 1: (rows(w_out_r), 0)}
    w_up_f = {i: (w_up_0, 0) if i == 0 else (w_up_r, i - 1) for i in range(DEPTH)}
    w_down_f = {i: (rows(w_down_0), 0) if i == 0 else (rows(w_down_r), i - 1) for i in range(DEPTH)}

    zero_d = jnp.zeros((D,), F32)
    mods = [[mod[i, k] for k in range(6)] for i in range(DEPTH)]
    saved = []
    xcur = xs
    h_next = None
    for i in range(DEPTH):
        kind, j = i % 3, i // 3
        sh1, sc1, g1, sh2, sc2, g2 = mods[i]
        st = {"x0": xcur}
        m_bias = zero_d
        if kind != 1:
            h = h_next if h_next is not None else _pro_fwd(f"pro1_fwd_{i}", xcur, _vecs([norm_g_f[i, 0], sc1, sh1], D))
        if kind == 0:
            p = _mm_nn(f"sc_in_{i}", h, *w_in_f[j],True, BF)
            z = _sc_act_fwd(f"sc_act_fwd_{i}", p, _vecs(list(sc_conv_f[j]), D))
            m = _mm_nn(f"sc_out_{i}", z, *w_out_f[j],False, F32)
            st.update(h=h, p=p)
        elif kind == 1:
            pool_vec = _vecs([norm_g_f[i, 0], sc1, sh1, pool_b[j], pool_scale[j]], D)
            pooled, m = _pool_fwd(f"pool_fwd_{i}", xcur, pool_f, pool_vec)
            st.update(pooled=pooled)
        else:
            p = _mm_nn(f"cf_pw1_{i}", h, *pw1_f,True, BF)
            taps = jnp.pad(cf_w_dw_f[j], ((0, 1), (0, 0)))
            cf_vec = _vecs([cf_b_pw1_f[j, :D], cf_b_pw1_f[j, D:], cf_b_dw_f[j], cf_ln_g_f[j], cf_ln_b_f[j]], D)
            wact, cv = _cf_act_fwd(f"cf_act_fwd_{i}", p, taps, cf_vec)
            m = _mm_nn(f"cf_pw2_{i}", wact, *pw2_f,False, F32)
            m_bias = cf_b_pw2_f[j]
            st.update(h=h, p=p, wact=wact, cv=cv, taps=taps, cf_vec=cf_vec)
        x1, h2 = _epi_pro_fwd(f"epi1_fwd_{i}", xcur, m,
                              _vecs([g1, norm_g_f[i, 1], m_bias, norm_g_f[i, 2], sc2, sh2], D), True)
        st.update(m=m, x1=x1, m_bias=m_bias)
        up = _mm_nn(f"ffn_up_{i}", h2, *w_up_f[i],True, BF)
        ffn_vec = _vecs(list(ffn_conv_f[i]) + [ffn_b_conv[i]], F)
        a = _ffn_act_fwd(f"ffn_act_fwd_{i}", up, ffn_vec)
        f = _mm_nn(f"ffn_down_{i}", a, *w_down_f[i],False, F32, tm=512)
        nxt = i + 1
        fuse_next = nxt < DEPTH and nxt % 3 != 1
        rows = [g2, norm_g_f[i, 3], zero_d]
        if fuse_next:
            rows += [norm_g_f[nxt, 0], mods[nxt][1], mods[nxt][0]]
        xcur, h_next = _epi_pro_fwd(f"epi2_fwd_{i}", x1, f, _vecs(rows, D), fuse_next)
        st.update(h2=h2, up=up, a=a, f=f, ffn_vec=ffn_vec)
        saved.append(st)

    dy, loss_cols = _loss_fwd_bwd(xcur, target)
    loss = lax.psum(jnp.sum(loss_cols[0]), ("x", "y", "c"))

    dmod = [None] * DEPTH
    d_norm_g = [None] * DEPTH
    d_sc_conv = [None, None]
    d_ffn_conv, d_ffn_b = [None] * DEPTH, [None] * DEPTH
    g_w_in, g_w_out, g_w_up, g_w_down = [None, None], [None, None], [None] * DEPTH, [None] * DEPTH
    def halves(gl):
        g = jnp.stack(gl) if isinstance(gl, list) else gl
        if g.ndim == 3:
            g = g.reshape(g.shape[0], N_CHIPS, g.shape[1] // N_CHIPS, g.shape[2])
        nl, _, k, n = g.shape
        return g.reshape(nl, N_CHIPS, 2, k // 2, n)

    dxo = dy
    last = DEPTH - 1
    pend = _epi_bwd(f"epi2_bwd_{last}", dy, saved[last]["f"], _vecs([mods[last][5], norm_g_f[last, 3], zero_d], D))
    for i in reversed(range(DEPTH)):
        kind, j = i % 3, i // 3
        st = saved[i]
        sh1, sc1, g1, sh2, sc2, g2 = mods[i]
        df, e2 = pend
        da = _mm_nt(f"ffn_down_dx_{i}", df, *w_down_f[i],False, BF, tn=F // 2)
        dup, fsum = _ffn_act_bwd(f"ffn_act_bwd_{i}", st["up"], da, st["ffn_vec"])
        g_w_down[i] = _mm_tn(f"ffn_down_dw_{i}", st["a"], df, False, tk=F // 2)
        dh2 = _mm_nt(f"ffn_up_dx_{i}", dup, *w_up_f[i],True, F32)
        g_w_up[i] = _mm_tn(f"ffn_up_dw_{i}", st["h2"], dup, True)
        dx1, dm, s21 = _pro_epi_bwd(f"pro2_epi1_bwd_{i}", dh2, st["x1"], dxo, st["m"],
                                    _vecs([norm_g_f[i, 2], sc2, g1, norm_g_f[i, 1], st["m_bias"]], D))
        p2, e1 = s21[0:3], s21[3:6]
        if kind == 0:
            dz = _mm_nt(f"sc_out_dx_{i}", dm, *w_out_f[j],False, BF)
            z, dp, ssum = _sc_act_bwd(f"sc_act_bwd_{i}", st["p"], dz, _vecs(list(sc_conv_f[j]), D))
            g_w_out[j] = _mm_tn(f"sc_out_dw_{i}", z, dm, False)
            dh = _mm_nt(f"sc_in_dx_{i}", dp, *w_in_f[j],True, F32)
            g_w_in[j] = _mm_tn(f"sc_in_dw_{i}", st["h"], dp, True)
            d_sc_conv[j] = ssum[0:3]
        elif kind == 1:
            dh, g_pool, psum = _pool_bwd(f"pool_bwd_{i}", st["pooled"], dm, pool_f, _vecs([pool_b[j], pool_scale[j]], D))
        else:
            dwact = _mm_nt(f"cf_pw2_dx_{i}", dm, *pw2_f,False, BF)
            g_pw2 = _mm_tn(f"cf_pw2_dw_{i}", st["wact"], dm, False)
            dp, tsum, csum = _cf_act_bwd(f"cf_act_bwd_{i}", st["p"], st["cv"], dwact, st["taps"], st["cf_vec"])
            dh = _mm_nt(f"cf_pw1_dx_{i}", dp, *pw1_f,True, F32)
            g_pw1 = _mm_tn(f"cf_pw1_dw_{i}", st["h"], dp, True)
            d_cf = dict(b_pw1=jnp.concatenate([csum[3], csum[4]])[None], w_dw=tsum[None, :CF_TAPS], b_dw=csum[0:1],
                        ln_g=csum[1:2], ln_b=csum[2:3], b_pw2=e1[2:3])
        if i > 0:
            prev = i - 1
            dxo, df_prev, s12 = _pro_epi_bwd(f"pro1_epi2_bwd_{i}", dh, st["x0"], dx1, saved[prev]["f"],
                                             _vecs([norm_g_f[i, 0], sc1, mods[prev][5], norm_g_f[prev, 3], zero_d], D))
            p1, pend = s12[0:3], (df_prev, s12[3:6])
        else:
            dxo, p1 = _pro_bwd(f"pro1_bwd_{i}", dh, st["x0"], dx1, _vecs([norm_g_f[i, 0], sc1], D))
        dmod[i] = jnp.concatenate([p1[0], p1[1], e1[0], p2[0], p2[1], e2[0]])
        d_norm_g[i] = jnp.stack([p1[2], e1[1], p2[2], e2[1]])
        if kind == 1:
            d_pool_b, d_pool_scale = psum[0:1], psum[1:2]
        if i == 1:
            rest_pieces = _grad_exchange(
                "grad_exchange_rest",
                [halves([g_w_in[1]]), halves([g_w_out[1]]), halves(g_pool.astype(BF)), halves([g_pw1]), halves([g_pw2]),
                 halves(g_w_up[1:]), halves(g_w_down[1:])], GRAD_REST_ID)
        saved[i] = None
        st = None
        if i == 0:
            grad_x = dxo[None]
        d_ffn_conv[i], d_ffn_b[i] = fsum[1:4], fsum[0]

    small_shapes = [(DEPTH, 6 * D), (DEPTH, 4, D), (2, 3, D), (1, D), (1, D), (1, 2 * D), (1, CF_TAPS, D), (1, D),
                    (1, D), (1, D), (1, D), (DEPTH, 3, F), (DEPTH, F)]
    small = [jnp.stack(dmod), jnp.stack(d_norm_g), jnp.stack(d_sc_conv), d_pool_b, d_pool_scale, d_cf["b_pw1"],
             d_cf["w_dw"], d_cf["b_dw"], d_cf["ln_g"], d_cf["ln_b"], d_cf["b_pw2"], jnp.stack(d_ffn_conv),
             jnp.stack(d_ffn_b)]
    gsl, ssl = _allgather_small("reduce_small_grads", _pack(small), True)
    tot = _unpack(ssl, small_shapes)
    (gb_mod, gt_norm_g, gt_sc_conv, g_pool_b, g_pool_scale, gt_b_pw1, gt_w_dw, gt_b_dw, gt_ln_g, gt_ln_b, gt_b_pw2,
     gt_ffn_conv, g_ffn_b) = tot
    dmod_all = jnp.stack([_unpack(gsl.reshape(N_DEV, -1, 128)[d], small_shapes[:1])[0] for d in range(N_DEV)], axis=1)
    dmod_sh = jnp.pad(_shard_last(dmod_all, chip, n_mod), ((0, 0), (0, 8), (0, 0)))
    g_w_mod = _mod_bwd(c16, dmod_sh)
    g_norm_g = _shard_last(gt_norm_g, chip, D // 4)
    g_sc_conv = _shard_last(gt_sc_conv, chip, D // 4)
    g_b_pw1 = _shard_last(gt_b_pw1, chip, 2 * D // 4)
    g_w_dw = _shard_last(gt_w_dw, chip, D // 4)
    g_b_dw = _shard_last(gt_b_dw, chip, D // 4)
    g_ln_g = _shard_last(gt_ln_g, chip, D // 4)
    g_ln_b = _shard_last(gt_ln_b, chip, D // 4)
    g_b_pw2 = _shard_last(gt_b_pw2, chip, D // 4)
    g_ffn_conv = _shard_last(gt_ffn_conv, chip, F // 4)

    first_partial = [halves([g_w_in[0]]), halves([g_w_out[0]]), halves([g_w_up[0]]), halves([g_w_down[0]])]
    pieces = list(_grad_exchange("grad_exchange", first_partial)) + list(rest_pieces)
    reduced = [_add_slots(f"grad_sum_{q}", pieces[q]) for q in range(len(pieces))]
    f_in, f_out, f_up, f_down, r_in, r_out, r_pool, r_pw1, r_pw2, r_up, r_down = _pair_share(reduced)
    cat = lambda a, b, like: jnp.concatenate([a, b], axis=0).reshape(like.shape)
    g_sc_w_in, g_sc_w_out = cat(f_in, r_in, sc_w_in), cat(f_out, r_out, sc_w_out)
    g_ffn_w_up, g_ffn_w_down = cat(f_up, r_up, ffn_w_up), cat(f_down, r_down, ffn_w_down)
    g_pool_w, g_cf_w_pw1, g_cf_w_pw2 = (r_pool.reshape(pool_w.shape), r_pw1.reshape(cf_w_pw1.shape),
                                        r_pw2.reshape(cf_w_pw2.shape))

    def adam_big(name, w, g, m, v):
        shp = w.shape
        two = lambda t: t.reshape(-1, shp[-1])
        return [o.reshape(shp) for o in _adamw(name, two(w), two(g), two(m), two(v))]

    grads = dict(w_mod=g_w_mod, b_mod=gb_mod, norm_g=g_norm_g, sc_w_in=g_sc_w_in, sc_conv=g_sc_conv,
                 sc_w_out=g_sc_w_out, pool_w=g_pool_w, pool_b=g_pool_b, pool_scale=g_pool_scale,
                 cf_w_pw1=g_cf_w_pw1, cf_b_pw1=g_b_pw1, cf_w_dw=g_w_dw, cf_b_dw=g_b_dw, cf_ln_g=g_ln_g,
                 cf_ln_b=g_ln_b, cf_w_pw2=g_cf_w_pw2, cf_b_pw2=g_b_pw2, ffn_w_up=g_ffn_w_up, ffn_conv=g_ffn_conv,
                 ffn_b_conv=g_ffn_b, ffn_w_down=g_ffn_w_down)
    weights = dict(w_mod=w_mod, b_mod=b_mod, norm_g=norm_g, sc_w_in=sc_w_in, sc_conv=sc_conv, sc_w_out=sc_w_out,
                   pool_w=pool_w, pool_b=pool_b, pool_scale=pool_scale, cf_w_pw1=cf_w_pw1, cf_b_pw1=cf_b_pw1,
                   cf_w_dw=cf_w_dw, cf_b_dw=cf_b_dw, cf_ln_g=cf_ln_g, cf_ln_b=cf_ln_b, cf_w_pw2=cf_w_pw2,
                   cf_b_pw2=cf_b_pw2, ffn_w_up=ffn_w_up, ffn_conv=ffn_conv, ffn_b_conv=ffn_b_conv,
                   ffn_w_down=ffn_w_down)
    m_in = dict(w_mod=m_w_mod, b_mod=m_b_mod, norm_g=m_norm_g, sc_w_in=m_sc_w_in, sc_conv=m_sc_conv,
                sc_w_out=m_sc_w_out, pool_w=m_pool_w, pool_b=m_pool_b, pool_scale=m_pool_scale,
                cf_w_pw1=m_cf_w_pw1, cf_b_pw1=m_cf_b_pw1, cf_w_dw=m_cf_w_dw, cf_b_dw=m_cf_b_dw, cf_ln_g=m_cf_ln_g,
                cf_ln_b=m_cf_ln_b, cf_w_pw2=m_cf_w_pw2, cf_b_pw2=m_cf_b_pw2, ffn_w_up=m_ffn_w_up,
                ffn_conv=m_ffn_conv, ffn_b_conv=m_ffn_b_conv, ffn_w_down=m_ffn_w_down)
    v_in = dict(w_mod=v_w_mod, b_mod=v_b_mod, norm_g=v_norm_g, sc_w_in=v_sc_w_in, sc_conv=v_sc_conv,
                sc_w_out=v_sc_w_out, pool_w=v_pool_w, pool_b=v_pool_b, pool_scale=v_pool_scale,
                cf_w_pw1=v_cf_w_pw1, cf_b_pw1=v_cf_b_pw1, cf_w_dw=v_cf_w_dw, cf_b_dw=v_cf_b_dw, cf_ln_g=v_cf_ln_g,
                cf_ln_b=v_cf_ln_b, cf_w_pw2=v_cf_w_pw2, cf_b_pw2=v_cf_b_pw2, ffn_w_up=v_ffn_w_up,
                ffn_conv=v_ffn_conv, ffn_b_conv=v_ffn_b_conv, ffn_w_down=v_ffn_w_down)
    names = list(weights)
    big_names = ["w_mod", "sc_w_in", "sc_w_out", "pool_w", "cf_w_pw1", "cf_w_pw2", "ffn_w_up", "ffn_w_down"]
    small_names = [n for n in names if n not in big_names]
    delta, new_m, new_v = {}, {}, {}
    for n in big_names:
        delta[n], new_m[n], new_v[n] = adam_big(f"adamw_{n}", weights[n], grads[n], m_in[n], v_in[n])
    grads = {n: grads[n].reshape(weights[n].shape) for n in names}
    sm_shapes = [weights[n].shape for n in small_names]
    sd, sm, sv = _adamw("adamw_small", _pack([weights[n] for n in small_names]), _pack([grads[n] for n in small_names]),
                        _pack([m_in[n] for n in small_names]), _pack([v_in[n] for n in small_names]))
    for n, d_, m_, v_ in zip(small_names, _unpack(sd, sm_shapes), _unpack(sm, sm_shapes), _unpack(sv, sm_shapes)):
        delta[n], new_m[n], new_v[n] = d_, m_, v_

    return (loss, grad_x, *[grads[n] for n in names], *[delta[n] for n in names], *[new_m[n] for n in names],
            *[new_v[n] for n in names])
```

```python
import itertools

import jax
import jax.numpy as jnp
from jax import lax
from jax.experimental import pallas as pl
from jax.experimental.pallas import tpu as pltpu
from jax.experimental.pallas import tpu_sc as plsc

D = 1024
F = 2816
DEPTH = 4
POOL_WINDOWS = (2, 4, 8, 16)
POOL_GROUP = 256
CF_TAPS = 31
RMS_EPS = 1e-6
LN_EPS = 1e-5
ADAM_LR = 0.001
ADAM_B1 = 0.9
ADAM_B2 = 0.999
ADAM_EPS = 1e-08
ADAM_WD = 0.01
ADAM_STEP = 10

BF = jnp.bfloat16
F32 = jnp.float32
MESH = pl.DeviceIdType.MESH
SDS = jax.ShapeDtypeStruct
N_CHIPS = 4
N_DEV = 8
GATHER_FIRST_ID, GATHER_REST_ID, GRAD_REST_ID, GRAD_FFN0_ID, SHARE_REST_ID = 1, 2, 3, 4, 5
VMEM_LIMIT_MB = 56
ROW_TILE_D = 256
ROW_TILE_F = 256
HALO = 8
HALO_BF = 16
CF_HALO = 32
POOL_HALO = 16


def _cp(n_axes):
    return pltpu.CompilerParams(dimension_semantics=("arbitrary",) * n_axes,
                                vmem_limit_bytes=VMEM_LIMIT_MB * 1024 * 1024)


def _tile(n, pref):
    t = min(n, pref)
    assert n % t == 0, (n, pref)
    return t


def _const(shape):
    nd = len(shape)
    return pl.BlockSpec(shape, lambda *_: (0,) * nd)


def _vecs(rows, width):
    v = jnp.stack([r.reshape(width).astype(F32) for r in rows])
    pad = (-v.shape[0]) % 8
    return jnp.pad(v, ((0, pad), (0, 0)))


def _sigmoid(v):
    return 0.5 * jnp.tanh(0.5 * v) + 0.5


def _down(prev8, g, k):
    n = g.shape[0]
    return pltpu.roll(jnp.concatenate([prev8, g], axis=0), k, 0)[8:8 + n]


def _up(g, next8, k):
    n = g.shape[0]
    return pltpu.roll(jnp.concatenate([g, next8], axis=0), n + 8 - k, 0)[0:n]


def _fold8(v):
    acc = v[0:8]
    for r in range(8, v.shape[0], 8):
        acc = acc + v[r:r + 8]
    return acc


def _mm(name, a, b, out_sds, grid, a_spec, b_spec, o_spec, acc_shape, dn):
    nk = grid[2]

    def body(a_ref, b_ref, o_ref, *acc):
        part = lax.dot_general(a_ref[...], b_ref[...], dn, preferred_element_type=F32)
        if nk == 1:
            o_ref[...] = part.astype(o_ref.dtype)
        else:
            acc_ref = acc[0]
            k = pl.program_id(2)

            @pl.when(k == 0)
            def _():
                acc_ref[...] = part

            @pl.when(k > 0)
            def _():
                acc_ref[...] += part

            @pl.when(k == nk - 1)
            def _():
                o_ref[...] = acc_ref[...].astype(o_ref.dtype)

    scratch = [] if nk == 1 else [pltpu.VMEM(acc_shape, F32)]
    return pl.pallas_call(body, name=name, out_shape=out_sds, grid=grid, in_specs=[a_spec, b_spec],
                          out_specs=o_spec, scratch_shapes=scratch, compiler_params=_cp(3))(a, b)


NN = (((1,), (0,)), ((), ()))
NT = (((1,), (1,)), ((), ()))
TN = (((0,), (0,)), ((), ()))


def _mm_nn(name, a, w, layer, col_sharded, out_dtype, tm=1024, tn=None):
    m, k = a.shape
    tm = _tile(m, tm)
    if col_sharded:
        n4 = w.shape[3]
        tn = n4 if tn is None else tn
        tpc = n4 // tn
        n = N_CHIPS * n4
        b_spec = pl.BlockSpec((None, None, k, tn), lambda i, j, kk: (layer, j // tpc, 0, j % tpc))
    else:
        n = w.shape[2]
        tn = n if tn is None else tn
        b_spec = pl.BlockSpec((None, k, tn), lambda i, j, kk: (layer, 0, j))
    return _mm(name, a, w, SDS((m, n), out_dtype), (m // tm, n // tn, 1),
               pl.BlockSpec((tm, k), lambda i, j, kk: (i, 0)), b_spec,
               pl.BlockSpec((tm, tn), lambda i, j, kk: (i, j)), None, NN)


def _mm_nt(name, g, w, layer, col_sharded, out_dtype, tm=1024, tn=None):
    m, n = g.shape
    if col_sharded:
        return _mm_nt_staged(name, g, w, layer, out_dtype)
    tm = _tile(m, tm)
    kdim = w.shape[1]
    tn = kdim if tn is None else tn
    return _mm(name, g, w, SDS((m, kdim), out_dtype), (m // tm, kdim // tn, 1),
               pl.BlockSpec((tm, n), lambda i, j, kk: (i, 0)),
               pl.BlockSpec((None, tn, n), lambda i, j, kk: (layer, j, 0)),
               pl.BlockSpec((tm, tn), lambda i, j, kk: (i, j)), None, NT)


def _mm_nt_staged(name, g, w, layer, out_dtype, tm=512):
    m, n = g.shape
    kdim, n4 = w.shape[2], w.shape[3]
    tm = _tile(m, tm)

    def body(g_ref, w_hbm, o_ref, wbuf, sems):
        @pl.when(pl.program_id(0) == 0)
        def _():
            cps = [pltpu.make_async_copy(w_hbm.at[layer, j], wbuf.at[:, pl.ds(j * n4, n4)], sems.at[j])
                   for j in range(N_CHIPS)]
            for cp in cps:
                cp.start()
            for cp in cps:
                cp.wait()

        o_ref[...] = lax.dot_general(g_ref[...], wbuf[...], NT, preferred_element_type=F32).astype(o_ref.dtype)

    return pl.pallas_call(body, name=name, out_shape=SDS((m, kdim), out_dtype), grid=(m // tm,),
                          in_specs=[pl.BlockSpec((tm, n), lambda i: (i, 0)), pl.BlockSpec(memory_space=pltpu.HBM)],
                          out_specs=pl.BlockSpec((tm, kdim), lambda i: (i, 0)),
                          scratch_shapes=[pltpu.VMEM((kdim, n), BF), pltpu.SemaphoreType.DMA((N_CHIPS,))],
                          compiler_params=_cp(1))(g, w)


def _mm_tn(name, a, g, col_sharded, tk=None, tn=None, ts=2048):
    s, k = a.shape
    n = g.shape[1]
    ts = _tile(s, ts)
    tk = k if tk is None else tk
    if col_sharded:
        n4 = n // N_CHIPS
        tn = n4 if tn is None else tn
        tpc = n4 // tn
        out_sds = SDS((N_CHIPS, k, n4), BF)
        o_spec = pl.BlockSpec((None, tk, tn), lambda i, j, ss: (j // tpc, i, j % tpc))
    else:
        tn = n if tn is None else tn
        out_sds = SDS((k, n), BF)
        o_spec = pl.BlockSpec((tk, tn), lambda i, j, ss: (i, j))
    return _mm(name, a, g, out_sds, (k // tk, n // tn, s // ts),
               pl.BlockSpec((ts, tk), lambda i, j, ss: (ss, i)),
               pl.BlockSpec((ts, tn), lambda i, j, ss: (ss, j)), o_spec, (tk, tn), TN)


def _rows(tm, width, col=0):
    return pl.BlockSpec((tm, width), lambda i: (i, col))


def _pro_fwd(name, x, vec):
    s = x.shape[0]
    tm = _tile(s, ROW_TILE_D)

    def body(x_ref, v_ref, h_ref):
        xv = x_ref[...]
        r = lax.rsqrt(jnp.mean(xv * xv, axis=-1, keepdims=True) + RMS_EPS)
        a = v_ref[0:1, :] * (1.0 + v_ref[1:2, :])
        h_ref[...] = (xv * r * a + v_ref[2:3, :]).astype(h_ref.dtype)

    return pl.pallas_call(body, name=name, out_shape=SDS((s, D), BF), grid=(s // tm,),
                          in_specs=[_rows(tm, D), _const((8, D))], out_specs=_rows(tm, D),
                          compiler_params=_cp(1))(x, vec)


def _epi_pro_fwd(name, x, m, vec, with_next):
    s = x.shape[0]
    tm = _tile(s, ROW_TILE_D)

    def body(x_ref, m_ref, v_ref, o_ref, *h_ref):
        mv = m_ref[...] + v_ref[2:3, :]
        rm = lax.rsqrt(jnp.mean(mv * mv, axis=-1, keepdims=True) + RMS_EPS)
        xo = x_ref[...] + v_ref[0:1, :] * (mv * rm * v_ref[1:2, :])
        o_ref[...] = xo
        if with_next:
            r = lax.rsqrt(jnp.mean(xo * xo, axis=-1, keepdims=True) + RMS_EPS)
            a = v_ref[3:4, :] * (1.0 + v_ref[4:5, :])
            h_ref[0][...] = (xo * r * a + v_ref[5:6, :]).astype(BF)

    out_shape = (SDS((s, D), F32),) + ((SDS((s, D), BF),) if with_next else ())
    res = pl.pallas_call(body, name=name, out_shape=out_shape, grid=(s // tm,),
                         in_specs=[_rows(tm, D), _rows(tm, D), _const((8, D))],
                         out_specs=tuple([_rows(tm, D)] * len(out_shape)), compiler_params=_cp(1))(x, m, vec)
    return res if with_next else (res[0], None)


def _loss_fwd_bwd(y, t):
    s = y.shape[0]
    tm = _tile(s, ROW_TILE_D)

    def body(y_ref, t_ref, dy_ref, acc_ref):
        @pl.when(pl.program_id(0) == 0)
        def _():
            acc_ref[...] = jnp.zeros_like(acc_ref)

        e = y_ref[...] - t_ref[...]
        dy_ref[...] = e * (1.0 / D)
        acc_ref[0:1, :] += jnp.sum(e * e, axis=0, keepdims=True) * (0.5 / D)

    return pl.pallas_call(body, name="loss", out_shape=(SDS((s, D), F32), SDS((8, D), F32)), grid=(s // tm,),
                          in_specs=[_rows(tm, D), _rows(tm, D)], out_specs=(_rows(tm, D), _const((8, D))),
                          compiler_params=_cp(1))(y, t)


def _epi_bwd_rows(dv, mv, g, ngb, dm_ref, acc_ref, row):
    rm = lax.rsqrt(jnp.mean(mv * mv, axis=-1, keepdims=True) + RMS_EPS)
    mn = mv * rm
    dmn = dv * (g * ngb)
    dm = rm * (dmn - mn * jnp.mean(dmn * mn, axis=-1, keepdims=True))
    dm_ref[...] = dm.astype(dm_ref.dtype)
    t = jnp.sum(dv * mn, axis=0, keepdims=True)
    acc_ref[row:row + 1, :] += t * ngb
    acc_ref[row + 1:row + 2, :] += t * g
    acc_ref[row + 2:row + 3, :] += jnp.sum(dm, axis=0, keepdims=True)


def _epi_bwd(name, dxo, m, vec):
    s = dxo.shape[0]
    tm = _tile(s, ROW_TILE_D)

    def body(d_ref, m_ref, v_ref, dm_ref, acc_ref):
        @pl.when(pl.program_id(0) == 0)
        def _():
            acc_ref[...] = jnp.zeros_like(acc_ref)

        _epi_bwd_rows(d_ref[...], m_ref[...] + v_ref[2:3, :], v_ref[0:1, :], v_ref[1:2, :], dm_ref, acc_ref, 0)

    return pl.pallas_call(body, name=name, out_shape=(SDS((s, D), BF), SDS((8, D), F32)), grid=(s // tm,),
                          in_specs=[_rows(tm, D), _rows(tm, D), _const((8, D))],
                          out_specs=(_rows(tm, D), _const((8, D))), compiler_params=_cp(1))(dxo, m, vec)


def _pro_epi_bwd(name, dh, x, dxo, m, vec):
    s = x.shape[0]
    tm = _tile(s, ROW_TILE_D)

    def body(dh_ref, x_ref, d_ref, m_ref, v_ref, dx_ref, dm_ref, acc_ref):
        @pl.when(pl.program_id(0) == 0)
        def _():
            acc_ref[...] = jnp.zeros_like(acc_ref)

        dx = _pro_bwd_rows(dh_ref[...].astype(F32), x_ref[...], d_ref[...], v_ref[0:1, :], v_ref[1:2, :], acc_ref)
        dx_ref[...] = dx
        _epi_bwd_rows(dx, m_ref[...] + v_ref[4:5, :], v_ref[2:3, :], v_ref[3:4, :], dm_ref, acc_ref, 3)

    return pl.pallas_call(body, name=name, out_shape=(SDS((s, D), F32), SDS((s, D), BF), SDS((8, D), F32)),
                          grid=(s // tm,),
                          in_specs=[_rows(tm, D), _rows(tm, D), _rows(tm, D), _rows(tm, D), _const((8, D))],
                          out_specs=(_rows(tm, D), _rows(tm, D), _const((8, D))),
                          compiler_params=_cp(1))(dh, x, dxo, m, vec)


def _pro_bwd_rows(dhv, xv, dxo, ng, sc, acc_ref):
    sc1 = 1.0 + sc
    r = lax.rsqrt(jnp.mean(xv * xv, axis=-1, keepdims=True) + RMS_EPS)
    xn = xv * r
    dxn = dhv * (ng * sc1)
    t = jnp.sum(dhv * xn, axis=0, keepdims=True)
    acc_ref[0:1, :] += jnp.sum(dhv, axis=0, keepdims=True)
    acc_ref[1:2, :] += t * ng
    acc_ref[2:3, :] += t * sc1
    return dxo + r * (dxn - xn * jnp.mean(dxn * xn, axis=-1, keepdims=True))


def _pro_bwd(name, dh, x, dxo, vec):
    s = x.shape[0]
    tm = _tile(s, ROW_TILE_D)

    def body(dh_ref, x_ref, d_ref, v_ref, dx_ref, acc_ref):
        @pl.when(pl.program_id(0) == 0)
        def _():
            acc_ref[...] = jnp.zeros_like(acc_ref)

        dx_ref[...] = _pro_bwd_rows(dh_ref[...].astype(F32), x_ref[...], d_ref[...], v_ref[0:1, :], v_ref[1:2, :],
                                    acc_ref)

    return pl.pallas_call(body, name=name, out_shape=(SDS((s, D), F32), SDS((8, D), F32)), grid=(s // tm,),
                          in_specs=[_rows(tm, D), _rows(tm, D), _rows(tm, D), _const((8, D))],
                          out_specs=(_rows(tm, D), _const((8, D))), compiler_params=_cp(1))(dh, x, dxo, vec)


def _carry_up(buf, tm, first, halo=HALO):
    @pl.when(first)
    def _():
        buf[tm:tm + halo, :] = jnp.zeros((halo, buf.shape[1]), F32)

    @pl.when(jnp.logical_not(first))
    def _():
        buf[tm:tm + halo, :] = buf[0:halo, :]


def _ffn_act_fwd(name, up, vec):
    s = up.shape[0]
    tm = _tile(s, ROW_TILE_F)
    rb_n = _tile(tm, 64)
    cw = 256

    def body(g_ref, v_ref, w_ref, a_ref, carry):
        @pl.when(pl.program_id(0) == 0)
        def _():
            carry[...] = jnp.zeros_like(carry)

        for cc in range(F // cw):
            cols = slice(cc * cw, (cc + 1) * cw)
            w0, w1, w2, b = w_ref[0:1, cols], w_ref[1:2, cols], w_ref[2:3, cols], w_ref[3:4, cols]

            def rb_body(rb, prev8):
                rows = pl.ds(pl.multiple_of(rb * rb_n, rb_n), rb_n)
                g = g_ref[rows, cols].astype(F32)
                gc = w2 * g + w1 * _down(prev8, g, 1) + w0 * _down(prev8, g, 2) + b
                a_ref[rows, cols] = (gc * _sigmoid(gc) * v_ref[rows, cols].astype(F32)).astype(a_ref.dtype)
                return g[rb_n - 8:rb_n]

            carry[:, cols] = lax.fori_loop(0, tm // rb_n, rb_body, carry[:, cols])

    return pl.pallas_call(body, name=name, out_shape=SDS((s, F), BF), grid=(s // tm,),
                          in_specs=[_rows(tm, F, 0), _rows(tm, F, 1), _const((8, F))], out_specs=_rows(tm, F),
                          scratch_shapes=[pltpu.VMEM((HALO, F), F32)], compiler_params=_cp(1))(up, up, vec)


def _prev_halo_spec(nt, tm, width, col):
    per = tm // HALO_BF
    return pl.BlockSpec((HALO_BF, width), lambda i: (jnp.maximum((nt - 1 - i) * per - 1, 0), col))


def _ffn_act_bwd(name, up, da, vec):
    s = up.shape[0]
    tm = _tile(s, ROW_TILE_F)
    nt = s // tm
    rev = lambda col: pl.BlockSpec((tm, F), lambda i: (nt - 1 - i, col))

    rb_n = _tile(tm, 64)
    nrb = tm // rb_n
    cw = 128

    def body(g_ref, gp_ref, v_ref, da_ref, w_ref, dup_ref, acc_ref, carry):
        step = pl.program_id(0)

        @pl.when(step == 0)
        def _():
            acc_ref[...] = jnp.zeros_like(acc_ref)
            carry[...] = jnp.zeros_like(carry)

        for cc in range(F // cw):
            cols = slice(cc * cw, (cc + 1) * cw)
            w0, w1, w2, b = w_ref[0:1, cols], w_ref[1:2, cols], w_ref[2:3, cols], w_ref[3:4, cols]
            halo = jnp.where(step < nt - 1, gp_ref[:, cols].astype(F32)[HALO_BF - 8:HALO_BF], 0.0)

            def rb_body(it, st):
                nxt8, ab, a0, a1, a2 = st
                rb = nrb - 1 - it
                r0 = pl.multiple_of(rb * rb_n, rb_n)
                rows = pl.ds(r0, rb_n)
                g = g_ref[rows, cols].astype(F32)
                ra = pl.multiple_of(jnp.maximum(r0 - HALO_BF, 0), HALO_BF)
                above = g_ref[pl.ds(ra, HALO_BF), cols].astype(F32)[HALO_BF - 8:HALO_BF]
                prev8 = jnp.where(rb == 0, halo, above)
                g1 = _down(prev8, g, 1)
                g2 = _down(prev8, g, 2)
                gc = w2 * g + w1 * g1 + w0 * g2 + b
                sg = _sigmoid(gc)
                sl = gc * sg
                val = v_ref[rows, cols].astype(F32)
                dav = da_ref[rows, cols].astype(F32)
                dup_ref[rows, F + cc * cw:F + (cc + 1) * cw] = (dav * sl).astype(dup_ref.dtype)
                dgc = (dav * val) * (sg + sl * (1.0 - sg))
                dup_ref[rows, cols] = (w2 * dgc + w1 * _up(dgc, nxt8, 1) + w0 * _up(dgc, nxt8, 2)).astype(dup_ref.dtype)
                return (dgc[0:8], ab + _fold8(dgc), a0 + _fold8(g2 * dgc), a1 + _fold8(g1 * dgc),
                        a2 + _fold8(g * dgc))

            z = jnp.zeros((8, cw), F32)
            nxt8, ab, a0, a1, a2 = lax.fori_loop(0, nrb, rb_body, (carry[:, cols], z, z, z, z))
            carry[:, cols] = nxt8
            acc_ref[0:1, cols] += jnp.sum(ab, axis=0, keepdims=True)
            acc_ref[1:2, cols] += jnp.sum(a0, axis=0, keepdims=True)
            acc_ref[2:3, cols] += jnp.sum(a1, axis=0, keepdims=True)
            acc_ref[3:4, cols] += jnp.sum(a2, axis=0, keepdims=True)

    return pl.pallas_call(
        body, name=name, out_shape=(SDS((s, 2 * F), BF), SDS((8, F), F32)), grid=(nt,),
        in_specs=[rev(0), _prev_halo_spec(nt, tm, F, 0), rev(1), rev(0), _const((8, F))],
        out_specs=(pl.BlockSpec((tm, 2 * F), lambda i: (nt - 1 - i, 0)), _const((8, F))),
        scratch_shapes=[pltpu.VMEM((HALO, F), F32)], compiler_params=_cp(1))(up, up, up, da, vec)


def _sc_act_fwd(name, p, vec):
    s = p.shape[0]
    tm = _tile(s, ROW_TILE_D)

    rb_n = _tile(tm, 64)
    cw = 256

    def body(b_ref, c_ref, h_ref, w_ref, z_ref, carry):
        @pl.when(pl.program_id(0) == 0)
        def _():
            carry[...] = jnp.zeros_like(carry)

        for cc in range(D // cw):
            cols = slice(cc * cw, (cc + 1) * cw)
            w0, w1, w2 = w_ref[0:1, cols], w_ref[1:2, cols], w_ref[2:3, cols]

            def rb_body(rb, prev8):
                rows = pl.ds(pl.multiple_of(rb * rb_n, rb_n), rb_n)
                q = c_ref[rows, cols].astype(F32) * h_ref[rows, cols].astype(F32)
                u = w2 * q + w1 * _down(prev8, q, 1) + w0 * _down(prev8, q, 2)
                z_ref[rows, cols] = (b_ref[rows, cols].astype(F32) * u).astype(z_ref.dtype)
                return q[rb_n - 8:rb_n]

            carry[:, cols] = lax.fori_loop(0, tm // rb_n, rb_body, carry[:, cols])

    return pl.pallas_call(body, name=name, out_shape=SDS((s, D), BF), grid=(s // tm,),
                          in_specs=[_rows(tm, D, 0), _rows(tm, D, 1), _rows(tm, D, 2), _const((8, D))],
                          out_specs=_rows(tm, D), scratch_shapes=[pltpu.VMEM((HALO, D), F32)],
                          compiler_params=_cp(1))(p, p, p, vec)


def _sc_act_bwd(name, p, dz, vec):
    s = p.shape[0]
    tm = _tile(s, ROW_TILE_D)
    nt = s // tm
    rev = lambda col: pl.BlockSpec((tm, D), lambda i: (nt - 1 - i, col))

    rb_n = _tile(tm, 64)
    nrb = tm // rb_n
    cw = 128

    def body(b_ref, c_ref, h_ref, cp_ref, hp_ref, dz_ref, w_ref, z_ref, dp_ref, acc_ref, carry):
        step = pl.program_id(0)

        @pl.when(step == 0)
        def _():
            acc_ref[...] = jnp.zeros_like(acc_ref)
            carry[...] = jnp.zeros_like(carry)

        for cc in range(D // cw):
            cols = slice(cc * cw, (cc + 1) * cw)
            w0, w1, w2 = w_ref[0:1, cols], w_ref[1:2, cols], w_ref[2:3, cols]
            halo = jnp.where(step < nt - 1,
                             (cp_ref[:, cols].astype(F32) * hp_ref[:, cols].astype(F32))[HALO_BF - 8:HALO_BF], 0.0)

            def rb_body(it, st):
                nxt8, a0, a1, a2 = st
                rb = nrb - 1 - it
                r0 = pl.multiple_of(rb * rb_n, rb_n)
                rows = pl.ds(r0, rb_n)
                cg = c_ref[rows, cols].astype(F32)
                hin = h_ref[rows, cols].astype(F32)
                bg = b_ref[rows, cols].astype(F32)
                q = cg * hin
                up_rows = pl.ds(pl.multiple_of(jnp.maximum(r0 - HALO_BF, 0), HALO_BF), HALO_BF)
                above = (c_ref[up_rows, cols].astype(F32) * h_ref[up_rows, cols].astype(F32))[HALO_BF - 8:HALO_BF]
                prev8 = jnp.where(rb == 0, halo, above)
                q1 = _down(prev8, q, 1)
                q2 = _down(prev8, q, 2)
                u = w2 * q + w1 * q1 + w0 * q2
                dzv = dz_ref[rows, cols].astype(F32)
                z_ref[rows, cols] = (bg * u).astype(z_ref.dtype)
                dp_ref[rows, cols] = (dzv * u).astype(dp_ref.dtype)
                du = dzv * bg
                dq = w2 * du + w1 * _up(du, nxt8, 1) + w0 * _up(du, nxt8, 2)
                dp_ref[rows, D + cc * cw:D + (cc + 1) * cw] = (dq * hin).astype(dp_ref.dtype)
                dp_ref[rows, 2 * D + cc * cw:2 * D + (cc + 1) * cw] = (dq * cg).astype(dp_ref.dtype)
                return du[0:8], a0 + _fold8(q2 * du), a1 + _fold8(q1 * du), a2 + _fold8(q * du)

            z = jnp.zeros((8, cw), F32)
            nxt8, a0, a1, a2 = lax.fori_loop(0, nrb, rb_body, (carry[:, cols], z, z, z))
            carry[:, cols] = nxt8
            acc_ref[0:1, cols] += jnp.sum(a0, axis=0, keepdims=True)
            acc_ref[1:2, cols] += jnp.sum(a1, axis=0, keepdims=True)
            acc_ref[2:3, cols] += jnp.sum(a2, axis=0, keepdims=True)

    return pl.pallas_call(
        body, name=name, out_shape=(SDS((s, D), BF), SDS((s, 3 * D), BF), SDS((8, D), F32)), grid=(nt,),
        in_specs=[rev(0), rev(1), rev(2), _prev_halo_spec(nt, tm, D, 1), _prev_halo_spec(nt, tm, D, 2), rev(0),
                  _const((8, D))],
        out_specs=(rev(0), pl.BlockSpec((tm, 3 * D), lambda i: (nt - 1 - i, 0)), _const((8, D))),
        scratch_shapes=[pltpu.VMEM((HALO, D), F32)], compiler_params=_cp(1))(p, p, p, p, p, dz, vec)


CF_ROW_BLOCK = 32
CF_LANES = 128


def _tap_conv_block(ext, rb_n, tap_of_offset, t_ref, cols, init, u=None, accs=None):
    n = ext.shape[0]
    out = init
    for b in range(8):
        rolled = ext if b == 0 else pltpu.roll(ext, n - b, 0)
        for a in range(n // 8):
            k = tap_of_offset(8 * a + b)
            if k is None:
                continue
            sl = rolled[8 * a:8 * a + rb_n]
            out = out + t_ref[k:k + 1, cols] * sl
            if accs is not None:
                accs[k] = accs[k] + _fold8(u * sl)
    return out


def _cf_act_fwd(name, p, taps, vec):
    s = p.shape[0]
    tm = _tile(s, ROW_TILE_D)
    rb_n = _tile(tm, CF_ROW_BLOCK)
    base = CF_HALO - (CF_TAPS - 1)
    tap_of = lambda off: off - base if 0 <= off - base < CF_TAPS else None

    def body(a_ref, g_ref, t_ref, v_ref, w_ref, cv_ref, buf):
        first = pl.program_id(0) == 0

        @pl.when(first)
        def _():
            buf[0:CF_HALO, :] = jnp.zeros((CF_HALO, D), F32)

        @pl.when(jnp.logical_not(first))
        def _():
            buf[0:CF_HALO, :] = buf[tm:tm + CF_HALO, :]

        a = a_ref[...].astype(F32) + v_ref[0:1, :]
        g = g_ref[...].astype(F32) + v_ref[1:2, :]
        buf[CF_HALO:CF_HALO + tm, :] = a * _sigmoid(g)
        for ci in range(D // CF_LANES):
            cols = slice(ci * CF_LANES, (ci + 1) * CF_LANES)

            def rb_body(rb, carry):
                r0 = pl.multiple_of(rb * rb_n, rb_n)
                ext = buf[pl.ds(r0, rb_n + CF_HALO), cols]
                init = jnp.zeros((rb_n, CF_LANES), F32) + v_ref[2:3, cols]
                cv_ref[pl.ds(r0, rb_n), cols] = _tap_conv_block(ext, rb_n, tap_of, t_ref, cols, init)
                return carry

            lax.fori_loop(0, tm // rb_n, rb_body, 0)
        cv = cv_ref[...]
        mu = jnp.mean(cv, axis=-1, keepdims=True)
        cc = cv - mu
        rstd = lax.rsqrt(jnp.mean(cc * cc, axis=-1, keepdims=True) + LN_EPS)
        ln = cc * rstd * v_ref[3:4, :] + v_ref[4:5, :]
        w_ref[...] = (ln * _sigmoid(ln)).astype(w_ref.dtype)

    return pl.pallas_call(body, name=name, out_shape=(SDS((s, D), BF), SDS((s, D), F32)), grid=(s // tm,),
                          in_specs=[_rows(tm, D, 0), _rows(tm, D, 1), _const((32, D)), _const((8, D))],
                          out_specs=(_rows(tm, D), _rows(tm, D)),
                          scratch_shapes=[pltpu.VMEM((tm + CF_HALO, D), F32)], compiler_params=_cp(1))(p, p, taps, vec)


def _cf_act_bwd(name, p, cv, dw, taps, vec):
    s = p.shape[0]
    tm = _tile(s, ROW_TILE_D)
    nt = s // tm
    rev = lambda col: pl.BlockSpec((tm, D), lambda i: (nt - 1 - i, col))
    rb_n = _tile(tm, CF_ROW_BLOCK)
    tap_of = lambda off: CF_TAPS - 1 - off if off < CF_TAPS else None

    def body(a_ref, g_ref, cv_ref, dw_ref, t_ref, v_ref, dp_ref, tacc_ref, acc_ref, nbuf, ubuf, dubuf):
        step = pl.program_id(0)

        @pl.when(step == 0)
        def _():
            acc_ref[...] = jnp.zeros_like(acc_ref)
            tacc_ref[...] = jnp.zeros_like(tacc_ref)

        a = a_ref[...].astype(F32) + v_ref[0:1, :]
        g = g_ref[...].astype(F32) + v_ref[1:2, :]
        sg = _sigmoid(g)
        ubuf[...] = a * sg
        cvv = cv_ref[...]
        mu = jnp.mean(cvv, axis=-1, keepdims=True)
        cc = cvv - mu
        rstd = lax.rsqrt(jnp.mean(cc * cc, axis=-1, keepdims=True) + LN_EPS)
        vhat = cc * rstd
        ln = vhat * v_ref[3:4, :] + v_ref[4:5, :]
        s2 = _sigmoid(ln)
        dln = dw_ref[...].astype(F32) * (s2 * (1.0 + ln * (1.0 - s2)))
        acc_ref[1:2, :] += jnp.sum(dln * vhat, axis=0, keepdims=True)
        acc_ref[2:3, :] += jnp.sum(dln, axis=0, keepdims=True)
        dvh = dln * v_ref[3:4, :]
        dcv = rstd * (dvh - jnp.mean(dvh, axis=-1, keepdims=True)
                      - vhat * jnp.mean(dvh * vhat, axis=-1, keepdims=True))
        acc_ref[0:1, :] += jnp.sum(dcv, axis=0, keepdims=True)
        _carry_up(nbuf, tm, step == 0, CF_HALO)
        nbuf[0:tm, :] = dcv
        for ci in range(D // CF_LANES):
            cols = slice(ci * CF_LANES, (ci + 1) * CF_LANES)

            def rb_body(rb, accs):
                r0 = pl.multiple_of(rb * rb_n, rb_n)
                ext = nbuf[pl.ds(r0, rb_n + CF_HALO), cols]
                accs = list(accs)
                dubuf[pl.ds(r0, rb_n), cols] = _tap_conv_block(
                    ext, rb_n, tap_of, t_ref, cols, jnp.zeros((rb_n, CF_LANES), F32),
                    ubuf[pl.ds(r0, rb_n), cols], accs)
                return tuple(accs)

            z = jnp.zeros((8, CF_LANES), F32)
            accs = lax.fori_loop(0, tm // rb_n, rb_body, tuple([z] * CF_TAPS))
            for k in range(CF_TAPS):
                tacc_ref[k:k + 1, cols] += jnp.sum(accs[k], axis=0, keepdims=True)
        a = a_ref[...].astype(F32) + v_ref[0:1, :]
        sg = _sigmoid(g_ref[...].astype(F32) + v_ref[1:2, :])
        da = dubuf[...] * sg
        dg = da * a * (1.0 - sg)
        dp_ref[:, 0:D] = da.astype(dp_ref.dtype)
        dp_ref[:, D:2 * D] = dg.astype(dp_ref.dtype)
        acc_ref[3:4, :] += jnp.sum(da, axis=0, keepdims=True)
        acc_ref[4:5, :] += jnp.sum(dg, axis=0, keepdims=True)

    return pl.pallas_call(
        body, name=name, out_shape=(SDS((s, 2 * D), BF), SDS((32, D), F32), SDS((8, D), F32)), grid=(nt,),
        in_specs=[rev(0), rev(1), rev(0), rev(0), _const((32, D)), _const((8, D))],
        out_specs=(pl.BlockSpec((tm, 2 * D), lambda i: (nt - 1 - i, 0)), _const((32, D)), _const((8, D))),
        scratch_shapes=[pltpu.VMEM((tm + CF_HALO, D), F32), pltpu.VMEM((tm, D), F32), pltpu.VMEM((tm, D), F32)],
        compiler_params=_cp(1))(p, p, cv, dw, taps, vec)


def _inv_count(row0, tm, window):
    t = row0 + lax.broadcasted_iota(jnp.int32, (tm, 1), 0)
    return 1.0 / jnp.minimum(t + 1, window).astype(F32)


def _pool_fwd(name, x, w, vec):
    s = x.shape[0]
    tm = _tile(s, ROW_TILE_D)
    G = POOL_GROUP

    def body(x_ref, w_ref, v_ref, pl_ref, m_ref, buf):
        i = pl.program_id(0)

        @pl.when(i == 0)
        def _():
            buf[0:POOL_HALO, :] = jnp.zeros((POOL_HALO, D), F32)

        @pl.when(i > 0)
        def _():
            buf[0:POOL_HALO, :] = buf[tm:tm + POOL_HALO, :]

        xv = x_ref[...]
        r = lax.rsqrt(jnp.mean(xv * xv, axis=-1, keepdims=True) + RMS_EPS)
        buf[POOL_HALO:POOL_HALO + tm, :] = xv * r * (v_ref[0:1, :] * (1.0 + v_ref[1:2, :])) + v_ref[2:3, :]
        for gi, win in enumerate(POOL_WINDOWS):
            cols = slice(gi * G, (gi + 1) * G)
            acc = buf[POOL_HALO:POOL_HALO + tm, cols]
            hg = acc
            for j in range(1, win):
                acc = acc + buf[POOL_HALO - j:POOL_HALO - j + tm, cols]
            pooled = (acc * _inv_count(i * tm, tm, win) - hg).astype(BF)
            pl_ref[:, cols] = pooled
            yg = jnp.dot(pooled, w_ref[gi], preferred_element_type=F32)
            m_ref[:, cols] = (yg + v_ref[3:4, cols]) * v_ref[4:5, cols]

    return pl.pallas_call(body, name=name, out_shape=(SDS((s, D), BF), SDS((s, D), F32)), grid=(s // tm,),
                          in_specs=[_rows(tm, D), _const((4, G, G)), _const((8, D))],
                          out_specs=(_rows(tm, D), _rows(tm, D)),
                          scratch_shapes=[pltpu.VMEM((tm + POOL_HALO, D), F32)], compiler_params=_cp(1))(x, w, vec)


def _pool_bwd(name, pooled, dm, w, vec):
    s = pooled.shape[0]
    tm = _tile(s, ROW_TILE_D)
    nt = s // tm
    G = POOL_GROUP
    rev = pl.BlockSpec((tm, D), lambda i: (nt - 1 - i, 0))

    def body(p_ref, dm_ref, w_ref, v_ref, dh_ref, dw_ref, acc_ref, nbuf):
        step = pl.program_id(0)
        row0 = (nt - 1 - step) * tm

        @pl.when(step == 0)
        def _():
            acc_ref[...] = jnp.zeros_like(acc_ref)
            dw_ref[...] = jnp.zeros_like(dw_ref)

        _carry_up(nbuf, tm, step == 0, POOL_HALO)
        dmv = dm_ref[...].astype(F32)
        acc_ref[0:1, :] += jnp.sum(dmv, axis=0, keepdims=True) * v_ref[1:2, :]
        dps = []
        for gi, win in enumerate(POOL_WINDOWS):
            cols = slice(gi * G, (gi + 1) * G)
            pg = p_ref[:, cols]
            yb = jnp.dot(pg, w_ref[gi], preferred_element_type=F32) + v_ref[0:1, cols]
            acc_ref[1:2, cols] += jnp.sum(dmv[:, cols] * yb, axis=0, keepdims=True)
            dy = (dmv[:, cols] * v_ref[1:2, cols]).astype(BF)
            dw_ref[gi] += lax.dot_general(pg, dy, TN, preferred_element_type=F32)
            dpg = lax.dot_general(dy, w_ref[gi], NT, preferred_element_type=F32)
            dps.append(dpg)
            nbuf[0:tm, cols] = dpg * _inv_count(row0, tm, win)
        for gi, win in enumerate(POOL_WINDOWS):
            cols = slice(gi * G, (gi + 1) * G)
            acc = nbuf[0:tm, cols]
            for j in range(1, win):
                acc = acc + nbuf[j:j + tm, cols]
            dh_ref[:, cols] = acc - dps[gi]

    return pl.pallas_call(
        body, name=name, out_shape=(SDS((s, D), F32), SDS((4, G, G), F32), SDS((8, D), F32)), grid=(nt,),
        in_specs=[rev, rev, _const((4, G, G)), _const((8, D))],
        out_specs=(rev, _const((4, G, G)), _const((8, D))),
        scratch_shapes=[pltpu.VMEM((tm + POOL_HALO, D), F32)], compiler_params=_cp(1))(pooled, dm, w, vec)


def _row_tile_2d(rows, width, bytes_per_row_elem=4, budget=2 * 1024 * 1024):
    t = max(8, budget // (width * bytes_per_row_elem))
    t = min(rows, 1 << (t.bit_length() - 1))
    while rows % t:
        t //= 2
    return t


def _add_slots(name, r):
    nl, _, k, n = r.shape
    tk = _row_tile_2d(k, n, 16)

    def body(r_ref, o_ref):
        f = lambda i: r_ref[i].astype(F32)
        o_ref[...] = ((f(7) + f(6)) + (f(0) + f(1))) + ((f(2) + f(3)) + (f(4) + f(5)))

    return pl.pallas_call(body, name=name, out_shape=SDS((nl, k, n), F32), grid=(nl, k // tk),
                          in_specs=[pl.BlockSpec((None, N_DEV, tk, n), lambda l, i: (l, 0, i, 0))],
                          out_specs=pl.BlockSpec((None, tk, n), lambda l, i: (l, i, 0)),
                          compiler_params=_cp(2))(r)


def _adamw(name, w, g, m, v):
    rows, width = w.shape
    tm = _row_tile_2d(rows, width, 4, 1024 * 1024)
    c1 = 1.0 - ADAM_B1 ** ADAM_STEP
    c2 = 1.0 - ADAM_B2 ** ADAM_STEP

    def body(w_ref, g_ref, m_ref, v_ref, d_ref, nm_ref, nv_ref):
        gv = g_ref[...]
        nm = ADAM_B1 * m_ref[...] + (1.0 - ADAM_B1) * gv
        nv = ADAM_B2 * v_ref[...] + (1.0 - ADAM_B2) * (gv * gv)
        nm_ref[...] = nm
        nv_ref[...] = nv
        d_ref[...] = -ADAM_LR * ((nm / c1) / (jnp.sqrt(nv / c2) + ADAM_EPS) + ADAM_WD * w_ref[...])

    spec = _rows(tm, width)
    sds = SDS((rows, width), F32)
    return pl.pallas_call(body, name=name, out_shape=(sds, sds, sds), grid=(rows // tm,),
                          in_specs=[spec] * 4, out_specs=(spec,) * 3, compiler_params=_cp(1))(w, g, m, v)


def _mod_fwd(c16, w_mod, b_sh):
    n = w_mod.shape[2]
    tn = _tile(n, 512)

    def body(c_ref, w_ref, b_ref, o_ref):
        cv = c_ref[...]
        ca = (cv * _sigmoid(cv)).astype(BF)
        o_ref[...] = jnp.dot(ca, w_ref[...].astype(BF), preferred_element_type=F32) + b_ref[0:1, :]

    return pl.pallas_call(body, name="mod_fwd", out_shape=SDS((DEPTH, 16, n), F32), grid=(DEPTH, n // tn),
                          in_specs=[_const((16, D)), pl.BlockSpec((None, D, tn), lambda l, j: (l, 0, j)),
                                    pl.BlockSpec((None, 8, tn), lambda l, j: (l, 0, j))],
                          out_specs=pl.BlockSpec((None, 16, tn), lambda l, j: (l, 0, j)),
                          compiler_params=_cp(2))(c16, w_mod, b_sh)


def _mod_bwd(c16, dmod):
    n = dmod.shape[2]
    tn = _tile(n, 512)

    def body(c_ref, d_ref, o_ref):
        cv = c_ref[...]
        ca = (cv * _sigmoid(cv)).astype(BF)
        o_ref[...] = lax.dot_general(ca, d_ref[...].astype(BF), TN, preferred_element_type=F32)

    return pl.pallas_call(body, name="mod_bwd", out_shape=SDS((DEPTH, D, n), F32), grid=(DEPTH, n // tn),
                          in_specs=[_const((16, D)), pl.BlockSpec((None, 16, tn), lambda l, j: (l, 0, j))],
                          out_specs=pl.BlockSpec((None, D, tn), lambda l, j: (l, 0, j)),
                          compiler_params=_cp(2))(c16, dmod)


def _place():
    x, y, c = lax.axis_index("x"), lax.axis_index("y"), lax.axis_index("c")
    other_chips = [(1 - x, y), (x, 1 - y), (1 - x, 1 - y)]
    return x, y, c, other_chips


def _allgather_small(name, v, with_sum):
    m, n = v.shape

    def body(x_ref, out_ref, *rest):
        if with_sum:
            sum_ref, send_sems, recv_sems, local_sem = rest
        else:
            send_sems, recv_sems, local_sem = rest
        x, y, c, chips = _place()
        me, sibling = (x, y, c), (x, y, 1 - c)

        def rows(px, py, pc):
            return out_ref.at[pl.ds((4 * px + 2 * py + pc) * m, m), :]

        def copy(k, block, to, src=None):
            return pltpu.make_async_remote_copy(
                src_ref=rows(*block) if src is None else src, dst_ref=rows(*block),
                send_sem=send_sems.at[k], recv_sem=recv_sems.at[k], device_id=to, device_id_type=MESH)

        mine = pltpu.make_async_copy(x_ref, rows(*me), local_sem)
        mine.start()
        first = [copy(0, me, sibling, src=x_ref)]
        first += [copy(1 + j, me, (*chip, c), src=x_ref) for j, chip in enumerate(chips)]
        for cp in first:
            cp.start()
        passed = [copy(4 + j, (*chip, c), sibling) for j, chip in enumerate(chips)]
        for j, chip in enumerate(chips):
            copy(1 + j, (*chip, c), me).wait_recv()
            passed[j].start()
        copy(0, sibling, me).wait_recv()
        for j, chip in enumerate(chips):
            copy(4 + j, (*chip, 1 - c), me).wait_recv()
        for cp in first + passed:
            cp.wait_send()
        mine.wait()
        if with_sum:
            acc = out_ref[0:m, :]
            for k in range(1, N_DEV):
                acc = acc + out_ref[k * m:(k + 1) * m, :]
            sum_ref[...] = acc

    vm = pl.BlockSpec(memory_space=pltpu.VMEM)
    out_shape = [SDS((N_DEV * m, n), F32)] + ([SDS((m, n), F32)] if with_sum else [])
    res = pl.pallas_call(
        body, name=name, out_shape=tuple(out_shape), in_specs=[vm], out_specs=tuple([vm] * len(out_shape)),
        scratch_shapes=[pltpu.SemaphoreType.DMA((7,)), pltpu.SemaphoreType.DMA((7,)), pltpu.SemaphoreType.DMA],
        compiler_params=pltpu.CompilerParams(vmem_limit_bytes=VMEM_LIMIT_MB * 1024 * 1024))(v)
    return res if with_sum else res[0]


HBM = pl.BlockSpec(memory_space=pltpu.HBM)


def _sem_scratch(n_remote, n_local):
    return [pltpu.SemaphoreType.DMA((n_remote,)), pltpu.SemaphoreType.DMA((n_remote,)),
            pltpu.SemaphoreType.DMA((n_local,))]


DMA_PIECE_BYTES = 1 << 20


def _pieces(src, dst):
    *lead, rows, n = src.shape
    nsplit = max(1, min(rows // 16, (rows * n * jnp.dtype(src.dtype).itemsize) // DMA_PIECE_BYTES))
    while rows % nsplit or (rows // nsplit) % 16:
        nsplit -= 1
    size = rows // nsplit
    out = []
    for idx in itertools.product(*[range(d) for d in lead]):
        for i in range(nsplit):
            sl = tuple(idx) + (pl.ds(i * size, size),)
            out.append((src.at[sl], dst.at[sl]))
    return out


def _local_copies(src, dst, sem):
    return ([pltpu.make_async_copy(s_, d_, sem) for s_, d_ in _pieces(src, dst)],
            pltpu.make_async_copy(src, dst, sem))


def _remote_copies(src, dst, send_sem, recv_sem, to):
    mk = lambda s_, d_: pltpu.make_async_remote_copy(src_ref=s_, dst_ref=d_, send_sem=send_sem, recv_sem=recv_sem,
                                                     device_id=to, device_id_type=MESH)
    return [mk(s_, d_) for s_, d_ in _pieces(src, dst)], mk(src, dst)


def _gather_weights(name, shards, sequencer_id=None):
    nq = len(shards)

    def exchange(ins, outs, send_sems, recv_sems, local_sems, own_barrier):
        x, y, c, chips = _place()
        if own_barrier:
            barrier = pltpu.get_barrier_semaphore()
            for px, py in chips:
                pl.semaphore_signal(barrier, inc=1, device_id=(px, py, c), device_id_type=MESH)
            pl.semaphore_wait(barrier, len(chips))
        me_chip = 2 * x + y
        started, local_all, send_all, recv_all = [], [], [], []
        for q in range(nq):
            cps, whole = _local_copies(ins[q], outs[q].at[:, me_chip], local_sems.at[q])
            started += cps
            local_all.append(whole)
            for r, (px, py) in enumerate(chips):
                k = 3 * q + r
                cps, whole = _remote_copies(ins[q], outs[q].at[:, me_chip], send_sems.at[k], recv_sems.at[k], (px, py, c))
                started += cps
                send_all.append(whole)
                recv_all.append(_remote_copies(ins[q], outs[q].at[:, 2 * px + py], send_sems.at[k], recv_sems.at[k],
                                               (px, py, c))[1])
        for cp in started:
            cp.start()
        for cp in recv_all:
            cp.wait_recv()
        for cp in send_all:
            cp.wait_send()
        for cp in local_all:
            cp.wait()

    out_shape = tuple(SDS((s.shape[0], N_CHIPS) + s.shape[1:], s.dtype) for s in shards)
    return _launch_exchange(exchange, name, shards, out_shape, _sem_scratch(3 * nq, nq), sequencer_id)


def _launch_exchange(exchange, name, arrays, out_shape, sems, sequencer_id):
    n = len(arrays)
    if sequencer_id is None:
        def body(*refs):
            exchange(refs[:n], refs[n:2 * n], *refs[2 * n:], own_barrier=False)

        return pl.pallas_call(body, name=name, out_shape=out_shape, in_specs=[HBM] * n,
                              out_specs=tuple([HBM] * n), scratch_shapes=sems)(*arrays)

    in_refs = [jax.new_ref(a, memory_space=pltpu.MemorySpace.HBM) for a in arrays]
    out_refs = [jax.empty_ref(o, memory_space=pltpu.MemorySpace.HBM) for o in out_shape]

    @pl.kernel(mesh=plsc.ScalarSubcoreMesh(axis_name="sequencer", num_cores=1), name=name, scratch_types=tuple(sems),
               compiler_params=pltpu.CompilerParams(collective_id=sequencer_id))
    def launch(send_sems, recv_sems, local_sems):
        exchange(in_refs, out_refs, send_sems, recv_sems, local_sems, own_barrier=True)

    launch()
    return [r[...] for r in out_refs]


def _grad_exchange(name, gs, sequencer_id=None):
    nq = len(gs)

    def exchange(ins, outs, send_sems, recv_sems, local_sems, own_barrier):
        x, y, c, chips = _place()
        peers = [(2 * r + e, (px, py, c if e == 0 else 1 - c)) for r, (px, py) in enumerate(chips) for e in (0, 1)]
        peers.append((6, (x, y, 1 - c)))
        if own_barrier:
            barrier = pltpu.get_barrier_semaphore()
            for _, peer in peers:
                pl.semaphore_signal(barrier, inc=1, device_id=peer, device_id_type=MESH)
            pl.semaphore_wait(barrier, len(peers))
        started, local_all, remote_all = [], [], []
        for q in range(nq):
            cps, whole = _local_copies(ins[q].at[:, 2 * x + y, c], outs[q].at[:, 7], local_sems.at[q])
            started += cps
            local_all.append(whole)
            for slot, (px, py, pc) in peers:
                k = 7 * q + slot
                cps, whole = _remote_copies(ins[q].at[:, 2 * px + py, pc], outs[q].at[:, slot], send_sems.at[k],
                                            recv_sems.at[k], (px, py, pc))
                started += cps
                remote_all.append(whole)
        for cp in started:
            cp.start()
        for cp in remote_all:
            cp.wait_recv()
        for cp in remote_all:
            cp.wait_send()
        for cp in local_all:
            cp.wait()

    out_shape = tuple(SDS((g.shape[0], N_DEV, g.shape[3], g.shape[4]), g.dtype) for g in gs)
    return _launch_exchange(exchange, name, gs, out_shape, _sem_scratch(7 * nq, nq), sequencer_id)


def _pair_share(name, rs, sequencer_id=None):
    nq = len(rs)

    def exchange(ins, outs, send_sems, recv_sems, local_sems, own_barrier):
        x, y, c, _ = _place()
        if own_barrier:
            barrier = pltpu.get_barrier_semaphore()
            pl.semaphore_signal(barrier, inc=1, device_id=(x, y, 1 - c), device_id_type=MESH)
            pl.semaphore_wait(barrier, 1)
        started, local_all, send_all, recv_all = [], [], [], []
        for q in range(nq):
            cps, whole = _local_copies(ins[q], outs[q].at[:, c], local_sems.at[q])
            started += cps
            local_all.append(whole)
            cps, whole = _remote_copies(ins[q], outs[q].at[:, c], send_sems.at[q], recv_sems.at[q], (x, y, 1 - c))
            started += cps
            send_all.append(whole)
            recv_all.append(_remote_copies(ins[q], outs[q].at[:, 1 - c], send_sems.at[q], recv_sems.at[q],
                                           (x, y, 1 - c))[1])
        for cp in started:
            cp.start()
        for cp in recv_all:
            cp.wait_recv()
        for cp in send_all:
            cp.wait_send()
        for cp in local_all:
            cp.wait()

    out_shape = tuple(SDS((r.shape[0], 2) + r.shape[1:], r.dtype) for r in rs)
    return _launch_exchange(exchange, name, rs, out_shape, _sem_scratch(nq, nq), sequencer_id)


def _pack(arrs, rows_multiple=8):
    flat = jnp.concatenate([a.astype(F32).reshape(-1) for a in arrs])
    pad = (-flat.shape[0]) % (128 * rows_multiple)
    return jnp.pad(flat, (0, pad)).reshape(-1, 128)


def _unpack(slab, shapes):
    flat = slab.reshape(-1)
    out, off = [], 0
    for shp in shapes:
        n = 1
        for d in shp:
            n *= d
        out.append(flat[off:off + n].reshape(shp))
        off += n
    return out


def _shard_last(a, chip, n):
    return lax.dynamic_slice_in_dim(a, chip * n, n, axis=a.ndim - 1)


def kernel(x, c, w_mod, b_mod, norm_g, sc_w_in, sc_conv, sc_w_out, pool_w, pool_b, pool_scale, cf_w_pw1, cf_b_pw1, cf_w_dw, cf_b_dw, cf_ln_g, cf_ln_b, cf_w_pw2, cf_b_pw2, ffn_w_up, ffn_conv, ffn_b_conv, ffn_w_down, loss_target, m_w_mod, m_b_mod, m_norm_g, m_sc_w_in, m_sc_conv, m_sc_w_out, m_pool_w, m_pool_b, m_pool_scale, m_cf_w_pw1, m_cf_b_pw1, m_cf_w_dw, m_cf_b_dw, m_cf_ln_g, m_cf_ln_b, m_cf_w_pw2, m_cf_b_pw2, m_ffn_w_up, m_ffn_conv, m_ffn_b_conv, m_ffn_w_down, v_w_mod, v_b_mod, v_norm_g, v_sc_w_in, v_sc_conv, v_sc_w_out, v_pool_w, v_pool_b, v_pool_scale, v_cf_w_pw1, v_cf_b_pw1, v_cf_w_dw, v_cf_b_dw, v_cf_ln_g, v_cf_ln_b, v_cf_w_pw2, v_cf_b_pw2, v_ffn_w_up, v_ffn_conv, v_ffn_b_conv, v_ffn_w_down):
    ax, ay, ac = lax.axis_index("x"), lax.axis_index("y"), lax.axis_index("c")
    chip = 2 * ax + ay
    dev = 4 * ax + 2 * ay + ac
    xs = x[0]
    target = loss_target[0]

    small_sharded = [norm_g, sc_conv, cf_b_pw1, cf_w_dw, cf_b_dw, cf_ln_g, cf_ln_b, cf_b_pw2, ffn_conv]
    slab = _pack([c] + small_sharded)
    gathered = _allgather_small("gather_small_params", slab, False).reshape(N_DEV, -1, 128)
    parts = [_unpack(gathered[d], [c.shape] + [a.shape for a in small_sharded]) for d in range(N_DEV)]
    c_all = jnp.concatenate([p[0] for p in parts], axis=0)
    full = [jnp.concatenate([parts[2 * j][1 + i] for j in range(N_CHIPS)], axis=-1)
            for i in range(len(small_sharded))]
    norm_g_f, sc_conv_f, cf_b_pw1_f, cf_w_dw_f, cf_b_dw_f, cf_ln_g_f, cf_ln_b_f, cf_b_pw2_f, ffn_conv_f = full
    c16 = jnp.pad(c_all, ((0, 8), (0, 0)))

    n_mod = w_mod.shape[2]
    b_sh = jnp.broadcast_to(_shard_last(b_mod, chip, n_mod)[:, None, :], (DEPTH, 8, n_mod))
    mod_part = _mod_fwd(c16, w_mod, b_sh)
    mod_g = _allgather_small("gather_mod", mod_part.reshape(DEPTH * 16, n_mod), False)
    mod_g = mod_g.reshape(N_DEV, DEPTH, 16, n_mod)
    mod_mine = jnp.concatenate(
        [lax.dynamic_index_in_dim(mod_g[2 * j], dev, axis=1, keepdims=False) for j in range(N_CHIPS)], axis=-1)
    mod = mod_mine.reshape(DEPTH, 6, D)

    bf = lambda a: a.astype(BF)
    rows = lambda w: w.reshape(w.shape[0], w.shape[1] * w.shape[2], w.shape[3])
    after = lambda a, done: a + (done[(0,) * done.ndim] * 0).astype(a.dtype)
    w_in_0, w_out_0 = _gather_weights("gather_weights", [bf(sc_w_in[0:1]), bf(sc_w_out[0:1])])
    w_up_0, w_down_0 = _gather_weights(
        "gather_weights_first", [after(bf(ffn_w_up[0:1]), w_in_0), bf(ffn_w_down[0:1])], GATHER_FIRST_ID)
    w_in_r, w_out_r, pool_f, pw1_f, pw2_f, w_up_r, w_down_r = _gather_weights(
        "gather_weights_rest", [after(bf(sc_w_in[1:]), w_down_0), bf(sc_w_out[1:]), bf(pool_w[0]), bf(cf_w_pw1),
                                bf(cf_w_pw2), bf(ffn_w_up[1:]), bf(ffn_w_down[1:])], GATHER_REST_ID)
    pool_f = pool_f.reshape(4, POOL_GROUP, POOL_GROUP)
    pw1_f, pw2_f = (pw1_f, 0), (rows(pw2_f), 0)
    w_in_f = {0: (w_in_0, 0), 1: (w_in_r, 0)}
    w_out_f = {0: (rows(w_out_0), 0), 1: (rows(w_out_r), 0)}
    w_up_f = {i: (w_up_0, 0) if i == 0 else (w_up_r, i - 1) for i in range(DEPTH)}
    w_down_f = {i: (rows(w_down_0), 0) if i == 0 else (rows(w_down_r), i - 1) for i in range(DEPTH)}

    zero_d = jnp.zeros((D,), F32)
    mods = [[mod[i, k] for k in range(6)] for i in range(DEPTH)]
    saved = []
    xcur = xs
    h_next = None
    for i in range(DEPTH):
        kind, j = i % 3, i // 3
        sh1, sc1, g1, sh2, sc2, g2 = mods[i]
        st = {"x0": xcur}
        m_bias = zero_d
        if kind != 1:
            h = h_next if h_next is not None else _pro_fwd(f"pro1_fwd_{i}", xcur, _vecs([norm_g_f[i, 0], sc1, sh1], D))
        if kind == 0:
            p = _mm_nn(f"sc_in_{i}", h, *w_in_f[j],True, BF)
            z = _sc_act_fwd(f"sc_act_fwd_{i}", p, _vecs(list(sc_conv_f[j]), D))
            m = _mm_nn(f"sc_out_{i}", z, *w_out_f[j],False, F32)
            st.update(h=h, p=p)
        elif kind == 1:
            pool_vec = _vecs([norm_g_f[i, 0], sc1, sh1, pool_b[j], pool_scale[j]], D)
            pooled, m = _pool_fwd(f"pool_fwd_{i}", xcur, pool_f, pool_vec)
            st.update(pooled=pooled)
        else:
            p = _mm_nn(f"cf_pw1_{i}", h, *pw1_f,True, BF)
            taps = jnp.pad(cf_w_dw_f[j], ((0, 1), (0, 0)))
            cf_vec = _vecs([cf_b_pw1_f[j, :D], cf_b_pw1_f[j, D:], cf_b_dw_f[j], cf_ln_g_f[j], cf_ln_b_f[j]], D)
            wact, cv = _cf_act_fwd(f"cf_act_fwd_{i}", p, taps, cf_vec)
            m = _mm_nn(f"cf_pw2_{i}", wact, *pw2_f,False, F32)
            m_bias = cf_b_pw2_f[j]
            st.update(h=h, p=p, wact=wact, cv=cv, taps=taps, cf_vec=cf_vec)
        x1, h2 = _epi_pro_fwd(f"epi1_fwd_{i}", xcur, m,
                              _vecs([g1, norm_g_f[i, 1], m_bias, norm_g_f[i, 2], sc2, sh2], D), True)
        st.update(m=m, x1=x1, m_bias=m_bias)
        up = _mm_nn(f"ffn_up_{i}", h2, *w_up_f[i],True, BF)
        ffn_vec = _vecs(list(ffn_conv_f[i]) + [ffn_b_conv[i]], F)
        a = _ffn_act_fwd(f"ffn_act_fwd_{i}", up, ffn_vec)
        f = _mm_nn(f"ffn_down_{i}", a, *w_down_f[i],False, F32, tm=512)
        nxt = i + 1
        fuse_next = nxt < DEPTH and nxt % 3 != 1
        rows = [g2, norm_g_f[i, 3], zero_d]
        if fuse_next:
            rows += [norm_g_f[nxt, 0], mods[nxt][1], mods[nxt][0]]
        xcur, h_next = _epi_pro_fwd(f"epi2_fwd_{i}", x1, f, _vecs(rows, D), fuse_next)
        st.update(h2=h2, up=up, a=a, f=f, ffn_vec=ffn_vec)
        saved.append(st)

    dy, loss_cols = _loss_fwd_bwd(xcur, target)
    loss = lax.psum(jnp.sum(loss_cols[0]), ("x", "y", "c"))

    dmod = [None] * DEPTH
    d_norm_g = [None] * DEPTH
    d_sc_conv = [None, None]
    d_ffn_conv, d_ffn_b = [None] * DEPTH, [None] * DEPTH
    g_w_in, g_w_out, g_w_up, g_w_down = [None, None], [None, None], [None] * DEPTH, [None] * DEPTH
    def halves(gl):
        g = jnp.stack(gl) if isinstance(gl, list) else gl
        if g.ndim == 3:
            g = g.reshape(g.shape[0], N_CHIPS, g.shape[1] // N_CHIPS, g.shape[2])
        nl, _, k, n = g.shape
        return g.reshape(nl, N_CHIPS, 2, k // 2, n)

    dxo = dy
    last = DEPTH - 1
    pend = _epi_bwd(f"epi2_bwd_{last}", dy, saved[last]["f"], _vecs([mods[last][5], norm_g_f[last, 3], zero_d], D))
    for i in reversed(range(DEPTH)):
        kind, j = i % 3, i // 3
        st = saved[i]
        sh1, sc1, g1, sh2, sc2, g2 = mods[i]
        df, e2 = pend
        da = _mm_nt(f"ffn_down_dx_{i}", df, *w_down_f[i],False, BF, tn=F // 2)
        dup, fsum = _ffn_act_bwd(f"ffn_act_bwd_{i}", st["up"], da, st["ffn_vec"])
        g_w_down[i] = _mm_tn(f"ffn_down_dw_{i}", st["a"], df, False, tk=F // 2)
        dh2 = _mm_nt(f"ffn_up_dx_{i}", dup, *w_up_f[i],True, F32)
        g_w_up[i] = _mm_tn(f"ffn_up_dw_{i}", st["h2"], dup, True)
        if i == 0:
            ffn0_pieces = _grad_exchange("grad_exchange_ffn0", [halves([g_w_up[0]]), halves([g_w_down[0]])],
                                         GRAD_FFN0_ID)
        dx1, dm, s21 = _pro_epi_bwd(f"pro2_epi1_bwd_{i}", dh2, st["x1"], dxo, st["m"],
                                    _vecs([norm_g_f[i, 2], sc2, g1, norm_g_f[i, 1], st["m_bias"]], D))
        p2, e1 = s21[0:3], s21[3:6]
        if kind == 0:
            dz = _mm_nt(f"sc_out_dx_{i}", dm, *w_out_f[j],False, BF)
            z, dp, ssum = _sc_act_bwd(f"sc_act_bwd_{i}", st["p"], dz, _vecs(list(sc_conv_f[j]), D))
            g_w_out[j] = _mm_tn(f"sc_out_dw_{i}", z, dm, False)
            dh = _mm_nt(f"sc_in_dx_{i}", dp, *w_in_f[j],True, F32)
            g_w_in[j] = _mm_tn(f"sc_in_dw_{i}", st["h"], dp, True)
            d_sc_conv[j] = ssum[0:3]
        elif kind == 1:
            dh, g_pool, psum = _pool_bwd(f"pool_bwd_{i}", st["pooled"], dm, pool_f, _vecs([pool_b[j], pool_scale[j]], D))
        else:
            dwact = _mm_nt(f"cf_pw2_dx_{i}", dm, *pw2_f,False, BF)
            g_pw2 = _mm_tn(f"cf_pw2_dw_{i}", st["wact"], dm, False)
            dp, tsum, csum = _cf_act_bwd(f"cf_act_bwd_{i}", st["p"], st["cv"], dwact, st["taps"], st["cf_vec"])
            dh = _mm_nt(f"cf_pw1_dx_{i}", dp, *pw1_f,True, F32)
            g_pw1 = _mm_tn(f"cf_pw1_dw_{i}", st["h"], dp, True)
            d_cf = dict(b_pw1=jnp.concatenate([csum[3], csum[4]])[None], w_dw=tsum[None, :CF_TAPS], b_dw=csum[0:1],
                        ln_g=csum[1:2], ln_b=csum[2:3], b_pw2=e1[2:3])
        if i > 0:
            prev = i - 1
            dxo, df_prev, s12 = _pro_epi_bwd(f"pro1_epi2_bwd_{i}", dh, st["x0"], dx1, saved[prev]["f"],
                                             _vecs([norm_g_f[i, 0], sc1, mods[prev][5], norm_g_f[prev, 3], zero_d], D))
            p1, pend = s12[0:3], (df_prev, s12[3:6])
        else:
            dxo, p1 = _pro_bwd(f"pro1_bwd_{i}", dh, st["x0"], dx1, _vecs([norm_g_f[i, 0], sc1], D))
        dmod[i] = jnp.concatenate([p1[0], p1[1], e1[0], p2[0], p2[1], e2[0]])
        d_norm_g[i] = jnp.stack([p1[2], e1[1], p2[2], e2[1]])
        if kind == 1:
            d_pool_b, d_pool_scale = psum[0:1], psum[1:2]
        if i == 1:
            rest_pieces = _grad_exchange(
                "grad_exchange_rest",
                [halves([g_w_in[1]]), halves([g_w_out[1]]), halves(g_pool.astype(BF)), halves([g_pw1]), halves([g_pw2]),
                 halves(g_w_up[1:]), halves(g_w_down[1:])], GRAD_REST_ID)
        saved[i] = None
        st = None
        if i == 0:
            grad_x = dxo[None]
        d_ffn_conv[i], d_ffn_b[i] = fsum[1:4], fsum[0]

    small_shapes = [(DEPTH, 6 * D), (DEPTH, 4, D), (2, 3, D), (1, D), (1, D), (1, 2 * D), (1, CF_TAPS, D), (1, D),
                    (1, D), (1, D), (1, D), (DEPTH, 3, F), (DEPTH, F)]
    small = [jnp.stack(dmod), jnp.stack(d_norm_g), jnp.stack(d_sc_conv), d_pool_b, d_pool_scale, d_cf["b_pw1"],
             d_cf["w_dw"], d_cf["b_dw"], d_cf["ln_g"], d_cf["ln_b"], d_cf["b_pw2"], jnp.stack(d_ffn_conv),
             jnp.stack(d_ffn_b)]
    gsl, ssl = _allgather_small("reduce_small_grads", _pack(small), True)
    tot = _unpack(ssl, small_shapes)
    (gb_mod, gt_norm_g, gt_sc_conv, g_pool_b, g_pool_scale, gt_b_pw1, gt_w_dw, gt_b_dw, gt_ln_g, gt_ln_b, gt_b_pw2,
     gt_ffn_conv, g_ffn_b) = tot
    dmod_all = jnp.stack([_unpack(gsl.reshape(N_DEV, -1, 128)[d], small_shapes[:1])[0] for d in range(N_DEV)], axis=1)
    dmod_sh = jnp.pad(_shard_last(dmod_all, chip, n_mod), ((0, 0), (0, 8), (0, 0)))
    g_w_mod = _mod_bwd(c16, dmod_sh)
    g_norm_g = _shard_last(gt_norm_g, chip, D // 4)
    g_sc_conv = _shard_last(gt_sc_conv, chip, D // 4)
    g_b_pw1 = _shard_last(gt_b_pw1, chip, 2 * D // 4)
    g_w_dw = _shard_last(gt_w_dw, chip, D // 4)
    g_b_dw = _shard_last(gt_b_dw, chip, D // 4)
    g_ln_g = _shard_last(gt_ln_g, chip, D // 4)
    g_ln_b = _shard_last(gt_ln_b, chip, D // 4)
    g_b_pw2 = _shard_last(gt_b_pw2, chip, D // 4)
    g_ffn_conv = _shard_last(gt_ffn_conv, chip, F // 4)

    rest_reduced = [_add_slots(f"grad_sum_rest_{q}", p) for q, p in enumerate(rest_pieces)]
    r_in, r_out, r_pool, r_pw1, r_pw2, r_up, r_down = _pair_share("grad_pair_share_rest", rest_reduced, SHARE_REST_ID)
    mixer0_pieces = _grad_exchange("grad_exchange", [after(halves([g_w_in[0]]), rest_reduced[0]), halves([g_w_out[0]])])
    first_reduced = [_add_slots(f"grad_sum_{q}", p) for q, p in enumerate(list(mixer0_pieces) + list(ffn0_pieces))]
    f_in, f_out, f_up, f_down = _pair_share("grad_pair_share", first_reduced)
    cat = lambda a, b, like: jnp.concatenate([a, b], axis=0).reshape(like.shape)
    g_sc_w_in, g_sc_w_out = cat(f_in, r_in, sc_w_in), cat(f_out, r_out, sc_w_out)
    g_ffn_w_up, g_ffn_w_down = cat(f_up, r_up, ffn_w_up), cat(f_down, r_down, ffn_w_down)
    g_pool_w, g_cf_w_pw1, g_cf_w_pw2 = (r_pool.reshape(pool_w.shape), r_pw1.reshape(cf_w_pw1.shape),
                                        r_pw2.reshape(cf_w_pw2.shape))

    def adam_big(name, w, g, m, v):
        shp = w.shape
        two = lambda t: t.reshape(-1, shp[-1])
        return [o.reshape(shp) for o in _adamw(name, two(w), two(g), two(m), two(v))]

    grads = dict(w_mod=g_w_mod, b_mod=gb_mod, norm_g=g_norm_g, sc_w_in=g_sc_w_in, sc_conv=g_sc_conv,
                 sc_w_out=g_sc_w_out, pool_w=g_pool_w, pool_b=g_pool_b, pool_scale=g_pool_scale,
                 cf_w_pw1=g_cf_w_pw1, cf_b_pw1=g_b_pw1, cf_w_dw=g_w_dw, cf_b_dw=g_b_dw, cf_ln_g=g_ln_g,
                 cf_ln_b=g_ln_b, cf_w_pw2=g_cf_w_pw2, cf_b_pw2=g_b_pw2, ffn_w_up=g_ffn_w_up, ffn_conv=g_ffn_conv,
                 ffn_b_conv=g_ffn_b, ffn_w_down=g_ffn_w_down)
    weights = dict(w_mod=w_mod, b_mod=b_mod, norm_g=norm_g, sc_w_in=sc_w_in, sc_conv=sc_conv, sc_w_out=sc_w_out,
                   pool_w=pool_w, pool_b=pool_b, pool_scale=pool_scale, cf_w_pw1=cf_w_pw1, cf_b_pw1=cf_b_pw1,
                   cf_w_dw=cf_w_dw, cf_b_dw=cf_b_dw, cf_ln_g=cf_ln_g, cf_ln_b=cf_ln_b, cf_w_pw2=cf_w_pw2,
                   cf_b_pw2=cf_b_pw2, ffn_w_up=ffn_w_up, ffn_conv=ffn_conv, ffn_b_conv=ffn_b_conv,
                   ffn_w_down=ffn_w_down)
    m_in = dict(w_mod=m_w_mod, b_mod=m_b_mod, norm_g=m_norm_g, sc_w_in=m_sc_w_in, sc_conv=m_sc_conv,
                sc_w_out=m_sc_w_out, pool_w=m_pool_w, pool_b=m_pool_b, pool_scale=m_pool_scale,
                cf_w_pw1=m_cf_w_pw1, cf_b_pw1=m_cf_b_pw1, cf_w_dw=m_cf_w_dw, cf_b_dw=m_cf_b_dw, cf_ln_g=m_cf_ln_g,
                cf_ln_b=m_cf_ln_b, cf_w_pw2=m_cf_w_pw2, cf_b_pw2=m_cf_b_pw2, ffn_w_up=m_ffn_w_up,
                ffn_conv=m_ffn_conv, ffn_b_conv=m_ffn_b_conv, ffn_w_down=m_ffn_w_down)
    v_in = dict(w_mod=v_w_mod, b_mod=v_b_mod, norm_g=v_norm_g, sc_w_in=v_sc_w_in, sc_conv=v_sc_conv,
                sc_w_out=v_sc_w_out, pool_w=v_pool_w, pool_b=v_pool_b, pool_scale=v_pool_scale,
                cf_w_pw1=v_cf_w_pw1, cf_b_pw1=v_cf_b_pw1, cf_w_dw=v_cf_w_dw, cf_b_dw=v_cf_b_dw, cf_ln_g=v_cf_ln_g,
                cf_ln_b=v_cf_ln_b, cf_w_pw2=v_cf_w_pw2, cf_b_pw2=v_cf_b_pw2, ffn_w_up=v_ffn_w_up,
                ffn_conv=v_ffn_conv, ffn_b_conv=v_ffn_b_conv, ffn_w_down=v_ffn_w_down)
    names = list(weights)
    big_names = ["w_mod", "sc_w_in", "sc_w_out", "pool_w", "cf_w_pw1", "cf_w_pw2", "ffn_w_up", "ffn_w_down"]
    small_names = [n for n in names if n not in big_names]
    delta, new_m, new_v = {}, {}, {}
    for n in big_names:
        delta[n], new_m[n], new_v[n] = adam_big(f"adamw_{n}", weights[n], grads[n], m_in[n], v_in[n])
    grads = {n: grads[n].reshape(weights[n].shape) for n in names}
    sm_shapes = [weights[n].shape for n in small_names]
    sd, sm, sv = _adamw("adamw_small", _pack([weights[n] for n in small_names]), _pack([grads[n] for n in small_names]),
                        _pack([m_in[n] for n in small_names]), _pack([v_in[n] for n in small_names]))
    for n, d_, m_, v_ in zip(small_names, _unpack(sd, sm_shapes), _unpack(sm, sm_shapes), _unpack(sv, sm_shapes)):
        delta[n], new_m[n], new_v[n] = d_, m_, v_

    return (loss, grad_x, *[grads[n] for n in names], *[delta[n] for n in names], *[new_m[n] for n in names],
            *[new_v[n] for n in names])
```

```python
import itertools

import jax
import jax.numpy as jnp
from jax import lax
from jax.experimental import pallas as pl
from jax.experimental.pallas import tpu as pltpu
from jax.experimental.pallas import tpu_sc as plsc

D = 1024
F = 2816
DEPTH = 4
POOL_WINDOWS = (2, 4, 8, 16)
POOL_GROUP = 256
CF_TAPS = 31
RMS_EPS = 1e-6
LN_EPS = 1e-5
ADAM_LR = 0.001
ADAM_B1 = 0.9
ADAM_B2 = 0.999
ADAM_EPS = 1e-08
ADAM_WD = 0.01
ADAM_STEP = 10

BF = jnp.bfloat16
F32 = jnp.float32
MESH = pl.DeviceIdType.MESH
SDS = jax.ShapeDtypeStruct
N_CHIPS = 4
N_DEV = 8
GATHER_FIRST_ID, GATHER_REST_ID, GRAD_REST_ID, GRAD_FFN0_ID, SHARE_REST_ID = 1, 2, 3, 4, 5
VMEM_LIMIT_MB = 56
ROW_TILE_D = 256
ROW_TILE_F = 256
HALO = 8
HALO_BF = 16
CF_HALO = 32
POOL_HALO = 16


def _cp(n_axes):
    return pltpu.CompilerParams(dimension_semantics=("arbitrary",) * n_axes,
                                vmem_limit_bytes=VMEM_LIMIT_MB * 1024 * 1024)


def _tile(n, pref):
    t = min(n, pref)
    assert n % t == 0, (n, pref)
    return t


def _const(shape):
    nd = len(shape)
    return pl.BlockSpec(shape, lambda *_: (0,) * nd)


def _vecs(rows, width):
    v = jnp.stack([r.reshape(width).astype(F32) for r in rows])
    pad = (-v.shape[0]) % 8
    return jnp.pad(v, ((0, pad), (0, 0)))


def _sigmoid(v):
    return 0.5 * jnp.tanh(0.5 * v) + 0.5


def _down(prev8, g, k):
    n = g.shape[0]
    return pltpu.roll(jnp.concatenate([prev8, g], axis=0), k, 0)[8:8 + n]


def _up(g, next8, k):
    n = g.shape[0]
    return pltpu.roll(jnp.concatenate([g, next8], axis=0), n + 8 - k, 0)[0:n]


def _fold8(v):
    acc = v[0:8]
    for r in range(8, v.shape[0], 8):
        acc = acc + v[r:r + 8]
    return acc


def _mm(name, a, b, out_sds, grid, a_spec, b_spec, o_spec, acc_shape, dn):
    nk = grid[2]

    def body(a_ref, b_ref, o_ref, *acc):
        part = lax.dot_general(a_ref[...], b_ref[...], dn, preferred_element_type=F32)
        if nk == 1:
            o_ref[...] = part.astype(o_ref.dtype)
        else:
            acc_ref = acc[0]
            k = pl.program_id(2)

            @pl.when(k == 0)
            def _():
                acc_ref[...] = part

            @pl.when(k > 0)
            def _():
                acc_ref[...] += part

            @pl.when(k == nk - 1)
            def _():
                o_ref[...] = acc_ref[...].astype(o_ref.dtype)

    scratch = [] if nk == 1 else [pltpu.VMEM(acc_shape, F32)]
    return pl.pallas_call(body, name=name, out_shape=out_sds, grid=grid, in_specs=[a_spec, b_spec],
                          out_specs=o_spec, scratch_shapes=scratch, compiler_params=_cp(3))(a, b)


NN = (((1,), (0,)), ((), ()))
NT = (((1,), (1,)), ((), ()))
TN = (((0,), (0,)), ((), ()))


def _mm_nn(name, a, w, layer, col_sharded, out_dtype, tm=1024, tn=None):
    m, k = a.shape
    tm = _tile(m, tm)
    if col_sharded:
        n4 = w.shape[3]
        tn = n4 if tn is None else tn
        tpc = n4 // tn
        n = N_CHIPS * n4
        b_spec = pl.BlockSpec((None, None, k, tn), lambda i, j, kk: (layer, j // tpc, 0, j % tpc))
    else:
        n = w.shape[2]
        tn = n if tn is None else tn
        b_spec = pl.BlockSpec((None, k, tn), lambda i, j, kk: (layer, 0, j))
    return _mm(name, a, w, SDS((m, n), out_dtype), (m // tm, n // tn, 1),
               pl.BlockSpec((tm, k), lambda i, j, kk: (i, 0)), b_spec,
               pl.BlockSpec((tm, tn), lambda i, j, kk: (i, j)), None, NN)


def _mm_nt(name, g, w, layer, col_sharded, out_dtype, tm=1024, tn=None):
    m, n = g.shape
    if col_sharded:
        return _mm_nt_staged(name, g, w, layer, out_dtype)
    tm = _tile(m, tm)
    kdim = w.shape[1]
    tn = kdim if tn is None else tn
    return _mm(name, g, w, SDS((m, kdim), out_dtype), (m // tm, kdim // tn, 1),
               pl.BlockSpec((tm, n), lambda i, j, kk: (i, 0)),
               pl.BlockSpec((None, tn, n), lambda i, j, kk: (layer, j, 0)),
               pl.BlockSpec((tm, tn), lambda i, j, kk: (i, j)), None, NT)


def _mm_nt_staged(name, g, w, layer, out_dtype, tm=512):
    m, n = g.shape
    kdim, n4 = w.shape[2], w.shape[3]
    tm = _tile(m, tm)

    def body(g_ref, w_hbm, o_ref, wbuf, sems):
        @pl.when(pl.program_id(0) == 0)
        def _():
            cps = [pltpu.make_async_copy(w_hbm.at[layer, j], wbuf.at[:, pl.ds(j * n4, n4)], sems.at[j])
                   for j in range(N_CHIPS)]
            for cp in cps:
                cp.start()
            for cp in cps:
                cp.wait()

        o_ref[...] = lax.dot_general(g_ref[...], wbuf[...], NT, preferred_element_type=F32).astype(o_ref.dtype)

    return pl.pallas_call(body, name=name, out_shape=SDS((m, kdim), out_dtype), grid=(m // tm,),
                          in_specs=[pl.BlockSpec((tm, n), lambda i: (i, 0)), pl.BlockSpec(memory_space=pltpu.HBM)],
                          out_specs=pl.BlockSpec((tm, kdim), lambda i: (i, 0)),
                          scratch_shapes=[pltpu.VMEM((kdim, n), BF), pltpu.SemaphoreType.DMA((N_CHIPS,))],
                          compiler_params=_cp(1))(g, w)


def _mm_tn(name, a, g, col_sharded, tk=None, tn=None, ts=2048):
    s, k = a.shape
    n = g.shape[1]
    ts = _tile(s, ts)
    tk = k if tk is None else tk
    if col_sharded:
        n4 = n // N_CHIPS
        tn = n4 if tn is None else tn
        tpc = n4 // tn
        out_sds = SDS((N_CHIPS, k, n4), BF)
        o_spec = pl.BlockSpec((None, tk, tn), lambda i, j, ss: (j // tpc, i, j % tpc))
    else:
        tn = n if tn is None else tn
        out_sds = SDS((k, n), BF)
        o_spec = pl.BlockSpec((tk, tn), lambda i, j, ss: (i, j))
    return _mm(name, a, g, out_sds, (k // tk, n // tn, s // ts),
               pl.BlockSpec((ts, tk), lambda i, j, ss: (ss, i)),
               pl.BlockSpec((ts, tn), lambda i, j, ss: (ss, j)), o_spec, (tk, tn), TN)


def _rows(tm, width, col=0):
    return pl.BlockSpec((tm, width), lambda i: (i, col))


def _pro_fwd(name, x, vec):
    s = x.shape[0]
    tm = _tile(s, ROW_TILE_D)

    def body(x_ref, v_ref, h_ref):
        xv = x_ref[...]
        r = lax.rsqrt(jnp.mean(xv * xv, axis=-1, keepdims=True) + RMS_EPS)
        a = v_ref[0:1, :] * (1.0 + v_ref[1:2, :])
        h_ref[...] = (xv * r * a + v_ref[2:3, :]).astype(h_ref.dtype)

    return pl.pallas_call(body, name=name, out_shape=SDS((s, D), BF), grid=(s // tm,),
                          in_specs=[_rows(tm, D), _const((8, D))], out_specs=_rows(tm, D),
                          compiler_params=_cp(1))(x, vec)


def _epi_pro_fwd(name, x, m, vec, with_next):
    s = x.shape[0]
    tm = _tile(s, ROW_TILE_D)

    def body(x_ref, m_ref, v_ref, o_ref, *h_ref):
        mv = m_ref[...] + v_ref[2:3, :]
        rm = lax.rsqrt(jnp.mean(mv * mv, axis=-1, keepdims=True) + RMS_EPS)
        xo = x_ref[...] + v_ref[0:1, :] * (mv * rm * v_ref[1:2, :])
        o_ref[...] = xo
        if with_next:
            r = lax.rsqrt(jnp.mean(xo * xo, axis=-1, keepdims=True) + RMS_EPS)
            a = v_ref[3:4, :] * (1.0 + v_ref[4:5, :])
            h_ref[0][...] = (xo * r * a + v_ref[5:6, :]).astype(BF)

    out_shape = (SDS((s, D), F32),) + ((SDS((s, D), BF),) if with_next else ())
    res = pl.pallas_call(body, name=name, out_shape=out_shape, grid=(s // tm,),
                         in_specs=[_rows(tm, D), _rows(tm, D), _const((8, D))],
                         out_specs=tuple([_rows(tm, D)] * len(out_shape)), compiler_params=_cp(1))(x, m, vec)
    return res if with_next else (res[0], None)


def _loss_fwd_bwd(y, t):
    s = y.shape[0]
    tm = _tile(s, ROW_TILE_D)

    def body(y_ref, t_ref, dy_ref, acc_ref):
        @pl.when(pl.program_id(0) == 0)
        def _():
            acc_ref[...] = jnp.zeros_like(acc_ref)

        e = y_ref[...] - t_ref[...]
        dy_ref[...] = e * (1.0 / D)
        acc_ref[0:1, :] += jnp.sum(e * e, axis=0, keepdims=True) * (0.5 / D)

    return pl.pallas_call(body, name="loss", out_shape=(SDS((s, D), F32), SDS((8, D), F32)), grid=(s // tm,),
                          in_specs=[_rows(tm, D), _rows(tm, D)], out_specs=(_rows(tm, D), _const((8, D))),
                          compiler_params=_cp(1))(y, t)


def _epi_bwd_rows(dv, mv, g, ngb, dm_ref, acc_ref, row):
    rm = lax.rsqrt(jnp.mean(mv * mv, axis=-1, keepdims=True) + RMS_EPS)
    mn = mv * rm
    dmn = dv * (g * ngb)
    dm = rm * (dmn - mn * jnp.mean(dmn * mn, axis=-1, keepdims=True))
    dm_ref[...] = dm.astype(dm_ref.dtype)
    t = jnp.sum(dv * mn, axis=0, keepdims=True)
    acc_ref[row:row + 1, :] += t * ngb
    acc_ref[row + 1:row + 2, :] += t * g
    acc_ref[row + 2:row + 3, :] += jnp.sum(dm, axis=0, keepdims=True)


def _epi_bwd(name, dxo, m, vec):
    s = dxo.shape[0]
    tm = _tile(s, ROW_TILE_D)

    def body(d_ref, m_ref, v_ref, dm_ref, acc_ref):
        @pl.when(pl.program_id(0) == 0)
        def _():
            acc_ref[...] = jnp.zeros_like(acc_ref)

        _epi_bwd_rows(d_ref[...], m_ref[...] + v_ref[2:3, :], v_ref[0:1, :], v_ref[1:2, :], dm_ref, acc_ref, 0)

    return pl.pallas_call(body, name=name, out_shape=(SDS((s, D), BF), SDS((8, D), F32)), grid=(s // tm,),
                          in_specs=[_rows(tm, D), _rows(tm, D), _const((8, D))],
                          out_specs=(_rows(tm, D), _const((8, D))), compiler_params=_cp(1))(dxo, m, vec)


def _pro_epi_bwd(name, dh, x, dxo, m, vec):
    s = x.shape[0]
    tm = _tile(s, ROW_TILE_D)

    def body(dh_ref, x_ref, d_ref, m_ref, v_ref, dx_ref, dm_ref, acc_ref):
        @pl.when(pl.program_id(0) == 0)
        def _():
            acc_ref[...] = jnp.zeros_like(acc_ref)

        dx = _pro_bwd_rows(dh_ref[...].astype(F32), x_ref[...], d_ref[...], v_ref[0:1, :], v_ref[1:2, :], acc_ref)
        dx_ref[...] = dx
        _epi_bwd_rows(dx, m_ref[...] + v_ref[4:5, :], v_ref[2:3, :], v_ref[3:4, :], dm_ref, acc_ref, 3)

    return pl.pallas_call(body, name=name, out_shape=(SDS((s, D), F32), SDS((s, D), BF), SDS((8, D), F32)),
                          grid=(s // tm,),
                          in_specs=[_rows(tm, D), _rows(tm, D), _rows(tm, D), _rows(tm, D), _const((8, D))],
                          out_specs=(_rows(tm, D), _rows(tm, D), _const((8, D))),
                          compiler_params=_cp(1))(dh, x, dxo, m, vec)


def _pro_bwd_rows(dhv, xv, dxo, ng, sc, acc_ref):
    sc1 = 1.0 + sc
    r = lax.rsqrt(jnp.mean(xv * xv, axis=-1, keepdims=True) + RMS_EPS)
    xn = xv * r
    dxn = dhv * (ng * sc1)
    t = jnp.sum(dhv * xn, axis=0, keepdims=True)
    acc_ref[0:1, :] += jnp.sum(dhv, axis=0, keepdims=True)
    acc_ref[1:2, :] += t * ng
    acc_ref[2:3, :] += t * sc1
    return dxo + r * (dxn - xn * jnp.mean(dxn * xn, axis=-1, keepdims=True))


def _pro_bwd(name, dh, x, dxo, vec):
    s = x.shape[0]
    tm = _tile(s, ROW_TILE_D)

    def body(dh_ref, x_ref, d_ref, v_ref, dx_ref, acc_ref):
        @pl.when(pl.program_id(0) == 0)
        def _():
            acc_ref[...] = jnp.zeros_like(acc_ref)

        dx_ref[...] = _pro_bwd_rows(dh_ref[...].astype(F32), x_ref[...], d_ref[...], v_ref[0:1, :], v_ref[1:2, :],
                                    acc_ref)

    return pl.pallas_call(body, name=name, out_shape=(SDS((s, D), F32), SDS((8, D), F32)), grid=(s // tm,),
                          in_specs=[_rows(tm, D), _rows(tm, D), _rows(tm, D), _const((8, D))],
                          out_specs=(_rows(tm, D), _const((8, D))), compiler_params=_cp(1))(dh, x, dxo, vec)


def _carry_up(buf, tm, first, halo=HALO):
    @pl.when(first)
    def _():
        buf[tm:tm + halo, :] = jnp.zeros((halo, buf.shape[1]), F32)

    @pl.when(jnp.logical_not(first))
    def _():
        buf[tm:tm + halo, :] = buf[0:halo, :]


def _ffn_act_fwd(name, up, vec):
    s = up.shape[0]
    tm = _tile(s, ROW_TILE_F)
    rb_n = _tile(tm, 64)
    cw = 256

    def body(g_ref, v_ref, w_ref, a_ref, carry):
        @pl.when(pl.program_id(0) == 0)
        def _():
            carry[...] = jnp.zeros_like(carry)

        for cc in range(F // cw):
            cols = slice(cc * cw, (cc + 1) * cw)
            w0, w1, w2, b = w_ref[0:1, cols], w_ref[1:2, cols], w_ref[2:3, cols], w_ref[3:4, cols]

            def rb_body(rb, prev8):
                rows = pl.ds(pl.multiple_of(rb * rb_n, rb_n), rb_n)
                g = g_ref[rows, cols].astype(F32)
                gc = w2 * g + w1 * _down(prev8, g, 1) + w0 * _down(prev8, g, 2) + b
                a_ref[rows, cols] = (gc * _sigmoid(gc) * v_ref[rows, cols].astype(F32)).astype(a_ref.dtype)
                return g[rb_n - 8:rb_n]

            carry[:, cols] = lax.fori_loop(0, tm // rb_n, rb_body, carry[:, cols])

    return pl.pallas_call(body, name=name, out_shape=SDS((s, F), BF), grid=(s // tm,),
                          in_specs=[_rows(tm, F, 0), _rows(tm, F, 1), _const((8, F))], out_specs=_rows(tm, F),
                          scratch_shapes=[pltpu.VMEM((HALO, F), F32)], compiler_params=_cp(1))(up, up, vec)


def _prev_halo_spec(nt, tm, width, col):
    per = tm // HALO_BF
    return pl.BlockSpec((HALO_BF, width), lambda i: (jnp.maximum((nt - 1 - i) * per - 1, 0), col))


def _ffn_act_bwd(name, up, da, vec):
    s = up.shape[0]
    tm = _tile(s, ROW_TILE_F)
    nt = s // tm
    rev = lambda col: pl.BlockSpec((tm, F), lambda i: (nt - 1 - i, col))

    rb_n = _tile(tm, 64)
    nrb = tm // rb_n
    cw = 128

    def body(g_ref, gp_ref, v_ref, da_ref, w_ref, dup_ref, acc_ref, carry):
        step = pl.program_id(0)

        @pl.when(step == 0)
        def _():
            acc_ref[...] = jnp.zeros_like(acc_ref)
            carry[...] = jnp.zeros_like(carry)

        for cc in range(F // cw):
            cols = slice(cc * cw, (cc + 1) * cw)
            w0, w1, w2, b = w_ref[0:1, cols], w_ref[1:2, cols], w_ref[2:3, cols], w_ref[3:4, cols]
            halo = jnp.where(step < nt - 1, gp_ref[:, cols].astype(F32)[HALO_BF - 8:HALO_BF], 0.0)

            def rb_body(it, st):
                nxt8, ab, a0, a1, a2 = st
                rb = nrb - 1 - it
                r0 = pl.multiple_of(rb * rb_n, rb_n)
                rows = pl.ds(r0, rb_n)
                g = g_ref[rows, cols].astype(F32)
                ra = pl.multiple_of(jnp.maximum(r0 - HALO_BF, 0), HALO_BF)
                above = g_ref[pl.ds(ra, HALO_BF), cols].astype(F32)[HALO_BF - 8:HALO_BF]
                prev8 = jnp.where(rb == 0, halo, above)
                g1 = _down(prev8, g, 1)
                g2 = _down(prev8, g, 2)
                gc = w2 * g + w1 * g1 + w0 * g2 + b
                sg = _sigmoid(gc)
                sl = gc * sg
                val = v_ref[rows, cols].astype(F32)
                dav = da_ref[rows, cols].astype(F32)
                dup_ref[rows, F + cc * cw:F + (cc + 1) * cw] = (dav * sl).astype(dup_ref.dtype)
                dgc = (dav * val) * (sg + sl * (1.0 - sg))
                dup_ref[rows, cols] = (w2 * dgc + w1 * _up(dgc, nxt8, 1) + w0 * _up(dgc, nxt8, 2)).astype(dup_ref.dtype)
                return (dgc[0:8], ab + _fold8(dgc), a0 + _fold8(g2 * dgc), a1 + _fold8(g1 * dgc),
                        a2 + _fold8(g * dgc))

            z = jnp.zeros((8, cw), F32)
            nxt8, ab, a0, a1, a2 = lax.fori_loop(0, nrb, rb_body, (carry[:, cols], z, z, z, z))
            carry[:, cols] = nxt8
            acc_ref[0:1, cols] += jnp.sum(ab, axis=0, keepdims=True)
            acc_ref[1:2, cols] += jnp.sum(a0, axis=0, keepdims=True)
            acc_ref[2:3, cols] += jnp.sum(a1, axis=0, keepdims=True)
            acc_ref[3:4, cols] += jnp.sum(a2, axis=0, keepdims=True)

    return pl.pallas_call(
        body, name=name, out_shape=(SDS((s, 2 * F), BF), SDS((8, F), F32)), grid=(nt,),
        in_specs=[rev(0), _prev_halo_spec(nt, tm, F, 0), rev(1), rev(0), _const((8, F))],
        out_specs=(pl.BlockSpec((tm, 2 * F), lambda i: (nt - 1 - i, 0)), _const((8, F))),
        scratch_shapes=[pltpu.VMEM((HALO, F), F32)], compiler_params=_cp(1))(up, up, up, da, vec)


def _sc_act_fwd(name, p, vec):
    s = p.shape[0]
    tm = _tile(s, ROW_TILE_D)

    rb_n = _tile(tm, 64)
    cw = 256

    def body(b_ref, c_ref, h_ref, w_ref, z_ref, carry):
        @pl.when(pl.program_id(0) == 0)
        def _():
            carry[...] = jnp.zeros_like(carry)

        for cc in range(D // cw):
            cols = slice(cc * cw, (cc + 1) * cw)
            w0, w1, w2 = w_ref[0:1, cols], w_ref[1:2, cols], w_ref[2:3, cols]

            def rb_body(rb, prev8):
                rows = pl.ds(pl.multiple_of(rb * rb_n, rb_n), rb_n)
                q = c_ref[rows, cols].astype(F32) * h_ref[rows, cols].astype(F32)
                u = w2 * q + w1 * _down(prev8, q, 1) + w0 * _down(prev8, q, 2)
                z_ref[rows, cols] = (b_ref[rows, cols].astype(F32) * u).astype(z_ref.dtype)
                return q[rb_n - 8:rb_n]

            carry[:, cols] = lax.fori_loop(0, tm // rb_n, rb_body, carry[:, cols])

    return pl.pallas_call(body, name=name, out_shape=SDS((s, D), BF), grid=(s // tm,),
                          in_specs=[_rows(tm, D, 0), _rows(tm, D, 1), _rows(tm, D, 2), _const((8, D))],
                          out_specs=_rows(tm, D), scratch_shapes=[pltpu.VMEM((HALO, D), F32)],
                          compiler_params=_cp(1))(p, p, p, vec)


def _sc_act_bwd(name, p, dz, vec):
    s = p.shape[0]
    tm = _tile(s, ROW_TILE_D)
    nt = s // tm
    rev = lambda col: pl.BlockSpec((tm, D), lambda i: (nt - 1 - i, col))

    rb_n = _tile(tm, 64)
    nrb = tm // rb_n
    cw = 128

    def body(b_ref, c_ref, h_ref, cp_ref, hp_ref, dz_ref, w_ref, z_ref, dp_ref, acc_ref, carry):
        step = pl.program_id(0)

        @pl.when(step == 0)
        def _():
            acc_ref[...] = jnp.zeros_like(acc_ref)
            carry[...] = jnp.zeros_like(carry)

        for cc in range(D // cw):
            cols = slice(cc * cw, (cc + 1) * cw)
            w0, w1, w2 = w_ref[0:1, cols], w_ref[1:2, cols], w_ref[2:3, cols]
            halo = jnp.where(step < nt - 1,
                             (cp_ref[:, cols].astype(F32) * hp_ref[:, cols].astype(F32))[HALO_BF - 8:HALO_BF], 0.0)

            def rb_body(it, st):
                nxt8, a0, a1, a2 = st
                rb = nrb - 1 - it
                r0 = pl.multiple_of(rb * rb_n, rb_n)
                rows = pl.ds(r0, rb_n)
                cg = c_ref[rows, cols].astype(F32)
                hin = h_ref[rows, cols].astype(F32)
                bg = b_ref[rows, cols].astype(F32)
                q = cg * hin
                up_rows = pl.ds(pl.multiple_of(jnp.maximum(r0 - HALO_BF, 0), HALO_BF), HALO_BF)
                above = (c_ref[up_rows, cols].astype(F32) * h_ref[up_rows, cols].astype(F32))[HALO_BF - 8:HALO_BF]
                prev8 = jnp.where(rb == 0, halo, above)
                q1 = _down(prev8, q, 1)
                q2 = _down(prev8, q, 2)
                u = w2 * q + w1 * q1 + w0 * q2
                dzv = dz_ref[rows, cols].astype(F32)
                z_ref[rows, cols] = (bg * u).astype(z_ref.dtype)
                dp_ref[rows, cols] = (dzv * u).astype(dp_ref.dtype)
                du = dzv * bg
                dq = w2 * du + w1 * _up(du, nxt8, 1) + w0 * _up(du, nxt8, 2)
                dp_ref[rows, D + cc * cw:D + (cc + 1) * cw] = (dq * hin).astype(dp_ref.dtype)
                dp_ref[rows, 2 * D + cc * cw:2 * D + (cc + 1) * cw] = (dq * cg).astype(dp_ref.dtype)
                return du[0:8], a0 + _fold8(q2 * du), a1 + _fold8(q1 * du), a2 + _fold8(q * du)

            z = jnp.zeros((8, cw), F32)
            nxt8, a0, a1, a2 = lax.fori_loop(0, nrb, rb_body, (carry[:, cols], z, z, z))
            carry[:, cols] = nxt8
            acc_ref[0:1, cols] += jnp.sum(a0, axis=0, keepdims=True)
            acc_ref[1:2, cols] += jnp.sum(a1, axis=0, keepdims=True)
            acc_ref[2:3, cols] += jnp.sum(a2, axis=0, keepdims=True)

    return pl.pallas_call(
        body, name=name, out_shape=(SDS((s, D), BF), SDS((s, 3 * D), BF), SDS((8, D), F32)), grid=(nt,),
        in_specs=[rev(0), rev(1), rev(2), _prev_halo_spec(nt, tm, D, 1), _prev_halo_spec(nt, tm, D, 2), rev(0),
                  _const((8, D))],
        out_specs=(rev(0), pl.BlockSpec((tm, 3 * D), lambda i: (nt - 1 - i, 0)), _const((8, D))),
        scratch_shapes=[pltpu.VMEM((HALO, D), F32)], compiler_params=_cp(1))(p, p, p, p, p, dz, vec)


CF_ROW_BLOCK = 32
CF_LANES = 128


def _tap_conv_block(ext, rb_n, tap_of_offset, t_ref, cols, init, u=None, accs=None):
    n = ext.shape[0]
    out = init
    for b in range(8):
        rolled = ext if b == 0 else pltpu.roll(ext, n - b, 0)
        for a in range(n // 8):
            k = tap_of_offset(8 * a + b)
            if k is None:
                continue
            sl = rolled[8 * a:8 * a + rb_n]
            out = out + t_ref[k:k + 1, cols] * sl
            if accs is not None:
                accs[k] = accs[k] + _fold8(u * sl)
    return out


def _cf_act_fwd(name, p, taps, vec):
    s = p.shape[0]
    tm = _tile(s, ROW_TILE_D)
    rb_n = _tile(tm, CF_ROW_BLOCK)
    base = CF_HALO - (CF_TAPS - 1)
    tap_of = lambda off: off - base if 0 <= off - base < CF_TAPS else None

    def body(a_ref, g_ref, t_ref, v_ref, w_ref, cv_ref, buf):
        first = pl.program_id(0) == 0

        @pl.when(first)
        def _():
            buf[0:CF_HALO, :] = jnp.zeros((CF_HALO, D), F32)

        @pl.when(jnp.logical_not(first))
        def _():
            buf[0:CF_HALO, :] = buf[tm:tm + CF_HALO, :]

        a = a_ref[...].astype(F32) + v_ref[0:1, :]
        g = g_ref[...].astype(F32) + v_ref[1:2, :]
        buf[CF_HALO:CF_HALO + tm, :] = a * _sigmoid(g)
        for ci in range(D // CF_LANES):
            cols = slice(ci * CF_LANES, (ci + 1) * CF_LANES)

            def rb_body(rb, carry):
                r0 = pl.multiple_of(rb * rb_n, rb_n)
                ext = buf[pl.ds(r0, rb_n + CF_HALO), cols]
                init = jnp.zeros((rb_n, CF_LANES), F32) + v_ref[2:3, cols]
                cv_ref[pl.ds(r0, rb_n), cols] = _tap_conv_block(ext, rb_n, tap_of, t_ref, cols, init)
                return carry

            lax.fori_loop(0, tm // rb_n, rb_body, 0)
        cv = cv_ref[...]
        mu = jnp.mean(cv, axis=-1, keepdims=True)
        cc = cv - mu
        rstd = lax.rsqrt(jnp.mean(cc * cc, axis=-1, keepdims=True) + LN_EPS)
        ln = cc * rstd * v_ref[3:4, :] + v_ref[4:5, :]
        w_ref[...] = (ln * _sigmoid(ln)).astype(w_ref.dtype)

    return pl.pallas_call(body, name=name, out_shape=(SDS((s, D), BF), SDS((s, D), F32)), grid=(s // tm,),
                          in_specs=[_rows(tm, D, 0), _rows(tm, D, 1), _const((32, D)), _const((8, D))],
                          out_specs=(_rows(tm, D), _rows(tm, D)),
                          scratch_shapes=[pltpu.VMEM((tm + CF_HALO, D), F32)], compiler_params=_cp(1))(p, p, taps, vec)


def _cf_act_bwd(name, p, cv, dw, taps, vec):
    s = p.shape[0]
    tm = _tile(s, ROW_TILE_D)
    nt = s // tm
    rev = lambda col: pl.BlockSpec((tm, D), lambda i: (nt - 1 - i, col))
    rb_n = _tile(tm, CF_ROW_BLOCK)
    tap_of = lambda off: CF_TAPS - 1 - off if off < CF_TAPS else None

    def body(a_ref, g_ref, cv_ref, dw_ref, t_ref, v_ref, dp_ref, tacc_ref, acc_ref, nbuf, ubuf, dubuf):
        step = pl.program_id(0)

        @pl.when(step == 0)
        def _():
            acc_ref[...] = jnp.zeros_like(acc_ref)
            tacc_ref[...] = jnp.zeros_like(tacc_ref)

        a = a_ref[...].astype(F32) + v_ref[0:1, :]
        g = g_ref[...].astype(F32) + v_ref[1:2, :]
        sg = _sigmoid(g)
        ubuf[...] = a * sg
        cvv = cv_ref[...]
        mu = jnp.mean(cvv, axis=-1, keepdims=True)
        cc = cvv - mu
        rstd = lax.rsqrt(jnp.mean(cc * cc, axis=-1, keepdims=True) + LN_EPS)
        vhat = cc * rstd
        ln = vhat * v_ref[3:4, :] + v_ref[4:5, :]
        s2 = _sigmoid(ln)
        dln = dw_ref[...].astype(F32) * (s2 * (1.0 + ln * (1.0 - s2)))
        acc_ref[1:2, :] += jnp.sum(dln * vhat, axis=0, keepdims=True)
        acc_ref[2:3, :] += jnp.sum(dln, axis=0, keepdims=True)
        dvh = dln * v_ref[3:4, :]
        dcv = rstd * (dvh - jnp.mean(dvh, axis=-1, keepdims=True)
                      - vhat * jnp.mean(dvh * vhat, axis=-1, keepdims=True))
        acc_ref[0:1, :] += jnp.sum(dcv, axis=0, keepdims=True)
        _carry_up(nbuf, tm, step == 0, CF_HALO)
        nbuf[0:tm, :] = dcv
        for ci in range(D // CF_LANES):
            cols = slice(ci * CF_LANES, (ci + 1) * CF_LANES)

            def rb_body(rb, accs):
                r0 = pl.multiple_of(rb * rb_n, rb_n)
                ext = nbuf[pl.ds(r0, rb_n + CF_HALO), cols]
                accs = list(accs)
                dubuf[pl.ds(r0, rb_n), cols] = _tap_conv_block(
                    ext, rb_n, tap_of, t_ref, cols, jnp.zeros((rb_n, CF_LANES), F32),
                    ubuf[pl.ds(r0, rb_n), cols], accs)
                return tuple(accs)

            z = jnp.zeros((8, CF_LANES), F32)
            accs = lax.fori_loop(0, tm // rb_n, rb_body, tuple([z] * CF_TAPS))
            for k in range(CF_TAPS):
                tacc_ref[k:k + 1, cols] += jnp.sum(accs[k], axis=0, keepdims=True)
        a = a_ref[...].astype(F32) + v_ref[0:1, :]
        sg = _sigmoid(g_ref[...].astype(F32) + v_ref[1:2, :])
        da = dubuf[...] * sg
        dg = da * a * (1.0 - sg)
        dp_ref[:, 0:D] = da.astype(dp_ref.dtype)
        dp_ref[:, D:2 * D] = dg.astype(dp_ref.dtype)
        acc_ref[3:4, :] += jnp.sum(da, axis=0, keepdims=True)
        acc_ref[4:5, :] += jnp.sum(dg, axis=0, keepdims=True)

    return pl.pallas_call(
        body, name=name, out_shape=(SDS((s, 2 * D), BF), SDS((32, D), F32), SDS((8, D), F32)), grid=(nt,),
        in_specs=[rev(0), rev(1), rev(0), rev(0), _const((32, D)), _const((8, D))],
        out_specs=(pl.BlockSpec((tm, 2 * D), lambda i: (nt - 1 - i, 0)), _const((32, D)), _const((8, D))),
        scratch_shapes=[pltpu.VMEM((tm + CF_HALO, D), F32), pltpu.VMEM((tm, D), F32), pltpu.VMEM((tm, D), F32)],
        compiler_params=_cp(1))(p, p, cv, dw, taps, vec)


def _inv_count(row0, tm, window):
    t = row0 + lax.broadcasted_iota(jnp.int32, (tm, 1), 0)
    return 1.0 / jnp.minimum(t + 1, window).astype(F32)


def _pool_fwd(name, x, w, vec):
    s = x.shape[0]
    tm = _tile(s, ROW_TILE_D)
    G = POOL_GROUP

    def body(x_ref, w_ref, v_ref, pl_ref, m_ref, buf):
        i = pl.program_id(0)

        @pl.when(i == 0)
        def _():
            buf[0:POOL_HALO, :] = jnp.zeros((POOL_HALO, D), F32)

        @pl.when(i > 0)
        def _():
            buf[0:POOL_HALO, :] = buf[tm:tm + POOL_HALO, :]

        xv = x_ref[...]
        r = lax.rsqrt(jnp.mean(xv * xv, axis=-1, keepdims=True) + RMS_EPS)
        buf[POOL_HALO:POOL_HALO + tm, :] = xv * r * (v_ref[0:1, :] * (1.0 + v_ref[1:2, :])) + v_ref[2:3, :]
        for gi, win in enumerate(POOL_WINDOWS):
            cols = slice(gi * G, (gi + 1) * G)
            acc = buf[POOL_HALO:POOL_HALO + tm, cols]
            hg = acc
            for j in range(1, win):
                acc = acc + buf[POOL_HALO - j:POOL_HALO - j + tm, cols]
            pooled = (acc * _inv_count(i * tm, tm, win) - hg).astype(BF)
            pl_ref[:, cols] = pooled
            yg = jnp.dot(pooled, w_ref[gi], preferred_element_type=F32)
            m_ref[:, cols] = (yg + v_ref[3:4, cols]) * v_ref[4:5, cols]

    return pl.pallas_call(body, name=name, out_shape=(SDS((s, D), BF), SDS((s, D), F32)), grid=(s // tm,),
                          in_specs=[_rows(tm, D), _const((4, G, G)), _const((8, D))],
                          out_specs=(_rows(tm, D), _rows(tm, D)),
                          scratch_shapes=[pltpu.VMEM((tm + POOL_HALO, D), F32)], compiler_params=_cp(1))(x, w, vec)


def _pool_bwd(name, pooled, dm, w, vec):
    s = pooled.shape[0]
    tm = _tile(s, ROW_TILE_D)
    nt = s // tm
    G = POOL_GROUP
    rev = pl.BlockSpec((tm, D), lambda i: (nt - 1 - i, 0))

    def body(p_ref, dm_ref, w_ref, v_ref, dh_ref, dw_ref, acc_ref, nbuf):
        step = pl.program_id(0)
        row0 = (nt - 1 - step) * tm

        @pl.when(step == 0)
        def _():
            acc_ref[...] = jnp.zeros_like(acc_ref)
            dw_ref[...] = jnp.zeros_like(dw_ref)

        _carry_up(nbuf, tm, step == 0, POOL_HALO)
        dmv = dm_ref[...].astype(F32)
        acc_ref[0:1, :] += jnp.sum(dmv, axis=0, keepdims=True) * v_ref[1:2, :]
        dps = []
        for gi, win in enumerate(POOL_WINDOWS):
            cols = slice(gi * G, (gi + 1) * G)
            pg = p_ref[:, cols]
            yb = jnp.dot(pg, w_ref[gi], preferred_element_type=F32) + v_ref[0:1, cols]
            acc_ref[1:2, cols] += jnp.sum(dmv[:, cols] * yb, axis=0, keepdims=True)
            dy = (dmv[:, cols] * v_ref[1:2, cols]).astype(BF)
            dw_ref[gi] += lax.dot_general(pg, dy, TN, preferred_element_type=F32)
            dpg = lax.dot_general(dy, w_ref[gi], NT, preferred_element_type=F32)
            dps.append(dpg)
            nbuf[0:tm, cols] = dpg * _inv_count(row0, tm, win)
        for gi, win in enumerate(POOL_WINDOWS):
            cols = slice(gi * G, (gi + 1) * G)
            acc = nbuf[0:tm, cols]
            for j in range(1, win):
                acc = acc + nbuf[j:j + tm, cols]
            dh_ref[:, cols] = acc - dps[gi]

    return pl.pallas_call(
        body, name=name, out_shape=(SDS((s, D), F32), SDS((4, G, G), F32), SDS((8, D), F32)), grid=(nt,),
        in_specs=[rev, rev, _const((4, G, G)), _const((8, D))],
        out_specs=(rev, _const((4, G, G)), _const((8, D))),
        scratch_shapes=[pltpu.VMEM((tm + POOL_HALO, D), F32)], compiler_params=_cp(1))(pooled, dm, w, vec)


def _row_tile_2d(rows, width, bytes_per_row_elem=4, budget=2 * 1024 * 1024):
    t = max(8, budget // (width * bytes_per_row_elem))
    t = min(rows, 1 << (t.bit_length() - 1))
    while rows % t:
        t //= 2
    return t


def _add_slots(name, r):
    nl, _, k, n = r.shape
    tk = _row_tile_2d(k, n, 16)

    def body(r_ref, o_ref):
        f = lambda i: r_ref[i].astype(F32)
        o_ref[...] = ((f(7) + f(6)) + (f(0) + f(1))) + ((f(2) + f(3)) + (f(4) + f(5)))

    return pl.pallas_call(body, name=name, out_shape=SDS((nl, k, n), F32), grid=(nl, k // tk),
                          in_specs=[pl.BlockSpec((None, N_DEV, tk, n), lambda l, i: (l, 0, i, 0))],
                          out_specs=pl.BlockSpec((None, tk, n), lambda l, i: (l, i, 0)),
                          compiler_params=_cp(2))(r)


def _adamw(name, w, g, m, v):
    rows, width = w.shape
    tm = _row_tile_2d(rows, width, 4, 1024 * 1024)
    c1 = 1.0 - ADAM_B1 ** ADAM_STEP
    c2 = 1.0 - ADAM_B2 ** ADAM_STEP

    def body(w_ref, g_ref, m_ref, v_ref, d_ref, nm_ref, nv_ref):
        gv = g_ref[...]
        nm = ADAM_B1 * m_ref[...] + (1.0 - ADAM_B1) * gv
        nv = ADAM_B2 * v_ref[...] + (1.0 - ADAM_B2) * (gv * gv)
        nm_ref[...] = nm
        nv_ref[...] = nv
        d_ref[...] = -ADAM_LR * ((nm / c1) / (jnp.sqrt(nv / c2) + ADAM_EPS) + ADAM_WD * w_ref[...])

    spec = _rows(tm, width)
    sds = SDS((rows, width), F32)
    return pl.pallas_call(body, name=name, out_shape=(sds, sds, sds), grid=(rows // tm,),
                          in_specs=[spec] * 4, out_specs=(spec,) * 3, compiler_params=_cp(1))(w, g, m, v)


def _mod_fwd(c16, w_mod, b_sh):
    n = w_mod.shape[2]
    tn = _tile(n, 512)

    def body(c_ref, w_ref, b_ref, o_ref):
        cv = c_ref[...]
        ca = (cv * _sigmoid(cv)).astype(BF)
        o_ref[...] = jnp.dot(ca, w_ref[...].astype(BF), preferred_element_type=F32) + b_ref[0:1, :]

    return pl.pallas_call(body, name="mod_fwd", out_shape=SDS((DEPTH, 16, n), F32), grid=(DEPTH, n // tn),
                          in_specs=[_const((16, D)), pl.BlockSpec((None, D, tn), lambda l, j: (l, 0, j)),
                                    pl.BlockSpec((None, 8, tn), lambda l, j: (l, 0, j))],
                          out_specs=pl.BlockSpec((None, 16, tn), lambda l, j: (l, 0, j)),
                          compiler_params=_cp(2))(c16, w_mod, b_sh)


def _mod_bwd(c16, dmod):
    n = dmod.shape[2]
    tn = _tile(n, 512)

    def body(c_ref, d_ref, o_ref):
        cv = c_ref[...]
        ca = (cv * _sigmoid(cv)).astype(BF)
        o_ref[...] = lax.dot_general(ca, d_ref[...].astype(BF), TN, preferred_element_type=F32)

    return pl.pallas_call(body, name="mod_bwd", out_shape=SDS((DEPTH, D, n), F32), grid=(DEPTH, n // tn),
                          in_specs=[_const((16, D)), pl.BlockSpec((None, 16, tn), lambda l, j: (l, 0, j))],
                          out_specs=pl.BlockSpec((None, D, tn), lambda l, j: (l, 0, j)),
                          compiler_params=_cp(2))(c16, dmod)


def _place():
    x, y, c = lax.axis_index("x"), lax.axis_index("y"), lax.axis_index("c")
    other_chips = [(1 - x, y), (x, 1 - y), (1 - x, 1 - y)]
    return x, y, c, other_chips


def _allgather_small(name, v, with_sum):
    m, n = v.shape

    def body(x_ref, out_ref, *rest):
        if with_sum:
            sum_ref, send_sems, recv_sems, local_sem = rest
        else:
            send_sems, recv_sems, local_sem = rest
        x, y, c, chips = _place()
        me, sibling = (x, y, c), (x, y, 1 - c)

        def rows(px, py, pc):
            return out_ref.at[pl.ds((4 * px + 2 * py + pc) * m, m), :]

        def copy(k, block, to, src=None):
            return pltpu.make_async_remote_copy(
                src_ref=rows(*block) if src is None else src, dst_ref=rows(*block),
                send_sem=send_sems.at[k], recv_sem=recv_sems.at[k], device_id=to, device_id_type=MESH)

        mine = pltpu.make_async_copy(x_ref, rows(*me), local_sem)
        mine.start()
        first = [copy(0, me, sibling, src=x_ref)]
        first += [copy(1 + j, me, (*chip, c), src=x_ref) for j, chip in enumerate(chips)]
        for cp in first:
            cp.start()
        passed = [copy(4 + j, (*chip, c), sibling) for j, chip in enumerate(chips)]
        for j, chip in enumerate(chips):
            copy(1 + j, (*chip, c), me).wait_recv()
            passed[j].start()
        copy(0, sibling, me).wait_recv()
        for j, chip in enumerate(chips):
            copy(4 + j, (*chip, 1 - c), me).wait_recv()
        for cp in first + passed:
            cp.wait_send()
        mine.wait()
        if with_sum:
            acc = out_ref[0:m, :]
            for k in range(1, N_DEV):
                acc = acc + out_ref[k * m:(k + 1) * m, :]
            sum_ref[...] = acc

    vm = pl.BlockSpec(memory_space=pltpu.VMEM)
    out_shape = [SDS((N_DEV * m, n), F32)] + ([SDS((m, n), F32)] if with_sum else [])
    res = pl.pallas_call(
        body, name=name, out_shape=tuple(out_shape), in_specs=[vm], out_specs=tuple([vm] * len(out_shape)),
        scratch_shapes=[pltpu.SemaphoreType.DMA((7,)), pltpu.SemaphoreType.DMA((7,)), pltpu.SemaphoreType.DMA],
        compiler_params=pltpu.CompilerParams(vmem_limit_bytes=VMEM_LIMIT_MB * 1024 * 1024))(v)
    return res if with_sum else res[0]


HBM = pl.BlockSpec(memory_space=pltpu.HBM)


def _sem_scratch(n_remote, n_local):
    return [pltpu.SemaphoreType.DMA((n_remote,)), pltpu.SemaphoreType.DMA((n_remote,)),
            pltpu.SemaphoreType.DMA((n_local,))]


DMA_PIECE_BYTES = 1 << 20


def _pieces(src, dst):
    *lead, rows, n = src.shape
    nsplit = max(1, min(rows // 16, (rows * n * jnp.dtype(src.dtype).itemsize) // DMA_PIECE_BYTES))
    while rows % nsplit or (rows // nsplit) % 16:
        nsplit -= 1
    size = rows // nsplit
    out = []
    for idx in itertools.product(*[range(d) for d in lead]):
        for i in range(nsplit):
            sl = tuple(idx) + (pl.ds(i * size, size),)
            out.append((src.at[sl], dst.at[sl]))
    return out


def _local_copies(src, dst, sem):
    return ([pltpu.make_async_copy(s_, d_, sem) for s_, d_ in _pieces(src, dst)],
            pltpu.make_async_copy(src, dst, sem))


def _remote_copies(src, dst, send_sem, recv_sem, to):
    mk = lambda s_, d_: pltpu.make_async_remote_copy(src_ref=s_, dst_ref=d_, send_sem=send_sem, recv_sem=recv_sem,
                                                     device_id=to, device_id_type=MESH)
    return [mk(s_, d_) for s_, d_ in _pieces(src, dst)], mk(src, dst)


def _gather_weights(name, shards, sequencer_id=None):
    nq = len(shards)

    def exchange(ins, outs, send_sems, recv_sems, local_sems, own_barrier):
        x, y, c, chips = _place()
        if own_barrier:
            barrier = pltpu.get_barrier_semaphore()
            for px, py in chips:
                pl.semaphore_signal(barrier, inc=1, device_id=(px, py, c), device_id_type=MESH)
            pl.semaphore_wait(barrier, len(chips))
        me_chip = 2 * x + y
        started, local_all, send_all, recv_all = [], [], [], []
        for q in range(nq):
            cps, whole = _local_copies(ins[q], outs[q].at[:, me_chip], local_sems.at[q])
            started += cps
            local_all.append(whole)
            for r, (px, py) in enumerate(chips):
                k = 3 * q + r
                cps, whole = _remote_copies(ins[q], outs[q].at[:, me_chip], send_sems.at[k], recv_sems.at[k], (px, py, c))
                started += cps
                send_all.append(whole)
                recv_all.append(_remote_copies(ins[q], outs[q].at[:, 2 * px + py], send_sems.at[k], recv_sems.at[k],
                                               (px, py, c))[1])
        for cp in started:
            cp.start()
        for cp in recv_all:
            cp.wait_recv()
        for cp in send_all:
            cp.wait_send()
        for cp in local_all:
            cp.wait()

    out_shape = tuple(SDS((s.shape[0], N_CHIPS) + s.shape[1:], s.dtype) for s in shards)
    return _launch_exchange(exchange, name, shards, out_shape, _sem_scratch(3 * nq, nq), sequencer_id)


def _launch_exchange(exchange, name, arrays, out_shape, sems, sequencer_id):
    n = len(arrays)
    if sequencer_id is None:
        def body(*refs):
            exchange(refs[:n], refs[n:2 * n], *refs[2 * n:], own_barrier=False)

        return pl.pallas_call(body, name=name, out_shape=out_shape, in_specs=[HBM] * n,
                              out_specs=tuple([HBM] * n), scratch_shapes=sems)(*arrays)

    in_refs = [jax.new_ref(a, memory_space=pltpu.MemorySpace.HBM) for a in arrays]
    out_refs = [jax.empty_ref(o, memory_space=pltpu.MemorySpace.HBM) for o in out_shape]

    @pl.kernel(mesh=plsc.ScalarSubcoreMesh(axis_name="sequencer", num_cores=1), name=name, scratch_types=tuple(sems),
               compiler_params=pltpu.CompilerParams(collective_id=sequencer_id))
    def launch(send_sems, recv_sems, local_sems):
        exchange(in_refs, out_refs, send_sems, recv_sems, local_sems, own_barrier=True)

    launch()
    return [r[...] for r in out_refs]


def _grad_exchange(name, gs, sequencer_id=None):
    nq = len(gs)

    def exchange(ins, outs, send_sems, recv_sems, local_sems, own_barrier):
        x, y, c, chips = _place()
        peers = [(2 * r + e, (px, py, c if e == 0 else 1 - c)) for r, (px, py) in enumerate(chips) for e in (0, 1)]
        peers.append((6, (x, y, 1 - c)))
        if own_barrier:
            barrier = pltpu.get_barrier_semaphore()
            for _, peer in peers:
                pl.semaphore_signal(barrier, inc=1, device_id=peer, device_id_type=MESH)
            pl.semaphore_wait(barrier, len(peers))
        started, local_all, remote_all = [], [], []
        for q in range(nq):
            cps, whole = _local_copies(ins[q].at[:, 2 * x + y, c], outs[q].at[:, 7], local_sems.at[q])
            started += cps
            local_all.append(whole)
            for slot, (px, py, pc) in peers:
                k = 7 * q + slot
                cps, whole = _remote_copies(ins[q].at[:, 2 * px + py, pc], outs[q].at[:, slot], send_sems.at[k],
                                            recv_sems.at[k], (px, py, pc))
                started += cps
                remote_all.append(whole)
        for cp in started:
            cp.start()
        for cp in remote_all:
            cp.wait_recv()
        for cp in remote_all:
            cp.wait_send()
        for cp in local_all:
            cp.wait()

    out_shape = tuple(SDS((g.shape[0], N_DEV, g.shape[3], g.shape[4]), g.dtype) for g in gs)
    return _launch_exchange(exchange, name, gs, out_shape, _sem_scratch(7 * nq, nq), sequencer_id)


def _pair_share(name, rs, sequencer_id=None):
    nq = len(rs)

    def exchange(ins, outs, send_sems, recv_sems, local_sems, own_barrier):
        x, y, c, _ = _place()
        if own_barrier:
            barrier = pltpu.get_barrier_semaphore()
            pl.semaphore_signal(barrier, inc=1, device_id=(x, y, 1 - c), device_id_type=MESH)
            pl.semaphore_wait(barrier, 1)
        started, local_all, send_all, recv_all = [], [], [], []
        for q in range(nq):
            cps, whole = _local_copies(ins[q], outs[q].at[:, c], local_sems.at[q])
            started += cps
            local_all.append(whole)
            cps, whole = _remote_copies(ins[q], outs[q].at[:, c], send_sems.at[q], recv_sems.at[q], (x, y, 1 - c))
            started += cps
            send_all.append(whole)
            recv_all.append(_remote_copies(ins[q], outs[q].at[:, 1 - c], send_sems.at[q], recv_sems.at[q],
                                           (x, y, 1 - c))[1])
        for cp in started:
            cp.start()
        for cp in recv_all:
            cp.wait_recv()
        for cp in send_all:
            cp.wait_send()
        for cp in local_all:
            cp.wait()

    out_shape = tuple(SDS((r.shape[0], 2) + r.shape[1:], r.dtype) for r in rs)
    return _launch_exchange(exchange, name, rs, out_shape, _sem_scratch(nq, nq), sequencer_id)


def _pack(arrs, rows_multiple=8):
    flat = jnp.concatenate([a.astype(F32).reshape(-1) for a in arrs])
    pad = (-flat.shape[0]) % (128 * rows_multiple)
    return jnp.pad(flat, (0, pad)).reshape(-1, 128)


def _unpack(slab, shapes):
    flat = slab.reshape(-1)
    out, off = [], 0
    for shp in shapes:
        n = 1
        for d in shp:
            n *= d
        out.append(flat[off:off + n].reshape(shp))
        off += n
    return out


def _shard_last(a, chip, n):
    return lax.dynamic_slice_in_dim(a, chip * n, n, axis=a.ndim - 1)


def kernel(x, c, w_mod, b_mod, norm_g, sc_w_in, sc_conv, sc_w_out, pool_w, pool_b, pool_scale, cf_w_pw1, cf_b_pw1, cf_w_dw, cf_b_dw, cf_ln_g, cf_ln_b, cf_w_pw2, cf_b_pw2, ffn_w_up, ffn_conv, ffn_b_conv, ffn_w_down, loss_target, m_w_mod, m_b_mod, m_norm_g, m_sc_w_in, m_sc_conv, m_sc_w_out, m_pool_w, m_pool_b, m_pool_scale, m_cf_w_pw1, m_cf_b_pw1, m_cf_w_dw, m_cf_b_dw, m_cf_ln_g, m_cf_ln_b, m_cf_w_pw2, m_cf_b_pw2, m_ffn_w_up, m_ffn_conv, m_ffn_b_conv, m_ffn_w_down, v_w_mod, v_b_mod, v_norm_g, v_sc_w_in, v_sc_conv, v_sc_w_out, v_pool_w, v_pool_b, v_pool_scale, v_cf_w_pw1, v_cf_b_pw1, v_cf_w_dw, v_cf_b_dw, v_cf_ln_g, v_cf_ln_b, v_cf_w_pw2, v_cf_b_pw2, v_ffn_w_up, v_ffn_conv, v_ffn_b_conv, v_ffn_w_down):
    ax, ay, ac = lax.axis_index("x"), lax.axis_index("y"), lax.axis_index("c")
    chip = 2 * ax + ay
    dev = 4 * ax + 2 * ay + ac
    xs = x[0]
    target = loss_target[0]

    small_sharded = [norm_g, sc_conv, cf_b_pw1, cf_w_dw, cf_b_dw, cf_ln_g, cf_ln_b, cf_b_pw2, ffn_conv]
    slab = _pack([c] + small_sharded)
    gathered = _allgather_small("gather_small_params", slab, False).reshape(N_DEV, -1, 128)
    parts = [_unpack(gathered[d], [c.shape] + [a.shape for a in small_sharded]) for d in range(N_DEV)]
    c_all = jnp.concatenate([p[0] for p in parts], axis=0)
    full = [jnp.concatenate([parts[2 * j][1 + i] for j in range(N_CHIPS)], axis=-1)
            for i in range(len(small_sharded))]
    norm_g_f, sc_conv_f, cf_b_pw1_f, cf_w_dw_f, cf_b_dw_f, cf_ln_g_f, cf_ln_b_f, cf_b_pw2_f, ffn_conv_f = full
    c16 = jnp.pad(c_all, ((0, 8), (0, 0)))

    n_mod = w_mod.shape[2]
    b_sh = jnp.broadcast_to(_shard_last(b_mod, chip, n_mod)[:, None, :], (DEPTH, 8, n_mod))
    mod_part = _mod_fwd(c16, w_mod, b_sh)
    mod_g = _allgather_small("gather_mod", mod_part.reshape(DEPTH * 16, n_mod), False)
    mod_g = mod_g.reshape(N_DEV, DEPTH, 16, n_mod)
    mod_mine = jnp.concatenate(
        [lax.dynamic_index_in_dim(mod_g[2 * j], dev, axis=1, keepdims=False) for j in range(N_CHIPS)], axis=-1)
    mod = mod_mine.reshape(DEPTH, 6, D)

    bf = lambda a: a.astype(BF)
    rows = lambda w: w.reshape(w.shape[0], w.shape[1] * w.shape[2], w.shape[3])
    after = lambda a, done: a + (done[(0,) * done.ndim] * 0).astype(a.dtype)
    w_in_0, w_out_0 = _gather_weights("gather_weights", [bf(sc_w_in[0:1]), bf(sc_w_out[0:1])])
    w_up_0, w_down_0 = _gather_weights(
        "gather_weights_first", [after(bf(ffn_w_up[0:1]), w_in_0), bf(ffn_w_down[0:1])], GATHER_FIRST_ID)
    w_in_r, w_out_r, pool_f, pw1_f, pw2_f, w_up_r, w_down_r = _gather_weights(
        "gather_weights_rest", [after(bf(sc_w_in[1:]), w_in_0), bf(sc_w_out[1:]), bf(pool_w[0]), bf(cf_w_pw1),
                                bf(cf_w_pw2), bf(ffn_w_up[1:]), bf(ffn_w_down[1:])], GATHER_REST_ID)
    pool_f = pool_f.reshape(4, POOL_GROUP, POOL_GROUP)
    pw1_f, pw2_f = (pw1_f, 0), (rows(pw2_f), 0)
    w_in_f = {0: (w_in_0, 0), 1: (w_in_r, 0)}
    w_out_f = {0: (rows(w_out_0), 0), 1: (rows(w_out_r), 0)}
    w_up_f = {i: (w_up_0, 0) if i == 0 else (w_up_r, i - 1) for i in range(DEPTH)}
    w_down_f = {i: (rows(w_down_0), 0) if i == 0 else (rows(w_down_r), i - 1) for i in range(DEPTH)}

    zero_d = jnp.zeros((D,), F32)
    mods = [[mod[i, k] for k in range(6)] for i in range(DEPTH)]
    saved = []
    xcur = xs
    h_next = None
    for i in range(DEPTH):
        kind, j = i % 3, i // 3
        sh1, sc1, g1, sh2, sc2, g2 = mods[i]
        st = {"x0": xcur}
        m_bias = zero_d
        if kind != 1:
            h = h_next if h_next is not None else _pro_fwd(f"pro1_fwd_{i}", xcur, _vecs([norm_g_f[i, 0], sc1, sh1], D))
        if kind == 0:
            p = _mm_nn(f"sc_in_{i}", h, *w_in_f[j],True, BF)
            z = _sc_act_fwd(f"sc_act_fwd_{i}", p, _vecs(list(sc_conv_f[j]), D))
            m = _mm_nn(f"sc_out_{i}", z, *w_out_f[j],False, F32)
            st.update(h=h, p=p)
        elif kind == 1:
            pool_vec = _vecs([norm_g_f[i, 0], sc1, sh1, pool_b[j], pool_scale[j]], D)
            pooled, m = _pool_fwd(f"pool_fwd_{i}", xcur, pool_f, pool_vec)
            st.update(pooled=pooled)
        else:
            p = _mm_nn(f"cf_pw1_{i}", h, *pw1_f,True, BF)
            taps = jnp.pad(cf_w_dw_f[j], ((0, 1), (0, 0)))
            cf_vec = _vecs([cf_b_pw1_f[j, :D], cf_b_pw1_f[j, D:], cf_b_dw_f[j], cf_ln_g_f[j], cf_ln_b_f[j]], D)
            wact, cv = _cf_act_fwd(f"cf_act_fwd_{i}", p, taps, cf_vec)
            m = _mm_nn(f"cf_pw2_{i}", wact, *pw2_f,False, F32)
            m_bias = cf_b_pw2_f[j]
            st.update(h=h, p=p, wact=wact, cv=cv, taps=taps, cf_vec=cf_vec)
        x1, h2 = _epi_pro_fwd(f"epi1_fwd_{i}", xcur, m,
                              _vecs([g1, norm_g_f[i, 1], m_bias, norm_g_f[i, 2], sc2, sh2], D), True)
        st.update(m=m, x1=x1, m_bias=m_bias)
        up = _mm_nn(f"ffn_up_{i}", h2, *w_up_f[i],True, BF)
        ffn_vec = _vecs(list(ffn_conv_f[i]) + [ffn_b_conv[i]], F)
        a = _ffn_act_fwd(f"ffn_act_fwd_{i}", up, ffn_vec)
        f = _mm_nn(f"ffn_down_{i}", a, *w_down_f[i],False, F32, tm=512)
        nxt = i + 1
        fuse_next = nxt < DEPTH and nxt % 3 != 1
        rows = [g2, norm_g_f[i, 3], zero_d]
        if fuse_next:
            rows += [norm_g_f[nxt, 0], mods[nxt][1], mods[nxt][0]]
        xcur, h_next = _epi_pro_fwd(f"epi2_fwd_{i}", x1, f, _vecs(rows, D), fuse_next)
        st.update(h2=h2, up=up, a=a, f=f, ffn_vec=ffn_vec)
        saved.append(st)

    dy, loss_cols = _loss_fwd_bwd(xcur, target)
    loss = lax.psum(jnp.sum(loss_cols[0]), ("x", "y", "c"))

    dmod = [None] * DEPTH
    d_norm_g = [None] * DEPTH
    d_sc_conv = [None, None]
    d_ffn_conv, d_ffn_b = [None] * DEPTH, [None] * DEPTH
    g_w_in, g_w_out, g_w_up, g_w_down = [None, None], [None, None], [None] * DEPTH, [None] * DEPTH
    def halves(gl):
        g = jnp.stack(gl) if isinstance(gl, list) else gl
        if g.ndim == 3:
            g = g.reshape(g.shape[0], N_CHIPS, g.shape[1] // N_CHIPS, g.shape[2])
        nl, _, k, n = g.shape
        return g.reshape(nl, N_CHIPS, 2, k // 2, n)

    dxo = dy
    last = DEPTH - 1
    pend = _epi_bwd(f"epi2_bwd_{last}", dy, saved[last]["f"], _vecs([mods[last][5], norm_g_f[last, 3], zero_d], D))
    for i in reversed(range(DEPTH)):
        kind, j = i % 3, i // 3
        st = saved[i]
        sh1, sc1, g1, sh2, sc2, g2 = mods[i]
        df, e2 = pend
        da = _mm_nt(f"ffn_down_dx_{i}", df, *w_down_f[i],False, BF, tn=F // 2)
        dup, fsum = _ffn_act_bwd(f"ffn_act_bwd_{i}", st["up"], da, st["ffn_vec"])
        g_w_down[i] = _mm_tn(f"ffn_down_dw_{i}", st["a"], df, False, tk=F // 2)
        dh2 = _mm_nt(f"ffn_up_dx_{i}", dup, *w_up_f[i],True, F32)
        g_w_up[i] = _mm_tn(f"ffn_up_dw_{i}", st["h2"], dup, True)
        if i == 0:
            ffn0_pieces = _grad_exchange("grad_exchange_ffn0", [halves([g_w_up[0]]), halves([g_w_down[0]])],
                                         GRAD_FFN0_ID)
        dx1, dm, s21 = _pro_epi_bwd(f"pro2_epi1_bwd_{i}", dh2, st["x1"], dxo, st["m"],
                                    _vecs([norm_g_f[i, 2], sc2, g1, norm_g_f[i, 1], st["m_bias"]], D))
        p2, e1 = s21[0:3], s21[3:6]
        if kind == 0:
            dz = _mm_nt(f"sc_out_dx_{i}", dm, *w_out_f[j],False, BF)
            z, dp, ssum = _sc_act_bwd(f"sc_act_bwd_{i}", st["p"], dz, _vecs(list(sc_conv_f[j]), D))
            g_w_out[j] = _mm_tn(f"sc_out_dw_{i}", z, dm, False)
            dh = _mm_nt(f"sc_in_dx_{i}", dp, *w_in_f[j],True, F32)
            g_w_in[j] = _mm_tn(f"sc_in_dw_{i}", st["h"], dp, True)
            d_sc_conv[j] = ssum[0:3]
        elif kind == 1:
            dh, g_pool, psum = _pool_bwd(f"pool_bwd_{i}", st["pooled"], dm, pool_f, _vecs([pool_b[j], pool_scale[j]], D))
        else:
            dwact = _mm_nt(f"cf_pw2_dx_{i}", dm, *pw2_f,False, BF)
            g_pw2 = _mm_tn(f"cf_pw2_dw_{i}", st["wact"], dm, False)
            dp, tsum, csum = _cf_act_bwd(f"cf_act_bwd_{i}", st["p"], st["cv"], dwact, st["taps"], st["cf_vec"])
            dh = _mm_nt(f"cf_pw1_dx_{i}", dp, *pw1_f,True, F32)
            g_pw1 = _mm_tn(f"cf_pw1_dw_{i}", st["h"], dp, True)
            d_cf = dict(b_pw1=jnp.concatenate([csum[3], csum[4]])[None], w_dw=tsum[None, :CF_TAPS], b_dw=csum[0:1],
                        ln_g=csum[1:2], ln_b=csum[2:3], b_pw2=e1[2:3])
        if i > 0:
            prev = i - 1
            dxo, df_prev, s12 = _pro_epi_bwd(f"pro1_epi2_bwd_{i}", dh, st["x0"], dx1, saved[prev]["f"],
                                             _vecs([norm_g_f[i, 0], sc1, mods[prev][5], norm_g_f[prev, 3], zero_d], D))
            p1, pend = s12[0:3], (df_prev, s12[3:6])
        else:
            dxo, p1 = _pro_bwd(f"pro1_bwd_{i}", dh, st["x0"], dx1, _vecs([norm_g_f[i, 0], sc1], D))
        dmod[i] = jnp.concatenate([p1[0], p1[1], e1[0], p2[0], p2[1], e2[0]])
        d_norm_g[i] = jnp.stack([p1[2], e1[1], p2[2], e2[1]])
        if kind == 1:
            d_pool_b, d_pool_scale = psum[0:1], psum[1:2]
        if i == 1:
            rest_pieces = _grad_exchange(
                "grad_exchange_rest",
                [halves([g_w_in[1]]), halves([g_w_out[1]]), halves(g_pool.astype(BF)), halves([g_pw1]), halves([g_pw2]),
                 halves(g_w_up[1:]), halves(g_w_down[1:])], GRAD_REST_ID)
        saved[i] = None
        st = None
        if i == 0:
            grad_x = dxo[None]
        d_ffn_conv[i], d_ffn_b[i] = fsum[1:4], fsum[0]

    small_shapes = [(DEPTH, 6 * D), (DEPTH, 4, D), (2, 3, D), (1, D), (1, D), (1, 2 * D), (1, CF_TAPS, D), (1, D),
                    (1, D), (1, D), (1, D), (DEPTH, 3, F), (DEPTH, F)]
    small = [jnp.stack(dmod), jnp.stack(d_norm_g), jnp.stack(d_sc_conv), d_pool_b, d_pool_scale, d_cf["b_pw1"],
             d_cf["w_dw"], d_cf["b_dw"], d_cf["ln_g"], d_cf["ln_b"], d_cf["b_pw2"], jnp.stack(d_ffn_conv),
             jnp.stack(d_ffn_b)]
    gsl, ssl = _allgather_small("reduce_small_grads", _pack(small), True)
    tot = _unpack(ssl, small_shapes)
    (gb_mod, gt_norm_g, gt_sc_conv, g_pool_b, g_pool_scale, gt_b_pw1, gt_w_dw, gt_b_dw, gt_ln_g, gt_ln_b, gt_b_pw2,
     gt_ffn_conv, g_ffn_b) = tot
    dmod_all = jnp.stack([_unpack(gsl.reshape(N_DEV, -1, 128)[d], small_shapes[:1])[0] for d in range(N_DEV)], axis=1)
    dmod_sh = jnp.pad(_shard_last(dmod_all, chip, n_mod), ((0, 0), (0, 8), (0, 0)))
    g_w_mod = _mod_bwd(c16, dmod_sh)
    g_norm_g = _shard_last(gt_norm_g, chip, D // 4)
    g_sc_conv = _shard_last(gt_sc_conv, chip, D // 4)
    g_b_pw1 = _shard_last(gt_b_pw1, chip, 2 * D // 4)
    g_w_dw = _shard_last(gt_w_dw, chip, D // 4)
    g_b_dw = _shard_last(gt_b_dw, chip, D // 4)
    g_ln_g = _shard_last(gt_ln_g, chip, D // 4)
    g_ln_b = _shard_last(gt_ln_b, chip, D // 4)
    g_b_pw2 = _shard_last(gt_b_pw2, chip, D // 4)
    g_ffn_conv = _shard_last(gt_ffn_conv, chip, F // 4)

    rest_reduced = [_add_slots(f"grad_sum_rest_{q}", p) for q, p in enumerate(rest_pieces)]
    r_in, r_out, r_pool, r_pw1, r_pw2, r_up, r_down = _pair_share("grad_pair_share_rest", rest_reduced, SHARE_REST_ID)
    mixer0_pieces = _grad_exchange("grad_exchange", [after(halves([g_w_in[0]]), rest_reduced[0]), halves([g_w_out[0]])])
    first_reduced = [_add_slots(f"grad_sum_{q}", p) for q, p in enumerate(list(mixer0_pieces) + list(ffn0_pieces))]
    f_in, f_out, f_up, f_down = _pair_share("grad_pair_share", first_reduced)
    cat = lambda a, b, like: jnp.concatenate([a, b], axis=0).reshape(like.shape)
    g_sc_w_in, g_sc_w_out = cat(f_in, r_in, sc_w_in), cat(f_out, r_out, sc_w_out)
    g_ffn_w_up, g_ffn_w_down = cat(f_up, r_up, ffn_w_up), cat(f_down, r_down, ffn_w_down)
    g_pool_w, g_cf_w_pw1, g_cf_w_pw2 = (r_pool.reshape(pool_w.shape), r_pw1.reshape(cf_w_pw1.shape),
                                        r_pw2.reshape(cf_w_pw2.shape))

    def adam_big(name, w, g, m, v):
        shp = w.shape
        two = lambda t: t.reshape(-1, shp[-1])
        return [o.reshape(shp) for o in _adamw(name, two(w), two(g), two(m), two(v))]

    grads = dict(w_mod=g_w_mod, b_mod=gb_mod, norm_g=g_norm_g, sc_w_in=g_sc_w_in, sc_conv=g_sc_conv,
                 sc_w_out=g_sc_w_out, pool_w=g_pool_w, pool_b=g_pool_b, pool_scale=g_pool_scale,
                 cf_w_pw1=g_cf_w_pw1, cf_b_pw1=g_b_pw1, cf_w_dw=g_w_dw, cf_b_dw=g_b_dw, cf_ln_g=g_ln_g,
                 cf_ln_b=g_ln_b, cf_w_pw2=g_cf_w_pw2, cf_b_pw2=g_b_pw2, ffn_w_up=g_ffn_w_up, ffn_conv=g_ffn_conv,
                 ffn_b_conv=g_ffn_b, ffn_w_down=g_ffn_w_down)
    weights = dict(w_mod=w_mod, b_mod=b_mod, norm_g=norm_g, sc_w_in=sc_w_in, sc_conv=sc_conv, sc_w_out=sc_w_out,
                   pool_w=pool_w, pool_b=pool_b, pool_scale=pool_scale, cf_w_pw1=cf_w_pw1, cf_b_pw1=cf_b_pw1,
                   cf_w_dw=cf_w_dw, cf_b_dw=cf_b_dw, cf_ln_g=cf_ln_g, cf_ln_b=cf_ln_b, cf_w_pw2=cf_w_pw2,
                   cf_b_pw2=cf_b_pw2, ffn_w_up=ffn_w_up, ffn_conv=ffn_conv, ffn_b_conv=ffn_b_conv,
                   ffn_w_down=ffn_w_down)
    m_in = dict(w_mod=m_w_mod, b_mod=m_b_mod, norm_g=m_norm_g, sc_w_in=m_sc_w_in, sc_conv=m_sc_conv,
                sc_w_out=m_sc_w_out, pool_w=m_pool_w, pool_b=m_pool_b, pool_scale=m_pool_scale,
                cf_w_pw1=m_cf_w_pw1, cf_b_pw1=m_cf_b_pw1, cf_w_dw=m_cf_w_dw, cf_b_dw=m_cf_b_dw, cf_ln_g=m_cf_ln_g,
                cf_ln_b=m_cf_ln_b, cf_w_pw2=m_cf_w_pw2, cf_b_pw2=m_cf_b_pw2, ffn_w_up=m_ffn_w_up,
                ffn_conv=m_ffn_conv, ffn_b_conv=m_ffn_b_conv, ffn_w_down=m_ffn_w_down)
    v_in = dict(w_mod=v_w_mod, b_mod=v_b_mod, norm_g=v_norm_g, sc_w_in=v_sc_w_in, sc_conv=v_sc_conv,
                sc_w_out=v_sc_w_out, pool_w=v_pool_w, pool_b=v_pool_b, pool_scale=v_pool_scale,
                cf_w_pw1=v_cf_w_pw1, cf_b_pw1=v_cf_b_pw1, cf_w_dw=v_cf_w_dw, cf_b_dw=v_cf_b_dw, cf_ln_g=v_cf_ln_g,
                cf_ln_b=v_cf_ln_b, cf_w_pw2=v_cf_w_pw2, cf_b_pw2=v_cf_b_pw2, ffn_w_up=v_ffn_w_up,
                ffn_conv=v_ffn_conv, ffn_b_conv=v_ffn_b_conv, ffn_w_down=v_ffn_w_down)
    names = list(weights)
    big_names = ["w_mod", "sc_w_in", "sc_w_out", "pool_w", "cf_w_pw1", "cf_w_pw2", "ffn_w_up", "ffn_w_down"]
    small_names = [n for n in names if n not in big_names]
    delta, new_m, new_v = {}, {}, {}
    for n in big_names:
        delta[n], new_m[n], new_v[n] = adam_big(f"adamw_{n}", weights[n], grads[n], m_in[n], v_in[n])
    grads = {n: grads[n].reshape(weights[n].shape) for n in names}
    sm_shapes = [weights[n].shape for n in small_names]
    sd, sm, sv = _adamw("adamw_small", _pack([weights[n] for n in small_names]), _pack([grads[n] for n in small_names]),
                        _pack([m_in[n] for n in small_names]), _pack([v_in[n] for n in small_names]))
    for n, d_, m_, v_ in zip(small_names, _unpack(sd, sm_shapes), _unpack(sm, sm_shapes), _unpack(sv, sm_shapes)):
        delta[n], new_m[n], new_v[n] = d_, m_, v_

    return (loss, grad_x, *[grads[n] for n in names], *[delta[n] for n in names], *[new_m[n] for n in names],
            *[new_v[n] for n in names])
```

```python
import itertools

import jax
import jax.numpy as jnp
from jax import lax
from jax.experimental import pallas as pl
from jax.experimental.pallas import tpu as pltpu
from jax.experimental.pallas import tpu_sc as plsc

D = 1024
F = 2816
DEPTH = 4
POOL_WINDOWS = (2, 4, 8, 16)
POOL_GROUP = 256
CF_TAPS = 31
RMS_EPS = 1e-6
LN_EPS = 1e-5
ADAM_LR = 0.001
ADAM_B1 = 0.9
ADAM_B2 = 0.999
ADAM_EPS = 1e-08
ADAM_WD = 0.01
ADAM_STEP = 10

BF = jnp.bfloat16
F32 = jnp.float32
MESH = pl.DeviceIdType.MESH
SDS = jax.ShapeDtypeStruct
N_CHIPS = 4
N_DEV = 8
GATHER_FIRST_ID, GATHER_REST_ID, GRAD_REST_ID, GRAD_FFN0_ID, SHARE_REST_ID, SHARE_FIRST_ID = 1, 2, 3, 4, 5, 6
VMEM_LIMIT_MB = 56
ROW_TILE_D = 256
ROW_TILE_F = 256
HALO = 8
HALO_BF = 16
CF_HALO = 32
POOL_HALO = 16


def _cp(n_axes):
    return pltpu.CompilerParams(dimension_semantics=("arbitrary",) * n_axes,
                                vmem_limit_bytes=VMEM_LIMIT_MB * 1024 * 1024)


def _tile(n, pref):
    t = min(n, pref)
    assert n % t == 0, (n, pref)
    return t


def _const(shape):
    nd = len(shape)
    return pl.BlockSpec(shape, lambda *_: (0,) * nd)


def _vecs(rows, width):
    v = jnp.stack([r.reshape(width).astype(F32) for r in rows])
    pad = (-v.shape[0]) % 8
    return jnp.pad(v, ((0, pad), (0, 0)))


def _sigmoid(v):
    return 0.5 * jnp.tanh(0.5 * v) + 0.5


def _down(prev8, g, k):
    n = g.shape[0]
    return pltpu.roll(jnp.concatenate([prev8, g], axis=0), k, 0)[8:8 + n]


def _up(g, next8, k):
    n = g.shape[0]
    return pltpu.roll(jnp.concatenate([g, next8], axis=0), n + 8 - k, 0)[0:n]


def _fold8(v):
    acc = v[0:8]
    for r in range(8, v.shape[0], 8):
        acc = acc + v[r:r + 8]
    return acc


def _mm(name, a, b, out_sds, grid, a_spec, b_spec, o_spec, acc_shape, dn):
    nk = grid[2]

    def body(a_ref, b_ref, o_ref, *acc):
        part = lax.dot_general(a_ref[...], b_ref[...], dn, preferred_element_type=F32)
        if nk == 1:
            o_ref[...] = part.astype(o_ref.dtype)
        else:
            acc_ref = acc[0]
            k = pl.program_id(2)

            @pl.when(k == 0)
            def _():
                acc_ref[...] = part

            @pl.when(k > 0)
            def _():
                acc_ref[...] += part

            @pl.when(k == nk - 1)
            def _():
                o_ref[...] = acc_ref[...].astype(o_ref.dtype)

    scratch = [] if nk == 1 else [pltpu.VMEM(acc_shape, F32)]
    return pl.pallas_call(body, name=name, out_shape=out_sds, grid=grid, in_specs=[a_spec, b_spec],
                          out_specs=o_spec, scratch_shapes=scratch, compiler_params=_cp(3))(a, b)


NN = (((1,), (0,)), ((), ()))
NT = (((1,), (1,)), ((), ()))
TN = (((0,), (0,)), ((), ()))


def _mm_nn(name, a, w, layer, col_sharded, out_dtype, tm=1024, tn=None):
    m, k = a.shape
    tm = _tile(m, tm)
    if col_sharded:
        n4 = w.shape[3]
        tn = n4 if tn is None else tn
        tpc = n4 // tn
        n = N_CHIPS * n4
        b_spec = pl.BlockSpec((None, None, k, tn), lambda i, j, kk: (layer, j // tpc, 0, j % tpc))
    else:
        n = w.shape[2]
        tn = n if tn is None else tn
        b_spec = pl.BlockSpec((None, k, tn), lambda i, j, kk: (layer, 0, j))
    return _mm(name, a, w, SDS((m, n), out_dtype), (m // tm, n // tn, 1),
               pl.BlockSpec((tm, k), lambda i, j, kk: (i, 0)), b_spec,
               pl.BlockSpec((tm, tn), lambda i, j, kk: (i, j)), None, NN)


def _mm_nt(name, g, w, layer, col_sharded, out_dtype, tm=1024, tn=None):
    m, n = g.shape
    if col_sharded:
        return _mm_nt_staged(name, g, w, layer, out_dtype)
    tm = _tile(m, tm)
    kdim = w.shape[1]
    tn = kdim if tn is None else tn
    return _mm(name, g, w, SDS((m, kdim), out_dtype), (m // tm, kdim // tn, 1),
               pl.BlockSpec((tm, n), lambda i, j, kk: (i, 0)),
               pl.BlockSpec((None, tn, n), lambda i, j, kk: (layer, j, 0)),
               pl.BlockSpec((tm, tn), lambda i, j, kk: (i, j)), None, NT)


def _mm_nt_staged(name, g, w, layer, out_dtype, tm=512):
    m, n = g.shape
    kdim, n4 = w.shape[2], w.shape[3]
    tm = _tile(m, tm)

    def body(g_ref, w_hbm, o_ref, wbuf, sems):
        @pl.when(pl.program_id(0) == 0)
        def _():
            cps = [pltpu.make_async_copy(w_hbm.at[layer, j], wbuf.at[:, pl.ds(j * n4, n4)], sems.at[j])
                   for j in range(N_CHIPS)]
            for cp in cps:
                cp.start()
            for cp in cps:
                cp.wait()

        o_ref[...] = lax.dot_general(g_ref[...], wbuf[...], NT, preferred_element_type=F32).astype(o_ref.dtype)

    return pl.pallas_call(body, name=name, out_shape=SDS((m, kdim), out_dtype), grid=(m // tm,),
                          in_specs=[pl.BlockSpec((tm, n), lambda i: (i, 0)), pl.BlockSpec(memory_space=pltpu.HBM)],
                          out_specs=pl.BlockSpec((tm, kdim), lambda i: (i, 0)),
                          scratch_shapes=[pltpu.VMEM((kdim, n), BF), pltpu.SemaphoreType.DMA((N_CHIPS,))],
                          compiler_params=_cp(1))(g, w)


def _mm_tn(name, a, g, col_sharded, tk=None, tn=None, ts=2048):
    s, k = a.shape
    n = g.shape[1]
    ts = _tile(s, ts)
    tk = k if tk is None else tk
    if col_sharded:
        n4 = n // N_CHIPS
        tn = n4 if tn is None else tn
        tpc = n4 // tn
        out_sds = SDS((N_CHIPS, k, n4), BF)
        o_spec = pl.BlockSpec((None, tk, tn), lambda i, j, ss: (j // tpc, i, j % tpc))
    else:
        tn = n if tn is None else tn
        out_sds = SDS((k, n), BF)
        o_spec = pl.BlockSpec((tk, tn), lambda i, j, ss: (i, j))
    return _mm(name, a, g, out_sds, (k // tk, n // tn, s // ts),
               pl.BlockSpec((ts, tk), lambda i, j, ss: (ss, i)),
               pl.BlockSpec((ts, tn), lambda i, j, ss: (ss, j)), o_spec, (tk, tn), TN)


def _rows(tm, width, col=0):
    return pl.BlockSpec((tm, width), lambda i: (i, col))


def _pro_fwd(name, x, vec):
    s = x.shape[0]
    tm = _tile(s, ROW_TILE_D)

    def body(x_ref, v_ref, h_ref):
        xv = x_ref[...]
        r = lax.rsqrt(jnp.mean(xv * xv, axis=-1, keepdims=True) + RMS_EPS)
        a = v_ref[0:1, :] * (1.0 + v_ref[1:2, :])
        h_ref[...] = (xv * r * a + v_ref[2:3, :]).astype(h_ref.dtype)

    return pl.pallas_call(body, name=name, out_shape=SDS((s, D), BF), grid=(s // tm,),
                          in_specs=[_rows(tm, D), _const((8, D))], out_specs=_rows(tm, D),
                          compiler_params=_cp(1))(x, vec)


def _epi_pro_fwd(name, x, m, vec, with_next):
    s = x.shape[0]
    tm = _tile(s, ROW_TILE_D)

    def body(x_ref, m_ref, v_ref, o_ref, *h_ref):
        mv = m_ref[...] + v_ref[2:3, :]
        rm = lax.rsqrt(jnp.mean(mv * mv, axis=-1, keepdims=True) + RMS_EPS)
        xo = x_ref[...] + v_ref[0:1, :] * (mv * rm * v_ref[1:2, :])
        o_ref[...] = xo
        if with_next:
            r = lax.rsqrt(jnp.mean(xo * xo, axis=-1, keepdims=True) + RMS_EPS)
            a = v_ref[3:4, :] * (1.0 + v_ref[4:5, :])
            h_ref[0][...] = (xo * r * a + v_ref[5:6, :]).astype(BF)

    out_shape = (SDS((s, D), F32),) + ((SDS((s, D), BF),) if with_next else ())
    res = pl.pallas_call(body, name=name, out_shape=out_shape, grid=(s // tm,),
                         in_specs=[_rows(tm, D), _rows(tm, D), _const((8, D))],
                         out_specs=tuple([_rows(tm, D)] * len(out_shape)), compiler_params=_cp(1))(x, m, vec)
    return res if with_next else (res[0], None)


def _loss_fwd_bwd(y, t):
    s = y.shape[0]
    tm = _tile(s, ROW_TILE_D)

    def body(y_ref, t_ref, dy_ref, acc_ref):
        @pl.when(pl.program_id(0) == 0)
        def _():
            acc_ref[...] = jnp.zeros_like(acc_ref)

        e = y_ref[...] - t_ref[...]
        dy_ref[...] = e * (1.0 / D)
        acc_ref[0:1, :] += jnp.sum(e * e, axis=0, keepdims=True) * (0.5 / D)

    return pl.pallas_call(body, name="loss", out_shape=(SDS((s, D), F32), SDS((8, D), F32)), grid=(s // tm,),
                          in_specs=[_rows(tm, D), _rows(tm, D)], out_specs=(_rows(tm, D), _const((8, D))),
                          compiler_params=_cp(1))(y, t)


def _epi_bwd_rows(dv, mv, g, ngb, dm_ref, acc_ref, row):
    rm = lax.rsqrt(jnp.mean(mv * mv, axis=-1, keepdims=True) + RMS_EPS)
    mn = mv * rm
    dmn = dv * (g * ngb)
    dm = rm * (dmn - mn * jnp.mean(dmn * mn, axis=-1, keepdims=True))
    dm_ref[...] = dm.astype(dm_ref.dtype)
    t = jnp.sum(dv * mn, axis=0, keepdims=True)
    acc_ref[row:row + 1, :] += t * ngb
    acc_ref[row + 1:row + 2, :] += t * g
    acc_ref[row + 2:row + 3, :] += jnp.sum(dm, axis=0, keepdims=True)


def _epi_bwd(name, dxo, m, vec):
    s = dxo.shape[0]
    tm = _tile(s, ROW_TILE_D)

    def body(d_ref, m_ref, v_ref, dm_ref, acc_ref):
        @pl.when(pl.program_id(0) == 0)
        def _():
            acc_ref[...] = jnp.zeros_like(acc_ref)

        _epi_bwd_rows(d_ref[...], m_ref[...] + v_ref[2:3, :], v_ref[0:1, :], v_ref[1:2, :], dm_ref, acc_ref, 0)

    return pl.pallas_call(body, name=name, out_shape=(SDS((s, D), BF), SDS((8, D), F32)), grid=(s // tm,),
                          in_specs=[_rows(tm, D), _rows(tm, D), _const((8, D))],
                          out_specs=(_rows(tm, D), _const((8, D))), compiler_params=_cp(1))(dxo, m, vec)


def _pro_epi_bwd(name, dh, x, dxo, m, vec):
    s = x.shape[0]
    tm = _tile(s, ROW_TILE_D)

    def body(dh_ref, x_ref, d_ref, m_ref, v_ref, dx_ref, dm_ref, acc_ref):
        @pl.when(pl.program_id(0) == 0)
        def _():
            acc_ref[...] = jnp.zeros_like(acc_ref)

        dx = _pro_bwd_rows(dh_ref[...].astype(F32), x_ref[...], d_ref[...], v_ref[0:1, :], v_ref[1:2, :], acc_ref)
        dx_ref[...] = dx
        _epi_bwd_rows(dx, m_ref[...] + v_ref[4:5, :], v_ref[2:3, :], v_ref[3:4, :], dm_ref, acc_ref, 3)

    return pl.pallas_call(body, name=name, out_shape=(SDS((s, D), F32), SDS((s, D), BF), SDS((8, D), F32)),
                          grid=(s // tm,),
                          in_specs=[_rows(tm, D), _rows(tm, D), _rows(tm, D), _rows(tm, D), _const((8, D))],
                          out_specs=(_rows(tm, D), _rows(tm, D), _const((8, D))),
                          compiler_params=_cp(1))(dh, x, dxo, m, vec)


def _pro_bwd_rows(dhv, xv, dxo, ng, sc, acc_ref):
    sc1 = 1.0 + sc
    r = lax.rsqrt(jnp.mean(xv * xv, axis=-1, keepdims=True) + RMS_EPS)
    xn = xv * r
    dxn = dhv * (ng * sc1)
    t = jnp.sum(dhv * xn, axis=0, keepdims=True)
    acc_ref[0:1, :] += jnp.sum(dhv, axis=0, keepdims=True)
    acc_ref[1:2, :] += t * ng
    acc_ref[2:3, :] += t * sc1
    return dxo + r * (dxn - xn * jnp.mean(dxn * xn, axis=-1, keepdims=True))


def _pro_bwd(name, dh, x, dxo, vec):
    s = x.shape[0]
    tm = _tile(s, ROW_TILE_D)

    def body(dh_ref, x_ref, d_ref, v_ref, dx_ref, acc_ref):
        @pl.when(pl.program_id(0) == 0)
        def _():
            acc_ref[...] = jnp.zeros_like(acc_ref)

        dx_ref[...] = _pro_bwd_rows(dh_ref[...].astype(F32), x_ref[...], d_ref[...], v_ref[0:1, :], v_ref[1:2, :],
                                    acc_ref)

    return pl.pallas_call(body, name=name, out_shape=(SDS((s, D), F32), SDS((8, D), F32)), grid=(s // tm,),
                          in_specs=[_rows(tm, D), _rows(tm, D), _rows(tm, D), _const((8, D))],
                          out_specs=(_rows(tm, D), _const((8, D))), compiler_params=_cp(1))(dh, x, dxo, vec)


def _carry_up(buf, tm, first, halo=HALO):
    @pl.when(first)
    def _():
        buf[tm:tm + halo, :] = jnp.zeros((halo, buf.shape[1]), F32)

    @pl.when(jnp.logical_not(first))
    def _():
        buf[tm:tm + halo, :] = buf[0:halo, :]


def _ffn_act_fwd(name, up, vec):
    s = up.shape[0]
    tm = _tile(s, ROW_TILE_F)
    rb_n = _tile(tm, 64)
    cw = 256

    def body(g_ref, v_ref, w_ref, a_ref, carry):
        @pl.when(pl.program_id(0) == 0)
        def _():
            carry[...] = jnp.zeros_like(carry)

        for cc in range(F // cw):
            cols = slice(cc * cw, (cc + 1) * cw)
            w0, w1, w2, b = w_ref[0:1, cols], w_ref[1:2, cols], w_ref[2:3, cols], w_ref[3:4, cols]

            def rb_body(rb, prev8):
                rows = pl.ds(pl.multiple_of(rb * rb_n, rb_n), rb_n)
                g = g_ref[rows, cols].astype(F32)
                gc = w2 * g + w1 * _down(prev8, g, 1) + w0 * _down(prev8, g, 2) + b
                a_ref[rows, cols] = (gc * _sigmoid(gc) * v_ref[rows, cols].astype(F32)).astype(a_ref.dtype)
                return g[rb_n - 8:rb_n]

            carry[:, cols] = lax.fori_loop(0, tm // rb_n, rb_body, carry[:, cols])

    return pl.pallas_call(body, name=name, out_shape=SDS((s, F), BF), grid=(s // tm,),
                          in_specs=[_rows(tm, F, 0), _rows(tm, F, 1), _const((8, F))], out_specs=_rows(tm, F),
                          scratch_shapes=[pltpu.VMEM((HALO, F), F32)], compiler_params=_cp(1))(up, up, vec)


def _prev_halo_spec(nt, tm, width, col):
    per = tm // HALO_BF
    return pl.BlockSpec((HALO_BF, width), lambda i: (jnp.maximum((nt - 1 - i) * per - 1, 0), col))


def _ffn_act_bwd(name, up, da, vec):
    s = up.shape[0]
    tm = _tile(s, ROW_TILE_F)
    nt = s // tm
    rev = lambda col: pl.BlockSpec((tm, F), lambda i: (nt - 1 - i, col))

    rb_n = _tile(tm, 64)
    nrb = tm // rb_n
    cw = 128

    def body(g_ref, gp_ref, v_ref, da_ref, w_ref, dup_ref, acc_ref, carry):
        step = pl.program_id(0)

        @pl.when(step == 0)
        def _():
            acc_ref[...] = jnp.zeros_like(acc_ref)
            carry[...] = jnp.zeros_like(carry)

        for cc in range(F // cw):
            cols = slice(cc * cw, (cc + 1) * cw)
            w0, w1, w2, b = w_ref[0:1, cols], w_ref[1:2, cols], w_ref[2:3, cols], w_ref[3:4, cols]
            halo = jnp.where(step < nt - 1, gp_ref[:, cols].astype(F32)[HALO_BF - 8:HALO_BF], 0.0)

            def rb_body(it, st):
                nxt8, ab, a0, a1, a2 = st
                rb = nrb - 1 - it
                r0 = pl.multiple_of(rb * rb_n, rb_n)
                rows = pl.ds(r0, rb_n)
                g = g_ref[rows, cols].astype(F32)
                ra = pl.multiple_of(jnp.maximum(r0 - HALO_BF, 0), HALO_BF)
                above = g_ref[pl.ds(ra, HALO_BF), cols].astype(F32)[HALO_BF - 8:HALO_BF]
                prev8 = jnp.where(rb == 0, halo, above)
                g1 = _down(prev8, g, 1)
                g2 = _down(prev8, g, 2)
                gc = w2 * g + w1 * g1 + w0 * g2 + b
                sg = _sigmoid(gc)
                sl = gc * sg
                val = v_ref[rows, cols].astype(F32)
                dav = da_ref[rows, cols].astype(F32)
                dup_ref[rows, F + cc * cw:F + (cc + 1) * cw] = (dav * sl).astype(dup_ref.dtype)
                dgc = (dav * val) * (sg + sl * (1.0 - sg))
                dup_ref[rows, cols] = (w2 * dgc + w1 * _up(dgc, nxt8, 1) + w0 * _up(dgc, nxt8, 2)).astype(dup_ref.dtype)
                return (dgc[0:8], ab + _fold8(dgc), a0 + _fold8(g2 * dgc), a1 + _fold8(g1 * dgc),
                        a2 + _fold8(g * dgc))

            z = jnp.zeros((8, cw), F32)
            nxt8, ab, a0, a1, a2 = lax.fori_loop(0, nrb, rb_body, (carry[:, cols], z, z, z, z))
            carry[:, cols] = nxt8
            acc_ref[0:1, cols] += jnp.sum(ab, axis=0, keepdims=True)
            acc_ref[1:2, cols] += jnp.sum(a0, axis=0, keepdims=True)
            acc_ref[2:3, cols] += jnp.sum(a1, axis=0, keepdims=True)
            acc_ref[3:4, cols] += jnp.sum(a2, axis=0, keepdims=True)

    return pl.pallas_call(
        body, name=name, out_shape=(SDS((s, 2 * F), BF), SDS((8, F), F32)), grid=(nt,),
        in_specs=[rev(0), _prev_halo_spec(nt, tm, F, 0), rev(1), rev(0), _const((8, F))],
        out_specs=(pl.BlockSpec((tm, 2 * F), lambda i: (nt - 1 - i, 0)), _const((8, F))),
        scratch_shapes=[pltpu.VMEM((HALO, F), F32)], compiler_params=_cp(1))(up, up, up, da, vec)


def _sc_act_fwd(name, p, vec):
    s = p.shape[0]
    tm = _tile(s, ROW_TILE_D)

    rb_n = _tile(tm, 64)
    cw = 256

    def body(b_ref, c_ref, h_ref, w_ref, z_ref, carry):
        @pl.when(pl.program_id(0) == 0)
        def _():
            carry[...] = jnp.zeros_like(carry)

        for cc in range(D // cw):
            cols = slice(cc * cw, (cc + 1) * cw)
            w0, w1, w2 = w_ref[0:1, cols], w_ref[1:2, cols], w_ref[2:3, cols]

            def rb_body(rb, prev8):
                rows = pl.ds(pl.multiple_of(rb * rb_n, rb_n), rb_n)
                q = c_ref[rows, cols].astype(F32) * h_ref[rows, cols].astype(F32)
                u = w2 * q + w1 * _down(prev8, q, 1) + w0 * _down(prev8, q, 2)
                z_ref[rows, cols] = (b_ref[rows, cols].astype(F32) * u).astype(z_ref.dtype)
                return q[rb_n - 8:rb_n]

            carry[:, cols] = lax.fori_loop(0, tm // rb_n, rb_body, carry[:, cols])

    return pl.pallas_call(body, name=name, out_shape=SDS((s, D), BF), grid=(s // tm,),
                          in_specs=[_rows(tm, D, 0), _rows(tm, D, 1), _rows(tm, D, 2), _const((8, D))],
                          out_specs=_rows(tm, D), scratch_shapes=[pltpu.VMEM((HALO, D), F32)],
                          compiler_params=_cp(1))(p, p, p, vec)


def _sc_act_bwd(name, p, dz, vec):
    s = p.shape[0]
    tm = _tile(s, ROW_TILE_D)
    nt = s // tm
    rev = lambda col: pl.BlockSpec((tm, D), lambda i: (nt - 1 - i, col))

    rb_n = _tile(tm, 64)
    nrb = tm // rb_n
    cw = 128

    def body(b_ref, c_ref, h_ref, cp_ref, hp_ref, dz_ref, w_ref, z_ref, dp_ref, acc_ref, carry):
        step = pl.program_id(0)

        @pl.when(step == 0)
        def _():
            acc_ref[...] = jnp.zeros_like(acc_ref)
            carry[...] = jnp.zeros_like(carry)

        for cc in range(D // cw):
            cols = slice(cc * cw, (cc + 1) * cw)
            w0, w1, w2 = w_ref[0:1, cols], w_ref[1:2, cols], w_ref[2:3, cols]
            halo = jnp.where(step < nt - 1,
                             (cp_ref[:, cols].astype(F32) * hp_ref[:, cols].astype(F32))[HALO_BF - 8:HALO_BF], 0.0)

            def rb_body(it, st):
                nxt8, a0, a1, a2 = st
                rb = nrb - 1 - it
                r0 = pl.multiple_of(rb * rb_n, rb_n)
                rows = pl.ds(r0, rb_n)
                cg = c_ref[rows, cols].astype(F32)
                hin = h_ref[rows, cols].astype(F32)
                bg = b_ref[rows, cols].astype(F32)
                q = cg * hin
                up_rows = pl.ds(pl.multiple_of(jnp.maximum(r0 - HALO_BF, 0), HALO_BF), HALO_BF)
                above = (c_ref[up_rows, cols].astype(F32) * h_ref[up_rows, cols].astype(F32))[HALO_BF - 8:HALO_BF]
                prev8 = jnp.where(rb == 0, halo, above)
                q1 = _down(prev8, q, 1)
                q2 = _down(prev8, q, 2)
                u = w2 * q + w1 * q1 + w0 * q2
                dzv = dz_ref[rows, cols].astype(F32)
                z_ref[rows, cols] = (bg * u).astype(z_ref.dtype)
                dp_ref[rows, cols] = (dzv * u).astype(dp_ref.dtype)
                du = dzv * bg
                dq = w2 * du + w1 * _up(du, nxt8, 1) + w0 * _up(du, nxt8, 2)
                dp_ref[rows, D + cc * cw:D + (cc + 1) * cw] = (dq * hin).astype(dp_ref.dtype)
                dp_ref[rows, 2 * D + cc * cw:2 * D + (cc + 1) * cw] = (dq * cg).astype(dp_ref.dtype)
                return du[0:8], a0 + _fold8(q2 * du), a1 + _fold8(q1 * du), a2 + _fold8(q * du)

            z = jnp.zeros((8, cw), F32)
            nxt8, a0, a1, a2 = lax.fori_loop(0, nrb, rb_body, (carry[:, cols], z, z, z))
            carry[:, cols] = nxt8
            acc_ref[0:1, cols] += jnp.sum(a0, axis=0, keepdims=True)
            acc_ref[1:2, cols] += jnp.sum(a1, axis=0, keepdims=True)
            acc_ref[2:3, cols] += jnp.sum(a2, axis=0, keepdims=True)

    return pl.pallas_call(
        body, name=name, out_shape=(SDS((s, D), BF), SDS((s, 3 * D), BF), SDS((8, D), F32)), grid=(nt,),
        in_specs=[rev(0), rev(1), rev(2), _prev_halo_spec(nt, tm, D, 1), _prev_halo_spec(nt, tm, D, 2), rev(0),
                  _const((8, D))],
        out_specs=(rev(0), pl.BlockSpec((tm, 3 * D), lambda i: (nt - 1 - i, 0)), _const((8, D))),
        scratch_shapes=[pltpu.VMEM((HALO, D), F32)], compiler_params=_cp(1))(p, p, p, p, p, dz, vec)


CF_ROW_BLOCK = 32
CF_LANES = 128


def _tap_conv_block(ext, rb_n, tap_of_offset, t_ref, cols, init, u=None, accs=None):
    n = ext.shape[0]
    out = init
    for b in range(8):
        rolled = ext if b == 0 else pltpu.roll(ext, n - b, 0)
        for a in range(n // 8):
            k = tap_of_offset(8 * a + b)
            if k is None:
                continue
            sl = rolled[8 * a:8 * a + rb_n]
            out = out + t_ref[k:k + 1, cols] * sl
            if accs is not None:
                accs[k] = accs[k] + _fold8(u * sl)
    return out


def _cf_act_fwd(name, p, taps, vec):
    s = p.shape[0]
    tm = _tile(s, ROW_TILE_D)
    rb_n = _tile(tm, CF_ROW_BLOCK)
    base = CF_HALO - (CF_TAPS - 1)
    tap_of = lambda off: off - base if 0 <= off - base < CF_TAPS else None

    def body(a_ref, g_ref, t_ref, v_ref, w_ref, cv_ref, buf):
        first = pl.program_id(0) == 0

        @pl.when(first)
        def _():
            buf[0:CF_HALO, :] = jnp.zeros((CF_HALO, D), F32)

        @pl.when(jnp.logical_not(first))
        def _():
            buf[0:CF_HALO, :] = buf[tm:tm + CF_HALO, :]

        a = a_ref[...].astype(F32) + v_ref[0:1, :]
        g = g_ref[...].astype(F32) + v_ref[1:2, :]
        buf[CF_HALO:CF_HALO + tm, :] = a * _sigmoid(g)
        for ci in range(D // CF_LANES):
            cols = slice(ci * CF_LANES, (ci + 1) * CF_LANES)

            def rb_body(rb, carry):
                r0 = pl.multiple_of(rb * rb_n, rb_n)
                ext = buf[pl.ds(r0, rb_n + CF_HALO), cols]
                init = jnp.zeros((rb_n, CF_LANES), F32) + v_ref[2:3, cols]
                cv_ref[pl.ds(r0, rb_n), cols] = _tap_conv_block(ext, rb_n, tap_of, t_ref, cols, init)
                return carry

            lax.fori_loop(0, tm // rb_n, rb_body, 0)
        cv = cv_ref[...]
        mu = jnp.mean(cv, axis=-1, keepdims=True)
        cc = cv - mu
        rstd = lax.rsqrt(jnp.mean(cc * cc, axis=-1, keepdims=True) + LN_EPS)
        ln = cc * rstd * v_ref[3:4, :] + v_ref[4:5, :]
        w_ref[...] = (ln * _sigmoid(ln)).astype(w_ref.dtype)

    return pl.pallas_call(body, name=name, out_shape=(SDS((s, D), BF), SDS((s, D), F32)), grid=(s // tm,),
                          in_specs=[_rows(tm, D, 0), _rows(tm, D, 1), _const((32, D)), _const((8, D))],
                          out_specs=(_rows(tm, D), _rows(tm, D)),
                          scratch_shapes=[pltpu.VMEM((tm + CF_HALO, D), F32)], compiler_params=_cp(1))(p, p, taps, vec)


def _cf_act_bwd(name, p, cv, dw, taps, vec):
    s = p.shape[0]
    tm = _tile(s, ROW_TILE_D)
    nt = s // tm
    rev = lambda col: pl.BlockSpec((tm, D), lambda i: (nt - 1 - i, col))
    rb_n = _tile(tm, CF_ROW_BLOCK)
    tap_of = lambda off: CF_TAPS - 1 - off if off < CF_TAPS else None

    def body(a_ref, g_ref, cv_ref, dw_ref, t_ref, v_ref, dp_ref, tacc_ref, acc_ref, nbuf, ubuf, dubuf):
        step = pl.program_id(0)

        @pl.when(step == 0)
        def _():
            acc_ref[...] = jnp.zeros_like(acc_ref)
            tacc_ref[...] = jnp.zeros_like(tacc_ref)

        a = a_ref[...].astype(F32) + v_ref[0:1, :]
        g = g_ref[...].astype(F32) + v_ref[1:2, :]
        sg = _sigmoid(g)
        ubuf[...] = a * sg
        cvv = cv_ref[...]
        mu = jnp.mean(cvv, axis=-1, keepdims=True)
        cc = cvv - mu
        rstd = lax.rsqrt(jnp.mean(cc * cc, axis=-1, keepdims=True) + LN_EPS)
        vhat = cc * rstd
        ln = vhat * v_ref[3:4, :] + v_ref[4:5, :]
        s2 = _sigmoid(ln)
        dln = dw_ref[...].astype(F32) * (s2 * (1.0 + ln * (1.0 - s2)))
        acc_ref[1:2, :] += jnp.sum(dln * vhat, axis=0, keepdims=True)
        acc_ref[2:3, :] += jnp.sum(dln, axis=0, keepdims=True)
        dvh = dln * v_ref[3:4, :]
        dcv = rstd * (dvh - jnp.mean(dvh, axis=-1, keepdims=True)
                      - vhat * jnp.mean(dvh * vhat, axis=-1, keepdims=True))
        acc_ref[0:1, :] += jnp.sum(dcv, axis=0, keepdims=True)
        _carry_up(nbuf, tm, step == 0, CF_HALO)
        nbuf[0:tm, :] = dcv
        for ci in range(D // CF_LANES):
            cols = slice(ci * CF_LANES, (ci + 1) * CF_LANES)

            def rb_body(rb, accs):
                r0 = pl.multiple_of(rb * rb_n, rb_n)
                ext = nbuf[pl.ds(r0, rb_n + CF_HALO), cols]
                accs = list(accs)
                dubuf[pl.ds(r0, rb_n), cols] = _tap_conv_block(
                    ext, rb_n, tap_of, t_ref, cols, jnp.zeros((rb_n, CF_LANES), F32),
                    ubuf[pl.ds(r0, rb_n), cols], accs)
                return tuple(accs)

            z = jnp.zeros((8, CF_LANES), F32)
            accs = lax.fori_loop(0, tm // rb_n, rb_body, tuple([z] * CF_TAPS))
            for k in range(CF_TAPS):
                tacc_ref[k:k + 1, cols] += jnp.sum(accs[k], axis=0, keepdims=True)
        a = a_ref[...].astype(F32) + v_ref[0:1, :]
        sg = _sigmoid(g_ref[...].astype(F32) + v_ref[1:2, :])
        da = dubuf[...] * sg
        dg = da * a * (1.0 - sg)
        dp_ref[:, 0:D] = da.astype(dp_ref.dtype)
        dp_ref[:, D:2 * D] = dg.astype(dp_ref.dtype)
        acc_ref[3:4, :] += jnp.sum(da, axis=0, keepdims=True)
        acc_ref[4:5, :] += jnp.sum(dg, axis=0, keepdims=True)

    return pl.pallas_call(
        body, name=name, out_shape=(SDS((s, 2 * D), BF), SDS((32, D), F32), SDS((8, D), F32)), grid=(nt,),
        in_specs=[rev(0), rev(1), rev(0), rev(0), _const((32, D)), _const((8, D))],
        out_specs=(pl.BlockSpec((tm, 2 * D), lambda i: (nt - 1 - i, 0)), _const((32, D)), _const((8, D))),
        scratch_shapes=[pltpu.VMEM((tm + CF_HALO, D), F32), pltpu.VMEM((tm, D), F32), pltpu.VMEM((tm, D), F32)],
        compiler_params=_cp(1))(p, p, cv, dw, taps, vec)


def _inv_count(row0, tm, window):
    t = row0 + lax.broadcasted_iota(jnp.int32, (tm, 1), 0)
    return 1.0 / jnp.minimum(t + 1, window).astype(F32)


def _pool_fwd(name, x, w, vec):
    s = x.shape[0]
    tm = _tile(s, ROW_TILE_D)
    G = POOL_GROUP

    def body(x_ref, w_ref, v_ref, pl_ref, m_ref, buf):
        i = pl.program_id(0)

        @pl.when(i == 0)
        def _():
            buf[0:POOL_HALO, :] = jnp.zeros((POOL_HALO, D), F32)

        @pl.when(i > 0)
        def _():
            buf[0:POOL_HALO, :] = buf[tm:tm + POOL_HALO, :]

        xv = x_ref[...]
        r = lax.rsqrt(jnp.mean(xv * xv, axis=-1, keepdims=True) + RMS_EPS)
        buf[POOL_HALO:POOL_HALO + tm, :] = xv * r * (v_ref[0:1, :] * (1.0 + v_ref[1:2, :])) + v_ref[2:3, :]
        for gi, win in enumerate(POOL_WINDOWS):
            cols = slice(gi * G, (gi + 1) * G)
            acc = buf[POOL_HALO:POOL_HALO + tm, cols]
            hg = acc
            for j in range(1, win):
                acc = acc + buf[POOL_HALO - j:POOL_HALO - j + tm, cols]
            pooled = (acc * _inv_count(i * tm, tm, win) - hg).astype(BF)
            pl_ref[:, cols] = pooled
            yg = jnp.dot(pooled, w_ref[gi], preferred_element_type=F32)
            m_ref[:, cols] = (yg + v_ref[3:4, cols]) * v_ref[4:5, cols]

    return pl.pallas_call(body, name=name, out_shape=(SDS((s, D), BF), SDS((s, D), F32)), grid=(s // tm,),
                          in_specs=[_rows(tm, D), _const((4, G, G)), _const((8, D))],
                          out_specs=(_rows(tm, D), _rows(tm, D)),
                          scratch_shapes=[pltpu.VMEM((tm + POOL_HALO, D), F32)], compiler_params=_cp(1))(x, w, vec)


def _pool_bwd(name, pooled, dm, w, vec):
    s = pooled.shape[0]
    tm = _tile(s, ROW_TILE_D)
    nt = s // tm
    G = POOL_GROUP
    rev = pl.BlockSpec((tm, D), lambda i: (nt - 1 - i, 0))

    def body(p_ref, dm_ref, w_ref, v_ref, dh_ref, dw_ref, acc_ref, nbuf):
        step = pl.program_id(0)
        row0 = (nt - 1 - step) * tm

        @pl.when(step == 0)
        def _():
            acc_ref[...] = jnp.zeros_like(acc_ref)
            dw_ref[...] = jnp.zeros_like(dw_ref)

        _carry_up(nbuf, tm, step == 0, POOL_HALO)
        dmv = dm_ref[...].astype(F32)
        acc_ref[0:1, :] += jnp.sum(dmv, axis=0, keepdims=True) * v_ref[1:2, :]
        dps = []
        for gi, win in enumerate(POOL_WINDOWS):
            cols = slice(gi * G, (gi + 1) * G)
            pg = p_ref[:, cols]
            yb = jnp.dot(pg, w_ref[gi], preferred_element_type=F32) + v_ref[0:1, cols]
            acc_ref[1:2, cols] += jnp.sum(dmv[:, cols] * yb, axis=0, keepdims=True)
            dy = (dmv[:, cols] * v_ref[1:2, cols]).astype(BF)
            dw_ref[gi] += lax.dot_general(pg, dy, TN, preferred_element_type=F32)
            dpg = lax.dot_general(dy, w_ref[gi], NT, preferred_element_type=F32)
            dps.append(dpg)
            nbuf[0:tm, cols] = dpg * _inv_count(row0, tm, win)
        for gi, win in enumerate(POOL_WINDOWS):
            cols = slice(gi * G, (gi + 1) * G)
            acc = nbuf[0:tm, cols]
            for j in range(1, win):
                acc = acc + nbuf[j:j + tm, cols]
            dh_ref[:, cols] = acc - dps[gi]

    return pl.pallas_call(
        body, name=name, out_shape=(SDS((s, D), F32), SDS((4, G, G), F32), SDS((8, D), F32)), grid=(nt,),
        in_specs=[rev, rev, _const((4, G, G)), _const((8, D))],
        out_specs=(rev, _const((4, G, G)), _const((8, D))),
        scratch_shapes=[pltpu.VMEM((tm + POOL_HALO, D), F32)], compiler_params=_cp(1))(pooled, dm, w, vec)


def _row_tile_2d(rows, width, bytes_per_row_elem=4, budget=2 * 1024 * 1024):
    t = max(8, budget // (width * bytes_per_row_elem))
    t = min(rows, 1 << (t.bit_length() - 1))
    while rows % t:
        t //= 2
    return t


def _add_slots(name, r):
    nl, _, k, n = r.shape
    tk = _row_tile_2d(k, n, 16)

    def body(r_ref, o_ref):
        f = lambda i: r_ref[i].astype(F32)
        o_ref[...] = ((f(7) + f(6)) + (f(0) + f(1))) + ((f(2) + f(3)) + (f(4) + f(5)))

    return pl.pallas_call(body, name=name, out_shape=SDS((nl, k, n), F32), grid=(nl, k // tk),
                          in_specs=[pl.BlockSpec((None, N_DEV, tk, n), lambda l, i: (l, 0, i, 0))],
                          out_specs=pl.BlockSpec((None, tk, n), lambda l, i: (l, i, 0)),
                          compiler_params=_cp(2))(r)


def _adamw(name, w, g, m, v):
    rows, width = w.shape
    tm = _row_tile_2d(rows, width, 4, 1024 * 1024)
    c1 = 1.0 - ADAM_B1 ** ADAM_STEP
    c2 = 1.0 - ADAM_B2 ** ADAM_STEP

    def body(w_ref, g_ref, m_ref, v_ref, d_ref, nm_ref, nv_ref):
        gv = g_ref[...]
        nm = ADAM_B1 * m_ref[...] + (1.0 - ADAM_B1) * gv
        nv = ADAM_B2 * v_ref[...] + (1.0 - ADAM_B2) * (gv * gv)
        nm_ref[...] = nm
        nv_ref[...] = nv
        d_ref[...] = -ADAM_LR * ((nm / c1) / (jnp.sqrt(nv / c2) + ADAM_EPS) + ADAM_WD * w_ref[...])

    spec = _rows(tm, width)
    sds = SDS((rows, width), F32)
    return pl.pallas_call(body, name=name, out_shape=(sds, sds, sds), grid=(rows // tm,),
                          in_specs=[spec] * 4, out_specs=(spec,) * 3, compiler_params=_cp(1))(w, g, m, v)


def _mod_fwd(c16, w_mod, b_sh):
    n = w_mod.shape[2]
    tn = _tile(n, 512)

    def body(c_ref, w_ref, b_ref, o_ref):
        cv = c_ref[...]
        ca = (cv * _sigmoid(cv)).astype(BF)
        o_ref[...] = jnp.dot(ca, w_ref[...].astype(BF), preferred_element_type=F32) + b_ref[0:1, :]

    return pl.pallas_call(body, name="mod_fwd", out_shape=SDS((DEPTH, 16, n), F32), grid=(DEPTH, n // tn),
                          in_specs=[_const((16, D)), pl.BlockSpec((None, D, tn), lambda l, j: (l, 0, j)),
                                    pl.BlockSpec((None, 8, tn), lambda l, j: (l, 0, j))],
                          out_specs=pl.BlockSpec((None, 16, tn), lambda l, j: (l, 0, j)),
                          compiler_params=_cp(2))(c16, w_mod, b_sh)


def _mod_bwd(c16, dmod):
    n = dmod.shape[2]
    tn = _tile(n, 512)

    def body(c_ref, d_ref, o_ref):
        cv = c_ref[...]
        ca = (cv * _sigmoid(cv)).astype(BF)
        o_ref[...] = lax.dot_general(ca, d_ref[...].astype(BF), TN, preferred_element_type=F32)

    return pl.pallas_call(body, name="mod_bwd", out_shape=SDS((DEPTH, D, n), F32), grid=(DEPTH, n // tn),
                          in_specs=[_const((16, D)), pl.BlockSpec((None, 16, tn), lambda l, j: (l, 0, j))],
                          out_specs=pl.BlockSpec((None, D, tn), lambda l, j: (l, 0, j)),
                          compiler_params=_cp(2))(c16, dmod)


def _place():
    x, y, c = lax.axis_index("x"), lax.axis_index("y"), lax.axis_index("c")
    other_chips = [(1 - x, y), (x, 1 - y), (1 - x, 1 - y)]
    return x, y, c, other_chips


def _allgather_small(name, v, with_sum):
    m, n = v.shape

    def body(x_ref, out_ref, *rest):
        if with_sum:
            sum_ref, send_sems, recv_sems, local_sem = rest
        else:
            send_sems, recv_sems, local_sem = rest
        x, y, c, chips = _place()
        me, sibling = (x, y, c), (x, y, 1 - c)

        def rows(px, py, pc):
            return out_ref.at[pl.ds((4 * px + 2 * py + pc) * m, m), :]

        def copy(k, block, to, src=None):
            return pltpu.make_async_remote_copy(
                src_ref=rows(*block) if src is None else src, dst_ref=rows(*block),
                send_sem=send_sems.at[k], recv_sem=recv_sems.at[k], device_id=to, device_id_type=MESH)

        mine = pltpu.make_async_copy(x_ref, rows(*me), local_sem)
        mine.start()
        first = [copy(0, me, sibling, src=x_ref)]
        first += [copy(1 + j, me, (*chip, c), src=x_ref) for j, chip in enumerate(chips)]
        for cp in first:
            cp.start()
        passed = [copy(4 + j, (*chip, c), sibling) for j, chip in enumerate(chips)]
        for j, chip in enumerate(chips):
            copy(1 + j, (*chip, c), me).wait_recv()
            passed[j].start()
        copy(0, sibling, me).wait_recv()
        for j, chip in enumerate(chips):
            copy(4 + j, (*chip, 1 - c), me).wait_recv()
        for cp in first + passed:
            cp.wait_send()
        mine.wait()
        if with_sum:
            acc = out_ref[0:m, :]
            for k in range(1, N_DEV):
                acc = acc + out_ref[k * m:(k + 1) * m, :]
            sum_ref[...] = acc

    vm = pl.BlockSpec(memory_space=pltpu.VMEM)
    out_shape = [SDS((N_DEV * m, n), F32)] + ([SDS((m, n), F32)] if with_sum else [])
    res = pl.pallas_call(
        body, name=name, out_shape=tuple(out_shape), in_specs=[vm], out_specs=tuple([vm] * len(out_shape)),
        scratch_shapes=[pltpu.SemaphoreType.DMA((7,)), pltpu.SemaphoreType.DMA((7,)), pltpu.SemaphoreType.DMA],
        compiler_params=pltpu.CompilerParams(vmem_limit_bytes=VMEM_LIMIT_MB * 1024 * 1024))(v)
    return res if with_sum else res[0]


HBM = pl.BlockSpec(memory_space=pltpu.HBM)


def _sem_scratch(n_remote, n_local):
    return [pltpu.SemaphoreType.DMA((n_remote,)), pltpu.SemaphoreType.DMA((n_remote,)),
            pltpu.SemaphoreType.DMA((n_local,))]


DMA_PIECE_BYTES = 1 << 20


def _pieces(src, dst):
    *lead, rows, n = src.shape
    nsplit = max(1, min(rows // 16, (rows * n * jnp.dtype(src.dtype).itemsize) // DMA_PIECE_BYTES))
    while rows % nsplit or (rows // nsplit) % 16:
        nsplit -= 1
    size = rows // nsplit
    out = []
    for idx in itertools.product(*[range(d) for d in lead]):
        for i in range(nsplit):
            sl = tuple(idx) + (pl.ds(i * size, size),)
            out.append((src.at[sl], dst.at[sl]))
    return out


def _local_copies(src, dst, sem):
    return ([pltpu.make_async_copy(s_, d_, sem) for s_, d_ in _pieces(src, dst)],
            pltpu.make_async_copy(src, dst, sem))


def _remote_copies(src, dst, send_sem, recv_sem, to):
    mk = lambda s_, d_: pltpu.make_async_remote_copy(src_ref=s_, dst_ref=d_, send_sem=send_sem, recv_sem=recv_sem,
                                                     device_id=to, device_id_type=MESH)
    return [mk(s_, d_) for s_, d_ in _pieces(src, dst)], mk(src, dst)


def _gather_weights(name, shards, sequencer_id=None):
    nq = len(shards)

    def exchange(ins, outs, send_sems, recv_sems, local_sems, own_barrier):
        x, y, c, chips = _place()
        if own_barrier:
            barrier = pltpu.get_barrier_semaphore()
            for px, py in chips:
                pl.semaphore_signal(barrier, inc=1, device_id=(px, py, c), device_id_type=MESH)
            pl.semaphore_wait(barrier, len(chips))
        me_chip = 2 * x + y
        started, local_all, send_all, recv_all = [], [], [], []
        for q in range(nq):
            cps, whole = _local_copies(ins[q], outs[q].at[:, me_chip], local_sems.at[q])
            started += cps
            local_all.append(whole)
            for r, (px, py) in enumerate(chips):
                k = 3 * q + r
                cps, whole = _remote_copies(ins[q], outs[q].at[:, me_chip], send_sems.at[k], recv_sems.at[k], (px, py, c))
                started += cps
                send_all.append(whole)
                recv_all.append(_remote_copies(ins[q], outs[q].at[:, 2 * px + py], send_sems.at[k], recv_sems.at[k],
                                               (px, py, c))[1])
        for cp in started:
            cp.start()
        for cp in recv_all:
            cp.wait_recv()
        for cp in send_all:
            cp.wait_send()
        for cp in local_all:
            cp.wait()

    out_shape = tuple(SDS((s.shape[0], N_CHIPS) + s.shape[1:], s.dtype) for s in shards)
    return _launch_exchange(exchange, name, shards, out_shape, _sem_scratch(3 * nq, nq), sequencer_id)


def _launch_exchange(exchange, name, arrays, out_shape, sems, sequencer_id):
    n = len(arrays)
    if sequencer_id is None:
        def body(*refs):
            exchange(refs[:n], refs[n:2 * n], *refs[2 * n:], own_barrier=False)

        return pl.pallas_call(body, name=name, out_shape=out_shape, in_specs=[HBM] * n,
                              out_specs=tuple([HBM] * n), scratch_shapes=sems)(*arrays)

    in_refs = [jax.new_ref(a, memory_space=pltpu.MemorySpace.HBM) for a in arrays]
    out_refs = [jax.empty_ref(o, memory_space=pltpu.MemorySpace.HBM) for o in out_shape]

    @pl.kernel(mesh=plsc.ScalarSubcoreMesh(axis_name="sequencer", num_cores=1), name=name, scratch_types=tuple(sems),
               compiler_params=pltpu.CompilerParams(collective_id=sequencer_id))
    def launch(send_sems, recv_sems, local_sems):
        exchange(in_refs, out_refs, send_sems, recv_sems, local_sems, own_barrier=True)

    launch()
    return [r[...] for r in out_refs]


def _grad_exchange(name, gs, sequencer_id=None):
    nq = len(gs)

    def exchange(ins, outs, send_sems, recv_sems, local_sems, own_barrier):
        x, y, c, chips = _place()
        peers = [(2 * r + e, (px, py, c if e == 0 else 1 - c)) for r, (px, py) in enumerate(chips) for e in (0, 1)]
        peers.append((6, (x, y, 1 - c)))
        if own_barrier:
            barrier = pltpu.get_barrier_semaphore()
            for _, peer in peers:
                pl.semaphore_signal(barrier, inc=1, device_id=peer, device_id_type=MESH)
            pl.semaphore_wait(barrier, len(peers))
        started, local_all, remote_all = [], [], []
        for q in range(nq):
            cps, whole = _local_copies(ins[q].at[:, 2 * x + y, c], outs[q].at[:, 7], local_sems.at[q])
            started += cps
            local_all.append(whole)
            for slot, (px, py, pc) in peers:
                k = 7 * q + slot
                cps, whole = _remote_copies(ins[q].at[:, 2 * px + py, pc], outs[q].at[:, slot], send_sems.at[k],
                                            recv_sems.at[k], (px, py, pc))
                started += cps
                remote_all.append(whole)
        for cp in started:
            cp.start()
        for cp in remote_all:
            cp.wait_recv()
        for cp in remote_all:
            cp.wait_send()
        for cp in local_all:
            cp.wait()

    out_shape = tuple(SDS((g.shape[0], N_DEV, g.shape[3], g.shape[4]), g.dtype) for g in gs)
    return _launch_exchange(exchange, name, gs, out_shape, _sem_scratch(7 * nq, nq), sequencer_id)


def _pair_share(name, rs, sequencer_id=None):
    nq = len(rs)

    def exchange(ins, outs, send_sems, recv_sems, local_sems, own_barrier):
        x, y, c, _ = _place()
        if own_barrier:
            barrier = pltpu.get_barrier_semaphore()
            pl.semaphore_signal(barrier, inc=1, device_id=(x, y, 1 - c), device_id_type=MESH)
            pl.semaphore_wait(barrier, 1)
        started, local_all, send_all, recv_all = [], [], [], []
        for q in range(nq):
            cps, whole = _local_copies(ins[q], outs[q].at[:, c], local_sems.at[q])
            started += cps
            local_all.append(whole)
            cps, whole = _remote_copies(ins[q], outs[q].at[:, c], send_sems.at[q], recv_sems.at[q], (x, y, 1 - c))
            started += cps
            send_all.append(whole)
            recv_all.append(_remote_copies(ins[q], outs[q].at[:, 1 - c], send_sems.at[q], recv_sems.at[q],
                                           (x, y, 1 - c))[1])
        for cp in started:
            cp.start()
        for cp in recv_all:
            cp.wait_recv()
        for cp in send_all:
            cp.wait_send()
        for cp in local_all:
            cp.wait()

    out_shape = tuple(SDS((r.shape[0], 2) + r.shape[1:], r.dtype) for r in rs)
    return _launch_exchange(exchange, name, rs, out_shape, _sem_scratch(nq, nq), sequencer_id)


def _pack(arrs, rows_multiple=8):
    flat = jnp.concatenate([a.astype(F32).reshape(-1) for a in arrs])
    pad = (-flat.shape[0]) % (128 * rows_multiple)
    return jnp.pad(flat, (0, pad)).reshape(-1, 128)


def _unpack(slab, shapes):
    flat = slab.reshape(-1)
    out, off = [], 0
    for shp in shapes:
        n = 1
        for d in shp:
            n *= d
        out.append(flat[off:off + n].reshape(shp))
        off += n
    return out


def _shard_last(a, chip, n):
    return lax.dynamic_slice_in_dim(a, chip * n, n, axis=a.ndim - 1)


def kernel(x, c, w_mod, b_mod, norm_g, sc_w_in, sc_conv, sc_w_out, pool_w, pool_b, pool_scale, cf_w_pw1, cf_b_pw1, cf_w_dw, cf_b_dw, cf_ln_g, cf_ln_b, cf_w_pw2, cf_b_pw2, ffn_w_up, ffn_conv, ffn_b_conv, ffn_w_down, loss_target, m_w_mod, m_b_mod, m_norm_g, m_sc_w_in, m_sc_conv, m_sc_w_out, m_pool_w, m_pool_b, m_pool_scale, m_cf_w_pw1, m_cf_b_pw1, m_cf_w_dw, m_cf_b_dw, m_cf_ln_g, m_cf_ln_b, m_cf_w_pw2, m_cf_b_pw2, m_ffn_w_up, m_ffn_conv, m_ffn_b_conv, m_ffn_w_down, v_w_mod, v_b_mod, v_norm_g, v_sc_w_in, v_sc_conv, v_sc_w_out, v_pool_w, v_pool_b, v_pool_scale, v_cf_w_pw1, v_cf_b_pw1, v_cf_w_dw, v_cf_b_dw, v_cf_ln_g, v_cf_ln_b, v_cf_w_pw2, v_cf_b_pw2, v_ffn_w_up, v_ffn_conv, v_ffn_b_conv, v_ffn_w_down):
    ax, ay, ac = lax.axis_index("x"), lax.axis_index("y"), lax.axis_index("c")
    chip = 2 * ax + ay
    dev = 4 * ax + 2 * ay + ac
    xs = x[0]
    target = loss_target[0]

    small_sharded = [norm_g, sc_conv, cf_b_pw1, cf_w_dw, cf_b_dw, cf_ln_g, cf_ln_b, cf_b_pw2, ffn_conv]
    slab = _pack([c] + small_sharded)
    gathered = _allgather_small("gather_small_params", slab, False).reshape(N_DEV, -1, 128)
    parts = [_unpack(gathered[d], [c.shape] + [a.shape for a in small_sharded]) for d in range(N_DEV)]
    c_all = jnp.concatenate([p[0] for p in parts], axis=0)
    full = [jnp.concatenate([parts[2 * j][1 + i] for j in range(N_CHIPS)], axis=-1)
            for i in range(len(small_sharded))]
    norm_g_f, sc_conv_f, cf_b_pw1_f, cf_w_dw_f, cf_b_dw_f, cf_ln_g_f, cf_ln_b_f, cf_b_pw2_f, ffn_conv_f = full
    c16 = jnp.pad(c_all, ((0, 8), (0, 0)))

    n_mod = w_mod.shape[2]
    b_sh = jnp.broadcast_to(_shard_last(b_mod, chip, n_mod)[:, None, :], (DEPTH, 8, n_mod))
    mod_part = _mod_fwd(c16, w_mod, b_sh)
    mod_g = _allgather_small("gather_mod", mod_part.reshape(DEPTH * 16, n_mod), False)
    mod_g = mod_g.reshape(N_DEV, DEPTH, 16, n_mod)
    mod_mine = jnp.concatenate(
        [lax.dynamic_index_in_dim(mod_g[2 * j], dev, axis=1, keepdims=False) for j in range(N_CHIPS)], axis=-1)
    mod = mod_mine.reshape(DEPTH, 6, D)

    bf = lambda a: a.astype(BF)
    rows = lambda w: w.reshape(w.shape[0], w.shape[1] * w.shape[2], w.shape[3])
    after = lambda a, done: a + (done[(0,) * done.ndim] * 0).astype(a.dtype)
    w_in_0, w_out_0 = _gather_weights("gather_weights", [bf(sc_w_in[0:1]), bf(sc_w_out[0:1])])
    w_up_0, w_down_0 = _gather_weights(
        "gather_weights_first", [after(bf(ffn_w_up[0:1]), w_in_0), bf(ffn_w_down[0:1])], GATHER_FIRST_ID)
    w_in_r, w_out_r, pool_f, pw1_f, pw2_f, w_up_r, w_down_r = _gather_weights(
        "gather_weights_rest", [after(bf(sc_w_in[1:]), w_in_0), bf(sc_w_out[1:]), bf(pool_w[0]), bf(cf_w_pw1),
                                bf(cf_w_pw2), bf(ffn_w_up[1:]), bf(ffn_w_down[1:])], GATHER_REST_ID)
    pool_f = pool_f.reshape(4, POOL_GROUP, POOL_GROUP)
    pw1_f, pw2_f = (pw1_f, 0), (rows(pw2_f), 0)
    w_in_f = {0: (w_in_0, 0), 1: (w_in_r, 0)}
    w_out_f = {0: (rows(w_out_0), 0), 1: (rows(w_out_r), 0)}
    w_up_f = {i: (w_up_0, 0) if i == 0 else (w_up_r, i - 1) for i in range(DEPTH)}
    w_down_f = {i: (rows(w_down_0), 0) if i == 0 else (rows(w_down_r), i - 1) for i in range(DEPTH)}

    zero_d = jnp.zeros((D,), F32)
    mods = [[mod[i, k] for k in range(6)] for i in range(DEPTH)]
    saved = []
    xcur = xs
    h_next = None
    for i in range(DEPTH):
        kind, j = i % 3, i // 3
        sh1, sc1, g1, sh2, sc2, g2 = mods[i]
        st = {"x0": xcur}
        m_bias = zero_d
        if kind != 1:
            h = h_next if h_next is not None else _pro_fwd(f"pro1_fwd_{i}", xcur, _vecs([norm_g_f[i, 0], sc1, sh1], D))
        if kind == 0:
            p = _mm_nn(f"sc_in_{i}", h, *w_in_f[j],True, BF)
            z = _sc_act_fwd(f"sc_act_fwd_{i}", p, _vecs(list(sc_conv_f[j]), D))
            m = _mm_nn(f"sc_out_{i}", z, *w_out_f[j],False, F32)
            st.update(h=h, p=p)
        elif kind == 1:
            pool_vec = _vecs([norm_g_f[i, 0], sc1, sh1, pool_b[j], pool_scale[j]], D)
            pooled, m = _pool_fwd(f"pool_fwd_{i}", xcur, pool_f, pool_vec)
            st.update(pooled=pooled)
        else:
            p = _mm_nn(f"cf_pw1_{i}", h, *pw1_f,True, BF)
            taps = jnp.pad(cf_w_dw_f[j], ((0, 1), (0, 0)))
            cf_vec = _vecs([cf_b_pw1_f[j, :D], cf_b_pw1_f[j, D:], cf_b_dw_f[j], cf_ln_g_f[j], cf_ln_b_f[j]], D)
            wact, cv = _cf_act_fwd(f"cf_act_fwd_{i}", p, taps, cf_vec)
            m = _mm_nn(f"cf_pw2_{i}", wact, *pw2_f,False, F32)
            m_bias = cf_b_pw2_f[j]
            st.update(h=h, p=p, wact=wact, cv=cv, taps=taps, cf_vec=cf_vec)
        x1, h2 = _epi_pro_fwd(f"epi1_fwd_{i}", xcur, m,
                              _vecs([g1, norm_g_f[i, 1], m_bias, norm_g_f[i, 2], sc2, sh2], D), True)
        st.update(m=m, x1=x1, m_bias=m_bias)
        up = _mm_nn(f"ffn_up_{i}", h2, *w_up_f[i],True, BF)
        ffn_vec = _vecs(list(ffn_conv_f[i]) + [ffn_b_conv[i]], F)
        a = _ffn_act_fwd(f"ffn_act_fwd_{i}", up, ffn_vec)
        f = _mm_nn(f"ffn_down_{i}", a, *w_down_f[i],False, F32, tm=512)
        nxt = i + 1
        fuse_next = nxt < DEPTH and nxt % 3 != 1
        rows = [g2, norm_g_f[i, 3], zero_d]
        if fuse_next:
            rows += [norm_g_f[nxt, 0], mods[nxt][1], mods[nxt][0]]
        xcur, h_next = _epi_pro_fwd(f"epi2_fwd_{i}", x1, f, _vecs(rows, D), fuse_next)
        st.update(h2=h2, up=up, a=a, f=f, ffn_vec=ffn_vec)
        saved.append(st)

    dy, loss_cols = _loss_fwd_bwd(xcur, target)
    loss = lax.psum(jnp.sum(loss_cols[0]), ("x", "y", "c"))

    dmod = [None] * DEPTH
    d_norm_g = [None] * DEPTH
    d_sc_conv = [None, None]
    d_ffn_conv, d_ffn_b = [None] * DEPTH, [None] * DEPTH
    g_w_in, g_w_out, g_w_up, g_w_down = [None, None], [None, None], [None] * DEPTH, [None] * DEPTH
    def halves(gl):
        g = jnp.stack(gl) if isinstance(gl, list) else gl
        if g.ndim == 3:
            g = g.reshape(g.shape[0], N_CHIPS, g.shape[1] // N_CHIPS, g.shape[2])
        nl, _, k, n = g.shape
        return g.reshape(nl, N_CHIPS, 2, k // 2, n)

    dxo = dy
    last = DEPTH - 1
    pend = _epi_bwd(f"epi2_bwd_{last}", dy, saved[last]["f"], _vecs([mods[last][5], norm_g_f[last, 3], zero_d], D))
    for i in reversed(range(DEPTH)):
        kind, j = i % 3, i // 3
        st = saved[i]
        sh1, sc1, g1, sh2, sc2, g2 = mods[i]
        df, e2 = pend
        da = _mm_nt(f"ffn_down_dx_{i}", df, *w_down_f[i],False, BF, tn=F // 2)
        dup, fsum = _ffn_act_bwd(f"ffn_act_bwd_{i}", st["up"], da, st["ffn_vec"])
        g_w_down[i] = _mm_tn(f"ffn_down_dw_{i}", st["a"], df, False, tk=F // 2)
        dh2 = _mm_nt(f"ffn_up_dx_{i}", dup, *w_up_f[i],True, F32)
        g_w_up[i] = _mm_tn(f"ffn_up_dw_{i}", st["h2"], dup, True)
        if i == 0:
            ffn0_pieces = _grad_exchange("grad_exchange_ffn0", [halves([g_w_up[0]]), halves([g_w_down[0]])],
                                         GRAD_FFN0_ID)
        dx1, dm, s21 = _pro_epi_bwd(f"pro2_epi1_bwd_{i}", dh2, st["x1"], dxo, st["m"],
                                    _vecs([norm_g_f[i, 2], sc2, g1, norm_g_f[i, 1], st["m_bias"]], D))
        p2, e1 = s21[0:3], s21[3:6]
        if kind == 0:
            dz = _mm_nt(f"sc_out_dx_{i}", dm, *w_out_f[j],False, BF)
            z, dp, ssum = _sc_act_bwd(f"sc_act_bwd_{i}", st["p"], dz, _vecs(list(sc_conv_f[j]), D))
            g_w_out[j] = _mm_tn(f"sc_out_dw_{i}", z, dm, False)
            dh = _mm_nt(f"sc_in_dx_{i}", dp, *w_in_f[j],True, F32)
            g_w_in[j] = _mm_tn(f"sc_in_dw_{i}", st["h"], dp, True)
            d_sc_conv[j] = ssum[0:3]
        elif kind == 1:
            dh, g_pool, psum = _pool_bwd(f"pool_bwd_{i}", st["pooled"], dm, pool_f, _vecs([pool_b[j], pool_scale[j]], D))
        else:
            dwact = _mm_nt(f"cf_pw2_dx_{i}", dm, *pw2_f,False, BF)
            g_pw2 = _mm_tn(f"cf_pw2_dw_{i}", st["wact"], dm, False)
            dp, tsum, csum = _cf_act_bwd(f"cf_act_bwd_{i}", st["p"], st["cv"], dwact, st["taps"], st["cf_vec"])
            dh = _mm_nt(f"cf_pw1_dx_{i}", dp, *pw1_f,True, F32)
            g_pw1 = _mm_tn(f"cf_pw1_dw_{i}", st["h"], dp, True)
            d_cf = dict(b_pw1=jnp.concatenate([csum[3], csum[4]])[None], w_dw=tsum[None, :CF_TAPS], b_dw=csum[0:1],
                        ln_g=csum[1:2], ln_b=csum[2:3], b_pw2=e1[2:3])
        if i > 0:
            prev = i - 1
            dxo, df_prev, s12 = _pro_epi_bwd(f"pro1_epi2_bwd_{i}", dh, st["x0"], dx1, saved[prev]["f"],
                                             _vecs([norm_g_f[i, 0], sc1, mods[prev][5], norm_g_f[prev, 3], zero_d], D))
            p1, pend = s12[0:3], (df_prev, s12[3:6])
        else:
            dxo, p1 = _pro_bwd(f"pro1_bwd_{i}", dh, st["x0"], dx1, _vecs([norm_g_f[i, 0], sc1], D))
        dmod[i] = jnp.concatenate([p1[0], p1[1], e1[0], p2[0], p2[1], e2[0]])
        d_norm_g[i] = jnp.stack([p1[2], e1[1], p2[2], e2[1]])
        if kind == 1:
            d_pool_b, d_pool_scale = psum[0:1], psum[1:2]
        if i == 1:
            rest_pieces = _grad_exchange(
                "grad_exchange_rest",
                [halves([g_w_in[1]]), halves([g_w_out[1]]), halves(g_pool.astype(BF)), halves([g_pw1]), halves([g_pw2]),
                 halves(g_w_up[1:]), halves(g_w_down[1:])], GRAD_REST_ID)
        saved[i] = None
        st = None
        if i == 0:
            grad_x = dxo[None]
        d_ffn_conv[i], d_ffn_b[i] = fsum[1:4], fsum[0]

    small_shapes = [(DEPTH, 6 * D), (DEPTH, 4, D), (2, 3, D), (1, D), (1, D), (1, 2 * D), (1, CF_TAPS, D), (1, D),
                    (1, D), (1, D), (1, D), (DEPTH, 3, F), (DEPTH, F)]
    small = [jnp.stack(dmod), jnp.stack(d_norm_g), jnp.stack(d_sc_conv), d_pool_b, d_pool_scale, d_cf["b_pw1"],
             d_cf["w_dw"], d_cf["b_dw"], d_cf["ln_g"], d_cf["ln_b"], d_cf["b_pw2"], jnp.stack(d_ffn_conv),
             jnp.stack(d_ffn_b)]
    gsl, ssl = _allgather_small("reduce_small_grads", _pack(small), True)
    tot = _unpack(ssl, small_shapes)
    (gb_mod, gt_norm_g, gt_sc_conv, g_pool_b, g_pool_scale, gt_b_pw1, gt_w_dw, gt_b_dw, gt_ln_g, gt_ln_b, gt_b_pw2,
     gt_ffn_conv, g_ffn_b) = tot
    dmod_all = jnp.stack([_unpack(gsl.reshape(N_DEV, -1, 128)[d], small_shapes[:1])[0] for d in range(N_DEV)], axis=1)
    dmod_sh = jnp.pad(_shard_last(dmod_all, chip, n_mod), ((0, 0), (0, 8), (0, 0)))
    g_w_mod = _mod_bwd(c16, dmod_sh)
    g_norm_g = _shard_last(gt_norm_g, chip, D // 4)
    g_sc_conv = _shard_last(gt_sc_conv, chip, D // 4)
    g_b_pw1 = _shard_last(gt_b_pw1, chip, 2 * D // 4)
    g_w_dw = _shard_last(gt_w_dw, chip, D // 4)
    g_b_dw = _shard_last(gt_b_dw, chip, D // 4)
    g_ln_g = _shard_last(gt_ln_g, chip, D // 4)
    g_ln_b = _shard_last(gt_ln_b, chip, D // 4)
    g_b_pw2 = _shard_last(gt_b_pw2, chip, D // 4)
    g_ffn_conv = _shard_last(gt_ffn_conv, chip, F // 4)

    rest_reduced = [_add_slots(f"grad_sum_rest_{q}", p) for q, p in enumerate(rest_pieces)]
    r_in, r_out, r_pool, r_pw1, r_pw2, r_up, r_down = _pair_share("grad_pair_share_rest", rest_reduced, SHARE_REST_ID)
    mixer0_pieces = _grad_exchange("grad_exchange", [after(halves([g_w_in[0]]), rest_reduced[0]), halves([g_w_out[0]])])
    first_reduced = [_add_slots(f"grad_sum_{q}", p) for q, p in enumerate(list(mixer0_pieces) + list(ffn0_pieces))]
    f_in, f_out, f_up, f_down = _pair_share("grad_pair_share_first", first_reduced, SHARE_FIRST_ID)
    cat = lambda a, b, like: jnp.concatenate([a, b], axis=0).reshape(like.shape)
    g_sc_w_in, g_sc_w_out = cat(f_in, r_in, sc_w_in), cat(f_out, r_out, sc_w_out)
    g_ffn_w_up, g_ffn_w_down = cat(f_up, r_up, ffn_w_up), cat(f_down, r_down, ffn_w_down)
    g_pool_w, g_cf_w_pw1, g_cf_w_pw2 = (r_pool.reshape(pool_w.shape), r_pw1.reshape(cf_w_pw1.shape),
                                        r_pw2.reshape(cf_w_pw2.shape))

    def adam_big(name, w, g, m, v):
        shp = w.shape
        two = lambda t: t.reshape(-1, shp[-1])
        return [o.reshape(shp) for o in _adamw(name, two(w), two(g), two(m), two(v))]

    grads = dict(w_mod=g_w_mod, b_mod=gb_mod, norm_g=g_norm_g, sc_w_in=g_sc_w_in, sc_conv=g_sc_conv,
                 sc_w_out=g_sc_w_out, pool_w=g_pool_w, pool_b=g_pool_b, pool_scale=g_pool_scale,
                 cf_w_pw1=g_cf_w_pw1, cf_b_pw1=g_b_pw1, cf_w_dw=g_w_dw, cf_b_dw=g_b_dw, cf_ln_g=g_ln_g,
                 cf_ln_b=g_ln_b, cf_w_pw2=g_cf_w_pw2, cf_b_pw2=g_b_pw2, ffn_w_up=g_ffn_w_up, ffn_conv=g_ffn_conv,
                 ffn_b_conv=g_ffn_b, ffn_w_down=g_ffn_w_down)
    weights = dict(w_mod=w_mod, b_mod=b_mod, norm_g=norm_g, sc_w_in=sc_w_in, sc_conv=sc_conv, sc_w_out=sc_w_out,
                   pool_w=pool_w, pool_b=pool_b, pool_scale=pool_scale, cf_w_pw1=cf_w_pw1, cf_b_pw1=cf_b_pw1,
                   cf_w_dw=cf_w_dw, cf_b_dw=cf_b_dw, cf_ln_g=cf_ln_g, cf_ln_b=cf_ln_b, cf_w_pw2=cf_w_pw2,
                   cf_b_pw2=cf_b_pw2, ffn_w_up=ffn_w_up, ffn_conv=ffn_conv, ffn_b_conv=ffn_b_conv,
                   ffn_w_down=ffn_w_down)
    m_in = dict(w_mod=m_w_mod, b_mod=m_b_mod, norm_g=m_norm_g, sc_w_in=m_sc_w_in, sc_conv=m_sc_conv,
                sc_w_out=m_sc_w_out, pool_w=m_pool_w, pool_b=m_pool_b, pool_scale=m_pool_scale,
                cf_w_pw1=m_cf_w_pw1, cf_b_pw1=m_cf_b_pw1, cf_w_dw=m_cf_w_dw, cf_b_dw=m_cf_b_dw, cf_ln_g=m_cf_ln_g,
                cf_ln_b=m_cf_ln_b, cf_w_pw2=m_cf_w_pw2, cf_b_pw2=m_cf_b_pw2, ffn_w_up=m_ffn_w_up,
                ffn_conv=m_ffn_conv, ffn_b_conv=m_ffn_b_conv, ffn_w_down=m_ffn_w_down)
    v_in = dict(w_mod=v_w_mod, b_mod=v_b_mod, norm_g=v_norm_g, sc_w_in=v_sc_w_in, sc_conv=v_sc_conv,
                sc_w_out=v_sc_w_out, pool_w=v_pool_w, pool_b=v_pool_b, pool_scale=v_pool_scale,
                cf_w_pw1=v_cf_w_pw1, cf_b_pw1=v_cf_b_pw1, cf_w_dw=v_cf_w_dw, cf_b_dw=v_cf_b_dw, cf_ln_g=v_cf_ln_g,
                cf_ln_b=v_cf_ln_b, cf_w_pw2=v_cf_w_pw2, cf_b_pw2=v_cf_b_pw2, ffn_w_up=v_ffn_w_up,
                ffn_conv=v_ffn_conv, ffn_b_conv=v_ffn_b_conv, ffn_w_down=v_ffn_w_down)
    names = list(weights)
    big_names = ["w_mod", "sc_w_in", "sc_w_out", "pool_w", "cf_w_pw1", "cf_w_pw2", "ffn_w_up", "ffn_w_down"]
    small_names = [n for n in names if n not in big_names]
    delta, new_m, new_v = {}, {}, {}
    for n in big_names:
        delta[n], new_m[n], new_v[n] = adam_big(f"adamw_{n}", weights[n], grads[n], m_in[n], v_in[n])
    grads = {n: grads[n].reshape(weights[n].shape) for n in names}
    sm_shapes = [weights[n].shape for n in small_names]
    sd, sm, sv = _adamw("adamw_small", _pack([weights[n] for n in small_names]), _pack([grads[n] for n in small_names]),
                        _pack([m_in[n] for n in small_names]), _pack([v_in[n] for n in small_names]))
    for n, d_, m_, v_ in zip(small_names, _unpack(sd, sm_shapes), _unpack(sm, sm_shapes), _unpack(sv, sm_shapes)):
        delta[n], new_m[n], new_v[n] = d_, m_, v_

    return (loss, grad_x, *[grads[n] for n in names], *[delta[n] for n in names], *[new_m[n] for n in names],
            *[new_v[n] for n in names])
```
